```python
import math
import jax, jax.numpy as jnp
from jax import lax
import numpy as np

D_MODEL = 1024
BATCH = 4
SEQ = 8192
DEPTH = 1
DEC_BATCH = 32
DEC_SEQ = 16
PAST_LEN = 4096

CHUNK = 64
DN_HEADS = 4
DN_HEAD_DIM = 128
DN_WIDTH = DN_HEADS * DN_HEAD_DIM
DN_CONV = 4
S5_WIDTH = D_MODEL - DN_WIDTH
S5_GROUP = 16
S5_GROUPS = S5_WIDTH // S5_GROUP
S5_STATE = 64
IN_DIM = 4 * DN_WIDTH + 2 * DN_HEADS + S5_WIDTH
D_FF = 2816
FFN_CONV = 3
EPS = 1e-6

kernel_name = 'hymba_deltanet_s5_streaming_step'


def rmsnorm(x, g):
    xf = x.astype(jnp.float32)
    y = xf * lax.rsqrt(jnp.mean(xf * xf, axis=-1, keepdims=True) + EPS) * g.astype(jnp.float32)
    return y.astype(x.dtype)


def l2norm(x):
    return x * lax.rsqrt(jnp.sum(x * x, axis=-1, keepdims=True) + EPS)


def causal_dwconv(x, buf, w):
    K = w.shape[0]
    L = x.shape[1]
    xx = jnp.concatenate([buf.astype(x.dtype), x], axis=1)
    out = xx[:, 0:L] * w[0]
    for j in range(1, K):
        out = out + xx[:, j:j + L] * w[j]
    return out, xx[:, L:]


def gated_delta_rule(q, k, v, g, beta, s0):
    bsz, L, H, dk = q.shape
    dv = v.shape[-1]
    C = CHUNK if L % CHUNK == 0 else L
    N = L // C

    def blocks(t):
        t = t.reshape((bsz, N, C, H) + t.shape[3:])
        perm = (1, 0, 3, 2) + tuple(range(4, t.ndim))
        return t.transpose(perm)

    causal = jnp.tril(jnp.ones((C, C), dtype=bool))
    strict = jnp.tril(jnp.ones((C, C), dtype=bool), -1)
    eye = jnp.eye(C, dtype=q.dtype)

    def step(S, blk):
        qc, kc, vc, gc, bc = blk
        G = jnp.cumsum(gc, axis=-1)
        decay = jnp.exp(jnp.where(causal, G[..., :, None] - G[..., None, :], -jnp.inf))
        kk = jnp.einsum('bhik,bhjk->bhij', kc, kc)
        A = eye + jnp.where(strict, bc[..., :, None] * kk * decay, 0.0)
        rhs = jnp.concatenate([vc * bc[..., None], kc * (bc * jnp.exp(G))[..., None]], axis=-1)
        sol = lax.linalg.triangular_solve(A, rhs, left_side=True, lower=True, unit_diagonal=True)
        u = sol[..., :dv] - jnp.einsum('bhck,bhkv->bhcv', sol[..., dv:], S)
        qk = jnp.einsum('bhik,bhjk->bhij', qc, kc) * decay
        o = (jnp.einsum('bhck,bhkv->bhcv', qc * jnp.exp(G)[..., None], S)
             + jnp.einsum('bhij,bhjv->bhiv', qk, u))
        g_last = G[..., -1:]
        S = (S * jnp.exp(g_last)[..., None]
             + jnp.einsum('bhck,bhcv->bhkv', kc * jnp.exp(g_last - G)[..., None], u))
        return S, o

    S, o = lax.scan(step, s0, (blocks(q), blocks(k), blocks(v), blocks(g), blocks(beta)))
    o = o.transpose(1, 0, 3, 2, 4).reshape(bsz, L, H, dv)
    return o, S


def complex_affine_combine(e1, e2):
    a1r, a1i, b1r, b1i = e1
    a2r, a2i, b2r, b2i = e2
    return (a2r * a1r - a2i * a1i,
            a2r * a1i + a2i * a1r,
            a2r * b1r - a2i * b1i + b2r,
            a2r * b1i + a2i * b1r + b2i)


def s5_ssm(u, x0_re, x0_im, A_re, A_im, log_dt, B_re, B_im, C_re, C_im, Dv):
    bsz, L, _ = u.shape
    dt = jnp.exp(log_dt)[:, None]
    mag = jnp.exp(A_re * dt)
    ang = A_im * dt
    lam_re = mag * jnp.cos(ang)
    lam_im = mag * jnp.sin(ang)
    den = A_re * A_re + A_im * A_im
    f_re = ((lam_re - 1.0) * A_re + lam_im * A_im) / den
    f_im = (lam_im * A_re - (lam_re - 1.0) * A_im) / den
    Bb_re = f_re[..., None] * B_re - f_im[..., None] * B_im
    Bb_im = f_re[..., None] * B_im + f_im[..., None] * B_re
    ug = u.reshape(bsz, L, S5_GROUPS, S5_GROUP)
    b_re = jnp.einsum('blgh,gph->blgp', ug, Bb_re)
    b_im = jnp.einsum('blgh,gph->blgp', ug, Bb_im)
    b_re = b_re.at[:, 0].add(lam_re * x0_re - lam_im * x0_im)
    b_im = b_im.at[:, 0].add(lam_re * x0_im + lam_im * x0_re)
    a_re = jnp.broadcast_to(lam_re, (1, L) + lam_re.shape)
    a_im = jnp.broadcast_to(lam_im, (1, L) + lam_im.shape)
    _, _, xr, xi = lax.associative_scan(complex_affine_combine, (a_re, a_im, b_re, b_im), axis=1)
    y = jnp.einsum('blgp,ghp->blgh', xr, C_re) - jnp.einsum('blgp,ghp->blgh', xi, C_im)
    y = y.reshape(bsz, L, S5_WIDTH) + Dv * u
    return y, xr[:, -1], xi[:, -1]


def layer(x, conv_dn, s_dn, s5_re, s5_im, conv_ffn, lw):
    (n1, w_in, dn_cw, A_log, dt_bias, dn_g, A_re, A_im, log_dt, B_re, B_im, C_re, C_im, Dv,
     glu_w, glu_b, s5_g, w_out, n2, w_up, fcw, fcb, w_down) = lw
    f32 = jnp.float32
    bsz, L, _ = x.shape
    h = rmsnorm(x, n1)
    p = h @ w_in
    o1, o2 = 3 * DN_WIDTH, 4 * DN_WIDTH
    o3, o4 = o2 + DN_HEADS, o2 + 2 * DN_HEADS
    qkv_raw, z, b_raw, a_raw, u = p[..., :o1], p[..., o1:o2], p[..., o2:o3], p[..., o3:o4], p[..., o4:]

    qkv, conv_dn_new = causal_dwconv(qkv_raw, conv_dn, dn_cw)
    qkv = jax.nn.silu(qkv).astype(f32)
    q = l2norm(qkv[..., :DN_WIDTH].reshape(bsz, L, DN_HEADS, DN_HEAD_DIM)) * (DN_HEAD_DIM ** -0.5)
    k = l2norm(qkv[..., DN_WIDTH:2 * DN_WIDTH].reshape(bsz, L, DN_HEADS, DN_HEAD_DIM))
    v = qkv[..., 2 * DN_WIDTH:].reshape(bsz, L, DN_HEADS, DN_HEAD_DIM)
    beta = jax.nn.sigmoid(b_raw.astype(f32))
    g = -jnp.exp(A_log.astype(f32)) * jax.nn.softplus(a_raw.astype(f32) + dt_bias.astype(f32))
    o_dn, s_dn_new = gated_delta_rule(q, k, v, g, beta, s_dn.astype(f32))
    zg = jax.nn.silu(z.astype(f32)).reshape(bsz, L, DN_HEADS, DN_HEAD_DIM)
    o_dn = (rmsnorm(o_dn, dn_g) * zg).reshape(bsz, L, DN_WIDTH).astype(x.dtype)

    uf = u.astype(f32)
    y5, xr, xi = s5_ssm(uf, s5_re.astype(f32), s5_im.astype(f32), A_re.astype(f32), A_im.astype(f32),
                        log_dt.astype(f32), B_re.astype(f32), B_im.astype(f32), C_re.astype(f32),
                        C_im.astype(f32), Dv.astype(f32))
    y5 = jax.nn.gelu(y5).astype(x.dtype)
    gl = y5 @ glu_w + glu_b
    o_s5 = gl[..., :S5_WIDTH] * jax.nn.sigmoid(gl[..., S5_WIDTH:])
    o_s5 = rmsnorm(o_s5, s5_g)

    x = x + jnp.concatenate([o_dn, o_s5], axis=-1) @ w_out

    h2 = rmsnorm(x, n2)
    up = h2 @ w_up
    gate, val = up[..., :D_FF], up[..., D_FF:]
    gate_c, conv_ffn_new = causal_dwconv(gate, conv_ffn, fcw)
    x = x + (jax.nn.silu(gate_c + fcb) * val) @ w_down
    new = (conv_dn_new, s_dn_new.astype(x.dtype), xr.astype(x.dtype), xi.astype(x.dtype), conv_ffn_new)
    return x, new


def run_trunk(x, conv_dn, s_dn, s5_re, s5_im, conv_ffn, weights, final_norm_g):
    outs = ([], [], [], [], [])
    for l in range(DEPTH):
        lw = tuple(w[l] for w in weights)
        x, new = layer(x, conv_dn[l], s_dn[l], s5_re[l], s5_im[l], conv_ffn[l], lw)
        for lst, s in zip(outs, new):
            lst.append(s)
    y = rmsnorm(x, final_norm_g)
    return y, tuple(jnp.stack(lst) for lst in outs)


def setup_inputs(seed: int = 0) -> dict:
    key = jax.random.key(seed)
    ks = jax.random.split(key, 40)
    ctr = [0]

    def nk():
        ctr[0] += 1
        return ks[ctr[0] - 1]

    f32 = jnp.float32

    def nrm(shape, s):
        return jax.random.normal(nk(), shape, f32) * s

    def unif(shape, lo, hi):
        return jax.random.uniform(nk(), shape, f32, lo, hi)

    x_prompt = nrm((BATCH, SEQ, D_MODEL), 1.0)
    x_sample = nrm((DEC_BATCH, DEC_SEQ, D_MODEL), 1.0)
    cache_dn_conv = nrm((DEPTH, DEC_BATCH, DN_CONV - 1, 3 * DN_WIDTH), 1.0)
    state_dn = nrm((DEPTH, DEC_BATCH, DN_HEADS, DN_HEAD_DIM, DN_HEAD_DIM), 0.05)
    state_s5_re = nrm((DEPTH, DEC_BATCH, S5_GROUPS, S5_STATE), 0.1)
    state_s5_im = nrm((DEPTH, DEC_BATCH, S5_GROUPS, S5_STATE), 0.1)
    cache_ffn_conv = nrm((DEPTH, DEC_BATCH, FFN_CONV - 1, D_FF), 1.0)

    norm1_g = 1.0 + nrm((DEPTH, D_MODEL), 0.02)
    w_in = nrm((DEPTH, D_MODEL, IN_DIM), D_MODEL ** -0.5)
    dn_conv_w = nrm((DEPTH, DN_CONV, 3 * DN_WIDTH), DN_CONV ** -0.5)
    dn_A_log = jnp.log(unif((DEPTH, DN_HEADS), 1.0, 16.0))
    dt = jnp.exp(unif((DEPTH, DN_HEADS), math.log(1e-3), math.log(1e-1)))
    dn_dt_bias = dt + jnp.log(-jnp.expm1(-dt))
    dn_norm_g = 1.0 + nrm((DEPTH, DN_HEAD_DIM), 0.02)
    n_idx = jnp.arange(S5_STATE, dtype=f32)
    s5_A_re = -0.5 + nrm((DEPTH, S5_GROUPS, S5_STATE), 0.01)
    s5_A_im = math.pi * n_idx + nrm((DEPTH, S5_GROUPS, S5_STATE), 0.01)
    s5_log_dt = unif((DEPTH, S5_GROUPS), math.log(1e-3), math.log(1e-1))
    s5_B_re = nrm((DEPTH, S5_GROUPS, S5_STATE, S5_GROUP), (2 * S5_GROUP) ** -0.5)
    s5_B_im = nrm((DEPTH, S5_GROUPS, S5_STATE, S5_GROUP), (2 * S5_GROUP) ** -0.5)
    s5_C_re = nrm((DEPTH, S5_GROUPS, S5_GROUP, S5_STATE), (2 * S5_STATE) ** -0.5)
    s5_C_im = nrm((DEPTH, S5_GROUPS, S5_GROUP, S5_STATE), (2 * S5_STATE) ** -0.5)
    s5_D = nrm((DEPTH, S5_WIDTH), 1.0)
    s5_glu_w = nrm((DEPTH, S5_WIDTH, 2 * S5_WIDTH), S5_WIDTH ** -0.5)
    s5_glu_b = nrm((DEPTH, 2 * S5_WIDTH), 0.01)
    s5_norm_g = 1.0 + nrm((DEPTH, S5_WIDTH), 0.02)
    w_out = nrm((DEPTH, DN_WIDTH + S5_WIDTH, D_MODEL), (DN_WIDTH + S5_WIDTH) ** -0.5)
    norm2_g = 1.0 + nrm((DEPTH, D_MODEL), 0.02)
    w_up = nrm((DEPTH, D_MODEL, 2 * D_FF), D_MODEL ** -0.5)
    ffn_conv_w = nrm((DEPTH, FFN_CONV, D_FF), FFN_CONV ** -0.5)
    ffn_conv_b = nrm((DEPTH, D_FF), 0.01)
    w_down = nrm((DEPTH, D_FF, D_MODEL), D_FF ** -0.5)
    final_norm_g = 1.0 + nrm((D_MODEL,), 0.02)
    return {'x_prompt': x_prompt, 'x_sample': x_sample,
            'cache_dn_conv': cache_dn_conv, 'state_dn': state_dn,
            'state_s5_re': state_s5_re, 'state_s5_im': state_s5_im, 'cache_ffn_conv': cache_ffn_conv,
            'norm1_g': norm1_g, 'w_in': w_in, 'dn_conv_w': dn_conv_w, 'dn_A_log': dn_A_log,
            'dn_dt_bias': dn_dt_bias, 'dn_norm_g': dn_norm_g, 's5_A_re': s5_A_re, 's5_A_im': s5_A_im,
            's5_log_dt': s5_log_dt, 's5_B_re': s5_B_re, 's5_B_im': s5_B_im, 's5_C_re': s5_C_re,
            's5_C_im': s5_C_im, 's5_D': s5_D, 's5_glu_w': s5_glu_w, 's5_glu_b': s5_glu_b,
            's5_norm_g': s5_norm_g, 'w_out': w_out, 'norm2_g': norm2_g, 'w_up': w_up,
            'ffn_conv_w': ffn_conv_w, 'ffn_conv_b': ffn_conv_b, 'w_down': w_down,
            'final_norm_g': final_norm_g}


def reference(x_prompt, x_sample, cache_dn_conv, state_dn, state_s5_re, state_s5_im, cache_ffn_conv,
              norm1_g, w_in, dn_conv_w, dn_A_log, dn_dt_bias, dn_norm_g, s5_A_re, s5_A_im, s5_log_dt,
              s5_B_re, s5_B_im, s5_C_re, s5_C_im, s5_D, s5_glu_w, s5_glu_b, s5_norm_g, w_out,
              norm2_g, w_up, ffn_conv_w, ffn_conv_b, w_down, final_norm_g):
    weights = (norm1_g, w_in, dn_conv_w, dn_A_log, dn_dt_bias, dn_norm_g, s5_A_re, s5_A_im, s5_log_dt,
               s5_B_re, s5_B_im, s5_C_re, s5_C_im, s5_D, s5_glu_w, s5_glu_b, s5_norm_g, w_out,
               norm2_g, w_up, ffn_conv_w, ffn_conv_b, w_down)
    bp = x_prompt.shape[0]
    dt_p = x_prompt.dtype
    z_conv = jnp.zeros((DEPTH, bp, DN_CONV - 1, 3 * DN_WIDTH), dt_p)
    z_dn = jnp.zeros((DEPTH, bp, DN_HEADS, DN_HEAD_DIM, DN_HEAD_DIM), dt_p)
    z_s5 = jnp.zeros((DEPTH, bp, S5_GROUPS, S5_STATE), dt_p)
    z_ffn = jnp.zeros((DEPTH, bp, FFN_CONV - 1, D_FF), dt_p)
    y_prompt, pst = run_trunk(x_prompt, z_conv, z_dn, z_s5, z_s5, z_ffn, weights, final_norm_g)
    p_dn_conv, p_dn_state, p_s5_re, p_s5_im, p_ffn_conv = pst
    y_sample, sst = run_trunk(x_sample, cache_dn_conv, state_dn, state_s5_re, state_s5_im, cache_ffn_conv,
                              weights, final_norm_g)
    s_dn_conv, s_dn_state, s_s5_re, s_s5_im, s_ffn_conv = sst
    return (y_prompt, y_sample, p_dn_conv, p_dn_state, p_s5_re, p_s5_im, p_ffn_conv,
            s_dn_conv, s_dn_state, s_s5_re, s_s5_im, s_ffn_conv)
```

```python
import functools

import jax
import jax.numpy as jnp
from jax import lax
from jax.experimental import pallas as pl
from jax.experimental.pallas import tpu as pltpu

F32 = jnp.float32
BF16 = jnp.bfloat16
EPS = 1e-6

D_MODEL = 1024
DN_HEADS = 4
DN_HEAD_DIM = 128
DN_WIDTH = DN_HEADS * DN_HEAD_DIM
DN_CONV = 4
QKV_WIDTH = 3 * DN_WIDTH
S5_WIDTH = D_MODEL - DN_WIDTH
S5_GROUP = 16
S5_GROUPS = S5_WIDTH // S5_GROUP
S5_STATE = 64
S5_LANES = S5_GROUPS * S5_STATE
D_FF = 2816
FFN_CONV = 3
CHUNK = 64

SUBLANES = 8
LANES = 128
BA_LANE_G = DN_HEADS
IN_PAD = QKV_WIDTH + DN_WIDTH + S5_WIDTH + LANES

VMEM_LIMIT = 56 * 1024 * 1024


def _dot(a, b):
    return jnp.dot(a, b, preferred_element_type=F32)


def _dot_nt(a, b):
    return lax.dot_general(a, b, (((1,), (1,)), ((), ())), preferred_element_type=F32)


def _dot_tn(a, b):
    return lax.dot_general(a, b, (((0,), (0,)), ((), ())), preferred_element_type=F32)


def _split_bf16(a):
    hi = a.astype(BF16)
    lo = (a - hi.astype(F32)).astype(BF16)
    return hi, lo


def _dot3(a, b):
    ah, al = _split_bf16(a)
    bh, bl = _split_bf16(b)
    return _dot(ah, bh) + (_dot(al, bh) + _dot(ah, bl))


def _rms(x, g):
    return x * lax.rsqrt(jnp.mean(x * x, axis=-1, keepdims=True) + EPS) * g


def _const_spec(shape):
    nd = len(shape)
    return pl.BlockSpec(shape, lambda *_: (0,) * nd, pipeline_mode=pl.Buffered(1))


def _params(n_axes):
    return pltpu.CompilerParams(dimension_semantics=("arbitrary",) * n_axes,
                                vmem_limit_bytes=VMEM_LIMIT)


def _s5prep_kernel(are_ref, aim_ref, ldt_ref, bre_ref, bim_ref, bcat_ref, tab_ref):
    are = are_ref[...]
    aim = aim_ref[...]
    dt = jnp.exp(ldt_ref[...])
    mag = jnp.exp(are * dt)
    ang = aim * dt
    lr = mag * jnp.cos(ang)
    li = mag * jnp.sin(ang)
    den = are * are + aim * aim
    f_re = ((lr - 1.0) * are + li * aim) / den
    f_im = (li * are - (lr - 1.0) * aim) / den
    bre = bre_ref[...]
    bim = bim_ref[...]
    bcat_ref[:, :S5_LANES] = (f_re * bre - f_im * bim).astype(BF16)
    bcat_ref[:, S5_LANES:] = (f_re * bim + f_im * bre).astype(BF16)

    pw = [(lr, li)]
    for _ in range(SUBLANES - 1):
        pr, pi = pw[-1]
        pw.append((pr * lr - pi * li, pr * li + pi * lr))
    row = lax.broadcasted_iota(jnp.int32, (SUBLANES, S5_LANES), 0)
    zero = jnp.zeros((SUBLANES, S5_LANES), F32)
    pre, pim = zero, zero
    for r in range(SUBLANES):
        pre = jnp.where(row == r, pw[r][0], pre)
        pim = jnp.where(row == r, pw[r][1], pim)
    tab_ref[0:8, :] = pre
    tab_ref[8:16, :] = pim
    for lvl, d in enumerate((1, 2, 4)):
        tab_ref[16 + 16 * lvl:24 + 16 * lvl, :] = jnp.where(row >= d, pw[d - 1][0], zero)
        tab_ref[24 + 16 * lvl:32 + 16 * lvl, :] = jnp.where(row >= d, pw[d - 1][1], zero)


def _s5_prep(are, aim, ldt, bre_bd, bim_bd):
    return pl.pallas_call(
        _s5prep_kernel,
        out_shape=(jax.ShapeDtypeStruct((S5_WIDTH, 2 * S5_LANES), BF16),
                   jax.ShapeDtypeStruct((64, S5_LANES), F32)),
        compiler_params=pltpu.CompilerParams(vmem_limit_bytes=VMEM_LIMIT),
        name="s5_prep",
    )(are, aim, ldt, bre_bd, bim_bd)


def _inproj_kernel(x_ref, g_ref, w_ref, qkv_ref, z_ref, u_ref, ba_ref):
    h = _rms(x_ref[...], g_ref[...]).astype(BF16)
    o1 = QKV_WIDTH
    o2 = o1 + DN_WIDTH
    o3 = o2 + S5_WIDTH
    qkv_ref[...] = _dot(h, w_ref[:, :o1])
    z_ref[...] = _dot(h, w_ref[:, o1:o2])
    u_ref[...] = _dot(h, w_ref[:, o2:o3])
    ba_ref[...] = _dot(h, w_ref[:, o3:])


def _in_proj(x2d, g, w, tm):
    n = x2d.shape[0]
    row = lambda w_: pl.BlockSpec((tm, w_), lambda i: (i, 0))
    return pl.pallas_call(
        _inproj_kernel,
        grid=(n // tm,),
        in_specs=[row(D_MODEL), _const_spec((1, D_MODEL)), _const_spec((D_MODEL, IN_PAD))],
        out_specs=(row(QKV_WIDTH), row(DN_WIDTH), row(S5_WIDTH), row(LANES)),
        out_shape=(jax.ShapeDtypeStruct((n, QKV_WIDTH), F32),
                   jax.ShapeDtypeStruct((n, DN_WIDTH), F32),
                   jax.ShapeDtypeStruct((n, S5_WIDTH), F32),
                   jax.ShapeDtypeStruct((n, LANES), F32)),
        compiler_params=_params(1),
        name="in_proj",
    )(x2d, g, w)


def _tri_inverse(lm, c):
    ri = lax.broadcasted_iota(jnp.int32, (c, c), 0)
    ci = lax.broadcasted_iota(jnp.int32, (c, c), 1)
    t = jnp.where(ri == ci, 1.0, 0.0) - jnp.where((ri >> 1) == (ci >> 1), lm, 0.0)
    s = 2
    while s < c:
        sh = s.bit_length()
        same = (ri >> sh) == (ci >> sh)
        lower_left = jnp.where(same, jnp.where((ri & s) != 0, jnp.where((ci & s) == 0, lm, 0.0), 0.0), 0.0)
        t = t - _dot3(_dot3(t, lower_left), t)
        s *= 2
    return t


def _dn_kernel(qkv_ref, z_ref, ba_ref, cache_ref, s0_ref, cw_ref, gp_ref, ng_ref,
               o_ref, sfin_ref, halo_s, state_s, *, tt, c):
    t = pl.program_id(1)

    @pl.when(t == 0)
    def _():
        halo_s[...] = cache_ref[0]
        state_s[...] = s0_ref[0]

    x = qkv_ref[0]
    xx = jnp.concatenate([halo_s[...], x], axis=0)
    halo_s[...] = x[tt - SUBLANES:, :]
    cw = cw_ref[...]
    off = SUBLANES - (DN_CONV - 1)
    acc = xx[off:off + tt] * cw[0:1]
    for j in range(1, DN_CONV):
        acc = acc + xx[off + j:off + j + tt] * cw[j:j + 1]
    act = acc * jax.nn.sigmoid(acc)

    ba = ba_ref[0]
    beta = jax.nn.sigmoid(ba)
    g = -jnp.exp(gp_ref[0:1, :]) * jax.nn.softplus(ba + gp_ref[1:2, :])
    row_in_chunk = lax.broadcasted_iota(jnp.int32, (tt, LANES), 0) & (c - 1)
    d = 1
    while d < c:
        g = g + jnp.where(row_in_chunk >= d, pltpu.roll(g, d, axis=0), 0.0)
        d *= 2

    ri = lax.broadcasted_iota(jnp.int32, (c, c), 0)
    ci = lax.broadcasted_iota(jnp.int32, (c, c), 1)
    causal = ri >= ci
    strict = ri > ci
    ng = ng_ref[...]
    z = z_ref[0]

    for ck in range(tt // c):
        r0 = ck * c
        gc = g[r0:r0 + c, :]
        gct = jnp.concatenate([gc, jnp.zeros((LANES - c, LANES), F32)], axis=0).T
        for h in range(DN_HEADS):
            lo = h * DN_HEAD_DIM
            qh = act[r0:r0 + c, lo:lo + DN_HEAD_DIM]
            kh = act[r0:r0 + c, DN_WIDTH + lo:DN_WIDTH + lo + DN_HEAD_DIM]
            vh = act[r0:r0 + c, 2 * DN_WIDTH + lo:2 * DN_WIDTH + lo + DN_HEAD_DIM]
            qh = qh * lax.rsqrt(jnp.sum(qh * qh, axis=-1, keepdims=True) + EPS) * (DN_HEAD_DIM ** -0.5)
            kh = kh * lax.rsqrt(jnp.sum(kh * kh, axis=-1, keepdims=True) + EPS)
            g_col = gc[:, BA_LANE_G + h:BA_LANE_G + h + 1]
            g_row = gct[BA_LANE_G + h:BA_LANE_G + h + 1, :c]
            b_col = beta[r0:r0 + c, h:h + 1]
            decay = jnp.exp(jnp.where(causal, g_col - g_row, -jnp.inf))
            kb = kh.astype(BF16)
            kk = _dot_nt(kb, kb)
            qk = _dot_nt(qh.astype(BF16), kb)
            tinv = _tri_inverse(jnp.where(strict, b_col * kk * decay, 0.0), c)
            e_g = jnp.exp(g_col)
            sol_v = _dot3(tinv, vh * b_col)
            sol_k = _dot3(tinv, kh * (b_col * e_g))
            s_old = state_s[h]
            sb = s_old.astype(BF16)
            u = sol_v - _dot(sol_k.astype(BF16), sb)
            ub = u.astype(BF16)
            o = _dot((qh * e_g).astype(BF16), sb) + _dot((qk * decay).astype(BF16), ub)
            g_last = g_col[c - 1:c, :]
            kd = kh * jnp.exp(g_last - g_col)
            state_s[h] = s_old * jnp.exp(g_last) + _dot_tn(kd.astype(BF16), ub)
            zh = z[r0:r0 + c, lo:lo + DN_HEAD_DIM]
            o = _rms(o, ng) * (zh * jax.nn.sigmoid(zh))
            o_ref[0, r0:r0 + c, lo:lo + DN_HEAD_DIM] = o.astype(BF16)

    sfin_ref[0] = state_s[...]


def _deltanet(qkv, z, ba, cache8, s0, cw, gp, ng, tt, c):
    b, l, _ = qkv.shape
    seq = lambda w_: pl.BlockSpec((1, tt, w_), lambda i, j: (i, j, 0))
    per_b = lambda *s: pl.BlockSpec((1,) + s, lambda i, j: (i,) + (0,) * len(s))
    return pl.pallas_call(
        functools.partial(_dn_kernel, tt=tt, c=c),
        grid=(b, l // tt),
        in_specs=[seq(QKV_WIDTH), seq(DN_WIDTH), seq(LANES), per_b(SUBLANES, QKV_WIDTH),
                  per_b(DN_HEADS, DN_HEAD_DIM, DN_HEAD_DIM),
                  _const_spec((SUBLANES, QKV_WIDTH)), _const_spec((SUBLANES, LANES)),
                  _const_spec((1, DN_HEAD_DIM))],
        out_specs=(seq(DN_WIDTH), per_b(DN_HEADS, DN_HEAD_DIM, DN_HEAD_DIM)),
        out_shape=(jax.ShapeDtypeStruct((b, l, DN_WIDTH), BF16),
                   jax.ShapeDtypeStruct((b, DN_HEADS, DN_HEAD_DIM, DN_HEAD_DIM), F32)),
        scratch_shapes=[pltpu.VMEM((SUBLANES, QKV_WIDTH), F32),
                        pltpu.VMEM((DN_HEADS, DN_HEAD_DIM, DN_HEAD_DIM), F32)],
        compiler_params=_params(2),
        name="deltanet",
    )(qkv, z, ba, cache8, s0, cw, gp, ng)


def _s5_kernel(u_ref, x0re_ref, x0im_ref, bcat_ref, tab_ref, cre_ref, cim_ref, dv_ref, gw_ref, gb_ref, ng_ref,
               o_ref, fre_ref, fim_ref, xr_s, xi_s, car_s, *, tt, ls, carry):
    t = pl.program_id(1)
    u = u_ref[0]
    ub = u.astype(BF16)
    xr_s[...] = _dot(ub, bcat_ref[:, :S5_LANES])
    xi_s[...] = _dot(ub, bcat_ref[:, S5_LANES:])

    if carry:
        @pl.when(t == 0)
        def _():
            car_s[0:1, :] = x0re_ref[0]
            car_s[1:2, :] = x0im_ref[0]

    def seq_body(sq, _):
        if carry:
            c0 = (car_s[0:1, :], car_s[1:2, :])
        else:
            c0 = (x0re_ref[sq], x0im_ref[sq])

        def block_body(rb, cin):
            c_re, c_im = cin
            r0 = pl.multiple_of(sq * ls + rb * SUBLANES, SUBLANES)
            xr = xr_s[pl.ds(r0, SUBLANES), :]
            xi = xi_s[pl.ds(r0, SUBLANES), :]
            for lvl, d in enumerate((1, 2, 4)):
                m_re = tab_ref[16 + 16 * lvl:24 + 16 * lvl, :]
                m_im = tab_ref[24 + 16 * lvl:32 + 16 * lvl, :]
                sr = pltpu.roll(xr, d, axis=0)
                si = pltpu.roll(xi, d, axis=0)
                xr, xi = xr + (m_re * sr - m_im * si), xi + (m_re * si + m_im * sr)
            p_re = tab_ref[0:8, :]
            p_im = tab_ref[8:16, :]
            xr, xi = xr + (p_re * c_re - p_im * c_im), xi + (p_re * c_im + p_im * c_re)
            xr_s[pl.ds(r0, SUBLANES), :] = xr
            xi_s[pl.ds(r0, SUBLANES), :] = xi
            return xr[SUBLANES - 1:, :], xi[SUBLANES - 1:, :]

        c_re, c_im = lax.fori_loop(0, ls // SUBLANES, block_body, c0)
        if carry:
            car_s[0:1, :] = c_re
            car_s[1:2, :] = c_im
            fre_ref[0] = c_re
            fim_ref[0] = c_im
        else:
            fre_ref[sq] = c_re
            fim_ref[sq] = c_im
        return 0

    lax.fori_loop(0, tt // ls, seq_body, 0)

    y = _dot(xr_s[...].astype(BF16), cre_ref[...]) - _dot(xi_s[...].astype(BF16), cim_ref[...])
    y = jax.nn.gelu(y + dv_ref[...] * u)
    gl = _dot(y.astype(BF16), gw_ref[...]) + gb_ref[...]
    o = gl[:, :S5_WIDTH] * jax.nn.sigmoid(gl[:, S5_WIDTH:])
    o_ref[0] = _rms(o, ng_ref[...]).astype(BF16)


def _s5(u, x0re, x0im, bcat, tab, cre_bd, cim_bd, dv, gw, gb, ng, tt, ls, carry):
    b, l, _ = u.shape
    nb = x0re.shape[0] // b
    seq = lambda w_: pl.BlockSpec((1, tt, w_), lambda i, j: (i, j, 0))
    st = pl.BlockSpec((nb, 1, S5_LANES), lambda i, j: (i, 0, 0))
    return pl.pallas_call(
        functools.partial(_s5_kernel, tt=tt, ls=ls, carry=carry),
        grid=(b, l // tt),
        in_specs=[seq(S5_WIDTH), st, st,
                  _const_spec((S5_WIDTH, 2 * S5_LANES)), _const_spec((64, S5_LANES)),
                  _const_spec((S5_LANES, S5_WIDTH)), _const_spec((S5_LANES, S5_WIDTH)),
                  _const_spec((1, S5_WIDTH)), _const_spec((S5_WIDTH, 2 * S5_WIDTH)),
                  _const_spec((1, 2 * S5_WIDTH)), _const_spec((1, S5_WIDTH))],
        out_specs=(seq(S5_WIDTH), st, st),
        out_shape=(jax.ShapeDtypeStruct((b, l, S5_WIDTH), BF16),
                   jax.ShapeDtypeStruct(x0re.shape, F32),
                   jax.ShapeDtypeStruct(x0re.shape, F32)),
        scratch_shapes=[pltpu.VMEM((tt, S5_LANES), F32), pltpu.VMEM((tt, S5_LANES), F32),
                        pltpu.VMEM((SUBLANES, S5_LANES), F32)],
        compiler_params=_params(2),
        name="s5",
    )(u, x0re, x0im, bcat, tab, cre_bd, cim_bd, dv, gw, gb, ng)


def _ffn_kernel(x_ref, odn_ref, os5_ref, prev_ref, woa_ref, wob_ref, n2_ref, wug_ref, wuv_ref,
                fcw_ref, fcb_ref, wd_ref, fg_ref, y_ref, tail_ref, halo_s, *, tm, ls, carry):
    t = pl.program_id(1)
    x1 = x_ref[0] + _dot(odn_ref[0], woa_ref[...]) + _dot(os5_ref[0], wob_ref[...])
    h2 = _rms(x1, n2_ref[...]).astype(BF16)
    gate = _dot(h2, wug_ref[...])
    val = _dot(h2, wuv_ref[...])
    if carry:
        @pl.when(t == 0)
        def _():
            halo_s[...] = prev_ref[0]

        xx = jnp.concatenate([halo_s[...], gate], axis=0)
        g2 = xx[SUBLANES - 2:SUBLANES - 2 + tm]
        g1 = xx[SUBLANES - 1:SUBLANES - 1 + tm]
        halo_s[...] = gate[tm - SUBLANES:, :]
        tail_ref[0] = gate[tm - SUBLANES:, :]
    else:
        r = lax.broadcasted_iota(jnp.int32, (tm, D_FF), 0) & (ls - 1)
        prev = prev_ref[0]
        g1 = jnp.where(r < 1, pltpu.roll(prev, tm - 1, axis=0), pltpu.roll(gate, 1, axis=0))
        g2 = jnp.where(r < 2, prev, pltpu.roll(gate, 2, axis=0))
        tail_ref[0] = gate
    cw = fcw_ref[...]
    conv = g2 * cw[0:1] + g1 * cw[1:2] + gate * cw[2:3]
    pre = conv + fcb_ref[...]
    act = pre * jax.nn.sigmoid(pre) * val
    x2 = x1 + _dot(act.astype(BF16), wd_ref[...])
    y_ref[0] = _rms(x2, fg_ref[...])


def _ffn(x, odn, os5, prev, woa, wob, n2, wug, wuv, fcw, fcb, wd, fg, tm, ls, carry):
    b, l, _ = x.shape
    seq = lambda w_: pl.BlockSpec((1, tm, w_), lambda i, j: (i, j, 0))
    if carry:
        prev_spec = pl.BlockSpec((1, SUBLANES, D_FF), lambda i, j: (i, 0, 0))
        tail_spec = pl.BlockSpec((1, SUBLANES, D_FF), lambda i, j: (i, 0, 0))
        tail_shape = jax.ShapeDtypeStruct((b, SUBLANES, D_FF), F32)
    else:
        prev_spec = seq(D_FF)
        tail_spec = seq(D_FF)
        tail_shape = jax.ShapeDtypeStruct((b, l, D_FF), F32)
    return pl.pallas_call(
        functools.partial(_ffn_kernel, tm=tm, ls=ls, carry=carry),
        grid=(b, l // tm),
        in_specs=[seq(D_MODEL), seq(DN_WIDTH), seq(S5_WIDTH), prev_spec,
                  _const_spec((DN_WIDTH, D_MODEL)), _const_spec((S5_WIDTH, D_MODEL)),
                  _const_spec((1, D_MODEL)), _const_spec((D_MODEL, D_FF)), _const_spec((D_MODEL, D_FF)),
                  _const_spec((SUBLANES, D_FF)), _const_spec((1, D_FF)), _const_spec((D_FF, D_MODEL)),
                  _const_spec((1, D_MODEL))],
        out_specs=(seq(D_MODEL), tail_spec),
        out_shape=(jax.ShapeDtypeStruct((b, l, D_MODEL), F32), tail_shape),
        scratch_shapes=[pltpu.VMEM((SUBLANES, D_FF), F32)],
        compiler_params=_params(2),
        name="ffn",
    )(x, odn, os5, prev, woa, wob, n2, wug, wuv, fcw, fcb, wd, fg)


def _pad_rows_top(a, rows):
    return jnp.pad(a, ((0, 0), (rows - a.shape[1], 0), (0, 0)))


def _block_diag_in(b):
    same = jnp.eye(S5_GROUPS, dtype=bool)[:, None, :, None]
    blocks = jnp.where(same, b.transpose(0, 2, 1)[:, :, None, :], 0.0)
    return blocks.reshape(S5_WIDTH, S5_LANES)


def _block_diag_out(c):
    same = jnp.eye(S5_GROUPS, dtype=bool)[:, None, :, None]
    blocks = jnp.where(same, c.transpose(0, 2, 1)[:, :, None, :], 0.0)
    return blocks.reshape(S5_LANES, S5_WIDTH)


def _trunk(x, conv_dn, s_dn, s5_re, s5_im, conv_ffn, w, prompt):
    b, l, _ = x.shape
    n = b * l
    c = CHUNK if l % CHUNK == 0 else l
    qkv, z, u, ba = _in_proj(x.reshape(n, D_MODEL), w['n1'], w['w_in'], min(n, 512))
    qkv = qkv.reshape(b, l, QKV_WIDTH)
    z = z.reshape(b, l, DN_WIDTH)
    ba = ba.reshape(b, l, LANES)

    tt = min(l, 256)
    o_dn, s_dn_new = _deltanet(qkv, z, ba, _pad_rows_top(conv_dn, SUBLANES), s_dn,
                               w['dn_cw'], w['dn_gp'], w['dn_g'], tt, c)
    assert l >= SUBLANES, "sequence shorter than one row tile"
    conv_dn_new = qkv[:, l - (DN_CONV - 1):]

    s5_args = (w['bcat'], w['tab'], w['cre_bd'], w['cim_bd'], w['dv'], w['glu_w'], w['glu_b'], w['s5_g'])
    x0re = s5_re.reshape(b, 1, S5_LANES)
    x0im = s5_im.reshape(b, 1, S5_LANES)
    if prompt:
        o_s5, fre, fim = _s5(u.reshape(b, l, S5_WIDTH), x0re, x0im, *s5_args, tt=256, ls=256, carry=True)
    else:
        o_s5, fre, fim = _s5(u.reshape(1, n, S5_WIDTH), x0re, x0im, *s5_args, tt=n, ls=l, carry=False)
        o_s5 = o_s5.reshape(b, l, S5_WIDTH)
    s5_re_new = fre.reshape(b, S5_GROUPS, S5_STATE)
    s5_im_new = fim.reshape(b, S5_GROUPS, S5_STATE)

    ffn_args = (w['w_out_a'], w['w_out_b'], w['n2'], w['w_up_g'], w['w_up_v'], w['fcw'], w['fcb'],
                w['w_down'], w['fg'])
    if prompt:
        prev = _pad_rows_top(conv_ffn, SUBLANES)
        y, tail = _ffn(x, o_dn, o_s5, prev, *ffn_args, tm=256, ls=256, carry=True)
        conv_ffn_new = tail[:, SUBLANES - (FFN_CONV - 1):]
    else:
        prev = jnp.pad(conv_ffn, ((0, 0), (0, l - (FFN_CONV - 1)), (0, 0))).reshape(1, n, D_FF)
        y, tail = _ffn(x.reshape(1, n, D_MODEL), o_dn.reshape(1, n, DN_WIDTH), o_s5.reshape(1, n, S5_WIDTH),
                       prev, *ffn_args, tm=min(n, 128), ls=l, carry=False)
        y = y.reshape(b, l, D_MODEL)
        conv_ffn_new = tail.reshape(b, l, D_FF)[:, l - (FFN_CONV - 1):]
    return y, (conv_dn_new[None], s_dn_new[None], s5_re_new[None], s5_im_new[None], conv_ffn_new[None])


def kernel(x_prompt, x_sample, cache_dn_conv, state_dn, state_s5_re, state_s5_im, cache_ffn_conv, norm1_g, w_in, dn_conv_w, dn_A_log, dn_dt_bias, dn_norm_g, s5_A_re, s5_A_im, s5_log_dt, s5_B_re, s5_B_im, s5_C_re, s5_C_im, s5_D, s5_glu_w, s5_glu_b, s5_norm_g, w_out, norm2_g, w_up, ffn_conv_w, ffn_conv_b, w_down, final_norm_g):
    assert w_in.shape[0] == 1, "single-layer trunk"
    o1 = QKV_WIDTH
    o2 = o1 + DN_WIDTH
    o4 = o2 + 2 * DN_HEADS
    wi = w_in[0]
    w_in_r = jnp.concatenate(
        [wi[:, :o2], wi[:, o4:], wi[:, o2:o4], jnp.zeros((D_MODEL, LANES - 2 * DN_HEADS), wi.dtype)],
        axis=1).astype(BF16)
    lane_pad = lambda v: jnp.pad(v, (BA_LANE_G, LANES - BA_LANE_G - DN_HEADS))
    bcat, tab = _s5_prep(s5_A_re[0].reshape(1, S5_LANES), s5_A_im[0].reshape(1, S5_LANES),
                         jnp.repeat(s5_log_dt[0], S5_STATE).reshape(1, S5_LANES),
                         _block_diag_in(s5_B_re[0]), _block_diag_in(s5_B_im[0]))
    w = {
        'n1': norm1_g, 'w_in': w_in_r,
        'dn_cw': jnp.pad(dn_conv_w[0], ((0, SUBLANES - DN_CONV), (0, 0))),
        'dn_gp': jnp.pad(jnp.stack([lane_pad(dn_A_log[0]), lane_pad(dn_dt_bias[0])]), ((0, SUBLANES - 2), (0, 0))),
        'dn_g': dn_norm_g,
        'bcat': bcat, 'tab': tab,
        'cre_bd': _block_diag_out(s5_C_re[0]).astype(BF16), 'cim_bd': _block_diag_out(s5_C_im[0]).astype(BF16),
        'dv': s5_D, 'glu_w': s5_glu_w[0].astype(BF16), 'glu_b': s5_glu_b, 's5_g': s5_norm_g,
        'w_out_a': w_out[0, :DN_WIDTH].astype(BF16), 'w_out_b': w_out[0, DN_WIDTH:].astype(BF16),
        'n2': norm2_g, 'w_up_g': w_up[0, :, :D_FF].astype(BF16), 'w_up_v': w_up[0, :, D_FF:].astype(BF16),
        'fcw': jnp.pad(ffn_conv_w[0], ((0, SUBLANES - FFN_CONV), (0, 0))), 'fcb': ffn_conv_b,
        'w_down': w_down[0].astype(BF16), 'fg': final_norm_g.reshape(1, D_MODEL),
    }
    bp = x_prompt.shape[0]
    zeros = lambda *s: jnp.zeros(s, F32)
    y_p, st_p = _trunk(x_prompt, zeros(bp, DN_CONV - 1, QKV_WIDTH), zeros(bp, DN_HEADS, DN_HEAD_DIM, DN_HEAD_DIM),
                       zeros(bp, S5_GROUPS, S5_STATE), zeros(bp, S5_GROUPS, S5_STATE),
                       zeros(bp, FFN_CONV - 1, D_FF), w, prompt=True)
    y_s, st_s = _trunk(x_sample, cache_dn_conv[0], state_dn[0], state_s5_re[0], state_s5_im[0],
                       cache_ffn_conv[0], w, prompt=False)
    return (y_p, y_s) + st_p + st_s
```

```python
import functools

import jax
import jax.numpy as jnp
from jax import lax
from jax.experimental import pallas as pl
from jax.experimental.pallas import tpu as pltpu

F32 = jnp.float32
BF16 = jnp.bfloat16
EPS = 1e-6

D_MODEL = 1024
DN_HEADS = 4
DN_HEAD_DIM = 128
DN_WIDTH = DN_HEADS * DN_HEAD_DIM
DN_CONV = 4
QKV_WIDTH = 3 * DN_WIDTH
S5_WIDTH = D_MODEL - DN_WIDTH
S5_GROUP = 16
S5_GROUPS = S5_WIDTH // S5_GROUP
S5_STATE = 64
S5_LANES = S5_GROUPS * S5_STATE
D_FF = 2816
FFN_CONV = 3
CHUNK = 64

SUBLANES = 8
LANES = 128
BA_LANE_G = DN_HEADS
IN_PAD = QKV_WIDTH + DN_WIDTH + S5_WIDTH + LANES

VMEM_LIMIT = 56 * 1024 * 1024


def _dot(a, b):
    return jnp.dot(a, b, preferred_element_type=F32)


def _dot_nt(a, b):
    return lax.dot_general(a, b, (((1,), (1,)), ((), ())), preferred_element_type=F32)


def _dot_tn(a, b):
    return lax.dot_general(a, b, (((0,), (0,)), ((), ())), preferred_element_type=F32)


def _split_bf16(a):
    hi = a.astype(BF16)
    lo = (a - hi.astype(F32)).astype(BF16)
    return hi, lo


def _dot3(a, b):
    ah, al = _split_bf16(a)
    bh, bl = _split_bf16(b)
    return _dot(ah, bh) + (_dot(al, bh) + _dot(ah, bl))


def _rms(x, g):
    return x * lax.rsqrt(jnp.mean(x * x, axis=-1, keepdims=True) + EPS) * g


def _const_spec(shape):
    nd = len(shape)
    return pl.BlockSpec(shape, lambda *_: (0,) * nd, pipeline_mode=pl.Buffered(1))


def _params(n_axes):
    return pltpu.CompilerParams(dimension_semantics=("arbitrary",) * n_axes,
                                vmem_limit_bytes=VMEM_LIMIT)


def _s5prep_kernel(are_ref, aim_ref, ldt_ref, bre_ref, bim_ref, bcat_ref, tab_ref):
    are = are_ref[...]
    aim = aim_ref[...]
    dt = jnp.exp(ldt_ref[...])
    mag = jnp.exp(are * dt)
    ang = aim * dt
    lr = mag * jnp.cos(ang)
    li = mag * jnp.sin(ang)
    den = are * are + aim * aim
    f_re = ((lr - 1.0) * are + li * aim) / den
    f_im = (li * are - (lr - 1.0) * aim) / den
    bre = bre_ref[...]
    bim = bim_ref[...]
    bcat_ref[:, :S5_LANES] = (f_re * bre - f_im * bim).astype(BF16)
    bcat_ref[:, S5_LANES:] = (f_re * bim + f_im * bre).astype(BF16)

    pw = [(lr, li)]
    for _ in range(SUBLANES - 1):
        pr, pi = pw[-1]
        pw.append((pr * lr - pi * li, pr * li + pi * lr))
    row = lax.broadcasted_iota(jnp.int32, (SUBLANES, S5_LANES), 0)
    zero = jnp.zeros((SUBLANES, S5_LANES), F32)
    pre, pim = zero, zero
    for r in range(SUBLANES):
        pre = jnp.where(row == r, pw[r][0], pre)
        pim = jnp.where(row == r, pw[r][1], pim)
    tab_ref[0:8, :] = pre
    tab_ref[8:16, :] = pim
    for lvl, d in enumerate((1, 2, 4)):
        tab_ref[16 + 16 * lvl:24 + 16 * lvl, :] = jnp.where(row >= d, pw[d - 1][0], zero)
        tab_ref[24 + 16 * lvl:32 + 16 * lvl, :] = jnp.where(row >= d, pw[d - 1][1], zero)


def _s5_prep(are, aim, ldt, bre_bd, bim_bd):
    return pl.pallas_call(
        _s5prep_kernel,
        out_shape=(jax.ShapeDtypeStruct((S5_WIDTH, 2 * S5_LANES), BF16),
                   jax.ShapeDtypeStruct((64, S5_LANES), F32)),
        compiler_params=pltpu.CompilerParams(vmem_limit_bytes=VMEM_LIMIT),
        name="s5_prep",
    )(are, aim, ldt, bre_bd, bim_bd)


def _inproj_kernel(x_ref, g_ref, w_ref, qkv_ref, z_ref, u_ref, ba_ref):
    h = _rms(x_ref[...], g_ref[...]).astype(BF16)
    o1 = QKV_WIDTH
    o2 = o1 + DN_WIDTH
    o3 = o2 + S5_WIDTH
    qkv_ref[...] = _dot(h, w_ref[:, :o1])
    z_ref[...] = _dot(h, w_ref[:, o1:o2])
    u_ref[...] = _dot(h, w_ref[:, o2:o3])
    ba_ref[...] = _dot(h, w_ref[:, o3:])


def _in_proj(x2d, g, w, tm):
    n = x2d.shape[0]
    row = lambda w_: pl.BlockSpec((tm, w_), lambda i: (i, 0))
    return pl.pallas_call(
        _inproj_kernel,
        grid=(n // tm,),
        in_specs=[row(D_MODEL), _const_spec((1, D_MODEL)), _const_spec((D_MODEL, IN_PAD))],
        out_specs=(row(QKV_WIDTH), row(DN_WIDTH), row(S5_WIDTH), row(LANES)),
        out_shape=(jax.ShapeDtypeStruct((n, QKV_WIDTH), F32),
                   jax.ShapeDtypeStruct((n, DN_WIDTH), F32),
                   jax.ShapeDtypeStruct((n, S5_WIDTH), F32),
                   jax.ShapeDtypeStruct((n, LANES), F32)),
        compiler_params=_params(1),
        name="in_proj",
    )(x2d, g, w)


def _bmm(a, b):
    return lax.dot_general(a, b, (((2,), (1,)), ((0,), (0,))), preferred_element_type=F32)


def _bmm_nt(a, b):
    return lax.dot_general(a, b, (((2,), (2,)), ((0,), (0,))), preferred_element_type=F32)


def _dn_kernel(qkv_ref, z_ref, ba_ref, cache_ref, s0_ref, cw_ref, gp_ref, ng_ref,
               o_ref, state_ref, halo_s, *, nb, tt, c):
    nh = DN_HEADS
    dh = DN_HEAD_DIM
    nck = tt // c
    npb = nb * nck
    hc = nh * c
    t = pl.program_id(1)

    @pl.when(t == 0)
    def _():
        halo_s[...] = cache_ref[...]
        state_ref[...] = s0_ref[...]

    x = qkv_ref[...]
    xx = jnp.concatenate([halo_s[...], x], axis=1)
    halo_s[...] = x[:, tt - SUBLANES:, :]
    cw = cw_ref[...]
    off = SUBLANES - (DN_CONV - 1)
    acc = xx[:, off:off + tt] * cw[0:1]
    for j in range(1, DN_CONV):
        acc = acc + xx[:, off + j:off + j + tt] * cw[j:j + 1]
    act = (acc * jax.nn.sigmoid(acc)).reshape(npb, c, QKV_WIDTH)

    def stack(a, base):
        return jnp.concatenate([a[:, :, base + h * dh:base + (h + 1) * dh] for h in range(nh)], axis=1)

    def unstack(a):
        return jnp.concatenate([a[:, h * c:(h + 1) * c, :] for h in range(nh)], axis=2)

    q = stack(act, 0)
    k = stack(act, DN_WIDTH)
    v = stack(act, 2 * DN_WIDTH)
    q = q * (lax.rsqrt(jnp.sum(q * q, axis=-1, keepdims=True) + EPS) * (dh ** -0.5))
    k = k * lax.rsqrt(jnp.sum(k * k, axis=-1, keepdims=True) + EPS)

    ba = ba_ref[...].reshape(nb * tt, LANES)
    beta = jax.nn.sigmoid(ba).reshape(npb, c, LANES)
    g2 = -jnp.exp(gp_ref[0:1, :]) * jax.nn.softplus(ba + gp_ref[1:2, :])
    row_in_chunk = lax.broadcasted_iota(jnp.int32, (nb * tt, LANES), 0) & (c - 1)
    d = 1
    while d < c:
        g2 = g2 + jnp.where(row_in_chunk >= d, pltpu.roll(g2, d, axis=0), 0.0)
        d *= 2
    g = g2.reshape(npb, c, LANES)

    g_cols = [g[:, :, BA_LANE_G + h:BA_LANE_G + h + 1] for h in range(nh)]
    b_cols = [beta[:, :, h:h + 1] for h in range(nh)]
    g_col = jnp.concatenate(g_cols, axis=1)
    b_col = jnp.concatenate(b_cols, axis=1)
    g_last = jnp.concatenate([jnp.broadcast_to(gc[:, c - 1:c, :], (npb, c, 1)) for gc in g_cols], axis=1)

    hp = min(nh, LANES // c)
    pieces = []
    for h0 in range(0, nh, hp):
        slab = jnp.concatenate(
            [(g2 if hh == 0 else pltpu.roll(g2, LANES - hh, axis=1)).reshape(npb, c, LANES)
             for hh in range(h0, h0 + hp)], axis=1)
        if hp * c < LANES:
            slab = jnp.concatenate([slab, jnp.zeros((npb, LANES - hp * c, LANES), F32)], axis=1)
        rows = [slab[p].T[BA_LANE_G:BA_LANE_G + 1, :hp * c] for p in range(npb)]
        pieces.append(jnp.stack(rows, axis=0))
    g_row = jnp.concatenate(pieces, axis=2)

    ri = lax.broadcasted_iota(jnp.int32, (c, hc), 0)
    lane = lax.broadcasted_iota(jnp.int32, (c, hc), 1)
    cj = lane & (c - 1)
    causal = (ri >= cj)[None]
    strict = (ri > cj)[None]

    def cat_from_cols(cols):
        out = jnp.broadcast_to(cols[nh - 1], (npb, c, hc))
        for h in range(nh - 2, -1, -1):
            out = jnp.where((lane < (h + 1) * c)[None], jnp.broadcast_to(cols[h], (npb, c, hc)), out)
        return out

    decay = jnp.exp(jnp.where(causal, cat_from_cols(g_cols) - g_row, -jnp.inf))

    kb = k.astype(BF16)
    br = lax.broadcasted_iota(jnp.int32, (hc, nh * dh), 0)
    bl = lax.broadcasted_iota(jnp.int32, (hc, nh * dh), 1)
    head_of_row = sum(jnp.where(br >= h * c, 1, 0) for h in range(1, nh))
    head_of_lane = sum(jnp.where(bl >= h * dh, 1, 0) for h in range(1, nh))
    k_bd = jnp.where((head_of_row == head_of_lane)[None], jnp.concatenate([kb] * nh, axis=2), 0.0)
    qk_lhs = jnp.concatenate([unstack(q), unstack(k)], axis=1).astype(BF16)
    qkk = _bmm_nt(qk_lhs, k_bd)
    qk = qkk[:, :c]
    kk = qkk[:, c:]

    sr = lax.broadcasted_iota(jnp.int32, (hc, hc), 0)
    sl = lax.broadcasted_iota(jnp.int32, (hc, hc), 1)
    shift = c.bit_length() - 1
    same_head = ((sr >> shift) == (sl >> shift))[None]

    def bd(m):
        return jnp.where(same_head, jnp.concatenate([m] * nh, axis=1), 0.0)

    lm = jnp.where(strict, cat_from_cols(b_cols) * kk * decay, 0.0)
    tinv = jnp.where((ri == cj)[None], 1.0, 0.0) - jnp.where(((ri >> 1) == (cj >> 1))[None], lm, 0.0)
    s = 2
    while s < c:
        sh = s.bit_length()
        lower_left = ((ri >> sh) == (cj >> sh)) & ((ri & s) != 0) & ((cj & s) == 0)
        a_off = jnp.where(lower_left[None], lm, 0.0).astype(BF16)
        xm = _bmm(tinv.astype(BF16), bd(a_off))
        tinv = tinv - _bmm(xm.astype(BF16), bd(tinv.astype(BF16)))
        s *= 2

    e_g = jnp.exp(g_col)
    rhs = jnp.concatenate([v * b_col, k * (b_col * e_g)], axis=2).astype(BF16)
    sol = _bmm(bd(tinv.astype(BF16)), rhs)
    sol_v = sol[:, :, :dh]
    sol_k = sol[:, :, dh:]
    w_qe = jnp.concatenate([sol_k, q * e_g], axis=2).astype(BF16)
    kd = (k * jnp.exp(g_last - g_col)).astype(BF16)
    qkd_bd = bd((qk * decay).astype(BF16))
    s_decay = jnp.exp(g_last)

    outs = []
    for b in range(nb):
        for ck in range(nck):
            p = b * nck + ck
            ws, qs = [], []
            states = [state_ref[b, h] for h in range(nh)]
            for h in range(nh):
                sb = states[h].astype(BF16)
                rows = w_qe[p, h * c:(h + 1) * c, :]
                ws.append(_dot(rows[:, :dh], sb))
                qs.append(_dot(rows[:, dh:], sb))
            u = sol_v[p] - jnp.concatenate(ws, axis=0)
            ub = u.astype(BF16)
            outs.append(jnp.concatenate(qs, axis=0) + _dot(qkd_bd[p], ub))
            for h in range(nh):
                r0 = h * c
                state_ref[b, h] = (states[h] * s_decay[p, r0:r0 + 1, :]
                                   + _dot_tn(kd[p, r0:r0 + c, :], ub[r0:r0 + c, :]))

    o = jnp.stack(outs, axis=0)
    zs = stack(z_ref[...].reshape(npb, c, DN_WIDTH), 0)
    o = _rms(o, ng_ref[...]) * (zs * jax.nn.sigmoid(zs))
    o_ref[...] = unstack(o).reshape(nb, tt, DN_WIDTH).astype(BF16)


def _deltanet(qkv, z, ba, cache8, s0, cw, gp, ng, nb, tt, c):
    b, l, _ = qkv.shape
    seq = lambda w_: pl.BlockSpec((nb, tt, w_), lambda i, j: (i, j, 0))
    per_b = lambda *s: pl.BlockSpec((nb,) + s, lambda i, j: (i,) + (0,) * len(s))
    return pl.pallas_call(
        functools.partial(_dn_kernel, nb=nb, tt=tt, c=c),
        grid=(b // nb, l // tt),
        in_specs=[seq(QKV_WIDTH), seq(DN_WIDTH), seq(LANES), per_b(SUBLANES, QKV_WIDTH),
                  per_b(DN_HEADS, DN_HEAD_DIM, DN_HEAD_DIM),
                  _const_spec((SUBLANES, QKV_WIDTH)), _const_spec((SUBLANES, LANES)),
                  _const_spec((1, DN_HEAD_DIM))],
        out_specs=(seq(DN_WIDTH), per_b(DN_HEADS, DN_HEAD_DIM, DN_HEAD_DIM)),
        out_shape=(jax.ShapeDtypeStruct((b, l, DN_WIDTH), BF16),
                   jax.ShapeDtypeStruct((b, DN_HEADS, DN_HEAD_DIM, DN_HEAD_DIM), F32)),
        scratch_shapes=[pltpu.VMEM((nb, SUBLANES, QKV_WIDTH), F32)],
        compiler_params=_params(2),
        name="deltanet",
    )(qkv, z, ba, cache8, s0, cw, gp, ng)


def _s5_kernel(u_ref, x0re_ref, x0im_ref, bcat_ref, tab_ref, cre_ref, cim_ref, dv_ref, gw_ref, gb_ref, ng_ref,
               o_ref, fre_ref, fim_ref, xr_s, xi_s, car_s, *, tt, ls, carry):
    t = pl.program_id(1)
    u = u_ref[0]
    ub = u.astype(BF16)
    xr_s[...] = _dot(ub, bcat_ref[:, :S5_LANES])
    xi_s[...] = _dot(ub, bcat_ref[:, S5_LANES:])

    if carry:
        @pl.when(t == 0)
        def _():
            car_s[0:1, :] = x0re_ref[0]
            car_s[1:2, :] = x0im_ref[0]

    def seq_body(sq, _):
        if carry:
            c0 = (car_s[0:1, :], car_s[1:2, :])
        else:
            c0 = (x0re_ref[sq], x0im_ref[sq])

        def block_body(rb, cin):
            c_re, c_im = cin
            r0 = pl.multiple_of(sq * ls + rb * SUBLANES, SUBLANES)
            xr = xr_s[pl.ds(r0, SUBLANES), :]
            xi = xi_s[pl.ds(r0, SUBLANES), :]
            for lvl, d in enumerate((1, 2, 4)):
                m_re = tab_ref[16 + 16 * lvl:24 + 16 * lvl, :]
                m_im = tab_ref[24 + 16 * lvl:32 + 16 * lvl, :]
                sr = pltpu.roll(xr, d, axis=0)
                si = pltpu.roll(xi, d, axis=0)
                xr, xi = xr + (m_re * sr - m_im * si), xi + (m_re * si + m_im * sr)
            p_re = tab_ref[0:8, :]
            p_im = tab_ref[8:16, :]
            xr, xi = xr + (p_re * c_re - p_im * c_im), xi + (p_re * c_im + p_im * c_re)
            xr_s[pl.ds(r0, SUBLANES), :] = xr
            xi_s[pl.ds(r0, SUBLANES), :] = xi
            return xr[SUBLANES - 1:, :], xi[SUBLANES - 1:, :]

        c_re, c_im = lax.fori_loop(0, ls // SUBLANES, block_body, c0)
        if carry:
            car_s[0:1, :] = c_re
            car_s[1:2, :] = c_im
            fre_ref[0] = c_re
            fim_ref[0] = c_im
        else:
            fre_ref[sq] = c_re
            fim_ref[sq] = c_im
        return 0

    lax.fori_loop(0, tt // ls, seq_body, 0)

    y = _dot(xr_s[...].astype(BF16), cre_ref[...]) - _dot(xi_s[...].astype(BF16), cim_ref[...])
    y = jax.nn.gelu(y + dv_ref[...] * u)
    gl = _dot(y.astype(BF16), gw_ref[...]) + gb_ref[...]
    o = gl[:, :S5_WIDTH] * jax.nn.sigmoid(gl[:, S5_WIDTH:])
    o_ref[0] = _rms(o, ng_ref[...]).astype(BF16)


def _s5(u, x0re, x0im, bcat, tab, cre_bd, cim_bd, dv, gw, gb, ng, tt, ls, carry):
    b, l, _ = u.shape
    nb = x0re.shape[0] // b
    seq = lambda w_: pl.BlockSpec((1, tt, w_), lambda i, j: (i, j, 0))
    st = pl.BlockSpec((nb, 1, S5_LANES), lambda i, j: (i, 0, 0))
    return pl.pallas_call(
        functools.partial(_s5_kernel, tt=tt, ls=ls, carry=carry),
        grid=(b, l // tt),
        in_specs=[seq(S5_WIDTH), st, st,
                  _const_spec((S5_WIDTH, 2 * S5_LANES)), _const_spec((64, S5_LANES)),
                  _const_spec((S5_LANES, S5_WIDTH)), _const_spec((S5_LANES, S5_WIDTH)),
                  _const_spec((1, S5_WIDTH)), _const_spec((S5_WIDTH, 2 * S5_WIDTH)),
                  _const_spec((1, 2 * S5_WIDTH)), _const_spec((1, S5_WIDTH))],
        out_specs=(seq(S5_WIDTH), st, st),
        out_shape=(jax.ShapeDtypeStruct((b, l, S5_WIDTH), BF16),
                   jax.ShapeDtypeStruct(x0re.shape, F32),
                   jax.ShapeDtypeStruct(x0re.shape, F32)),
        scratch_shapes=[pltpu.VMEM((tt, S5_LANES), F32), pltpu.VMEM((tt, S5_LANES), F32),
                        pltpu.VMEM((SUBLANES, S5_LANES), F32)],
        compiler_params=_params(2),
        name="s5",
    )(u, x0re, x0im, bcat, tab, cre_bd, cim_bd, dv, gw, gb, ng)


def _ffn_kernel(x_ref, odn_ref, os5_ref, prev_ref, woa_ref, wob_ref, n2_ref, wug_ref, wuv_ref,
                fcw_ref, fcb_ref, wd_ref, fg_ref, y_ref, tail_ref, halo_s, *, tm, ls, carry):
    t = pl.program_id(1)
    x1 = x_ref[0] + _dot(odn_ref[0], woa_ref[...]) + _dot(os5_ref[0], wob_ref[...])
    h2 = _rms(x1, n2_ref[...]).astype(BF16)
    gate = _dot(h2, wug_ref[...])
    val = _dot(h2, wuv_ref[...])
    if carry:
        @pl.when(t == 0)
        def _():
            halo_s[...] = prev_ref[0]

        xx = jnp.concatenate([halo_s[...], gate], axis=0)
        g2 = xx[SUBLANES - 2:SUBLANES - 2 + tm]
        g1 = xx[SUBLANES - 1:SUBLANES - 1 + tm]
        halo_s[...] = gate[tm - SUBLANES:, :]
        tail_ref[0] = gate[tm - SUBLANES:, :]
    else:
        r = lax.broadcasted_iota(jnp.int32, (tm, D_FF), 0) & (ls - 1)
        prev = prev_ref[0]
        g1 = jnp.where(r < 1, pltpu.roll(prev, tm - 1, axis=0), pltpu.roll(gate, 1, axis=0))
        g2 = jnp.where(r < 2, prev, pltpu.roll(gate, 2, axis=0))
        tail_ref[0] = gate
    cw = fcw_ref[...]
    conv = g2 * cw[0:1] + g1 * cw[1:2] + gate * cw[2:3]
    pre = conv + fcb_ref[...]
    act = pre * jax.nn.sigmoid(pre) * val
    x2 = x1 + _dot(act.astype(BF16), wd_ref[...])
    y_ref[0] = _rms(x2, fg_ref[...])


def _ffn(x, odn, os5, prev, woa, wob, n2, wug, wuv, fcw, fcb, wd, fg, tm, ls, carry):
    b, l, _ = x.shape
    seq = lambda w_: pl.BlockSpec((1, tm, w_), lambda i, j: (i, j, 0))
    if carry:
        prev_spec = pl.BlockSpec((1, SUBLANES, D_FF), lambda i, j: (i, 0, 0))
        tail_spec = pl.BlockSpec((1, SUBLANES, D_FF), lambda i, j: (i, 0, 0))
        tail_shape = jax.ShapeDtypeStruct((b, SUBLANES, D_FF), F32)
    else:
        prev_spec = seq(D_FF)
        tail_spec = seq(D_FF)
        tail_shape = jax.ShapeDtypeStruct((b, l, D_FF), F32)
    return pl.pallas_call(
        functools.partial(_ffn_kernel, tm=tm, ls=ls, carry=carry),
        grid=(b, l // tm),
        in_specs=[seq(D_MODEL), seq(DN_WIDTH), seq(S5_WIDTH), prev_spec,
                  _const_spec((DN_WIDTH, D_MODEL)), _const_spec((S5_WIDTH, D_MODEL)),
                  _const_spec((1, D_MODEL)), _const_spec((D_MODEL, D_FF)), _const_spec((D_MODEL, D_FF)),
                  _const_spec((SUBLANES, D_FF)), _const_spec((1, D_FF)), _const_spec((D_FF, D_MODEL)),
                  _const_spec((1, D_MODEL))],
        out_specs=(seq(D_MODEL), tail_spec),
        out_shape=(jax.ShapeDtypeStruct((b, l, D_MODEL), F32), tail_shape),
        scratch_shapes=[pltpu.VMEM((SUBLANES, D_FF), F32)],
        compiler_params=_params(2),
        name="ffn",
    )(x, odn, os5, prev, woa, wob, n2, wug, wuv, fcw, fcb, wd, fg)


def _pad_rows_top(a, rows):
    return jnp.pad(a, ((0, 0), (rows - a.shape[1], 0), (0, 0)))


def _block_diag_in(b):
    same = jnp.eye(S5_GROUPS, dtype=bool)[:, None, :, None]
    blocks = jnp.where(same, b.transpose(0, 2, 1)[:, :, None, :], 0.0)
    return blocks.reshape(S5_WIDTH, S5_LANES)


def _block_diag_out(c):
    same = jnp.eye(S5_GROUPS, dtype=bool)[:, None, :, None]
    blocks = jnp.where(same, c.transpose(0, 2, 1)[:, :, None, :], 0.0)
    return blocks.reshape(S5_LANES, S5_WIDTH)


def _trunk(x, conv_dn, s_dn, s5_re, s5_im, conv_ffn, w, prompt):
    b, l, _ = x.shape
    n = b * l
    c = CHUNK if l % CHUNK == 0 else l
    qkv, z, u, ba = _in_proj(x.reshape(n, D_MODEL), w['n1'], w['w_in'], min(n, 512))
    qkv = qkv.reshape(b, l, QKV_WIDTH)
    z = z.reshape(b, l, DN_WIDTH)
    ba = ba.reshape(b, l, LANES)

    o_dn, s_dn_new = _deltanet(qkv, z, ba, _pad_rows_top(conv_dn, SUBLANES), s_dn,
                               w['dn_cw'], w['dn_gp'], w['dn_g'], nb=min(b, 4 if prompt else 8),
                               tt=min(l, 128), c=c)
    assert l >= SUBLANES, "sequence shorter than one row tile"
    conv_dn_new = qkv[:, l - (DN_CONV - 1):]

    s5_args = (w['bcat'], w['tab'], w['cre_bd'], w['cim_bd'], w['dv'], w['glu_w'], w['glu_b'], w['s5_g'])
    x0re = s5_re.reshape(b, 1, S5_LANES)
    x0im = s5_im.reshape(b, 1, S5_LANES)
    if prompt:
        o_s5, fre, fim = _s5(u.reshape(b, l, S5_WIDTH), x0re, x0im, *s5_args, tt=256, ls=256, carry=True)
    else:
        o_s5, fre, fim = _s5(u.reshape(1, n, S5_WIDTH), x0re, x0im, *s5_args, tt=n, ls=l, carry=False)
        o_s5 = o_s5.reshape(b, l, S5_WIDTH)
    s5_re_new = fre.reshape(b, S5_GROUPS, S5_STATE)
    s5_im_new = fim.reshape(b, S5_GROUPS, S5_STATE)

    ffn_args = (w['w_out_a'], w['w_out_b'], w['n2'], w['w_up_g'], w['w_up_v'], w['fcw'], w['fcb'],
                w['w_down'], w['fg'])
    if prompt:
        prev = _pad_rows_top(conv_ffn, SUBLANES)
        y, tail = _ffn(x, o_dn, o_s5, prev, *ffn_args, tm=256, ls=256, carry=True)
        conv_ffn_new = tail[:, SUBLANES - (FFN_CONV - 1):]
    else:
        prev = jnp.pad(conv_ffn, ((0, 0), (0, l - (FFN_CONV - 1)), (0, 0))).reshape(1, n, D_FF)
        y, tail = _ffn(x.reshape(1, n, D_MODEL), o_dn.reshape(1, n, DN_WIDTH), o_s5.reshape(1, n, S5_WIDTH),
                       prev, *ffn_args, tm=min(n, 128), ls=l, carry=False)
        y = y.reshape(b, l, D_MODEL)
        conv_ffn_new = tail.reshape(b, l, D_FF)[:, l - (FFN_CONV - 1):]
    return y, (conv_dn_new[None], s_dn_new[None], s5_re_new[None], s5_im_new[None], conv_ffn_new[None])


def kernel(x_prompt, x_sample, cache_dn_conv, state_dn, state_s5_re, state_s5_im, cache_ffn_conv, norm1_g, w_in, dn_conv_w, dn_A_log, dn_dt_bias, dn_norm_g, s5_A_re, s5_A_im, s5_log_dt, s5_B_re, s5_B_im, s5_C_re, s5_C_im, s5_D, s5_glu_w, s5_glu_b, s5_norm_g, w_out, norm2_g, w_up, ffn_conv_w, ffn_conv_b, w_down, final_norm_g):
    assert w_in.shape[0] == 1, "single-layer trunk"
    o1 = QKV_WIDTH
    o2 = o1 + DN_WIDTH
    o4 = o2 + 2 * DN_HEADS
    wi = w_in[0]
    w_in_r = jnp.concatenate(
        [wi[:, :o2], wi[:, o4:], wi[:, o2:o4], jnp.zeros((D_MODEL, LANES - 2 * DN_HEADS), wi.dtype)],
        axis=1).astype(BF16)
    lane_pad = lambda v: jnp.pad(v, (BA_LANE_G, LANES - BA_LANE_G - DN_HEADS))
    bcat, tab = _s5_prep(s5_A_re[0].reshape(1, S5_LANES), s5_A_im[0].reshape(1, S5_LANES),
                         jnp.repeat(s5_log_dt[0], S5_STATE).reshape(1, S5_LANES),
                         _block_diag_in(s5_B_re[0]), _block_diag_in(s5_B_im[0]))
    w = {
        'n1': norm1_g, 'w_in': w_in_r,
        'dn_cw': jnp.pad(dn_conv_w[0], ((0, SUBLANES - DN_CONV), (0, 0))),
        'dn_gp': jnp.pad(jnp.stack([lane_pad(dn_A_log[0]), lane_pad(dn_dt_bias[0])]), ((0, SUBLANES - 2), (0, 0))),
        'dn_g': dn_norm_g,
        'bcat': bcat, 'tab': tab,
        'cre_bd': _block_diag_out(s5_C_re[0]).astype(BF16), 'cim_bd': _block_diag_out(s5_C_im[0]).astype(BF16),
        'dv': s5_D, 'glu_w': s5_glu_w[0].astype(BF16), 'glu_b': s5_glu_b, 's5_g': s5_norm_g,
        'w_out_a': w_out[0, :DN_WIDTH].astype(BF16), 'w_out_b': w_out[0, DN_WIDTH:].astype(BF16),
        'n2': norm2_g, 'w_up_g': w_up[0, :, :D_FF].astype(BF16), 'w_up_v': w_up[0, :, D_FF:].astype(BF16),
        'fcw': jnp.pad(ffn_conv_w[0], ((0, SUBLANES - FFN_CONV), (0, 0))), 'fcb': ffn_conv_b,
        'w_down': w_down[0].astype(BF16), 'fg': final_norm_g.reshape(1, D_MODEL),
    }
    bp = x_prompt.shape[0]
    zeros = lambda *s: jnp.zeros(s, F32)
    y_p, st_p = _trunk(x_prompt, zeros(bp, DN_CONV - 1, QKV_WIDTH), zeros(bp, DN_HEADS, DN_HEAD_DIM, DN_HEAD_DIM),
                       zeros(bp, S5_GROUPS, S5_STATE), zeros(bp, S5_GROUPS, S5_STATE),
                       zeros(bp, FFN_CONV - 1, D_FF), w, prompt=True)
    y_s, st_s = _trunk(x_sample, cache_dn_conv[0], state_dn[0], state_s5_re[0], state_s5_im[0],
                       cache_ffn_conv[0], w, prompt=False)
    return (y_p, y_s) + st_p + st_s
```

```python
import functools

import jax
import jax.numpy as jnp
from jax import lax
from jax.experimental import pallas as pl
from jax.experimental.pallas import tpu as pltpu

F32 = jnp.float32
BF16 = jnp.bfloat16
EPS = 1e-6

D_MODEL = 1024
DN_HEADS = 4
DN_HEAD_DIM = 128
DN_WIDTH = DN_HEADS * DN_HEAD_DIM
DN_CONV = 4
QKV_WIDTH = 3 * DN_WIDTH
S5_WIDTH = D_MODEL - DN_WIDTH
S5_GROUP = 16
S5_GROUPS = S5_WIDTH // S5_GROUP
S5_STATE = 64
S5_LANES = S5_GROUPS * S5_STATE
S5_SLOTS = 4
S5_QUADS = S5_GROUPS // 4
QUAD_LANES = 4 * S5_STATE
D_FF = 2816
FFN_CONV = 3
CHUNK = 64

SUBLANES = 8
LANES = 128
BA_LANE_G = DN_HEADS
IN_PAD = QKV_WIDTH + DN_WIDTH + S5_WIDTH + LANES
MXU_DIM = 256
FF_CHUNK = 3 * MXU_DIM

VMEM_LIMIT = 56 * 1024 * 1024


def _dot(a, b):
    return jnp.dot(a, b, preferred_element_type=F32)


def _dot_nt(a, b):
    return lax.dot_general(a, b, (((1,), (1,)), ((), ())), preferred_element_type=F32)


def _dot_tn(a, b):
    return lax.dot_general(a, b, (((0,), (0,)), ((), ())), preferred_element_type=F32)


def _split_bf16(a):
    hi = a.astype(BF16)
    lo = (a - hi.astype(F32)).astype(BF16)
    return hi, lo


def _bmm(a, b):
    return lax.dot_general(a, b, (((2,), (1,)), ((0,), (0,))), preferred_element_type=F32)


def _bmm_nt(a, b):
    return lax.dot_general(a, b, (((2,), (2,)), ((0,), (0,))), preferred_element_type=F32)


def _bmm_nt3(a, b):
    ah, al = _split_bf16(a)
    bh, bl = _split_bf16(b)
    return _bmm_nt(ah, bh) + (_bmm_nt(al, bh) + _bmm_nt(ah, bl))


def _rms(x, g):
    return x * lax.rsqrt(jnp.mean(x * x, axis=-1, keepdims=True) + EPS) * g


def _const_spec(shape):
    nd = len(shape)
    return pl.BlockSpec(shape, lambda *_: (0,) * nd, pipeline_mode=pl.Buffered(1))


def _params(n_axes):
    return pltpu.CompilerParams(dimension_semantics=("arbitrary",) * n_axes,
                                vmem_limit_bytes=VMEM_LIMIT)


def _cmul(a, b):
    return a[0] * b[0] - a[1] * b[1], a[0] * b[1] + a[1] * b[0]


def _zoh(are, aim, ldt):
    dt = jnp.exp(ldt)
    mag = jnp.exp(are * dt)
    ang = aim * dt
    lr = mag * jnp.cos(ang)
    li = mag * jnp.sin(ang)
    den = are * are + aim * aim
    f_re = ((lr - 1.0) * are + li * aim) / den
    f_im = (li * are - (lr - 1.0) * aim) / den
    return (lr, li), (f_re, f_im)


def _write_scan_tables(tab_ref, step):
    pw = [step]
    for _ in range(SUBLANES - 1):
        pw.append(_cmul(pw[-1], step))
    shape = (SUBLANES, step[0].shape[-1])
    row = lax.broadcasted_iota(jnp.int32, shape, 0)
    zero = jnp.zeros(shape, F32)
    pre, pim = zero, zero
    for r in range(SUBLANES):
        pre = jnp.where(row == r, pw[r][0], pre)
        pim = jnp.where(row == r, pw[r][1], pim)
    tab_ref[0:8, :] = pre
    tab_ref[8:16, :] = pim
    for lvl, d in enumerate((1, 2, 4)):
        tab_ref[16 + 16 * lvl:24 + 16 * lvl, :] = jnp.where(row >= d, pw[d - 1][0], zero)
        tab_ref[24 + 16 * lvl:32 + 16 * lvl, :] = jnp.where(row >= d, pw[d - 1][1], zero)


def _s5prep_kernel(are_ref, aim_ref, ldt_ref, bre_ref, bim_ref, bcat_ref, tab_ref, tabc_ref):
    lam, (f_re, f_im) = _zoh(are_ref[...], aim_ref[...], ldt_ref[...])
    bre = bre_ref[...]
    bim = bim_ref[...]
    bcat_ref[:, :S5_LANES] = (f_re * bre - f_im * bim).astype(BF16)
    bcat_ref[:, S5_LANES:] = (f_re * bim + f_im * bre).astype(BF16)
    _write_scan_tables(tab_ref, lam)
    lam_c = lam
    for _ in range(S5_SLOTS - 1):
        lam_c = _cmul(lam_c, lam)
    _write_scan_tables(tabc_ref, lam_c)


def _s5_prep(are, aim, ldt, bre_bd, bim_bd):
    return pl.pallas_call(
        _s5prep_kernel,
        out_shape=(jax.ShapeDtypeStruct((S5_WIDTH, 2 * S5_LANES), BF16),
                   jax.ShapeDtypeStruct((64, S5_LANES), F32),
                   jax.ShapeDtypeStruct((64, S5_LANES), F32)),
        compiler_params=pltpu.CompilerParams(vmem_limit_bytes=VMEM_LIMIT),
        name="s5_prep",
    )(are, aim, ldt, bre_bd, bim_bd)


def _s5prepq_kernel(are_ref, aim_ref, ldt_ref, bre_ref, bim_ref, cre_ref, cim_ref, pq_ref, qq_ref, mq_ref):
    lam, f = _zoh(are_ref[...], aim_ref[...], ldt_ref[...])
    bb = _cmul(f, (bre_ref[...], bim_ref[...]))
    ct = (cre_ref[...], cim_ref[...])
    one = (jnp.ones_like(lam[0]), jnp.zeros_like(lam[0]))
    pw = [one]
    for _ in range(S5_SLOTS):
        pw.append(_cmul(pw[-1], lam))
    rows = S5_SLOTS * 4 * S5_GROUP
    blk = 4 * S5_GROUP
    for s in range(S5_SLOTS):
        p_re, p_im = _cmul(pw[S5_SLOTS - 1 - s], bb)
        pq_ref[:, s * blk:(s + 1) * blk, :QUAD_LANES] = p_re.astype(BF16)
        pq_ref[:, s * blk:(s + 1) * blk, QUAD_LANES:] = p_im.astype(BF16)
    cl = [_cmul(pw[e], ct) for e in range(S5_SLOTS + 1)]
    zero = jnp.zeros_like(cl[0][0])
    qt_re = jnp.concatenate([cl[t + 1][0] for t in range(S5_SLOTS)], axis=1)
    qt_im = jnp.concatenate([-cl[t + 1][1] for t in range(S5_SLOTS)], axis=1)
    for n in range(S5_QUADS):
        qq_ref[n, :QUAD_LANES, :] = qt_re[n].T.astype(BF16)
        qq_ref[n, QUAD_LANES:, :] = qt_im[n].T.astype(BF16)
    for s in range(S5_SLOTS):
        wide_re = jnp.concatenate([cl[t - s][0] if t >= s else zero for t in range(S5_SLOTS)], axis=1)
        wide_im = jnp.concatenate([cl[t - s][1] if t >= s else zero for t in range(S5_SLOTS)], axis=1)
        m = _bmm_nt3(bb[0], wide_re) - _bmm_nt3(bb[1], wide_im)
        mq_ref[:, s * blk:(s + 1) * blk, :] = m.astype(BF16)
    assert rows == QUAD_LANES


def _s5_prep_quads(are_q, aim_q, ldt_q, bre_q, bim_q, cre_q, cim_q):
    w = lambda r, c_: jax.ShapeDtypeStruct((S5_QUADS, r, c_), BF16)
    return pl.pallas_call(
        _s5prepq_kernel,
        out_shape=(w(QUAD_LANES, 2 * QUAD_LANES), w(2 * QUAD_LANES, QUAD_LANES), w(QUAD_LANES, QUAD_LANES)),
        compiler_params=pltpu.CompilerParams(vmem_limit_bytes=VMEM_LIMIT),
        name="s5_prep_quads",
    )(are_q, aim_q, ldt_q, bre_q, bim_q, cre_q, cim_q)


def _inproj_kernel(x_ref, g_ref, w_ref, qkv_ref, z_ref, u_ref, ba_ref):
    h = _rms(x_ref[...], g_ref[...]).astype(BF16)
    o1 = QKV_WIDTH
    o2 = o1 + DN_WIDTH
    o3 = o2 + S5_WIDTH
    qkv_ref[...] = _dot(h, w_ref[:, :o1])
    z_ref[...] = _dot(h, w_ref[:, o1:o2])
    u_ref[...] = _dot(h, w_ref[:, o2:o3])
    ba_ref[...] = _dot(h, w_ref[:, o3:])


def _in_proj(x2d, g, w, tm):
    n = x2d.shape[0]
    row = lambda w_: pl.BlockSpec((tm, w_), lambda i: (i, 0))
    return pl.pallas_call(
        _inproj_kernel,
        grid=(n // tm,),
        in_specs=[row(D_MODEL), _const_spec((1, D_MODEL)), _const_spec((D_MODEL, IN_PAD))],
        out_specs=(row(QKV_WIDTH), row(DN_WIDTH), row(S5_WIDTH), row(LANES)),
        out_shape=(jax.ShapeDtypeStruct((n, QKV_WIDTH), F32),
                   jax.ShapeDtypeStruct((n, DN_WIDTH), F32),
                   jax.ShapeDtypeStruct((n, S5_WIDTH), F32),
                   jax.ShapeDtypeStruct((n, LANES), F32)),
        compiler_params=_params(1),
        name="in_proj",
    )(x2d, g, w)


def _dn_kernel(qkv_ref, z_ref, ba_ref, cache_ref, s0_ref, cw_ref, gp_ref, ng_ref,
               o_ref, state_ref, halo_s, *, nb, tt, c):
    nh = DN_HEADS
    dh = DN_HEAD_DIM
    nck = tt // c
    npb = nb * nck
    hc = nh * c
    t = pl.program_id(1)

    @pl.when(t == 0)
    def _():
        halo_s[...] = cache_ref[...]
        state_ref[...] = s0_ref[...]

    x = qkv_ref[...]
    xx = jnp.concatenate([halo_s[...], x], axis=1)
    halo_s[...] = x[:, tt - SUBLANES:, :]
    cw = cw_ref[...]
    off = SUBLANES - (DN_CONV - 1)
    acc = xx[:, off:off + tt] * cw[0:1]
    for j in range(1, DN_CONV):
        acc = acc + xx[:, off + j:off + j + tt] * cw[j:j + 1]
    act = (acc * jax.nn.sigmoid(acc)).reshape(npb, c, QKV_WIDTH)

    def stack(a, base):
        return jnp.concatenate([a[:, :, base + h * dh:base + (h + 1) * dh] for h in range(nh)], axis=1)

    def unstack(a):
        return jnp.concatenate([a[:, h * c:(h + 1) * c, :] for h in range(nh)], axis=2)

    q = stack(act, 0)
    k = stack(act, DN_WIDTH)
    v = stack(act, 2 * DN_WIDTH)
    q = q * (lax.rsqrt(jnp.sum(q * q, axis=-1, keepdims=True) + EPS) * (dh ** -0.5))
    k = k * lax.rsqrt(jnp.sum(k * k, axis=-1, keepdims=True) + EPS)

    ba = ba_ref[...].reshape(nb * tt, LANES)
    beta = jax.nn.sigmoid(ba).reshape(npb, c, LANES)
    g2 = -jnp.exp(gp_ref[0:1, :]) * jax.nn.softplus(ba + gp_ref[1:2, :])
    row_in_chunk = lax.broadcasted_iota(jnp.int32, (nb * tt, LANES), 0) & (c - 1)
    d = 1
    while d < c:
        g2 = g2 + jnp.where(row_in_chunk >= d, pltpu.roll(g2, d, axis=0), 0.0)
        d *= 2
    g = g2.reshape(npb, c, LANES)

    g_cols = [g[:, :, BA_LANE_G + h:BA_LANE_G + h + 1] for h in range(nh)]
    b_cols = [beta[:, :, h:h + 1] for h in range(nh)]
    g_col = jnp.concatenate(g_cols, axis=1)
    b_col = jnp.concatenate(b_cols, axis=1)
    g_last = jnp.concatenate([jnp.broadcast_to(gc[:, c - 1:c, :], (npb, c, 1)) for gc in g_cols], axis=1)

    hp = min(nh, LANES // c)
    pieces = []
    for h0 in range(0, nh, hp):
        slab = jnp.concatenate(
            [(g2 if hh == 0 else pltpu.roll(g2, LANES - hh, axis=1)).reshape(npb, c, LANES)
             for hh in range(h0, h0 + hp)], axis=1)
        if hp * c < LANES:
            slab = jnp.concatenate([slab, jnp.zeros((npb, LANES - hp * c, LANES), F32)], axis=1)
        rows = [slab[p].T[BA_LANE_G:BA_LANE_G + 1, :hp * c] for p in range(npb)]
        pieces.append(jnp.stack(rows, axis=0))
    g_row = jnp.concatenate(pieces, axis=2)

    ri = lax.broadcasted_iota(jnp.int32, (c, hc), 0)
    lane = lax.broadcasted_iota(jnp.int32, (c, hc), 1)
    cj = lane & (c - 1)
    causal = (ri >= cj)[None]
    strict = (ri > cj)[None]

    def cat_from_cols(cols):
        out = jnp.broadcast_to(cols[nh - 1], (npb, c, hc))
        for h in range(nh - 2, -1, -1):
            out = jnp.where((lane < (h + 1) * c)[None], jnp.broadcast_to(cols[h], (npb, c, hc)), out)
        return out

    decay = jnp.exp(jnp.where(causal, cat_from_cols(g_cols) - g_row, -jnp.inf))

    kb = k.astype(BF16)
    br = lax.broadcasted_iota(jnp.int32, (hc, nh * dh), 0)
    bl = lax.broadcasted_iota(jnp.int32, (hc, nh * dh), 1)
    head_of_row = sum(jnp.where(br >= h * c, 1, 0) for h in range(1, nh))
    head_of_lane = sum(jnp.where(bl >= h * dh, 1, 0) for h in range(1, nh))
    k_bd = jnp.where((head_of_row == head_of_lane)[None], jnp.concatenate([kb] * nh, axis=2), 0.0)
    qk_lhs = jnp.concatenate([unstack(q), unstack(k)], axis=1).astype(BF16)
    qkk = _bmm_nt(qk_lhs, k_bd)
    qk = qkk[:, :c]
    kk = qkk[:, c:]

    sr = lax.broadcasted_iota(jnp.int32, (hc, hc), 0)
    sl = lax.broadcasted_iota(jnp.int32, (hc, hc), 1)
    shift = c.bit_length() - 1
    same_head = ((sr >> shift) == (sl >> shift))[None]

    def bd(m):
        return jnp.where(same_head, jnp.concatenate([m] * nh, axis=1), 0.0)

    lm = jnp.where(strict, cat_from_cols(b_cols) * kk * decay, 0.0)
    tinv = jnp.where((ri == cj)[None], 1.0, 0.0) - jnp.where(((ri >> 1) == (cj >> 1))[None], lm, 0.0)
    s = 2
    while s < c:
        sh = s.bit_length()
        lower_left = ((ri >> sh) == (cj >> sh)) & ((ri & s) != 0) & ((cj & s) == 0)
        a_off = jnp.where(lower_left[None], lm, 0.0).astype(BF16)
        xm = _bmm(tinv.astype(BF16), bd(a_off))
        tinv = tinv - _bmm(xm.astype(BF16), bd(tinv.astype(BF16)))
        s *= 2

    e_g = jnp.exp(g_col)
    rhs = jnp.concatenate([v * b_col, k * (b_col * e_g)], axis=2).astype(BF16)
    sol = _bmm(bd(tinv.astype(BF16)), rhs)
    sol_v = sol[:, :, :dh]
    sol_k = sol[:, :, dh:]
    w_qe = jnp.concatenate([sol_k, q * e_g], axis=2).astype(BF16)
    kd = (k * jnp.exp(g_last - g_col)).astype(BF16)
    qkd_bd = bd((qk * decay).astype(BF16))
    s_decay = jnp.exp(g_last)

    outs = []
    for b in range(nb):
        for ck in range(nck):
            p = b * nck + ck
            ws, qs = [], []
            states = [state_ref[b, h] for h in range(nh)]
            for h in range(nh):
                sb = states[h].astype(BF16)
                rows = w_qe[p, h * c:(h + 1) * c, :]
                ws.append(_dot(rows[:, :dh], sb))
                qs.append(_dot(rows[:, dh:], sb))
            u = sol_v[p] - jnp.concatenate(ws, axis=0)
            ub = u.astype(BF16)
            outs.append(jnp.concatenate(qs, axis=0) + _dot(qkd_bd[p], ub))
            for h in range(nh):
                r0 = h * c
                state_ref[b, h] = (states[h] * s_decay[p, r0:r0 + 1, :]
                                   + _dot_tn(kd[p, r0:r0 + c, :], ub[r0:r0 + c, :]))

    o = jnp.stack(outs, axis=0)
    zs = stack(z_ref[...].reshape(npb, c, DN_WIDTH), 0)
    o = _rms(o, ng_ref[...]) * (zs * jax.nn.sigmoid(zs))
    o_ref[...] = unstack(o).reshape(nb, tt, DN_WIDTH).astype(BF16)


def _deltanet(qkv, z, ba, cache8, s0, cw, gp, ng, nb, tt, c):
    b, l, _ = qkv.shape
    seq = lambda w_: pl.BlockSpec((nb, tt, w_), lambda i, j: (i, j, 0))
    per_b = lambda *s: pl.BlockSpec((nb,) + s, lambda i, j: (i,) + (0,) * len(s))
    return pl.pallas_call(
        functools.partial(_dn_kernel, nb=nb, tt=tt, c=c),
        grid=(b // nb, l // tt),
        in_specs=[seq(QKV_WIDTH), seq(DN_WIDTH), seq(LANES), per_b(SUBLANES, QKV_WIDTH),
                  per_b(DN_HEADS, DN_HEAD_DIM, DN_HEAD_DIM),
                  _const_spec((SUBLANES, QKV_WIDTH)), _const_spec((SUBLANES, LANES)),
                  _const_spec((1, DN_HEAD_DIM))],
        out_specs=(seq(DN_WIDTH), per_b(DN_HEADS, DN_HEAD_DIM, DN_HEAD_DIM)),
        out_shape=(jax.ShapeDtypeStruct((b, l, DN_WIDTH), BF16),
                   jax.ShapeDtypeStruct((b, DN_HEADS, DN_HEAD_DIM, DN_HEAD_DIM), F32)),
        scratch_shapes=[pltpu.VMEM((nb, SUBLANES, QKV_WIDTH), F32)],
        compiler_params=_params(2),
        name="deltanet",
    )(qkv, z, ba, cache8, s0, cw, gp, ng)


def _s5_kernel(u_ref, x0re_ref, x0im_ref, bcat_ref, tab_ref, cre_ref, cim_ref, dv_ref, gw_ref, gb_ref, ng_ref,
               o_ref, fre_ref, fim_ref, xr_s, xi_s, car_s, *, tt, ls, carry):
    t = pl.program_id(1)
    u = u_ref[0]
    ub = u.astype(BF16)
    xr_s[...] = _dot(ub, bcat_ref[:, :S5_LANES])
    xi_s[...] = _dot(ub, bcat_ref[:, S5_LANES:])

    if carry:
        @pl.when(t == 0)
        def _():
            car_s[0:1, :] = x0re_ref[0]
            car_s[1:2, :] = x0im_ref[0]

    def seq_body(sq, _):
        if carry:
            c0 = (car_s[0:1, :], car_s[1:2, :])
        else:
            c0 = (x0re_ref[sq], x0im_ref[sq])

        def block_body(rb, cin):
            c_re, c_im = cin
            r0 = pl.multiple_of(sq * ls + rb * SUBLANES, SUBLANES)
            xr = xr_s[pl.ds(r0, SUBLANES), :]
            xi = xi_s[pl.ds(r0, SUBLANES), :]
            for lvl, d in enumerate((1, 2, 4)):
                m_re = tab_ref[16 + 16 * lvl:24 + 16 * lvl, :]
                m_im = tab_ref[24 + 16 * lvl:32 + 16 * lvl, :]
                sr = pltpu.roll(xr, d, axis=0)
                si = pltpu.roll(xi, d, axis=0)
                xr, xi = xr + (m_re * sr - m_im * si), xi + (m_re * si + m_im * sr)
            p_re = tab_ref[0:8, :]
            p_im = tab_ref[8:16, :]
            xr, xi = xr + (p_re * c_re - p_im * c_im), xi + (p_re * c_im + p_im * c_re)
            xr_s[pl.ds(r0, SUBLANES), :] = xr
            xi_s[pl.ds(r0, SUBLANES), :] = xi
            return xr[SUBLANES - 1:, :], xi[SUBLANES - 1:, :]

        c_re, c_im = lax.fori_loop(0, ls // SUBLANES, block_body, c0)
        if carry:
            car_s[0:1, :] = c_re
            car_s[1:2, :] = c_im
            fre_ref[0] = c_re
            fim_ref[0] = c_im
        else:
            fre_ref[sq] = c_re
            fim_ref[sq] = c_im
        return 0

    lax.fori_loop(0, tt // ls, seq_body, 0)

    y = _dot(xr_s[...].astype(BF16), cre_ref[...]) - _dot(xi_s[...].astype(BF16), cim_ref[...])
    y = jax.nn.gelu(y + dv_ref[...] * u)
    gl = _dot(y.astype(BF16), gw_ref[...]) + gb_ref[...]
    o = gl[:, :S5_WIDTH] * jax.nn.sigmoid(gl[:, S5_WIDTH:])
    o_ref[0] = _rms(o, ng_ref[...]).astype(BF16)


def _s5(u, x0re, x0im, bcat, tab, cre_bd, cim_bd, dv, gw, gb, ng, tt, ls, carry):
    b, l, _ = u.shape
    nb = x0re.shape[0] // b
    seq = lambda w_: pl.BlockSpec((1, tt, w_), lambda i, j: (i, j, 0))
    st = pl.BlockSpec((nb, 1, S5_LANES), lambda i, j: (i, 0, 0))
    return pl.pallas_call(
        functools.partial(_s5_kernel, tt=tt, ls=ls, carry=carry),
        grid=(b, l // tt),
        in_specs=[seq(S5_WIDTH), st, st,
                  _const_spec((S5_WIDTH, 2 * S5_LANES)), _const_spec((64, S5_LANES)),
                  _const_spec((S5_LANES, S5_WIDTH)), _const_spec((S5_LANES, S5_WIDTH)),
                  _const_spec((1, S5_WIDTH)), _const_spec((S5_WIDTH, 2 * S5_WIDTH)),
                  _const_spec((1, 2 * S5_WIDTH)), _const_spec((1, S5_WIDTH))],
        out_specs=(seq(S5_WIDTH), st, st),
        out_shape=(jax.ShapeDtypeStruct((b, l, S5_WIDTH), BF16),
                   jax.ShapeDtypeStruct(x0re.shape, F32),
                   jax.ShapeDtypeStruct(x0re.shape, F32)),
        scratch_shapes=[pltpu.VMEM((tt, S5_LANES), F32), pltpu.VMEM((tt, S5_LANES), F32),
                        pltpu.VMEM((SUBLANES, S5_LANES), F32)],
        compiler_params=_params(2),
        name="s5",
    )(u, x0re, x0im, bcat, tab, cre_bd, cim_bd, dv, gw, gb, ng)


def _s5c_kernel(u4_ref, x0re_ref, x0im_ref, pq_ref, qq_ref, mq_ref, tab_ref, dv4_ref, gw_ref, gb_ref, ng_ref,
                o4_ref, fre_ref, fim_ref, xr_s, xi_s, car_s, *, rows):
    t = pl.program_id(1)
    u4 = u4_ref[0]
    ub = u4.astype(BF16)
    blk = 4 * S5_GROUP
    ql = QUAD_LANES

    @pl.when(t == 0)
    def _():
        car_s[0:1, :] = x0re_ref[0]
        car_s[1:2, :] = x0im_ref[0]

    uq = [jnp.concatenate([ub[:, s * S5_WIDTH + n * blk:s * S5_WIDTH + (n + 1) * blk] for s in range(S5_SLOTS)],
                          axis=1) for n in range(S5_QUADS)]
    for n in range(S5_QUADS):
        inc = _dot(uq[n], pq_ref[n])
        xr_s[:, n * ql:(n + 1) * ql] = inc[:, :ql]
        xi_s[:, n * ql:(n + 1) * ql] = inc[:, ql:]

    row = lax.broadcasted_iota(jnp.int32, (SUBLANES, S5_LANES), 0)

    def block_body(rb, cin):
        c_re, c_im = cin
        r0 = pl.multiple_of(rb * SUBLANES, SUBLANES)
        xr = xr_s[pl.ds(r0, SUBLANES), :]
        xi = xi_s[pl.ds(r0, SUBLANES), :]
        for lvl, d in enumerate((1, 2, 4)):
            m_re = tab_ref[16 + 16 * lvl:24 + 16 * lvl, :]
            m_im = tab_ref[24 + 16 * lvl:32 + 16 * lvl, :]
            sr = pltpu.roll(xr, d, axis=0)
            si = pltpu.roll(xi, d, axis=0)
            xr, xi = xr + (m_re * sr - m_im * si), xi + (m_re * si + m_im * sr)
        p_re = tab_ref[0:8, :]
        p_im = tab_ref[8:16, :]
        xr, xi = xr + (p_re * c_re - p_im * c_im), xi + (p_re * c_im + p_im * c_re)
        xr_s[pl.ds(r0, SUBLANES), :] = jnp.where(row == 0, c_re, pltpu.roll(xr, 1, axis=0))
        xi_s[pl.ds(r0, SUBLANES), :] = jnp.where(row == 0, c_im, pltpu.roll(xi, 1, axis=0))
        return xr[SUBLANES - 1:, :], xi[SUBLANES - 1:, :]

    c_re, c_im = lax.fori_loop(0, rows // SUBLANES, block_body, (car_s[0:1, :], car_s[1:2, :]))
    car_s[0:1, :] = c_re
    car_s[1:2, :] = c_im
    fre_ref[0] = c_re
    fim_ref[0] = c_im

    ys = []
    for n in range(S5_QUADS):
        xs = jnp.concatenate([xr_s[:, n * ql:(n + 1) * ql], xi_s[:, n * ql:(n + 1) * ql]], axis=1).astype(BF16)
        ys.append(_dot(uq[n], mq_ref[n]) + _dot(xs, qq_ref[n]))
    y4 = jnp.concatenate([ys[n][:, s * blk:(s + 1) * blk] for s in range(S5_SLOTS) for n in range(S5_QUADS)],
                         axis=1)
    y4 = jax.nn.gelu(y4 + dv4_ref[...] * u4).astype(BF16)
    for s in range(S5_SLOTS):
        gl = _dot(y4[:, s * S5_WIDTH:(s + 1) * S5_WIDTH], gw_ref[...]) + gb_ref[...]
        o = gl[:, :S5_WIDTH] * jax.nn.sigmoid(gl[:, S5_WIDTH:])
        o4_ref[0, :, s * S5_WIDTH:(s + 1) * S5_WIDTH] = _rms(o, ng_ref[...]).astype(BF16)


def _s5_chunked(u4, x0re, x0im, pq, qq, mq, tabc, dv4, gw, gb, ng, rows):
    b, l4, w4 = u4.shape
    seq = pl.BlockSpec((1, rows, w4), lambda i, j: (i, j, 0))
    st = pl.BlockSpec((1, 1, S5_LANES), lambda i, j: (i, 0, 0))
    return pl.pallas_call(
        functools.partial(_s5c_kernel, rows=rows),
        grid=(b, l4 // rows),
        in_specs=[seq, st, st,
                  _const_spec((S5_QUADS, QUAD_LANES, 2 * QUAD_LANES)),
                  _const_spec((S5_QUADS, 2 * QUAD_LANES, QUAD_LANES)),
                  _const_spec((S5_QUADS, QUAD_LANES, QUAD_LANES)), _const_spec((64, S5_LANES)),
                  _const_spec((1, w4)), _const_spec((S5_WIDTH, 2 * S5_WIDTH)),
                  _const_spec((1, 2 * S5_WIDTH)), _const_spec((1, S5_WIDTH))],
        out_specs=(seq, st, st),
        out_shape=(jax.ShapeDtypeStruct((b, l4, w4), BF16),
                   jax.ShapeDtypeStruct(x0re.shape, F32),
                   jax.ShapeDtypeStruct(x0re.shape, F32)),
        scratch_shapes=[pltpu.VMEM((rows, S5_LANES), F32), pltpu.VMEM((rows, S5_LANES), F32),
                        pltpu.VMEM((SUBLANES, S5_LANES), F32)],
        compiler_params=_params(2),
        name="s5_chunked",
    )(u4, x0re, x0im, pq, qq, mq, tabc, dv4, gw, gb, ng)


def _ffn_kernel(x_ref, odn_ref, os5_ref, prev_ref, woa_ref, wob_ref, n2_ref, wug_ref, wuv_ref,
                fcw_ref, fcb_ref, wd_ref, fg_ref, y_ref, tail_ref, halo_s, *, tm, ls, carry):
    t = pl.program_id(1)
    x1 = x_ref[0] + _dot(odn_ref[0], woa_ref[...]) + _dot(os5_ref[0], wob_ref[...])
    h2 = _rms(x1, n2_ref[...]).astype(BF16)
    if carry:
        @pl.when(t == 0)
        def _():
            halo_s[...] = prev_ref[0]

    down = None
    for lo in range(0, D_FF, FF_CHUNK):
        hi = min(lo + FF_CHUNK, D_FF)
        gate = _dot(h2, wug_ref[:, lo:hi])
        val = _dot(h2, wuv_ref[:, lo:hi])
        if carry:
            xx = jnp.concatenate([halo_s[:, lo:hi], gate], axis=0)
            g2 = xx[SUBLANES - 2:SUBLANES - 2 + tm]
            g1 = xx[SUBLANES - 1:SUBLANES - 1 + tm]
            halo_s[:, lo:hi] = gate[tm - SUBLANES:, :]
            tail_ref[0, :, lo:hi] = gate[tm - SUBLANES:, :]
        else:
            r = lax.broadcasted_iota(jnp.int32, (tm, hi - lo), 0) & (ls - 1)
            prev = prev_ref[0, :, lo:hi]
            g1 = jnp.where(r < 1, pltpu.roll(prev, tm - 1, axis=0), pltpu.roll(gate, 1, axis=0))
            g2 = jnp.where(r < 2, prev, pltpu.roll(gate, 2, axis=0))
            tail_ref[0, :, lo:hi] = gate
        cw = fcw_ref[:, lo:hi]
        pre = g2 * cw[0:1] + g1 * cw[1:2] + gate * cw[2:3] + fcb_ref[:, lo:hi]
        act = pre * jax.nn.sigmoid(pre) * val
        part = _dot(act.astype(BF16), wd_ref[lo:hi, :])
        down = part if down is None else down + part
    y_ref[0] = _rms(x1 + down, fg_ref[...])


def _ffn(x, odn, os5, prev, woa, wob, n2, wug, wuv, fcw, fcb, wd, fg, tm, ls, carry):
    b, l, _ = x.shape
    seq = lambda w_: pl.BlockSpec((1, tm, w_), lambda i, j: (i, j, 0))
    if carry:
        prev_spec = pl.BlockSpec((1, SUBLANES, D_FF), lambda i, j: (i, 0, 0))
        tail_spec = pl.BlockSpec((1, SUBLANES, D_FF), lambda i, j: (i, 0, 0))
        tail_shape = jax.ShapeDtypeStruct((b, SUBLANES, D_FF), F32)
    else:
        prev_spec = seq(D_FF)
        tail_spec = seq(D_FF)
        tail_shape = jax.ShapeDtypeStruct((b, l, D_FF), F32)
    return pl.pallas_call(
        functools.partial(_ffn_kernel, tm=tm, ls=ls, carry=carry),
        grid=(b, l // tm),
        in_specs=[seq(D_MODEL), seq(DN_WIDTH), seq(S5_WIDTH), prev_spec,
                  _const_spec((DN_WIDTH, D_MODEL)), _const_spec((S5_WIDTH, D_MODEL)),
                  _const_spec((1, D_MODEL)), _const_spec((D_MODEL, D_FF)), _const_spec((D_MODEL, D_FF)),
                  _const_spec((SUBLANES, D_FF)), _const_spec((1, D_FF)), _const_spec((D_FF, D_MODEL)),
                  _const_spec((1, D_MODEL))],
        out_specs=(seq(D_MODEL), tail_spec),
        out_shape=(jax.ShapeDtypeStruct((b, l, D_MODEL), F32), tail_shape),
        scratch_shapes=[pltpu.VMEM((SUBLANES, D_FF), F32)],
        compiler_params=_params(2),
        name="ffn",
    )(x, odn, os5, prev, woa, wob, n2, wug, wuv, fcw, fcb, wd, fg)


def _pad_rows_top(a, rows):
    return jnp.pad(a, ((0, 0), (rows - a.shape[1], 0), (0, 0)))


def _block_diag_in(b):
    same = jnp.eye(S5_GROUPS, dtype=bool)[:, None, :, None]
    blocks = jnp.where(same, b.transpose(0, 2, 1)[:, :, None, :], 0.0)
    return blocks.reshape(S5_WIDTH, S5_LANES)


def _block_diag_out(c):
    same = jnp.eye(S5_GROUPS, dtype=bool)[:, None, :, None]
    blocks = jnp.where(same, c.transpose(0, 2, 1)[:, :, None, :], 0.0)
    return blocks.reshape(S5_LANES, S5_WIDTH)


def _quad_blocks(m):
    same = jnp.eye(4, dtype=bool)[None, :, None, :, None]
    blocks = jnp.where(same, m.reshape(S5_QUADS, 4, S5_GROUP, 1, S5_STATE), 0.0)
    return blocks.reshape(S5_QUADS, 4 * S5_GROUP, QUAD_LANES)


def _trunk(x, conv_dn, s_dn, s5_re, s5_im, conv_ffn, w, prompt):
    b, l, _ = x.shape
    n = b * l
    c = CHUNK if l % CHUNK == 0 else l
    qkv, z, u, ba = _in_proj(x.reshape(n, D_MODEL), w['n1'], w['w_in'], min(n, 512))
    qkv = qkv.reshape(b, l, QKV_WIDTH)
    z = z.reshape(b, l, DN_WIDTH)
    ba = ba.reshape(b, l, LANES)

    o_dn, s_dn_new = _deltanet(qkv, z, ba, _pad_rows_top(conv_dn, SUBLANES), s_dn,
                               w['dn_cw'], w['dn_gp'], w['dn_g'], nb=min(b, 4 if prompt else 8),
                               tt=min(l, 128), c=c)
    assert l >= SUBLANES, "sequence shorter than one row tile"
    conv_dn_new = qkv[:, l - (DN_CONV - 1):]

    x0re = s5_re.reshape(b, 1, S5_LANES)
    x0im = s5_im.reshape(b, 1, S5_LANES)
    if prompt:
        l4 = l // S5_SLOTS
        o_s5, fre, fim = _s5_chunked(u.reshape(b, l4, S5_SLOTS * S5_WIDTH), x0re, x0im, w['pq'], w['qq'], w['mq'],
                                     w['tabc'], w['dv4'], w['glu_w'], w['glu_b'], w['s5_g'], rows=min(l4, 256))
        o_s5 = o_s5.reshape(b, l, S5_WIDTH)
    else:
        s5_args = (w['bcat'], w['tab'], w['cre_bd'], w['cim_bd'], w['dv'], w['glu_w'], w['glu_b'], w['s5_g'])
        o_s5, fre, fim = _s5(u.reshape(1, n, S5_WIDTH), x0re, x0im, *s5_args, tt=n, ls=l, carry=False)
        o_s5 = o_s5.reshape(b, l, S5_WIDTH)
    s5_re_new = fre.reshape(b, S5_GROUPS, S5_STATE)
    s5_im_new = fim.reshape(b, S5_GROUPS, S5_STATE)

    ffn_args = (w['w_out_a'], w['w_out_b'], w['n2'], w['w_up_g'], w['w_up_v'], w['fcw'], w['fcb'],
                w['w_down'], w['fg'])
    if prompt:
        prev = _pad_rows_top(conv_ffn, SUBLANES)
        y, tail = _ffn(x, o_dn, o_s5, prev, *ffn_args, tm=512, ls=512, carry=True)
        conv_ffn_new = tail[:, SUBLANES - (FFN_CONV - 1):]
    else:
        prev = jnp.pad(conv_ffn, ((0, 0), (0, l - (FFN_CONV - 1)), (0, 0))).reshape(1, n, D_FF)
        y, tail = _ffn(x.reshape(1, n, D_MODEL), o_dn.reshape(1, n, DN_WIDTH), o_s5.reshape(1, n, S5_WIDTH),
                       prev, *ffn_args, tm=min(n, 128), ls=l, carry=False)
        y = y.reshape(b, l, D_MODEL)
        conv_ffn_new = tail.reshape(b, l, D_FF)[:, l - (FFN_CONV - 1):]
    return y, (conv_dn_new[None], s_dn_new[None], s5_re_new[None], s5_im_new[None], conv_ffn_new[None])


def kernel(x_prompt, x_sample, cache_dn_conv, state_dn, state_s5_re, state_s5_im, cache_ffn_conv, norm1_g, w_in, dn_conv_w, dn_A_log, dn_dt_bias, dn_norm_g, s5_A_re, s5_A_im, s5_log_dt, s5_B_re, s5_B_im, s5_C_re, s5_C_im, s5_D, s5_glu_w, s5_glu_b, s5_norm_g, w_out, norm2_g, w_up, ffn_conv_w, ffn_conv_b, w_down, final_norm_g):
    assert w_in.shape[0] == 1, "single-layer trunk"
    o1 = QKV_WIDTH
    o2 = o1 + DN_WIDTH
    o4 = o2 + 2 * DN_HEADS
    wi = w_in[0]
    w_in_r = jnp.concatenate(
        [wi[:, :o2], wi[:, o4:], wi[:, o2:o4], jnp.zeros((D_MODEL, LANES - 2 * DN_HEADS), wi.dtype)],
        axis=1).astype(BF16)
    lane_pad = lambda v: jnp.pad(v, (BA_LANE_G, LANES - BA_LANE_G - DN_HEADS))
    ldt_lanes = jnp.repeat(s5_log_dt[0], S5_STATE)
    bcat, tab, tabc = _s5_prep(s5_A_re[0].reshape(1, S5_LANES), s5_A_im[0].reshape(1, S5_LANES),
                               ldt_lanes.reshape(1, S5_LANES),
                               _block_diag_in(s5_B_re[0]), _block_diag_in(s5_B_im[0]))
    quad_row = lambda v: v.reshape(S5_QUADS, 1, QUAD_LANES)
    pq, qq, mq = _s5_prep_quads(quad_row(s5_A_re[0]), quad_row(s5_A_im[0]), quad_row(ldt_lanes),
                                _quad_blocks(s5_B_re[0].transpose(0, 2, 1)), _quad_blocks(s5_B_im[0].transpose(0, 2, 1)),
                                _quad_blocks(s5_C_re[0]), _quad_blocks(s5_C_im[0]))
    w = {
        'n1': norm1_g, 'w_in': w_in_r,
        'dn_cw': jnp.pad(dn_conv_w[0], ((0, SUBLANES - DN_CONV), (0, 0))),
        'dn_gp': jnp.pad(jnp.stack([lane_pad(dn_A_log[0]), lane_pad(dn_dt_bias[0])]), ((0, SUBLANES - 2), (0, 0))),
        'dn_g': dn_norm_g,
        'bcat': bcat, 'tab': tab, 'tabc': tabc, 'pq': pq, 'qq': qq, 'mq': mq, 'dv4': jnp.tile(s5_D, (1, S5_SLOTS)),
        'cre_bd': _block_diag_out(s5_C_re[0]).astype(BF16), 'cim_bd': _block_diag_out(s5_C_im[0]).astype(BF16),
        'dv': s5_D, 'glu_w': s5_glu_w[0].astype(BF16), 'glu_b': s5_glu_b, 's5_g': s5_norm_g,
        'w_out_a': w_out[0, :DN_WIDTH].astype(BF16), 'w_out_b': w_out[0, DN_WIDTH:].astype(BF16),
        'n2': norm2_g, 'w_up_g': w_up[0, :, :D_FF].astype(BF16), 'w_up_v': w_up[0, :, D_FF:].astype(BF16),
        'fcw': jnp.pad(ffn_conv_w[0], ((0, SUBLANES - FFN_CONV), (0, 0))), 'fcb': ffn_conv_b,
        'w_down': w_down[0].astype(BF16), 'fg': final_norm_g.reshape(1, D_MODEL),
    }
    bp = x_prompt.shape[0]
    zeros = lambda *s: jnp.zeros(s, F32)
    y_p, st_p = _trunk(x_prompt, zeros(bp, DN_CONV - 1, QKV_WIDTH), zeros(bp, DN_HEADS, DN_HEAD_DIM, DN_HEAD_DIM),
                       zeros(bp, S5_GROUPS, S5_STATE), zeros(bp, S5_GROUPS, S5_STATE),
                       zeros(bp, FFN_CONV - 1, D_FF), w, prompt=True)
    y_s, st_s = _trunk(x_sample, cache_dn_conv[0], state_dn[0], state_s5_re[0], state_s5_im[0],
                       cache_ffn_conv[0], w, prompt=False)
    return (y_p, y_s) + st_p + st_s
```

```python
import functools

import jax
import jax.numpy as jnp
from jax import lax
from jax.experimental import pallas as pl
from jax.experimental.pallas import tpu as pltpu

F32 = jnp.float32
BF16 = jnp.bfloat16
EPS = 1e-6

D_MODEL = 1024
DN_HEADS = 4
DN_HEAD_DIM = 128
DN_WIDTH = DN_HEADS * DN_HEAD_DIM
DN_CONV = 4
QKV_WIDTH = 3 * DN_WIDTH
S5_WIDTH = D_MODEL - DN_WIDTH
S5_GROUP = 16
S5_GROUPS = S5_WIDTH // S5_GROUP
S5_STATE = 64
S5_LANES = S5_GROUPS * S5_STATE
S5_SLOTS = 4
S5_QUADS = S5_GROUPS // 4
QUAD_LANES = 4 * S5_STATE
D_FF = 2816
FFN_CONV = 3
CHUNK = 64

SUBLANES = 8
LANES = 128
BA_LANE_G = DN_HEADS
IN_PAD = QKV_WIDTH + DN_WIDTH + S5_WIDTH + LANES
MXU_DIM = 256
FF_CHUNK = 3 * MXU_DIM

VMEM_LIMIT = 56 * 1024 * 1024


def _dot(a, b):
    return jnp.dot(a, b, preferred_element_type=F32)


def _dot_nt(a, b):
    return lax.dot_general(a, b, (((1,), (1,)), ((), ())), preferred_element_type=F32)


def _dot_tn(a, b):
    return lax.dot_general(a, b, (((0,), (0,)), ((), ())), preferred_element_type=F32)


def _split_bf16(a):
    hi = a.astype(BF16)
    lo = (a - hi.astype(F32)).astype(BF16)
    return hi, lo


def _bmm(a, b):
    return lax.dot_general(a, b, (((2,), (1,)), ((0,), (0,))), preferred_element_type=F32)


def _bmm_nt(a, b):
    return lax.dot_general(a, b, (((2,), (2,)), ((0,), (0,))), preferred_element_type=F32)


def _bmm_nt3(a, b):
    ah, al = _split_bf16(a)
    bh, bl = _split_bf16(b)
    return _bmm_nt(ah, bh) + (_bmm_nt(al, bh) + _bmm_nt(ah, bl))


def _rms(x, g):
    return x * lax.rsqrt(jnp.mean(x * x, axis=-1, keepdims=True) + EPS) * g


def _const_spec(shape):
    nd = len(shape)
    return pl.BlockSpec(shape, lambda *_: (0,) * nd, pipeline_mode=pl.Buffered(1))


def _params(n_axes):
    return pltpu.CompilerParams(dimension_semantics=("arbitrary",) * n_axes,
                                vmem_limit_bytes=VMEM_LIMIT)


def _cmul(a, b):
    return a[0] * b[0] - a[1] * b[1], a[0] * b[1] + a[1] * b[0]


def _zoh(are, aim, ldt):
    dt = jnp.exp(ldt)
    mag = jnp.exp(are * dt)
    ang = aim * dt
    lr = mag * jnp.cos(ang)
    li = mag * jnp.sin(ang)
    den = are * are + aim * aim
    f_re = ((lr - 1.0) * are + li * aim) / den
    f_im = (li * are - (lr - 1.0) * aim) / den
    return (lr, li), (f_re, f_im)


def _write_scan_tables(tab_ref, step):
    pw = [step]
    for _ in range(SUBLANES - 1):
        pw.append(_cmul(pw[-1], step))
    shape = (SUBLANES, step[0].shape[-1])
    row = lax.broadcasted_iota(jnp.int32, shape, 0)
    zero = jnp.zeros(shape, F32)
    pre, pim = zero, zero
    for r in range(SUBLANES):
        pre = jnp.where(row == r, pw[r][0], pre)
        pim = jnp.where(row == r, pw[r][1], pim)
    tab_ref[0:8, :] = pre
    tab_ref[8:16, :] = pim
    for lvl, d in enumerate((1, 2, 4)):
        tab_ref[16 + 16 * lvl:24 + 16 * lvl, :] = jnp.where(row >= d, pw[d - 1][0], zero)
        tab_ref[24 + 16 * lvl:32 + 16 * lvl, :] = jnp.where(row >= d, pw[d - 1][1], zero)


def _s5prep_kernel(are_ref, aim_ref, ldt_ref, bre_ref, bim_ref, bcat_ref, tab_ref, tabc_ref):
    lam, (f_re, f_im) = _zoh(are_ref[...], aim_ref[...], ldt_ref[...])
    bre = bre_ref[...]
    bim = bim_ref[...]
    bcat_ref[:, :S5_LANES] = (f_re * bre - f_im * bim).astype(BF16)
    bcat_ref[:, S5_LANES:] = (f_re * bim + f_im * bre).astype(BF16)
    _write_scan_tables(tab_ref, lam)
    lam_c = lam
    for _ in range(S5_SLOTS - 1):
        lam_c = _cmul(lam_c, lam)
    _write_scan_tables(tabc_ref, lam_c)


def _s5_prep(are, aim, ldt, bre_bd, bim_bd):
    return pl.pallas_call(
        _s5prep_kernel,
        out_shape=(jax.ShapeDtypeStruct((S5_WIDTH, 2 * S5_LANES), BF16),
                   jax.ShapeDtypeStruct((64, S5_LANES), F32),
                   jax.ShapeDtypeStruct((64, S5_LANES), F32)),
        compiler_params=pltpu.CompilerParams(vmem_limit_bytes=VMEM_LIMIT),
        name="s5_prep",
    )(are, aim, ldt, bre_bd, bim_bd)


def _s5prepq_kernel(are_ref, aim_ref, ldt_ref, bre_ref, bim_ref, cre_ref, cim_ref, pq_ref, qq_ref, mq_ref):
    lam, f = _zoh(are_ref[...], aim_ref[...], ldt_ref[...])
    bb = _cmul(f, (bre_ref[...], bim_ref[...]))
    ct = (cre_ref[...], cim_ref[...])
    one = (jnp.ones_like(lam[0]), jnp.zeros_like(lam[0]))
    pw = [one]
    for _ in range(S5_SLOTS):
        pw.append(_cmul(pw[-1], lam))
    rows = S5_SLOTS * 4 * S5_GROUP
    blk = 4 * S5_GROUP
    for s in range(S5_SLOTS):
        p_re, p_im = _cmul(pw[S5_SLOTS - 1 - s], bb)
        pq_ref[:, s * blk:(s + 1) * blk, :QUAD_LANES] = p_re.astype(BF16)
        pq_ref[:, s * blk:(s + 1) * blk, QUAD_LANES:] = p_im.astype(BF16)
    cl = [_cmul(pw[e], ct) for e in range(S5_SLOTS + 1)]
    zero = jnp.zeros_like(cl[0][0])
    qt_re = jnp.concatenate([cl[t + 1][0] for t in range(S5_SLOTS)], axis=1)
    qt_im = jnp.concatenate([-cl[t + 1][1] for t in range(S5_SLOTS)], axis=1)
    for n in range(S5_QUADS):
        qq_ref[n, :QUAD_LANES, :] = qt_re[n].T.astype(BF16)
        qq_ref[n, QUAD_LANES:, :] = qt_im[n].T.astype(BF16)
    for s in range(S5_SLOTS):
        wide_re = jnp.concatenate([cl[t - s][0] if t >= s else zero for t in range(S5_SLOTS)], axis=1)
        wide_im = jnp.concatenate([cl[t - s][1] if t >= s else zero for t in range(S5_SLOTS)], axis=1)
        m = _bmm_nt3(bb[0], wide_re) - _bmm_nt3(bb[1], wide_im)
        mq_ref[:, s * blk:(s + 1) * blk, :] = m.astype(BF16)
    assert rows == QUAD_LANES


def _s5_prep_quads(are_q, aim_q, ldt_q, bre_q, bim_q, cre_q, cim_q):
    w = lambda r, c_: jax.ShapeDtypeStruct((S5_QUADS, r, c_), BF16)
    return pl.pallas_call(
        _s5prepq_kernel,
        out_shape=(w(QUAD_LANES, 2 * QUAD_LANES), w(2 * QUAD_LANES, QUAD_LANES), w(QUAD_LANES, QUAD_LANES)),
        compiler_params=pltpu.CompilerParams(vmem_limit_bytes=VMEM_LIMIT),
        name="s5_prep_quads",
    )(are_q, aim_q, ldt_q, bre_q, bim_q, cre_q, cim_q)


def _inproj_kernel(x_ref, g_ref, w_ref, qkv_ref, z_ref, u_ref, ba_ref):
    h = _rms(x_ref[...], g_ref[...]).astype(BF16)
    o1 = QKV_WIDTH
    o2 = o1 + DN_WIDTH
    o3 = o2 + S5_WIDTH
    qkv_ref[...] = _dot(h, w_ref[:, :o1])
    z_ref[...] = _dot(h, w_ref[:, o1:o2])
    u_ref[...] = _dot(h, w_ref[:, o2:o3])
    ba_ref[...] = _dot(h, w_ref[:, o3:])


def _in_proj(x2d, g, w, tm):
    n = x2d.shape[0]
    row = lambda w_: pl.BlockSpec((tm, w_), lambda i: (i, 0))
    return pl.pallas_call(
        _inproj_kernel,
        grid=(n // tm,),
        in_specs=[row(D_MODEL), _const_spec((1, D_MODEL)), _const_spec((D_MODEL, IN_PAD))],
        out_specs=(row(QKV_WIDTH), row(DN_WIDTH), row(S5_WIDTH), row(LANES)),
        out_shape=(jax.ShapeDtypeStruct((n, QKV_WIDTH), F32),
                   jax.ShapeDtypeStruct((n, DN_WIDTH), F32),
                   jax.ShapeDtypeStruct((n, S5_WIDTH), F32),
                   jax.ShapeDtypeStruct((n, LANES), F32)),
        compiler_params=_params(1),
        name="in_proj",
    )(x2d, g, w)


def _dn_kernel(qkv_ref, z_ref, ba_ref, cache_ref, s0_ref, cw_ref, gp_ref, ng_ref,
               o_ref, state_ref, halo_s, *, nb, tt, c):
    nh = DN_HEADS
    dh = DN_HEAD_DIM
    nck = tt // c
    npb = nb * nck
    hc = nh * c
    t = pl.program_id(1)

    @pl.when(t == 0)
    def _():
        halo_s[...] = cache_ref[...]
        state_ref[...] = s0_ref[...]

    cw = cw_ref[...]
    off = SUBLANES - (DN_CONV - 1)
    head = jnp.concatenate([halo_s[...], qkv_ref[:, 0:SUBLANES, :]], axis=1)
    acc_head = head[:, off:off + SUBLANES] * cw[0:1]
    acc_rest = qkv_ref[:, off:off + tt - SUBLANES, :] * cw[0:1]
    for j in range(1, DN_CONV):
        acc_head = acc_head + head[:, off + j:off + j + SUBLANES] * cw[j:j + 1]
        acc_rest = acc_rest + qkv_ref[:, off + j:off + j + tt - SUBLANES, :] * cw[j:j + 1]
    halo_s[...] = qkv_ref[:, tt - SUBLANES:, :]
    acc = jnp.concatenate([acc_head, acc_rest], axis=1)
    act = (acc * jax.nn.sigmoid(acc)).reshape(npb, c, QKV_WIDTH)

    def stack(a, base):
        return jnp.concatenate([a[:, :, base + h * dh:base + (h + 1) * dh] for h in range(nh)], axis=1)

    def unstack(a):
        return jnp.concatenate([a[:, h * c:(h + 1) * c, :] for h in range(nh)], axis=2)

    q = stack(act, 0)
    k = stack(act, DN_WIDTH)
    v = stack(act, 2 * DN_WIDTH)
    q = q * (lax.rsqrt(jnp.sum(q * q, axis=-1, keepdims=True) + EPS) * (dh ** -0.5))
    k = k * lax.rsqrt(jnp.sum(k * k, axis=-1, keepdims=True) + EPS)

    ba = ba_ref[...].reshape(nb * tt, LANES)
    beta = jax.nn.sigmoid(ba).reshape(npb, c, LANES)
    g2 = -jnp.exp(gp_ref[0:1, :]) * jax.nn.softplus(ba + gp_ref[1:2, :])
    row_in_chunk = lax.broadcasted_iota(jnp.int32, (nb * tt, LANES), 0) & (c - 1)
    d = 1
    while d < c:
        g2 = g2 + jnp.where(row_in_chunk >= d, pltpu.roll(g2, d, axis=0), 0.0)
        d *= 2
    g = g2.reshape(npb, c, LANES)

    g_cols = [g[:, :, BA_LANE_G + h:BA_LANE_G + h + 1] for h in range(nh)]
    b_cols = [beta[:, :, h:h + 1] for h in range(nh)]
    g_col = jnp.concatenate(g_cols, axis=1)
    b_col = jnp.concatenate(b_cols, axis=1)
    g_last = jnp.concatenate([jnp.broadcast_to(gc[:, c - 1:c, :], (npb, c, 1)) for gc in g_cols], axis=1)

    hp = min(nh, LANES // c)
    pieces = []
    for h0 in range(0, nh, hp):
        slab = jnp.concatenate(
            [(g2 if hh == 0 else pltpu.roll(g2, LANES - hh, axis=1)).reshape(npb, c, LANES)
             for hh in range(h0, h0 + hp)], axis=1)
        if hp * c < LANES:
            slab = jnp.concatenate([slab, jnp.zeros((npb, LANES - hp * c, LANES), F32)], axis=1)
        rows = [slab[p].T[BA_LANE_G:BA_LANE_G + 1, :hp * c] for p in range(npb)]
        pieces.append(jnp.stack(rows, axis=0))
    g_row = jnp.concatenate(pieces, axis=2)

    ri = lax.broadcasted_iota(jnp.int32, (c, hc), 0)
    lane = lax.broadcasted_iota(jnp.int32, (c, hc), 1)
    cj = lane & (c - 1)
    causal = (ri >= cj)[None]
    strict = (ri > cj)[None]

    def cat_from_cols(cols):
        out = jnp.broadcast_to(cols[nh - 1], (npb, c, hc))
        for h in range(nh - 2, -1, -1):
            out = jnp.where((lane < (h + 1) * c)[None], jnp.broadcast_to(cols[h], (npb, c, hc)), out)
        return out

    decay = jnp.exp(jnp.where(causal, cat_from_cols(g_cols) - g_row, -jnp.inf))

    kb = k.astype(BF16)
    br = lax.broadcasted_iota(jnp.int32, (hc, nh * dh), 0)
    bl = lax.broadcasted_iota(jnp.int32, (hc, nh * dh), 1)
    head_of_row = sum(jnp.where(br >= h * c, 1, 0) for h in range(1, nh))
    head_of_lane = sum(jnp.where(bl >= h * dh, 1, 0) for h in range(1, nh))
    k_bd = jnp.where((head_of_row == head_of_lane)[None], jnp.concatenate([kb] * nh, axis=2), 0.0)
    qk_lhs = jnp.concatenate([unstack(q), unstack(k)], axis=1).astype(BF16)
    qkk = _bmm_nt(qk_lhs, k_bd)
    qk = qkk[:, :c]
    kk = qkk[:, c:]

    sr = lax.broadcasted_iota(jnp.int32, (hc, hc), 0)
    sl = lax.broadcasted_iota(jnp.int32, (hc, hc), 1)
    shift = c.bit_length() - 1
    same_head = ((sr >> shift) == (sl >> shift))[None]

    def bd(m):
        return jnp.where(same_head, jnp.concatenate([m] * nh, axis=1), 0.0)

    lm = jnp.where(strict, cat_from_cols(b_cols) * kk * decay, 0.0)
    tinv = jnp.where((ri == cj)[None], 1.0, 0.0) - jnp.where(((ri >> 1) == (cj >> 1))[None], lm, 0.0)
    s = 2
    while s < c:
        sh = s.bit_length()
        lower_left = ((ri >> sh) == (cj >> sh)) & ((ri & s) != 0) & ((cj & s) == 0)
        a_off = jnp.where(lower_left[None], lm, 0.0).astype(BF16)
        xm = _bmm(tinv.astype(BF16), bd(a_off))
        tinv = tinv - _bmm(xm.astype(BF16), bd(tinv.astype(BF16)))
        s *= 2

    e_g = jnp.exp(g_col)
    rhs = jnp.concatenate([v * b_col, k * (b_col * e_g)], axis=2).astype(BF16)
    sol = _bmm(bd(tinv.astype(BF16)), rhs)
    sol_v = sol[:, :, :dh]
    sol_k = sol[:, :, dh:]
    w_qe = jnp.concatenate([sol_k, q * e_g], axis=2).astype(BF16)
    kd = (k * jnp.exp(g_last - g_col)).astype(BF16)
    qkd_bd = bd((qk * decay).astype(BF16))
    s_decay = jnp.exp(g_last)

    outs = []
    for b in range(nb):
        for ck in range(nck):
            p = b * nck + ck
            ws, qs = [], []
            states = [state_ref[b, h] for h in range(nh)]
            for h in range(nh):
                sb = states[h].astype(BF16)
                rows = w_qe[p, h * c:(h + 1) * c, :]
                ws.append(_dot(rows[:, :dh], sb))
                qs.append(_dot(rows[:, dh:], sb))
            u = sol_v[p] - jnp.concatenate(ws, axis=0)
            ub = u.astype(BF16)
            outs.append(jnp.concatenate(qs, axis=0) + _dot(qkd_bd[p], ub))
            for h in range(nh):
                r0 = h * c
                state_ref[b, h] = (states[h] * s_decay[p, r0:r0 + 1, :]
                                   + _dot_tn(kd[p, r0:r0 + c, :], ub[r0:r0 + c, :]))

    o = jnp.stack(outs, axis=0)
    zs = stack(z_ref[...].reshape(npb, c, DN_WIDTH), 0)
    o = _rms(o, ng_ref[...]) * (zs * jax.nn.sigmoid(zs))
    o_ref[...] = unstack(o).reshape(nb, tt, DN_WIDTH).astype(BF16)


def _deltanet(qkv, z, ba, cache8, s0, cw, gp, ng, nb, tt, c):
    b, l, _ = qkv.shape
    seq = lambda w_: pl.BlockSpec((nb, tt, w_), lambda i, j: (i, j, 0))
    per_b = lambda *s: pl.BlockSpec((nb,) + s, lambda i, j: (i,) + (0,) * len(s))
    return pl.pallas_call(
        functools.partial(_dn_kernel, nb=nb, tt=tt, c=c),
        grid=(b // nb, l // tt),
        in_specs=[seq(QKV_WIDTH), seq(DN_WIDTH), seq(LANES), per_b(SUBLANES, QKV_WIDTH),
                  per_b(DN_HEADS, DN_HEAD_DIM, DN_HEAD_DIM),
                  _const_spec((SUBLANES, QKV_WIDTH)), _const_spec((SUBLANES, LANES)),
                  _const_spec((1, DN_HEAD_DIM))],
        out_specs=(seq(DN_WIDTH), per_b(DN_HEADS, DN_HEAD_DIM, DN_HEAD_DIM)),
        out_shape=(jax.ShapeDtypeStruct((b, l, DN_WIDTH), BF16),
                   jax.ShapeDtypeStruct((b, DN_HEADS, DN_HEAD_DIM, DN_HEAD_DIM), F32)),
        scratch_shapes=[pltpu.VMEM((nb, SUBLANES, QKV_WIDTH), F32)],
        compiler_params=_params(2),
        name="deltanet",
    )(qkv, z, ba, cache8, s0, cw, gp, ng)


def _s5_kernel(u_ref, x0re_ref, x0im_ref, bcat_ref, tab_ref, cre_ref, cim_ref, dv_ref, gw_ref, gb_ref, ng_ref,
               o_ref, fre_ref, fim_ref, xr_s, xi_s, car_s, *, tt, ls, carry):
    t = pl.program_id(1)
    u = u_ref[0]
    ub = u.astype(BF16)
    xr_s[...] = _dot(ub, bcat_ref[:, :S5_LANES])
    xi_s[...] = _dot(ub, bcat_ref[:, S5_LANES:])

    if carry:
        @pl.when(t == 0)
        def _():
            car_s[0:1, :] = x0re_ref[0]
            car_s[1:2, :] = x0im_ref[0]

    def seq_body(sq, _):
        if carry:
            c0 = (car_s[0:1, :], car_s[1:2, :])
        else:
            c0 = (x0re_ref[sq], x0im_ref[sq])

        def block_body(rb, cin):
            c_re, c_im = cin
            r0 = pl.multiple_of(sq * ls + rb * SUBLANES, SUBLANES)
            xr = xr_s[pl.ds(r0, SUBLANES), :]
            xi = xi_s[pl.ds(r0, SUBLANES), :]
            for lvl, d in enumerate((1, 2, 4)):
                m_re = tab_ref[16 + 16 * lvl:24 + 16 * lvl, :]
                m_im = tab_ref[24 + 16 * lvl:32 + 16 * lvl, :]
                sr = pltpu.roll(xr, d, axis=0)
                si = pltpu.roll(xi, d, axis=0)
                xr, xi = xr + (m_re * sr - m_im * si), xi + (m_re * si + m_im * sr)
            p_re = tab_ref[0:8, :]
            p_im = tab_ref[8:16, :]
            xr, xi = xr + (p_re * c_re - p_im * c_im), xi + (p_re * c_im + p_im * c_re)
            xr_s[pl.ds(r0, SUBLANES), :] = xr
            xi_s[pl.ds(r0, SUBLANES), :] = xi
            return xr[SUBLANES - 1:, :], xi[SUBLANES - 1:, :]

        c_re, c_im = lax.fori_loop(0, ls // SUBLANES, block_body, c0)
        if carry:
            car_s[0:1, :] = c_re
            car_s[1:2, :] = c_im
            fre_ref[0] = c_re
            fim_ref[0] = c_im
        else:
            fre_ref[sq] = c_re
            fim_ref[sq] = c_im
        return 0

    lax.fori_loop(0, tt // ls, seq_body, 0)

    y = _dot(xr_s[...].astype(BF16), cre_ref[...]) - _dot(xi_s[...].astype(BF16), cim_ref[...])
    y = jax.nn.gelu(y + dv_ref[...] * u)
    gl = _dot(y.astype(BF16), gw_ref[...]) + gb_ref[...]
    o = gl[:, :S5_WIDTH] * jax.nn.sigmoid(gl[:, S5_WIDTH:])
    o_ref[0] = _rms(o, ng_ref[...]).astype(BF16)


def _s5(u, x0re, x0im, bcat, tab, cre_bd, cim_bd, dv, gw, gb, ng, tt, ls, carry):
    b, l, _ = u.shape
    nb = x0re.shape[0] // b
    seq = lambda w_: pl.BlockSpec((1, tt, w_), lambda i, j: (i, j, 0))
    st = pl.BlockSpec((nb, 1, S5_LANES), lambda i, j: (i, 0, 0))
    return pl.pallas_call(
        functools.partial(_s5_kernel, tt=tt, ls=ls, carry=carry),
        grid=(b, l // tt),
        in_specs=[seq(S5_WIDTH), st, st,
                  _const_spec((S5_WIDTH, 2 * S5_LANES)), _const_spec((64, S5_LANES)),
                  _const_spec((S5_LANES, S5_WIDTH)), _const_spec((S5_LANES, S5_WIDTH)),
                  _const_spec((1, S5_WIDTH)), _const_spec((S5_WIDTH, 2 * S5_WIDTH)),
                  _const_spec((1, 2 * S5_WIDTH)), _const_spec((1, S5_WIDTH))],
        out_specs=(seq(S5_WIDTH), st, st),
        out_shape=(jax.ShapeDtypeStruct((b, l, S5_WIDTH), BF16),
                   jax.ShapeDtypeStruct(x0re.shape, F32),
                   jax.ShapeDtypeStruct(x0re.shape, F32)),
        scratch_shapes=[pltpu.VMEM((tt, S5_LANES), F32), pltpu.VMEM((tt, S5_LANES), F32),
                        pltpu.VMEM((SUBLANES, S5_LANES), F32)],
        compiler_params=_params(2),
        name="s5",
    )(u, x0re, x0im, bcat, tab, cre_bd, cim_bd, dv, gw, gb, ng)


def _s5c_kernel(u_ref, x0re_ref, x0im_ref, pq_ref, qq_ref, mq_ref, tab_ref, dv_ref, gw_ref, gb_ref, ng_ref,
                o_ref, fre_ref, fim_ref, xr_s, xi_s, car_s, u_s, o_s, *, rows):
    t = pl.program_id(1)
    ncol = S5_WIDTH // LANES
    for k in range(ncol):
        u_s[k] = u_ref[0, :, k * LANES:(k + 1) * LANES]
    us = [jnp.concatenate([u_s[k, pl.ds(s, rows, stride=S5_SLOTS), :] for k in range(ncol)], axis=1)
          for s in range(S5_SLOTS)]
    ubs = [a.astype(BF16) for a in us]
    blk = 4 * S5_GROUP
    ql = QUAD_LANES

    @pl.when(t == 0)
    def _():
        car_s[0:1, :] = x0re_ref[0]
        car_s[1:2, :] = x0im_ref[0]

    uq = [jnp.concatenate([ubs[s][:, n * blk:(n + 1) * blk] for s in range(S5_SLOTS)], axis=1)
          for n in range(S5_QUADS)]
    for n in range(S5_QUADS):
        inc = _dot(uq[n], pq_ref[n])
        xr_s[:, n * ql:(n + 1) * ql] = inc[:, :ql]
        xi_s[:, n * ql:(n + 1) * ql] = inc[:, ql:]

    row = lax.broadcasted_iota(jnp.int32, (SUBLANES, S5_LANES), 0)

    def block_body(rb, cin):
        c_re, c_im = cin
        r0 = pl.multiple_of(rb * SUBLANES, SUBLANES)
        xr = xr_s[pl.ds(r0, SUBLANES), :]
        xi = xi_s[pl.ds(r0, SUBLANES), :]
        for lvl, d in enumerate((1, 2, 4)):
            m_re = tab_ref[16 + 16 * lvl:24 + 16 * lvl, :]
            m_im = tab_ref[24 + 16 * lvl:32 + 16 * lvl, :]
            sr = pltpu.roll(xr, d, axis=0)
            si = pltpu.roll(xi, d, axis=0)
            xr, xi = xr + (m_re * sr - m_im * si), xi + (m_re * si + m_im * sr)
        p_re = tab_ref[0:8, :]
        p_im = tab_ref[8:16, :]
        xr, xi = xr + (p_re * c_re - p_im * c_im), xi + (p_re * c_im + p_im * c_re)
        xr_s[pl.ds(r0, SUBLANES), :] = jnp.where(row == 0, c_re, pltpu.roll(xr, 1, axis=0))
        xi_s[pl.ds(r0, SUBLANES), :] = jnp.where(row == 0, c_im, pltpu.roll(xi, 1, axis=0))
        return xr[SUBLANES - 1:, :], xi[SUBLANES - 1:, :]

    c_re, c_im = lax.fori_loop(0, rows // SUBLANES, block_body, (car_s[0:1, :], car_s[1:2, :]))
    car_s[0:1, :] = c_re
    car_s[1:2, :] = c_im
    fre_ref[0] = c_re
    fim_ref[0] = c_im

    ys = []
    for n in range(S5_QUADS):
        xs = jnp.concatenate([xr_s[:, n * ql:(n + 1) * ql], xi_s[:, n * ql:(n + 1) * ql]], axis=1).astype(BF16)
        ys.append(_dot(uq[n], mq_ref[n]) + _dot(xs, qq_ref[n]))
    for s in range(S5_SLOTS):
        y = jnp.concatenate([ys[n][:, s * blk:(s + 1) * blk] for n in range(S5_QUADS)], axis=1)
        y = jax.nn.gelu(y + dv_ref[...] * us[s]).astype(BF16)
        gl = _dot(y, gw_ref[...]) + gb_ref[...]
        o = gl[:, :S5_WIDTH] * jax.nn.sigmoid(gl[:, S5_WIDTH:])
        o = _rms(o, ng_ref[...])
        for k in range(ncol):
            o_s[k, pl.ds(s, rows, stride=S5_SLOTS), :] = o[:, k * LANES:(k + 1) * LANES]
    o_ref[0] = jnp.concatenate([o_s[k] for k in range(ncol)], axis=1).astype(BF16)


def _s5_chunked(u, x0re, x0im, pq, qq, mq, tabc, dv, gw, gb, ng, rows):
    b, l, _ = u.shape
    tt = rows * S5_SLOTS
    seq = pl.BlockSpec((1, tt, S5_WIDTH), lambda i, j: (i, j, 0))
    st = pl.BlockSpec((1, 1, S5_LANES), lambda i, j: (i, 0, 0))
    return pl.pallas_call(
        functools.partial(_s5c_kernel, rows=rows),
        grid=(b, l // tt),
        in_specs=[seq, st, st,
                  _const_spec((S5_QUADS, QUAD_LANES, 2 * QUAD_LANES)),
                  _const_spec((S5_QUADS, 2 * QUAD_LANES, QUAD_LANES)),
                  _const_spec((S5_QUADS, QUAD_LANES, QUAD_LANES)), _const_spec((64, S5_LANES)),
                  _const_spec((1, S5_WIDTH)), _const_spec((S5_WIDTH, 2 * S5_WIDTH)),
                  _const_spec((1, 2 * S5_WIDTH)), _const_spec((1, S5_WIDTH))],
        out_specs=(seq, st, st),
        out_shape=(jax.ShapeDtypeStruct((b, l, S5_WIDTH), BF16),
                   jax.ShapeDtypeStruct(x0re.shape, F32),
                   jax.ShapeDtypeStruct(x0re.shape, F32)),
        scratch_shapes=[pltpu.VMEM((rows, S5_LANES), F32), pltpu.VMEM((rows, S5_LANES), F32),
                        pltpu.VMEM((SUBLANES, S5_LANES), F32),
                        pltpu.VMEM((S5_WIDTH // LANES, tt, LANES), F32),
                        pltpu.VMEM((S5_WIDTH // LANES, tt, LANES), F32)],
        compiler_params=_params(2),
        name="s5_chunked",
    )(u, x0re, x0im, pq, qq, mq, tabc, dv, gw, gb, ng)


def _ffn_kernel(x_ref, odn_ref, os5_ref, prev_ref, woa_ref, wob_ref, n2_ref, wug_ref, wuv_ref,
                fcw_ref, fcb_ref, wd_ref, fg_ref, y_ref, tail_ref, halo_s, *, tm, ls, carry):
    t = pl.program_id(1)
    x1 = x_ref[0] + _dot(odn_ref[0], woa_ref[...]) + _dot(os5_ref[0], wob_ref[...])
    h2 = _rms(x1, n2_ref[...]).astype(BF16)
    if carry:
        @pl.when(t == 0)
        def _():
            halo_s[...] = prev_ref[0]

    down = None
    for lo in range(0, D_FF, FF_CHUNK):
        hi = min(lo + FF_CHUNK, D_FF)
        gate = _dot(h2, wug_ref[:, lo:hi])
        val = _dot(h2, wuv_ref[:, lo:hi])
        if carry:
            xx = jnp.concatenate([halo_s[:, lo:hi], gate], axis=0)
            g2 = xx[SUBLANES - 2:SUBLANES - 2 + tm]
            g1 = xx[SUBLANES - 1:SUBLANES - 1 + tm]
            halo_s[:, lo:hi] = gate[tm - SUBLANES:, :]
            tail_ref[0, :, lo:hi] = gate[tm - SUBLANES:, :]
        else:
            r = lax.broadcasted_iota(jnp.int32, (tm, hi - lo), 0) & (ls - 1)
            prev = prev_ref[0, :, lo:hi]
            g1 = jnp.where(r < 1, pltpu.roll(prev, tm - 1, axis=0), pltpu.roll(gate, 1, axis=0))
            g2 = jnp.where(r < 2, prev, pltpu.roll(gate, 2, axis=0))
            tail_ref[0, :, lo:hi] = gate
        cw = fcw_ref[:, lo:hi]
        pre = g2 * cw[0:1] + g1 * cw[1:2] + gate * cw[2:3] + fcb_ref[:, lo:hi]
        act = pre * jax.nn.sigmoid(pre) * val
        part = _dot(act.astype(BF16), wd_ref[lo:hi, :])
        down = part if down is None else down + part
    y_ref[0] = _rms(x1 + down, fg_ref[...])


def _ffn(x, odn, os5, prev, woa, wob, n2, wug, wuv, fcw, fcb, wd, fg, tm, ls, carry):
    b, l, _ = x.shape
    seq = lambda w_: pl.BlockSpec((1, tm, w_), lambda i, j: (i, j, 0))
    if carry:
        prev_spec = pl.BlockSpec((1, SUBLANES, D_FF), lambda i, j: (i, 0, 0))
        tail_spec = pl.BlockSpec((1, SUBLANES, D_FF), lambda i, j: (i, 0, 0))
        tail_shape = jax.ShapeDtypeStruct((b, SUBLANES, D_FF), F32)
    else:
        prev_spec = seq(D_FF)
        tail_spec = seq(D_FF)
        tail_shape = jax.ShapeDtypeStruct((b, l, D_FF), F32)
    return pl.pallas_call(
        functools.partial(_ffn_kernel, tm=tm, ls=ls, carry=carry),
        grid=(b, l // tm),
        in_specs=[seq(D_MODEL), seq(DN_WIDTH), seq(S5_WIDTH), prev_spec,
                  _const_spec((DN_WIDTH, D_MODEL)), _const_spec((S5_WIDTH, D_MODEL)),
                  _const_spec((1, D_MODEL)), _const_spec((D_MODEL, D_FF)), _const_spec((D_MODEL, D_FF)),
                  _const_spec((SUBLANES, D_FF)), _const_spec((1, D_FF)), _const_spec((D_FF, D_MODEL)),
                  _const_spec((1, D_MODEL))],
        out_specs=(seq(D_MODEL), tail_spec),
        out_shape=(jax.ShapeDtypeStruct((b, l, D_MODEL), F32), tail_shape),
        scratch_shapes=[pltpu.VMEM((SUBLANES, D_FF), F32)],
        compiler_params=_params(2),
        name="ffn",
    )(x, odn, os5, prev, woa, wob, n2, wug, wuv, fcw, fcb, wd, fg)


def _pad_rows_top(a, rows):
    return jnp.pad(a, ((0, 0), (rows - a.shape[1], 0), (0, 0)))


def _block_diag_in(b):
    same = jnp.eye(S5_GROUPS, dtype=bool)[:, None, :, None]
    blocks = jnp.where(same, b.transpose(0, 2, 1)[:, :, None, :], 0.0)
    return blocks.reshape(S5_WIDTH, S5_LANES)


def _block_diag_out(c):
    same = jnp.eye(S5_GROUPS, dtype=bool)[:, None, :, None]
    blocks = jnp.where(same, c.transpose(0, 2, 1)[:, :, None, :], 0.0)
    return blocks.reshape(S5_LANES, S5_WIDTH)


def _quad_blocks(m):
    same = jnp.eye(4, dtype=bool)[None, :, None, :, None]
    blocks = jnp.where(same, m.reshape(S5_QUADS, 4, S5_GROUP, 1, S5_STATE), 0.0)
    return blocks.reshape(S5_QUADS, 4 * S5_GROUP, QUAD_LANES)


def _trunk(x, conv_dn, s_dn, s5_re, s5_im, conv_ffn, w, prompt):
    b, l, _ = x.shape
    n = b * l
    c = CHUNK if l % CHUNK == 0 else l
    qkv, z, u, ba = _in_proj(x.reshape(n, D_MODEL), w['n1'], w['w_in'], min(n, 512))
    qkv = qkv.reshape(b, l, QKV_WIDTH)
    z = z.reshape(b, l, DN_WIDTH)
    ba = ba.reshape(b, l, LANES)

    o_dn, s_dn_new = _deltanet(qkv, z, ba, _pad_rows_top(conv_dn, SUBLANES), s_dn,
                               w['dn_cw'], w['dn_gp'], w['dn_g'], nb=min(b, 4 if prompt else 8),
                               tt=min(l, 128), c=c)
    assert l >= SUBLANES, "sequence shorter than one row tile"
    conv_dn_new = qkv[:, l - (DN_CONV - 1):]

    x0re = s5_re.reshape(b, 1, S5_LANES)
    x0im = s5_im.reshape(b, 1, S5_LANES)
    if prompt:
        o_s5, fre, fim = _s5_chunked(u.reshape(b, l, S5_WIDTH), x0re, x0im, w['pq'], w['qq'], w['mq'],
                                     w['tabc'], w['dv'], w['glu_w'], w['glu_b'], w['s5_g'],
                                     rows=min(l // S5_SLOTS, 256))
    else:
        s5_args = (w['bcat'], w['tab'], w['cre_bd'], w['cim_bd'], w['dv'], w['glu_w'], w['glu_b'], w['s5_g'])
        o_s5, fre, fim = _s5(u.reshape(1, n, S5_WIDTH), x0re, x0im, *s5_args, tt=n, ls=l, carry=False)
        o_s5 = o_s5.reshape(b, l, S5_WIDTH)
    s5_re_new = fre.reshape(b, S5_GROUPS, S5_STATE)
    s5_im_new = fim.reshape(b, S5_GROUPS, S5_STATE)

    ffn_args = (w['w_out_a'], w['w_out_b'], w['n2'], w['w_up_g'], w['w_up_v'], w['fcw'], w['fcb'],
                w['w_down'], w['fg'])
    if prompt:
        prev = _pad_rows_top(conv_ffn, SUBLANES)
        y, tail = _ffn(x, o_dn, o_s5, prev, *ffn_args, tm=512, ls=512, carry=True)
        conv_ffn_new = tail[:, SUBLANES - (FFN_CONV - 1):]
    else:
        prev = jnp.pad(conv_ffn, ((0, 0), (0, l - (FFN_CONV - 1)), (0, 0))).reshape(1, n, D_FF)
        y, tail = _ffn(x.reshape(1, n, D_MODEL), o_dn.reshape(1, n, DN_WIDTH), o_s5.reshape(1, n, S5_WIDTH),
                       prev, *ffn_args, tm=min(n, 128), ls=l, carry=False)
        y = y.reshape(b, l, D_MODEL)
        conv_ffn_new = tail.reshape(b, l, D_FF)[:, l - (FFN_CONV - 1):]
    return y, (conv_dn_new[None], s_dn_new[None], s5_re_new[None], s5_im_new[None], conv_ffn_new[None])


def kernel(x_prompt, x_sample, cache_dn_conv, state_dn, state_s5_re, state_s5_im, cache_ffn_conv, norm1_g, w_in, dn_conv_w, dn_A_log, dn_dt_bias, dn_norm_g, s5_A_re, s5_A_im, s5_log_dt, s5_B_re, s5_B_im, s5_C_re, s5_C_im, s5_D, s5_glu_w, s5_glu_b, s5_norm_g, w_out, norm2_g, w_up, ffn_conv_w, ffn_conv_b, w_down, final_norm_g):
    assert w_in.shape[0] == 1, "single-layer trunk"
    o1 = QKV_WIDTH
    o2 = o1 + DN_WIDTH
    o4 = o2 + 2 * DN_HEADS
    wi = w_in[0]
    w_in_r = jnp.concatenate(
        [wi[:, :o2], wi[:, o4:], wi[:, o2:o4], jnp.zeros((D_MODEL, LANES - 2 * DN_HEADS), wi.dtype)],
        axis=1).astype(BF16)
    lane_pad = lambda v: jnp.pad(v, (BA_LANE_G, LANES - BA_LANE_G - DN_HEADS))
    ldt_lanes = jnp.repeat(s5_log_dt[0], S5_STATE)
    bcat, tab, tabc = _s5_prep(s5_A_re[0].reshape(1, S5_LANES), s5_A_im[0].reshape(1, S5_LANES),
                               ldt_lanes.reshape(1, S5_LANES),
                               _block_diag_in(s5_B_re[0]), _block_diag_in(s5_B_im[0]))
    quad_row = lambda v: v.reshape(S5_QUADS, 1, QUAD_LANES)
    pq, qq, mq = _s5_prep_quads(quad_row(s5_A_re[0]), quad_row(s5_A_im[0]), quad_row(ldt_lanes),
                                _quad_blocks(s5_B_re[0].transpose(0, 2, 1)), _quad_blocks(s5_B_im[0].transpose(0, 2, 1)),
                                _quad_blocks(s5_C_re[0]), _quad_blocks(s5_C_im[0]))
    w = {
        'n1': norm1_g, 'w_in': w_in_r,
        'dn_cw': jnp.pad(dn_conv_w[0], ((0, SUBLANES - DN_CONV), (0, 0))),
        'dn_gp': jnp.pad(jnp.stack([lane_pad(dn_A_log[0]), lane_pad(dn_dt_bias[0])]), ((0, SUBLANES - 2), (0, 0))),
        'dn_g': dn_norm_g,
        'bcat': bcat, 'tab': tab, 'tabc': tabc, 'pq': pq, 'qq': qq, 'mq': mq,
        'cre_bd': _block_diag_out(s5_C_re[0]).astype(BF16), 'cim_bd': _block_diag_out(s5_C_im[0]).astype(BF16),
        'dv': s5_D, 'glu_w': s5_glu_w[0].astype(BF16), 'glu_b': s5_glu_b, 's5_g': s5_norm_g,
        'w_out_a': w_out[0, :DN_WIDTH].astype(BF16), 'w_out_b': w_out[0, DN_WIDTH:].astype(BF16),
        'n2': norm2_g, 'w_up_g': w_up[0, :, :D_FF].astype(BF16), 'w_up_v': w_up[0, :, D_FF:].astype(BF16),
        'fcw': jnp.pad(ffn_conv_w[0], ((0, SUBLANES - FFN_CONV), (0, 0))), 'fcb': ffn_conv_b,
        'w_down': w_down[0].astype(BF16), 'fg': final_norm_g.reshape(1, D_MODEL),
    }
    bp = x_prompt.shape[0]
    zeros = lambda *s: jnp.zeros(s, F32)
    y_p, st_p = _trunk(x_prompt, zeros(bp, DN_CONV - 1, QKV_WIDTH), zeros(bp, DN_HEADS, DN_HEAD_DIM, DN_HEAD_DIM),
                       zeros(bp, S5_GROUPS, S5_STATE), zeros(bp, S5_GROUPS, S5_STATE),
                       zeros(bp, FFN_CONV - 1, D_FF), w, prompt=True)
    y_s, st_s = _trunk(x_sample, cache_dn_conv[0], state_dn[0], state_s5_re[0], state_s5_im[0],
                       cache_ffn_conv[0], w, prompt=False)
    return (y_p, y_s) + st_p + st_s
```

```python
import functools

import jax
import jax.numpy as jnp
from jax import lax
from jax.experimental import pallas as pl
from jax.experimental.pallas import tpu as pltpu

F32 = jnp.float32
BF16 = jnp.bfloat16
EPS = 1e-6

D_MODEL = 1024
DN_HEADS = 4
DN_HEAD_DIM = 128
DN_WIDTH = DN_HEADS * DN_HEAD_DIM
DN_CONV = 4
QKV_WIDTH = 3 * DN_WIDTH
S5_WIDTH = D_MODEL - DN_WIDTH
S5_GROUP = 16
S5_GROUPS = S5_WIDTH // S5_GROUP
S5_STATE = 64
S5_LANES = S5_GROUPS * S5_STATE
S5_SLOTS = 4
S5_QUADS = S5_GROUPS // 4
QUAD_LANES = 4 * S5_STATE
D_FF = 2816
FFN_CONV = 3
CHUNK = 64

SUBLANES = 8
LANES = 128
BA_LANE_G = DN_HEADS
IN_PAD = QKV_WIDTH + DN_WIDTH + S5_WIDTH + LANES
MXU_DIM = 256
FF_CHUNK = 6 * MXU_DIM

VMEM_LIMIT = 56 * 1024 * 1024


def _dot(a, b):
    return jnp.dot(a, b, preferred_element_type=F32)


def _dot_nt(a, b):
    return lax.dot_general(a, b, (((1,), (1,)), ((), ())), preferred_element_type=F32)


def _dot_tn(a, b):
    return lax.dot_general(a, b, (((0,), (0,)), ((), ())), preferred_element_type=F32)


def _split_bf16(a):
    hi = a.astype(BF16)
    lo = (a - hi.astype(F32)).astype(BF16)
    return hi, lo


def _bmm(a, b):
    return lax.dot_general(a, b, (((2,), (1,)), ((0,), (0,))), preferred_element_type=F32)


def _bmm_nt(a, b):
    return lax.dot_general(a, b, (((2,), (2,)), ((0,), (0,))), preferred_element_type=F32)


def _bmm_nt3(a, b):
    ah, al = _split_bf16(a)
    bh, bl = _split_bf16(b)
    return _bmm_nt(ah, bh) + (_bmm_nt(al, bh) + _bmm_nt(ah, bl))


def _rms(x, g):
    return x * lax.rsqrt(jnp.mean(x * x, axis=-1, keepdims=True) + EPS) * g


def _const_spec(shape):
    nd = len(shape)
    return pl.BlockSpec(shape, lambda *_: (0,) * nd, pipeline_mode=pl.Buffered(1))


def _params(n_axes):
    return pltpu.CompilerParams(dimension_semantics=("arbitrary",) * n_axes,
                                vmem_limit_bytes=VMEM_LIMIT)


def _cmul(a, b):
    return a[0] * b[0] - a[1] * b[1], a[0] * b[1] + a[1] * b[0]


def _zoh(are, aim, ldt):
    dt = jnp.exp(ldt)
    mag = jnp.exp(are * dt)
    ang = aim * dt
    lr = mag * jnp.cos(ang)
    li = mag * jnp.sin(ang)
    den = are * are + aim * aim
    f_re = ((lr - 1.0) * are + li * aim) / den
    f_im = (li * are - (lr - 1.0) * aim) / den
    return (lr, li), (f_re, f_im)


def _write_scan_tables(tab_ref, step):
    pw = [step]
    for _ in range(SUBLANES - 1):
        pw.append(_cmul(pw[-1], step))
    shape = (SUBLANES, step[0].shape[-1])
    row = lax.broadcasted_iota(jnp.int32, shape, 0)
    zero = jnp.zeros(shape, F32)
    pre, pim = zero, zero
    for r in range(SUBLANES):
        pre = jnp.where(row == r, pw[r][0], pre)
        pim = jnp.where(row == r, pw[r][1], pim)
    tab_ref[0:8, :] = pre
    tab_ref[8:16, :] = pim
    for lvl, d in enumerate((1, 2, 4)):
        tab_ref[16 + 16 * lvl:24 + 16 * lvl, :] = jnp.where(row >= d, pw[d - 1][0], zero)
        tab_ref[24 + 16 * lvl:32 + 16 * lvl, :] = jnp.where(row >= d, pw[d - 1][1], zero)


def _s5prep_kernel(are_ref, aim_ref, ldt_ref, bre_ref, bim_ref, bcat_ref, tab_ref, tabc_ref):
    lam, (f_re, f_im) = _zoh(are_ref[...], aim_ref[...], ldt_ref[...])
    bre = bre_ref[...]
    bim = bim_ref[...]
    bcat_ref[:, :S5_LANES] = (f_re * bre - f_im * bim).astype(BF16)
    bcat_ref[:, S5_LANES:] = (f_re * bim + f_im * bre).astype(BF16)
    _write_scan_tables(tab_ref, lam)
    lam_c = lam
    for _ in range(S5_SLOTS - 1):
        lam_c = _cmul(lam_c, lam)
    _write_scan_tables(tabc_ref, lam_c)


def _s5_prep(are, aim, ldt, bre_bd, bim_bd):
    return pl.pallas_call(
        _s5prep_kernel,
        out_shape=(jax.ShapeDtypeStruct((S5_WIDTH, 2 * S5_LANES), BF16),
                   jax.ShapeDtypeStruct((64, S5_LANES), F32),
                   jax.ShapeDtypeStruct((64, S5_LANES), F32)),
        compiler_params=pltpu.CompilerParams(vmem_limit_bytes=VMEM_LIMIT),
        name="s5_prep",
    )(are, aim, ldt, bre_bd, bim_bd)


def _s5prepq_kernel(are_ref, aim_ref, ldt_ref, bre_ref, bim_ref, cre_ref, cim_ref, pq_ref, qq_ref, mq_ref):
    lam, f = _zoh(are_ref[...], aim_ref[...], ldt_ref[...])
    bb = _cmul(f, (bre_ref[...], bim_ref[...]))
    ct = (cre_ref[...], cim_ref[...])
    one = (jnp.ones_like(lam[0]), jnp.zeros_like(lam[0]))
    pw = [one]
    for _ in range(S5_SLOTS):
        pw.append(_cmul(pw[-1], lam))
    rows = S5_SLOTS * 4 * S5_GROUP
    blk = 4 * S5_GROUP
    for s in range(S5_SLOTS):
        p_re, p_im = _cmul(pw[S5_SLOTS - 1 - s], bb)
        pq_ref[:, s * blk:(s + 1) * blk, :QUAD_LANES] = p_re.astype(BF16)
        pq_ref[:, s * blk:(s + 1) * blk, QUAD_LANES:] = p_im.astype(BF16)
    cl = [_cmul(pw[e], ct) for e in range(S5_SLOTS + 1)]
    zero = jnp.zeros_like(cl[0][0])
    qt_re = jnp.concatenate([cl[t + 1][0] for t in range(S5_SLOTS)], axis=1)
    qt_im = jnp.concatenate([-cl[t + 1][1] for t in range(S5_SLOTS)], axis=1)
    for n in range(S5_QUADS):
        qq_ref[n, :QUAD_LANES, :] = qt_re[n].T.astype(BF16)
        qq_ref[n, QUAD_LANES:, :] = qt_im[n].T.astype(BF16)
    for s in range(S5_SLOTS):
        wide_re = jnp.concatenate([cl[t - s][0] if t >= s else zero for t in range(S5_SLOTS)], axis=1)
        wide_im = jnp.concatenate([cl[t - s][1] if t >= s else zero for t in range(S5_SLOTS)], axis=1)
        m = _bmm_nt3(bb[0], wide_re) - _bmm_nt3(bb[1], wide_im)
        mq_ref[:, s * blk:(s + 1) * blk, :] = m.astype(BF16)
    assert rows == QUAD_LANES


def _s5_prep_quads(are_q, aim_q, ldt_q, bre_q, bim_q, cre_q, cim_q):
    w = lambda r, c_: jax.ShapeDtypeStruct((S5_QUADS, r, c_), BF16)
    return pl.pallas_call(
        _s5prepq_kernel,
        out_shape=(w(QUAD_LANES, 2 * QUAD_LANES), w(2 * QUAD_LANES, QUAD_LANES), w(QUAD_LANES, QUAD_LANES)),
        compiler_params=pltpu.CompilerParams(vmem_limit_bytes=VMEM_LIMIT),
        name="s5_prep_quads",
    )(are_q, aim_q, ldt_q, bre_q, bim_q, cre_q, cim_q)


def _inproj_kernel(x_ref, g_ref, w_ref, qkv_ref, z_ref, u_ref, ba_ref):
    h = _rms(x_ref[...], g_ref[...]).astype(BF16)
    o1 = QKV_WIDTH
    o2 = o1 + DN_WIDTH
    o3 = o2 + S5_WIDTH
    qkv_ref[...] = _dot(h, w_ref[:, :o1])
    z_ref[...] = _dot(h, w_ref[:, o1:o2])
    u_ref[...] = _dot(h, w_ref[:, o2:o3])
    ba_ref[...] = _dot(h, w_ref[:, o3:])


def _in_proj(x2d, g, w, tm):
    n = x2d.shape[0]
    row = lambda w_: pl.BlockSpec((tm, w_), lambda i: (i, 0))
    return pl.pallas_call(
        _inproj_kernel,
        grid=(n // tm,),
        in_specs=[row(D_MODEL), _const_spec((1, D_MODEL)), _const_spec((D_MODEL, IN_PAD))],
        out_specs=(row(QKV_WIDTH), row(DN_WIDTH), row(S5_WIDTH), row(LANES)),
        out_shape=(jax.ShapeDtypeStruct((n, QKV_WIDTH), F32),
                   jax.ShapeDtypeStruct((n, DN_WIDTH), F32),
                   jax.ShapeDtypeStruct((n, S5_WIDTH), F32),
                   jax.ShapeDtypeStruct((n, LANES), F32)),
        compiler_params=_params(1),
        name="in_proj",
    )(x2d, g, w)


def _dn_kernel(*refs, nb, tt, c):
    _dn_body(pl.program_id(1) == 0, *refs, nb=nb, tt=tt, c=c)


def _dn_body(first, qkv_ref, z_ref, ba_ref, cache_ref, s0_ref, cw_ref, gp_ref, ng_ref,
             o_ref, state_ref, halo_s, *, nb, tt, c, side=()):
    nh = DN_HEADS
    dh = DN_HEAD_DIM
    nck = tt // c
    npb = nb * nck
    hc = nh * c

    @pl.when(first)
    def _():
        halo_s[...] = cache_ref[...]
        state_ref[...] = s0_ref[...]

    side = list(side)

    def side_work():
        if side:
            side.pop(0)()

    def stack(a):
        return jnp.concatenate([a[:, :, h * dh:(h + 1) * dh] for h in range(nh)], axis=1)

    def unstack(a):
        return jnp.concatenate([a[:, h * c:(h + 1) * c, :] for h in range(nh)], axis=2)

    off = SUBLANES - (DN_CONV - 1)
    conv_parts = []
    for lo in range(0, QKV_WIDTH, MXU_DIM):
        side_work()
        hi = lo + MXU_DIM
        cw = cw_ref[:, lo:hi]
        head = jnp.concatenate([halo_s[:, :, lo:hi], qkv_ref[:, 0:SUBLANES, lo:hi]], axis=1)
        acc_head = head[:, off:off + SUBLANES] * cw[0:1]
        acc_rest = qkv_ref[:, off:off + tt - SUBLANES, lo:hi] * cw[0:1]
        for j in range(1, DN_CONV):
            acc_head = acc_head + head[:, off + j:off + j + SUBLANES] * cw[j:j + 1]
            acc_rest = acc_rest + qkv_ref[:, off + j:off + j + tt - SUBLANES, lo:hi] * cw[j:j + 1]
        halo_s[:, :, lo:hi] = qkv_ref[:, tt - SUBLANES:, lo:hi]
        acc = jnp.concatenate([acc_head, acc_rest], axis=1)
        conv_parts.append((acc * jax.nn.sigmoid(acc)).reshape(npb, c, hi - lo))
    per = DN_WIDTH // MXU_DIM
    q, k, v = [stack(jnp.concatenate(conv_parts[i * per:(i + 1) * per], axis=2)) for i in range(3)]
    side_work()
    q = q * (lax.rsqrt(jnp.sum(q * q, axis=-1, keepdims=True) + EPS) * (dh ** -0.5))
    k = k * lax.rsqrt(jnp.sum(k * k, axis=-1, keepdims=True) + EPS)

    side_work()
    ba = ba_ref[...].reshape(nb * tt, LANES)
    beta = jax.nn.sigmoid(ba).reshape(npb, c, LANES)
    g2 = -jnp.exp(gp_ref[0:1, :]) * jax.nn.softplus(ba + gp_ref[1:2, :])
    row_in_chunk = lax.broadcasted_iota(jnp.int32, (nb * tt, LANES), 0) & (c - 1)
    d = 1
    while d < c:
        g2 = g2 + jnp.where(row_in_chunk >= d, pltpu.roll(g2, d, axis=0), 0.0)
        d *= 2
    g = g2.reshape(npb, c, LANES)

    g_cols = [g[:, :, BA_LANE_G + h:BA_LANE_G + h + 1] for h in range(nh)]
    b_cols = [beta[:, :, h:h + 1] for h in range(nh)]
    g_col = jnp.concatenate(g_cols, axis=1)
    b_col = jnp.concatenate(b_cols, axis=1)
    g_last = jnp.concatenate([jnp.broadcast_to(gc[:, c - 1:c, :], (npb, c, 1)) for gc in g_cols], axis=1)

    hp = min(nh, LANES // c)
    pieces = []
    for h0 in range(0, nh, hp):
        slab = jnp.concatenate(
            [(g2 if hh == 0 else pltpu.roll(g2, LANES - hh, axis=1)).reshape(npb, c, LANES)
             for hh in range(h0, h0 + hp)], axis=1)
        if hp * c < LANES:
            slab = jnp.concatenate([slab, jnp.zeros((npb, LANES - hp * c, LANES), F32)], axis=1)
        rows = [slab[p].T[BA_LANE_G:BA_LANE_G + 1, :hp * c] for p in range(npb)]
        pieces.append(jnp.stack(rows, axis=0))
    g_row = jnp.concatenate(pieces, axis=2)

    ri = lax.broadcasted_iota(jnp.int32, (c, hc), 0)
    lane = lax.broadcasted_iota(jnp.int32, (c, hc), 1)
    cj = lane & (c - 1)
    causal = (ri >= cj)[None]
    strict = (ri > cj)[None]

    def cat_from_cols(cols):
        out = jnp.broadcast_to(cols[nh - 1], (npb, c, hc))
        for h in range(nh - 2, -1, -1):
            out = jnp.where((lane < (h + 1) * c)[None], jnp.broadcast_to(cols[h], (npb, c, hc)), out)
        return out

    decay = jnp.exp(jnp.where(causal, cat_from_cols(g_cols) - g_row, -jnp.inf))

    side_work()
    kb = k.astype(BF16)
    br = lax.broadcasted_iota(jnp.int32, (hc, nh * dh), 0)
    bl = lax.broadcasted_iota(jnp.int32, (hc, nh * dh), 1)
    head_of_row = sum(jnp.where(br >= h * c, 1, 0) for h in range(1, nh))
    head_of_lane = sum(jnp.where(bl >= h * dh, 1, 0) for h in range(1, nh))
    k_bd = jnp.where((head_of_row == head_of_lane)[None], jnp.concatenate([kb] * nh, axis=2), 0.0)
    qk_lhs = jnp.concatenate([unstack(q), unstack(k)], axis=1).astype(BF16)
    qkk = _bmm_nt(qk_lhs, k_bd)
    qk = qkk[:, :c]
    kk = qkk[:, c:]

    sr = lax.broadcasted_iota(jnp.int32, (hc, hc), 0)
    sl = lax.broadcasted_iota(jnp.int32, (hc, hc), 1)
    shift = c.bit_length() - 1
    same_head = ((sr >> shift) == (sl >> shift))[None]

    def bd(m):
        return jnp.where(same_head, jnp.concatenate([m] * nh, axis=1), 0.0)

    lm = jnp.where(strict, cat_from_cols(b_cols) * kk * decay, 0.0)
    tinv = jnp.where((ri == cj)[None], 1.0, 0.0) - jnp.where(((ri >> 1) == (cj >> 1))[None], lm, 0.0)
    s = 2
    while s < c:
        sh = s.bit_length()
        lower_left = ((ri >> sh) == (cj >> sh)) & ((ri & s) != 0) & ((cj & s) == 0)
        side_work()
        a_off = jnp.where(lower_left[None], lm, 0.0).astype(BF16)
        xm = _bmm(tinv.astype(BF16), bd(a_off))
        tinv = tinv - _bmm(xm.astype(BF16), bd(tinv.astype(BF16)))
        s *= 2

    e_g = jnp.exp(g_col)
    rhs = jnp.concatenate([v * b_col, k * (b_col * e_g)], axis=2).astype(BF16)
    sol = _bmm(bd(tinv.astype(BF16)), rhs)
    sol_v = sol[:, :, :dh]
    sol_k = sol[:, :, dh:]
    w_qe = jnp.concatenate([sol_k, q * e_g], axis=2).astype(BF16)
    kd = (k * jnp.exp(g_last - g_col)).astype(BF16)
    qkd_bd = bd((qk * decay).astype(BF16))
    s_decay = jnp.exp(g_last)

    outs = []
    for b in range(nb):
        for ck in range(nck):
            p = b * nck + ck
            ws, qs = [], []
            states = [state_ref[b, h] for h in range(nh)]
            for h in range(nh):
                sb = states[h].astype(BF16)
                rows = w_qe[p, h * c:(h + 1) * c, :]
                ws.append(_dot(rows[:, :dh], sb))
                qs.append(_dot(rows[:, dh:], sb))
            u = sol_v[p] - jnp.concatenate(ws, axis=0)
            ub = u.astype(BF16)
            outs.append(jnp.concatenate(qs, axis=0) + _dot(qkd_bd[p], ub))
            for h in range(nh):
                r0 = h * c
                state_ref[b, h] = (states[h] * s_decay[p, r0:r0 + 1, :]
                                   + _dot_tn(kd[p, r0:r0 + c, :], ub[r0:r0 + c, :]))

    o = jnp.stack(outs, axis=0)
    zs = stack(z_ref[...].reshape(npb, c, DN_WIDTH))
    o = _rms(o, ng_ref[...]) * (zs * jax.nn.sigmoid(zs))
    o_ref[...] = unstack(o).reshape(nb, tt, DN_WIDTH).astype(BF16)
    while side:
        side_work()


def _deltanet(qkv, z, ba, cache8, s0, cw, gp, ng, nb, tt, c):
    b, l, _ = qkv.shape
    seq = lambda w_: pl.BlockSpec((nb, tt, w_), lambda i, j: (i, j, 0))
    per_b = lambda *s: pl.BlockSpec((nb,) + s, lambda i, j: (i,) + (0,) * len(s))
    return pl.pallas_call(
        functools.partial(_dn_kernel, nb=nb, tt=tt, c=c),
        grid=(b // nb, l // tt),
        in_specs=[seq(QKV_WIDTH), seq(DN_WIDTH), seq(LANES), per_b(SUBLANES, QKV_WIDTH),
                  per_b(DN_HEADS, DN_HEAD_DIM, DN_HEAD_DIM),
                  _const_spec((SUBLANES, QKV_WIDTH)), _const_spec((SUBLANES, LANES)),
                  _const_spec((1, DN_HEAD_DIM))],
        out_specs=(seq(DN_WIDTH), per_b(DN_HEADS, DN_HEAD_DIM, DN_HEAD_DIM)),
        out_shape=(jax.ShapeDtypeStruct((b, l, DN_WIDTH), BF16),
                   jax.ShapeDtypeStruct((b, DN_HEADS, DN_HEAD_DIM, DN_HEAD_DIM), F32)),
        scratch_shapes=[pltpu.VMEM((nb, SUBLANES, QKV_WIDTH), F32)],
        compiler_params=_params(2),
        name="deltanet",
    )(qkv, z, ba, cache8, s0, cw, gp, ng)


def _mixer_kernel(x_ref, g_ref, w_ref, cache_ref, s0_ref, cw_ref, gp_ref, ng_ref,
                  u_ref, o_ref, state_ref, tail_ref, p_cur, p_nxt, halo_s, *, nb, tt, c):
    i = pl.program_id(0)
    o1 = QKV_WIDTH
    o2 = o1 + DN_WIDTH
    o3 = o2 + S5_WIDTH

    @pl.when(i == 0)
    def _():
        p_cur[...] = jnp.zeros((nb, tt, IN_PAD), F32)

    h = _rms(x_ref[...].reshape(nb * tt, D_MODEL), g_ref[...]).astype(BF16)

    def project(lo, hi):
        def run():
            res = _dot(h, w_ref[:, lo:hi]).reshape(nb, tt, hi - lo)
            if o2 <= lo < o3:
                u_ref[:, :, lo - o2:hi - o2] = res
            else:
                p_nxt[:, :, lo:hi] = res
        return run

    side = [project(lo, min(lo + MXU_DIM, IN_PAD)) for lo in range(0, IN_PAD, MXU_DIM)]
    _dn_body(i <= 1, p_cur.at[:, :, 0:o1], p_cur.at[:, :, o1:o2], p_cur.at[:, :, o3:IN_PAD],
             cache_ref, s0_ref, cw_ref, gp_ref, ng_ref, o_ref, state_ref, halo_s, nb=nb, tt=tt, c=c, side=side)
    tail_ref[...] = halo_s[...]
    for lo, hi in ((0, o1), (o1, o2), (o3, IN_PAD)):
        p_cur[:, :, lo:hi] = p_nxt[:, :, lo:hi]


def _mixer(x, g, w, cache8, s0, cw, gp, ng, tt, c):
    nb, l, _ = x.shape
    nt = l // tt
    cur = lambda w_: pl.BlockSpec((nb, tt, w_), lambda i: (0, jnp.minimum(i, nt - 1), 0))
    prev = lambda w_: pl.BlockSpec((nb, tt, w_), lambda i: (0, jnp.maximum(i - 1, 0), 0))
    return pl.pallas_call(
        functools.partial(_mixer_kernel, nb=nb, tt=tt, c=c),
        grid=(nt + 1,),
        in_specs=[cur(D_MODEL), _const_spec((1, D_MODEL)), _const_spec((D_MODEL, IN_PAD)),
                  _const_spec((nb, SUBLANES, QKV_WIDTH)), _const_spec((nb, DN_HEADS, DN_HEAD_DIM, DN_HEAD_DIM)),
                  _const_spec((SUBLANES, QKV_WIDTH)), _const_spec((SUBLANES, LANES)),
                  _const_spec((1, DN_HEAD_DIM))],
        out_specs=(cur(S5_WIDTH), prev(DN_WIDTH),
                   pl.BlockSpec((nb, DN_HEADS, DN_HEAD_DIM, DN_HEAD_DIM), lambda i: (0, 0, 0, 0)),
                   pl.BlockSpec((nb, SUBLANES, QKV_WIDTH), lambda i: (0, 0, 0))),
        out_shape=(jax.ShapeDtypeStruct((nb, l, S5_WIDTH), F32),
                   jax.ShapeDtypeStruct((nb, l, DN_WIDTH), BF16),
                   jax.ShapeDtypeStruct((nb, DN_HEADS, DN_HEAD_DIM, DN_HEAD_DIM), F32),
                   jax.ShapeDtypeStruct((nb, SUBLANES, QKV_WIDTH), F32)),
        scratch_shapes=[pltpu.VMEM((nb, tt, IN_PAD), F32), pltpu.VMEM((nb, tt, IN_PAD), F32),
                        pltpu.VMEM((nb, SUBLANES, QKV_WIDTH), F32)],
        compiler_params=_params(1),
        name="mixer",
    )(x, g, w, cache8, s0, cw, gp, ng)


def _s5_kernel(u_ref, x0re_ref, x0im_ref, bcat_ref, tab_ref, cre_ref, cim_ref, dv_ref, gw_ref, gb_ref, ng_ref,
               o_ref, fre_ref, fim_ref, xr_s, xi_s, car_s, *, tt, ls, carry):
    t = pl.program_id(1)
    u = u_ref[0]
    ub = u.astype(BF16)
    xr_s[...] = _dot(ub, bcat_ref[:, :S5_LANES])
    xi_s[...] = _dot(ub, bcat_ref[:, S5_LANES:])

    if carry:
        @pl.when(t == 0)
        def _():
            car_s[0:1, :] = x0re_ref[0]
            car_s[1:2, :] = x0im_ref[0]

    def seq_body(sq, _):
        if carry:
            c0 = (car_s[0:1, :], car_s[1:2, :])
        else:
            c0 = (x0re_ref[sq], x0im_ref[sq])

        def block_body(rb, cin):
            c_re, c_im = cin
            r0 = pl.multiple_of(sq * ls + rb * SUBLANES, SUBLANES)
            xr = xr_s[pl.ds(r0, SUBLANES), :]
            xi = xi_s[pl.ds(r0, SUBLANES), :]
            for lvl, d in enumerate((1, 2, 4)):
                m_re = tab_ref[16 + 16 * lvl:24 + 16 * lvl, :]
                m_im = tab_ref[24 + 16 * lvl:32 + 16 * lvl, :]
                sr = pltpu.roll(xr, d, axis=0)
                si = pltpu.roll(xi, d, axis=0)
                xr, xi = xr + (m_re * sr - m_im * si), xi + (m_re * si + m_im * sr)
            p_re = tab_ref[0:8, :]
            p_im = tab_ref[8:16, :]
            xr, xi = xr + (p_re * c_re - p_im * c_im), xi + (p_re * c_im + p_im * c_re)
            xr_s[pl.ds(r0, SUBLANES), :] = xr
            xi_s[pl.ds(r0, SUBLANES), :] = xi
            return xr[SUBLANES - 1:, :], xi[SUBLANES - 1:, :]

        c_re, c_im = lax.fori_loop(0, ls // SUBLANES, block_body, c0)
        if carry:
            car_s[0:1, :] = c_re
            car_s[1:2, :] = c_im
            fre_ref[0] = c_re
            fim_ref[0] = c_im
        else:
            fre_ref[sq] = c_re
            fim_ref[sq] = c_im
        return 0

    lax.fori_loop(0, tt // ls, seq_body, 0)

    y = _dot(xr_s[...].astype(BF16), cre_ref[...]) - _dot(xi_s[...].astype(BF16), cim_ref[...])
    y = jax.nn.gelu(y + dv_ref[...] * u)
    gl = _dot(y.astype(BF16), gw_ref[...]) + gb_ref[...]
    o = gl[:, :S5_WIDTH] * jax.nn.sigmoid(gl[:, S5_WIDTH:])
    o_ref[0] = _rms(o, ng_ref[...]).astype(BF16)


def _s5(u, x0re, x0im, bcat, tab, cre_bd, cim_bd, dv, gw, gb, ng, tt, ls, carry):
    b, l, _ = u.shape
    nb = x0re.shape[0] // b
    seq = lambda w_: pl.BlockSpec((1, tt, w_), lambda i, j: (i, j, 0))
    st = pl.BlockSpec((nb, 1, S5_LANES), lambda i, j: (i, 0, 0))
    return pl.pallas_call(
        functools.partial(_s5_kernel, tt=tt, ls=ls, carry=carry),
        grid=(b, l // tt),
        in_specs=[seq(S5_WIDTH), st, st,
                  _const_spec((S5_WIDTH, 2 * S5_LANES)), _const_spec((64, S5_LANES)),
                  _const_spec((S5_LANES, S5_WIDTH)), _const_spec((S5_LANES, S5_WIDTH)),
                  _const_spec((1, S5_WIDTH)), _const_spec((S5_WIDTH, 2 * S5_WIDTH)),
                  _const_spec((1, 2 * S5_WIDTH)), _const_spec((1, S5_WIDTH))],
        out_specs=(seq(S5_WIDTH), st, st),
        out_shape=(jax.ShapeDtypeStruct((b, l, S5_WIDTH), BF16),
                   jax.ShapeDtypeStruct(x0re.shape, F32),
                   jax.ShapeDtypeStruct(x0re.shape, F32)),
        scratch_shapes=[pltpu.VMEM((tt, S5_LANES), F32), pltpu.VMEM((tt, S5_LANES), F32),
                        pltpu.VMEM((SUBLANES, S5_LANES), F32)],
        compiler_params=_params(2),
        name="s5",
    )(u, x0re, x0im, bcat, tab, cre_bd, cim_bd, dv, gw, gb, ng)


def _s5c_kernel(u_ref, x0re_ref, x0im_ref, pq_ref, qq_ref, mq_ref, tab_ref, dv_ref, gw_ref, gb_ref, ng_ref,
                o_ref, fre_ref, fim_ref, xr_s, xi_s, car_s, u_s, o_s, *, rows):
    t = pl.program_id(1)
    ncol = S5_WIDTH // LANES
    for k in range(ncol):
        u_s[k] = u_ref[0, :, k * LANES:(k + 1) * LANES]
    us = [jnp.concatenate([u_s[k, pl.ds(s, rows, stride=S5_SLOTS), :] for k in range(ncol)], axis=1)
          for s in range(S5_SLOTS)]
    ubs = [a.astype(BF16) for a in us]
    blk = 4 * S5_GROUP
    ql = QUAD_LANES

    @pl.when(t == 0)
    def _():
        car_s[0:1, :] = x0re_ref[0]
        car_s[1:2, :] = x0im_ref[0]

    uq = [jnp.concatenate([ubs[s][:, n * blk:(n + 1) * blk] for s in range(S5_SLOTS)], axis=1)
          for n in range(S5_QUADS)]
    for n in range(S5_QUADS):
        inc = _dot(uq[n], pq_ref[n])
        xr_s[:, n * ql:(n + 1) * ql] = inc[:, :ql]
        xi_s[:, n * ql:(n + 1) * ql] = inc[:, ql:]

    row = lax.broadcasted_iota(jnp.int32, (SUBLANES, S5_LANES), 0)

    def block_body(rb, cin):
        c_re, c_im = cin
        r0 = pl.multiple_of(rb * SUBLANES, SUBLANES)
        xr = xr_s[pl.ds(r0, SUBLANES), :]
        xi = xi_s[pl.ds(r0, SUBLANES), :]
        for lvl, d in enumerate((1, 2, 4)):
            m_re = tab_ref[16 + 16 * lvl:24 + 16 * lvl, :]
            m_im = tab_ref[24 + 16 * lvl:32 + 16 * lvl, :]
            sr = pltpu.roll(xr, d, axis=0)
            si = pltpu.roll(xi, d, axis=0)
            xr, xi = xr + (m_re * sr - m_im * si), xi + (m_re * si + m_im * sr)
        p_re = tab_ref[0:8, :]
        p_im = tab_ref[8:16, :]
        xr, xi = xr + (p_re * c_re - p_im * c_im), xi + (p_re * c_im + p_im * c_re)
        xr_s[pl.ds(r0, SUBLANES), :] = jnp.where(row == 0, c_re, pltpu.roll(xr, 1, axis=0))
        xi_s[pl.ds(r0, SUBLANES), :] = jnp.where(row == 0, c_im, pltpu.roll(xi, 1, axis=0))
        return xr[SUBLANES - 1:, :], xi[SUBLANES - 1:, :]

    c_re, c_im = lax.fori_loop(0, rows // SUBLANES, block_body, (car_s[0:1, :], car_s[1:2, :]))
    car_s[0:1, :] = c_re
    car_s[1:2, :] = c_im
    fre_ref[0] = c_re
    fim_ref[0] = c_im

    ys = []
    for n in range(S5_QUADS):
        xs = jnp.concatenate([xr_s[:, n * ql:(n + 1) * ql], xi_s[:, n * ql:(n + 1) * ql]], axis=1).astype(BF16)
        ys.append(_dot(uq[n], mq_ref[n]) + _dot(xs, qq_ref[n]))
    for s in range(S5_SLOTS):
        y = jnp.concatenate([ys[n][:, s * blk:(s + 1) * blk] for n in range(S5_QUADS)], axis=1)
        y = jax.nn.gelu(y + dv_ref[...] * us[s]).astype(BF16)
        gl = _dot(y, gw_ref[...]) + gb_ref[...]
        o = gl[:, :S5_WIDTH] * jax.nn.sigmoid(gl[:, S5_WIDTH:])
        o = _rms(o, ng_ref[...])
        for k in range(ncol):
            o_s[k, pl.ds(s, rows, stride=S5_SLOTS), :] = o[:, k * LANES:(k + 1) * LANES]
    o_ref[0] = jnp.concatenate([o_s[k] for k in range(ncol)], axis=1).astype(BF16)


def _s5_chunked(u, x0re, x0im, pq, qq, mq, tabc, dv, gw, gb, ng, rows):
    b, l, _ = u.shape
    tt = rows * S5_SLOTS
    seq = pl.BlockSpec((1, tt, S5_WIDTH), lambda i, j: (i, j, 0))
    st = pl.BlockSpec((1, 1, S5_LANES), lambda i, j: (i, 0, 0))
    return pl.pallas_call(
        functools.partial(_s5c_kernel, rows=rows),
        grid=(b, l // tt),
        in_specs=[seq, st, st,
                  _const_spec((S5_QUADS, QUAD_LANES, 2 * QUAD_LANES)),
                  _const_spec((S5_QUADS, 2 * QUAD_LANES, QUAD_LANES)),
                  _const_spec((S5_QUADS, QUAD_LANES, QUAD_LANES)), _const_spec((64, S5_LANES)),
                  _const_spec((1, S5_WIDTH)), _const_spec((S5_WIDTH, 2 * S5_WIDTH)),
                  _const_spec((1, 2 * S5_WIDTH)), _const_spec((1, S5_WIDTH))],
        out_specs=(seq, st, st),
        out_shape=(jax.ShapeDtypeStruct((b, l, S5_WIDTH), BF16),
                   jax.ShapeDtypeStruct(x0re.shape, F32),
                   jax.ShapeDtypeStruct(x0re.shape, F32)),
        scratch_shapes=[pltpu.VMEM((rows, S5_LANES), F32), pltpu.VMEM((rows, S5_LANES), F32),
                        pltpu.VMEM((SUBLANES, S5_LANES), F32),
                        pltpu.VMEM((S5_WIDTH // LANES, tt, LANES), F32),
                        pltpu.VMEM((S5_WIDTH // LANES, tt, LANES), F32)],
        compiler_params=_params(2),
        name="s5_chunked",
    )(u, x0re, x0im, pq, qq, mq, tabc, dv, gw, gb, ng)


def _ffn_kernel(x_ref, odn_ref, os5_ref, prev_ref, woa_ref, wob_ref, n2_ref, wug_ref, wuv_ref,
                fcw_ref, fcb_ref, wd_ref, fg_ref, y_ref, tail_ref, halo_s, *, tm, ls, carry):
    t = pl.program_id(1)
    x1 = x_ref[0] + _dot(odn_ref[0], woa_ref[...]) + _dot(os5_ref[0], wob_ref[...])
    h2 = _rms(x1, n2_ref[...]).astype(BF16)
    if carry:
        @pl.when(t == 0)
        def _():
            halo_s[...] = prev_ref[0]

    down = None
    for lo in range(0, D_FF, FF_CHUNK):
        hi = min(lo + FF_CHUNK, D_FF)
        gate = _dot(h2, wug_ref[:, lo:hi])
        val = _dot(h2, wuv_ref[:, lo:hi])
        if carry:
            xx = jnp.concatenate([halo_s[:, lo:hi], gate], axis=0)
            g2 = xx[SUBLANES - 2:SUBLANES - 2 + tm]
            g1 = xx[SUBLANES - 1:SUBLANES - 1 + tm]
            halo_s[:, lo:hi] = gate[tm - SUBLANES:, :]
            tail_ref[0, :, lo:hi] = gate[tm - SUBLANES:, :]
        else:
            r = lax.broadcasted_iota(jnp.int32, (tm, hi - lo), 0) & (ls - 1)
            prev = prev_ref[0, :, lo:hi]
            g1 = jnp.where(r < 1, pltpu.roll(prev, tm - 1, axis=0), pltpu.roll(gate, 1, axis=0))
            g2 = jnp.where(r < 2, prev, pltpu.roll(gate, 2, axis=0))
            tail_ref[0, :, lo:hi] = gate
        cw = fcw_ref[:, lo:hi]
        pre = g2 * cw[0:1] + g1 * cw[1:2] + gate * cw[2:3] + fcb_ref[:, lo:hi]
        act = pre * jax.nn.sigmoid(pre) * val
        part = _dot(act.astype(BF16), wd_ref[lo:hi, :])
        down = part if down is None else down + part
    y_ref[0] = _rms(x1 + down, fg_ref[...])


def _ffn(x, odn, os5, prev, woa, wob, n2, wug, wuv, fcw, fcb, wd, fg, tm, ls, carry):
    b, l, _ = x.shape
    seq = lambda w_: pl.BlockSpec((1, tm, w_), lambda i, j: (i, j, 0))
    if carry:
        prev_spec = pl.BlockSpec((1, SUBLANES, D_FF), lambda i, j: (i, 0, 0))
        tail_spec = pl.BlockSpec((1, SUBLANES, D_FF), lambda i, j: (i, 0, 0))
        tail_shape = jax.ShapeDtypeStruct((b, SUBLANES, D_FF), F32)
    else:
        prev_spec = seq(D_FF)
        tail_spec = seq(D_FF)
        tail_shape = jax.ShapeDtypeStruct((b, l, D_FF), F32)
    return pl.pallas_call(
        functools.partial(_ffn_kernel, tm=tm, ls=ls, carry=carry),
        grid=(b, l // tm),
        in_specs=[seq(D_MODEL), seq(DN_WIDTH), seq(S5_WIDTH), prev_spec,
                  _const_spec((DN_WIDTH, D_MODEL)), _const_spec((S5_WIDTH, D_MODEL)),
                  _const_spec((1, D_MODEL)), _const_spec((D_MODEL, D_FF)), _const_spec((D_MODEL, D_FF)),
                  _const_spec((SUBLANES, D_FF)), _const_spec((1, D_FF)), _const_spec((D_FF, D_MODEL)),
                  _const_spec((1, D_MODEL))],
        out_specs=(seq(D_MODEL), tail_spec),
        out_shape=(jax.ShapeDtypeStruct((b, l, D_MODEL), F32), tail_shape),
        scratch_shapes=[pltpu.VMEM((SUBLANES, D_FF), F32)],
        compiler_params=_params(2),
        name="ffn",
    )(x, odn, os5, prev, woa, wob, n2, wug, wuv, fcw, fcb, wd, fg)


def _pad_rows_top(a, rows):
    return jnp.pad(a, ((0, 0), (rows - a.shape[1], 0), (0, 0)))


def _block_diag_in(b):
    same = jnp.eye(S5_GROUPS, dtype=bool)[:, None, :, None]
    blocks = jnp.where(same, b.transpose(0, 2, 1)[:, :, None, :], 0.0)
    return blocks.reshape(S5_WIDTH, S5_LANES)


def _block_diag_out(c):
    same = jnp.eye(S5_GROUPS, dtype=bool)[:, None, :, None]
    blocks = jnp.where(same, c.transpose(0, 2, 1)[:, :, None, :], 0.0)
    return blocks.reshape(S5_LANES, S5_WIDTH)


def _quad_blocks(m):
    same = jnp.eye(4, dtype=bool)[None, :, None, :, None]
    blocks = jnp.where(same, m.reshape(S5_QUADS, 4, S5_GROUP, 1, S5_STATE), 0.0)
    return blocks.reshape(S5_QUADS, 4 * S5_GROUP, QUAD_LANES)


def _trunk(x, conv_dn, s_dn, s5_re, s5_im, conv_ffn, w, prompt):
    b, l, _ = x.shape
    n = b * l
    c = CHUNK if l % CHUNK == 0 else l
    assert l >= 2 * SUBLANES, "sequence shorter than two row tiles"
    cache8 = _pad_rows_top(conv_dn, SUBLANES)
    if prompt:
        u, o_dn, s_dn_new, tail = _mixer(x, w['n1'], w['w_in'], cache8, s_dn, w['dn_cw'], w['dn_gp'], w['dn_g'],
                                         tt=min(l, 128), c=c)
        conv_dn_new = tail[:, SUBLANES - (DN_CONV - 1):]
    else:
        qkv, z, u, ba = _in_proj(x.reshape(n, D_MODEL), w['n1'], w['w_in'], min(n, 512))
        qkv = qkv.reshape(b, l, QKV_WIDTH)
        o_dn, s_dn_new = _deltanet(qkv, z.reshape(b, l, DN_WIDTH), ba.reshape(b, l, LANES), cache8, s_dn,
                                   w['dn_cw'], w['dn_gp'], w['dn_g'], nb=min(b, 8), tt=l, c=c)
        conv_dn_new = qkv[:, l - (DN_CONV - 1):]

    x0re = s5_re.reshape(b, 1, S5_LANES)
    x0im = s5_im.reshape(b, 1, S5_LANES)
    if prompt:
        o_s5, fre, fim = _s5_chunked(u.reshape(b, l, S5_WIDTH), x0re, x0im, w['pq'], w['qq'], w['mq'],
                                     w['tabc'], w['dv'], w['glu_w'], w['glu_b'], w['s5_g'],
                                     rows=min(l // S5_SLOTS, 256))
    else:
        s5_args = (w['bcat'], w['tab'], w['cre_bd'], w['cim_bd'], w['dv'], w['glu_w'], w['glu_b'], w['s5_g'])
        o_s5, fre, fim = _s5(u.reshape(1, n, S5_WIDTH), x0re, x0im, *s5_args, tt=n, ls=l, carry=False)
        o_s5 = o_s5.reshape(b, l, S5_WIDTH)
    s5_re_new = fre.reshape(b, S5_GROUPS, S5_STATE)
    s5_im_new = fim.reshape(b, S5_GROUPS, S5_STATE)

    ffn_args = (w['w_out_a'], w['w_out_b'], w['n2'], w['w_up_g'], w['w_up_v'], w['fcw'], w['fcb'],
                w['w_down'], w['fg'])
    if prompt:
        prev = _pad_rows_top(conv_ffn, SUBLANES)
        y, tail = _ffn(x, o_dn, o_s5, prev, *ffn_args, tm=512, ls=512, carry=True)
        conv_ffn_new = tail[:, SUBLANES - (FFN_CONV - 1):]
    else:
        prev = jnp.pad(conv_ffn, ((0, 0), (0, l - (FFN_CONV - 1)), (0, 0))).reshape(1, n, D_FF)
        y, tail = _ffn(x.reshape(1, n, D_MODEL), o_dn.reshape(1, n, DN_WIDTH), o_s5.reshape(1, n, S5_WIDTH),
                       prev, *ffn_args, tm=min(n, 128), ls=l, carry=False)
        y = y.reshape(b, l, D_MODEL)
        conv_ffn_new = tail.reshape(b, l, D_FF)[:, l - (FFN_CONV - 1):]
    return y, (conv_dn_new[None], s_dn_new[None], s5_re_new[None], s5_im_new[None], conv_ffn_new[None])


def kernel(x_prompt, x_sample, cache_dn_conv, state_dn, state_s5_re, state_s5_im, cache_ffn_conv, norm1_g, w_in, dn_conv_w, dn_A_log, dn_dt_bias, dn_norm_g, s5_A_re, s5_A_im, s5_log_dt, s5_B_re, s5_B_im, s5_C_re, s5_C_im, s5_D, s5_glu_w, s5_glu_b, s5_norm_g, w_out, norm2_g, w_up, ffn_conv_w, ffn_conv_b, w_down, final_norm_g):
    assert w_in.shape[0] == 1, "single-layer trunk"
    o1 = QKV_WIDTH
    o2 = o1 + DN_WIDTH
    o4 = o2 + 2 * DN_HEADS
    wi = w_in[0]
    w_in_r = jnp.concatenate(
        [wi[:, :o2], wi[:, o4:], wi[:, o2:o4], jnp.zeros((D_MODEL, LANES - 2 * DN_HEADS), wi.dtype)],
        axis=1).astype(BF16)
    lane_pad = lambda v: jnp.pad(v, (BA_LANE_G, LANES - BA_LANE_G - DN_HEADS))
    ldt_lanes = jnp.repeat(s5_log_dt[0], S5_STATE)
    bcat, tab, tabc = _s5_prep(s5_A_re[0].reshape(1, S5_LANES), s5_A_im[0].reshape(1, S5_LANES),
                               ldt_lanes.reshape(1, S5_LANES),
                               _block_diag_in(s5_B_re[0]), _block_diag_in(s5_B_im[0]))
    quad_row = lambda v: v.reshape(S5_QUADS, 1, QUAD_LANES)
    pq, qq, mq = _s5_prep_quads(quad_row(s5_A_re[0]), quad_row(s5_A_im[0]), quad_row(ldt_lanes),
                                _quad_blocks(s5_B_re[0].transpose(0, 2, 1)), _quad_blocks(s5_B_im[0].transpose(0, 2, 1)),
                                _quad_blocks(s5_C_re[0]), _quad_blocks(s5_C_im[0]))
    w = {
        'n1': norm1_g, 'w_in': w_in_r,
        'dn_cw': jnp.pad(dn_conv_w[0], ((0, SUBLANES - DN_CONV), (0, 0))),
        'dn_gp': jnp.pad(jnp.stack([lane_pad(dn_A_log[0]), lane_pad(dn_dt_bias[0])]), ((0, SUBLANES - 2), (0, 0))),
        'dn_g': dn_norm_g,
        'bcat': bcat, 'tab': tab, 'tabc': tabc, 'pq': pq, 'qq': qq, 'mq': mq,
        'cre_bd': _block_diag_out(s5_C_re[0]).astype(BF16), 'cim_bd': _block_diag_out(s5_C_im[0]).astype(BF16),
        'dv': s5_D, 'glu_w': s5_glu_w[0].astype(BF16), 'glu_b': s5_glu_b, 's5_g': s5_norm_g,
        'w_out_a': w_out[0, :DN_WIDTH].astype(BF16), 'w_out_b': w_out[0, DN_WIDTH:].astype(BF16),
        'n2': norm2_g, 'w_up_g': w_up[0, :, :D_FF].astype(BF16), 'w_up_v': w_up[0, :, D_FF:].astype(BF16),
        'fcw': jnp.pad(ffn_conv_w[0], ((0, SUBLANES - FFN_CONV), (0, 0))), 'fcb': ffn_conv_b,
        'w_down': w_down[0].astype(BF16), 'fg': final_norm_g.reshape(1, D_MODEL),
    }
    bp = x_prompt.shape[0]
    zeros = lambda *s: jnp.zeros(s, F32)
    y_p, st_p = _trunk(x_prompt, zeros(bp, DN_CONV - 1, QKV_WIDTH), zeros(bp, DN_HEADS, DN_HEAD_DIM, DN_HEAD_DIM),
                       zeros(bp, S5_GROUPS, S5_STATE), zeros(bp, S5_GROUPS, S5_STATE),
                       zeros(bp, FFN_CONV - 1, D_FF), w, prompt=True)
    y_s, st_s = _trunk(x_sample, cache_dn_conv[0], state_dn[0], state_s5_re[0], state_s5_im[0],
                       cache_ffn_conv[0], w, prompt=False)
    return (y_p, y_s) + st_p + st_s
```

```python
import functools

import jax
import jax.numpy as jnp
from jax import lax
from jax.experimental import pallas as pl
from jax.experimental.pallas import tpu as pltpu

F32 = jnp.float32
BF16 = jnp.bfloat16
EPS = 1e-6

D_MODEL = 1024
DN_HEADS = 4
DN_HEAD_DIM = 128
DN_WIDTH = DN_HEADS * DN_HEAD_DIM
DN_CONV = 4
QKV_WIDTH = 3 * DN_WIDTH
S5_WIDTH = D_MODEL - DN_WIDTH
S5_GROUP = 16
S5_GROUPS = S5_WIDTH // S5_GROUP
S5_STATE = 64
S5_LANES = S5_GROUPS * S5_STATE
S5_SLOTS = 4
S5_QUADS = S5_GROUPS // 4
QUAD_LANES = 4 * S5_STATE
D_FF = 2816
FFN_CONV = 3
CHUNK = 64

SUBLANES = 8
LANES = 128
BA_LANE_G = DN_HEADS
IN_PAD = QKV_WIDTH + DN_WIDTH + S5_WIDTH + LANES
MXU_DIM = 256
FF_CHUNK = 6 * MXU_DIM

VMEM_LIMIT = 56 * 1024 * 1024


def _dot(a, b):
    return jnp.dot(a, b, preferred_element_type=F32)


def _dot_tn(a, b):
    return lax.dot_general(a, b, (((0,), (0,)), ((), ())), preferred_element_type=F32)


def _split_bf16(a):
    hi = a.astype(BF16)
    lo = (a - hi.astype(F32)).astype(BF16)
    return hi, lo


def _bmm(a, b):
    return lax.dot_general(a, b, (((2,), (1,)), ((0,), (0,))), preferred_element_type=F32)


def _bmm_nt(a, b):
    return lax.dot_general(a, b, (((2,), (2,)), ((0,), (0,))), preferred_element_type=F32)


def _bmm_nt3(a, b):
    ah, al = _split_bf16(a)
    bh, bl = _split_bf16(b)
    return _bmm_nt(ah, bh) + (_bmm_nt(al, bh) + _bmm_nt(ah, bl))


def _rms(x, g):
    return x * lax.rsqrt(jnp.mean(x * x, axis=-1, keepdims=True) + EPS) * g


def _const_spec(shape):
    nd = len(shape)
    return pl.BlockSpec(shape, lambda *_: (0,) * nd, pipeline_mode=pl.Buffered(1))


def _params(n_axes):
    return pltpu.CompilerParams(dimension_semantics=("arbitrary",) * n_axes,
                                vmem_limit_bytes=VMEM_LIMIT)


def _cmul(a, b):
    return a[0] * b[0] - a[1] * b[1], a[0] * b[1] + a[1] * b[0]


def _zoh(are, aim, ldt):
    dt = jnp.exp(ldt)
    mag = jnp.exp(are * dt)
    ang = aim * dt
    lr = mag * jnp.cos(ang)
    li = mag * jnp.sin(ang)
    den = are * are + aim * aim
    f_re = ((lr - 1.0) * are + li * aim) / den
    f_im = (li * are - (lr - 1.0) * aim) / den
    return (lr, li), (f_re, f_im)


def _write_scan_tables(tab_ref, step, seg):
    pw = [step]
    for _ in range(seg - 1):
        pw.append(_cmul(pw[-1], step))
    shape = (SUBLANES, step[0].shape[-1])
    row = lax.broadcasted_iota(jnp.int32, shape, 0) & (seg - 1)
    zero = jnp.zeros(shape, F32)
    pre, pim = zero, zero
    for r in range(seg):
        pre = jnp.where(row == r, pw[r][0], pre)
        pim = jnp.where(row == r, pw[r][1], pim)
    tab_ref[0:8, :] = pre
    tab_ref[8:16, :] = pim
    for lvl, d in enumerate(_scan_levels(seg)):
        tab_ref[16 + 16 * lvl:24 + 16 * lvl, :] = jnp.where(row >= d, pw[d - 1][0], zero)
        tab_ref[24 + 16 * lvl:32 + 16 * lvl, :] = jnp.where(row >= d, pw[d - 1][1], zero)


def _scan_levels(seg):
    return [d for d in (1, 2, 4) if d < seg]


def _s5prep_kernel(are_ref, aim_ref, ldt_ref, tab_ref, *, seg):
    lam, _ = _zoh(are_ref[...], aim_ref[...], ldt_ref[...])
    lam_c = lam
    for _ in range(S5_SLOTS - 1):
        lam_c = _cmul(lam_c, lam)
    _write_scan_tables(tab_ref, lam_c, seg)


def _s5_prep(are, aim, ldt, seg):
    return pl.pallas_call(
        functools.partial(_s5prep_kernel, seg=seg),
        out_shape=jax.ShapeDtypeStruct((64, S5_LANES), F32),
        compiler_params=pltpu.CompilerParams(vmem_limit_bytes=VMEM_LIMIT),
        name="s5_prep",
    )(are, aim, ldt)


def _s5prepq_kernel(are_ref, aim_ref, ldt_ref, bre_ref, bim_ref, cre_ref, cim_ref, pq_ref, qq_ref, mq_ref):
    lam, f = _zoh(are_ref[...], aim_ref[...], ldt_ref[...])
    bb = _cmul(f, (bre_ref[...], bim_ref[...]))
    ct = (cre_ref[...], cim_ref[...])
    one = (jnp.ones_like(lam[0]), jnp.zeros_like(lam[0]))
    pw = [one]
    for _ in range(S5_SLOTS):
        pw.append(_cmul(pw[-1], lam))
    rows = S5_SLOTS * 4 * S5_GROUP
    blk = 4 * S5_GROUP
    for s in range(S5_SLOTS):
        p_re, p_im = _cmul(pw[S5_SLOTS - 1 - s], bb)
        pq_ref[:, s * blk:(s + 1) * blk, :QUAD_LANES] = p_re.astype(BF16)
        pq_ref[:, s * blk:(s + 1) * blk, QUAD_LANES:] = p_im.astype(BF16)
    cl = [_cmul(pw[e], ct) for e in range(S5_SLOTS + 1)]
    zero = jnp.zeros_like(cl[0][0])
    qt_re = jnp.concatenate([cl[t + 1][0] for t in range(S5_SLOTS)], axis=1)
    qt_im = jnp.concatenate([-cl[t + 1][1] for t in range(S5_SLOTS)], axis=1)
    for n in range(S5_QUADS):
        qq_ref[n, :QUAD_LANES, :] = qt_re[n].T.astype(BF16)
        qq_ref[n, QUAD_LANES:, :] = qt_im[n].T.astype(BF16)
    for s in range(S5_SLOTS):
        wide_re = jnp.concatenate([cl[t - s][0] if t >= s else zero for t in range(S5_SLOTS)], axis=1)
        wide_im = jnp.concatenate([cl[t - s][1] if t >= s else zero for t in range(S5_SLOTS)], axis=1)
        m = _bmm_nt3(bb[0], wide_re) - _bmm_nt3(bb[1], wide_im)
        mq_ref[:, s * blk:(s + 1) * blk, :] = m.astype(BF16)
    assert rows == QUAD_LANES


def _s5_prep_quads(are_q, aim_q, ldt_q, bre_q, bim_q, cre_q, cim_q):
    w = lambda r, c_: jax.ShapeDtypeStruct((S5_QUADS, r, c_), BF16)
    return pl.pallas_call(
        _s5prepq_kernel,
        out_shape=(w(QUAD_LANES, 2 * QUAD_LANES), w(2 * QUAD_LANES, QUAD_LANES), w(QUAD_LANES, QUAD_LANES)),
        compiler_params=pltpu.CompilerParams(vmem_limit_bytes=VMEM_LIMIT),
        name="s5_prep_quads",
    )(are_q, aim_q, ldt_q, bre_q, bim_q, cre_q, cim_q)


def _inproj_kernel(x_ref, g_ref, w_ref, qkv_ref, z_ref, u_ref, ba_ref):
    h = _rms(x_ref[...], g_ref[...]).astype(BF16)
    o1 = QKV_WIDTH
    o2 = o1 + DN_WIDTH
    o3 = o2 + S5_WIDTH
    qkv_ref[...] = _dot(h, w_ref[:, :o1])
    z_ref[...] = _dot(h, w_ref[:, o1:o2])
    u_ref[...] = _dot(h, w_ref[:, o2:o3])
    ba_ref[...] = _dot(h, w_ref[:, o3:])


def _in_proj(x2d, g, w, tm):
    n = x2d.shape[0]
    row = lambda w_: pl.BlockSpec((tm, w_), lambda i: (i, 0))
    return pl.pallas_call(
        _inproj_kernel,
        grid=(n // tm,),
        in_specs=[row(D_MODEL), _const_spec((1, D_MODEL)), _const_spec((D_MODEL, IN_PAD))],
        out_specs=(row(QKV_WIDTH), row(DN_WIDTH), row(S5_WIDTH), row(LANES)),
        out_shape=(jax.ShapeDtypeStruct((n, QKV_WIDTH), F32),
                   jax.ShapeDtypeStruct((n, DN_WIDTH), F32),
                   jax.ShapeDtypeStruct((n, S5_WIDTH), F32),
                   jax.ShapeDtypeStruct((n, LANES), F32)),
        compiler_params=_params(1),
        name="in_proj",
    )(x2d, g, w)


def _dn_kernel(*refs, nb, tt, c):
    _dn_body(pl.program_id(1) == 0, *refs, nb=nb, tt=tt, c=c)


def _dn_body(first, qkv_ref, z_ref, ba_ref, cache_ref, s0_ref, cw_ref, gp_ref, ng_ref,
             o_ref, state_ref, halo_s, *, nb, tt, c, side=()):
    nh = DN_HEADS
    dh = DN_HEAD_DIM
    nck = tt // c
    npb = nb * nck
    hc = nh * c

    @pl.when(first)
    def _():
        halo_s[...] = cache_ref[...]
        state_ref[...] = s0_ref[...]

    side = list(side)

    def side_work():
        if side:
            side.pop(0)()

    def stack(a):
        return jnp.concatenate([a[:, :, h * dh:(h + 1) * dh] for h in range(nh)], axis=1)

    def unstack(a):
        return jnp.concatenate([a[:, h * c:(h + 1) * c, :] for h in range(nh)], axis=2)

    first_row = lax.broadcasted_iota(jnp.int32, (SUBLANES, MXU_DIM), 0) == 0

    def shift_rows(y, carried):
        rolled = pltpu.roll(y, 1, axis=1)
        top = jnp.where(first_row[None], carried, rolled[:, :SUBLANES])
        return jnp.concatenate([top, rolled[:, SUBLANES:]], axis=1)

    conv_parts = []
    for lo in range(0, QKV_WIDTH, MXU_DIM):
        side_work()
        hi = lo + MXU_DIM
        cw = cw_ref[:, lo:hi]
        x = qkv_ref[:, :, lo:hi]
        prev = halo_s[:, :, lo:hi]
        acc = x * cw[0:1]
        carried = prev[:, SUBLANES - 1:] * cw[0:1]
        for j in range(1, DN_CONV):
            acc = shift_rows(acc, carried) + x * cw[j:j + 1]
            if j < DN_CONV - 1:
                carried = prev[:, SUBLANES - 1 - j:SUBLANES - j] * cw[0:1]
                for i in range(1, j + 1):
                    carried = carried + prev[:, SUBLANES - 1 - j + i:SUBLANES - j + i] * cw[i:i + 1]
        halo_s[:, :, lo:hi] = x[:, tt - SUBLANES:]
        conv_parts.append((acc * jax.nn.sigmoid(acc)).reshape(npb, c, hi - lo))
    per = DN_WIDTH // MXU_DIM
    q, k, v = [stack(jnp.concatenate(conv_parts[i * per:(i + 1) * per], axis=2)) for i in range(3)]
    side_work()
    q = q * (lax.rsqrt(jnp.sum(q * q, axis=-1, keepdims=True) + EPS) * (dh ** -0.5))
    k = k * lax.rsqrt(jnp.sum(k * k, axis=-1, keepdims=True) + EPS)

    side_work()
    ba = ba_ref[...].reshape(nb * tt, LANES)
    beta = jax.nn.sigmoid(ba).reshape(npb, c, LANES)
    g2 = -jnp.exp(gp_ref[0:1, :]) * jax.nn.softplus(ba + gp_ref[1:2, :])
    row_in_chunk = lax.broadcasted_iota(jnp.int32, (nb * tt, LANES), 0) & (c - 1)
    d = 1
    while d < c:
        g2 = g2 + jnp.where(row_in_chunk >= d, pltpu.roll(g2, d, axis=0), 0.0)
        d *= 2
    g = g2.reshape(npb, c, LANES)

    g_cols = [g[:, :, BA_LANE_G + h:BA_LANE_G + h + 1] for h in range(nh)]
    b_cols = [beta[:, :, h:h + 1] for h in range(nh)]
    g_col = jnp.concatenate(g_cols, axis=1)
    b_col = jnp.concatenate(b_cols, axis=1)
    g_last = jnp.concatenate([jnp.broadcast_to(gc[:, c - 1:c, :], (npb, c, 1)) for gc in g_cols], axis=1)

    hp = min(nh, LANES // c)
    pieces = []
    for h0 in range(0, nh, hp):
        slab = jnp.concatenate(
            [(g2 if hh == 0 else pltpu.roll(g2, LANES - hh, axis=1)).reshape(npb, c, LANES)
             for hh in range(h0, h0 + hp)], axis=1)
        if hp * c < LANES:
            slab = jnp.concatenate([slab, jnp.zeros((npb, LANES - hp * c, LANES), F32)], axis=1)
        rows = [slab[p].T[BA_LANE_G:BA_LANE_G + 1, :hp * c] for p in range(npb)]
        pieces.append(jnp.stack(rows, axis=0))
    g_row = jnp.concatenate(pieces, axis=2)

    ri = lax.broadcasted_iota(jnp.int32, (c, hc), 0)
    lane = lax.broadcasted_iota(jnp.int32, (c, hc), 1)
    cj = lane & (c - 1)
    causal = (ri >= cj)[None]
    strict = (ri > cj)[None]

    def cat_from_cols(cols):
        out = jnp.broadcast_to(cols[nh - 1], (npb, c, hc))
        for h in range(nh - 2, -1, -1):
            out = jnp.where((lane < (h + 1) * c)[None], jnp.broadcast_to(cols[h], (npb, c, hc)), out)
        return out

    decay = jnp.exp(jnp.where(causal, cat_from_cols(g_cols) - g_row, -jnp.inf))

    side_work()
    kb = k.astype(BF16)
    br = lax.broadcasted_iota(jnp.int32, (hc, nh * dh), 0)
    bl = lax.broadcasted_iota(jnp.int32, (hc, nh * dh), 1)
    head_of_row = sum(jnp.where(br >= h * c, 1, 0) for h in range(1, nh))
    head_of_lane = sum(jnp.where(bl >= h * dh, 1, 0) for h in range(1, nh))
    k_bd = jnp.where((head_of_row == head_of_lane)[None], jnp.concatenate([kb] * nh, axis=2), 0.0)
    qk_lhs = jnp.concatenate([unstack(q), unstack(k)], axis=1).astype(BF16)
    qkk = _bmm_nt(qk_lhs, k_bd)
    qk = qkk[:, :c]
    kk = qkk[:, c:]

    sr = lax.broadcasted_iota(jnp.int32, (hc, hc), 0)
    sl = lax.broadcasted_iota(jnp.int32, (hc, hc), 1)
    shift = c.bit_length() - 1
    same_head = ((sr >> shift) == (sl >> shift))[None]

    def bd(m):
        return jnp.where(same_head, jnp.concatenate([m] * nh, axis=1), 0.0)

    lm = jnp.where(strict, cat_from_cols(b_cols) * kk * decay, 0.0)
    tinv = jnp.where((ri == cj)[None], 1.0, 0.0) - jnp.where(((ri >> 1) == (cj >> 1))[None], lm, 0.0)
    s = 2
    while s < c:
        sh = s.bit_length()
        lower_left = ((ri >> sh) == (cj >> sh)) & ((ri & s) != 0) & ((cj & s) == 0)
        side_work()
        a_off = jnp.where(lower_left[None], lm, 0.0).astype(BF16)
        xm = _bmm(tinv.astype(BF16), bd(a_off))
        tinv = tinv - _bmm(xm.astype(BF16), bd(tinv.astype(BF16)))
        s *= 2

    e_g = jnp.exp(g_col)
    rhs = jnp.concatenate([v * b_col, k * (b_col * e_g)], axis=2).astype(BF16)
    sol = _bmm(bd(tinv.astype(BF16)), rhs)
    sol_v = sol[:, :, :dh]
    sol_k = sol[:, :, dh:]
    w_qe = jnp.concatenate([sol_k, q * e_g], axis=2).astype(BF16)
    kd = (k * jnp.exp(g_last - g_col)).astype(BF16)
    qkd_bd = bd((qk * decay).astype(BF16))
    s_decay = jnp.exp(g_last)

    outs = []
    for b in range(nb):
        for ck in range(nck):
            p = b * nck + ck
            ws, qs = [], []
            states = [state_ref[b, h] for h in range(nh)]
            for h in range(nh):
                sb = states[h].astype(BF16)
                rows = w_qe[p, h * c:(h + 1) * c, :]
                ws.append(_dot(rows[:, :dh], sb))
                qs.append(_dot(rows[:, dh:], sb))
            u = sol_v[p] - jnp.concatenate(ws, axis=0)
            ub = u.astype(BF16)
            outs.append(jnp.concatenate(qs, axis=0) + _dot(qkd_bd[p], ub))
            for h in range(nh):
                r0 = h * c
                state_ref[b, h] = (states[h] * s_decay[p, r0:r0 + 1, :]
                                   + _dot_tn(kd[p, r0:r0 + c, :], ub[r0:r0 + c, :]))

    o = jnp.stack(outs, axis=0)
    zs = stack(z_ref[...].reshape(npb, c, DN_WIDTH))
    o = _rms(o, ng_ref[...]) * (zs * jax.nn.sigmoid(zs))
    o_ref[...] = unstack(o).reshape(nb, tt, DN_WIDTH).astype(BF16)
    while side:
        side_work()


def _deltanet(qkv, z, ba, cache8, s0, cw, gp, ng, nb, tt, c):
    b, l, _ = qkv.shape
    seq = lambda w_: pl.BlockSpec((nb, tt, w_), lambda i, j: (i, j, 0))
    per_b = lambda *s: pl.BlockSpec((nb,) + s, lambda i, j: (i,) + (0,) * len(s))
    return pl.pallas_call(
        functools.partial(_dn_kernel, nb=nb, tt=tt, c=c),
        grid=(b // nb, l // tt),
        in_specs=[seq(QKV_WIDTH), seq(DN_WIDTH), seq(LANES), per_b(SUBLANES, QKV_WIDTH),
                  per_b(DN_HEADS, DN_HEAD_DIM, DN_HEAD_DIM),
                  _const_spec((SUBLANES, QKV_WIDTH)), _const_spec((SUBLANES, LANES)),
                  _const_spec((1, DN_HEAD_DIM))],
        out_specs=(seq(DN_WIDTH), per_b(DN_HEADS, DN_HEAD_DIM, DN_HEAD_DIM)),
        out_shape=(jax.ShapeDtypeStruct((b, l, DN_WIDTH), BF16),
                   jax.ShapeDtypeStruct((b, DN_HEADS, DN_HEAD_DIM, DN_HEAD_DIM), F32)),
        scratch_shapes=[pltpu.VMEM((nb, SUBLANES, QKV_WIDTH), F32)],
        compiler_params=_params(2),
        name="deltanet",
    )(qkv, z, ba, cache8, s0, cw, gp, ng)


def _mixer_kernel(x_ref, g_ref, w_ref, cache_ref, s0_ref, cw_ref, gp_ref, ng_ref,
                  u_ref, o_ref, state_ref, tail_ref, p_cur, p_nxt, halo_s, *, nb, tt, c):
    i = pl.program_id(0)
    o1 = QKV_WIDTH
    o2 = o1 + DN_WIDTH
    o3 = o2 + S5_WIDTH

    @pl.when(i == 0)
    def _():
        p_cur[...] = jnp.zeros((nb, tt, IN_PAD), F32)

    h = _rms(x_ref[...].reshape(nb * tt, D_MODEL), g_ref[...]).astype(BF16)

    def project(lo, hi):
        def run():
            res = _dot(h, w_ref[:, lo:hi]).reshape(nb, tt, hi - lo)
            if o2 <= lo < o3:
                u_ref[:, :, lo - o2:hi - o2] = res
            else:
                p_nxt[:, :, lo:hi] = res
        return run

    side = [project(lo, min(lo + MXU_DIM, IN_PAD)) for lo in range(0, IN_PAD, MXU_DIM)]
    _dn_body(i <= 1, p_cur.at[:, :, 0:o1], p_cur.at[:, :, o1:o2], p_cur.at[:, :, o3:IN_PAD],
             cache_ref, s0_ref, cw_ref, gp_ref, ng_ref, o_ref, state_ref, halo_s, nb=nb, tt=tt, c=c, side=side)
    tail_ref[...] = halo_s[...]
    for lo, hi in ((0, o1), (o1, o2), (o3, IN_PAD)):
        p_cur[:, :, lo:hi] = p_nxt[:, :, lo:hi]


def _mixer(x, g, w, cache8, s0, cw, gp, ng, tt, c):
    nb, l, _ = x.shape
    nt = l // tt
    cur = lambda w_: pl.BlockSpec((nb, tt, w_), lambda i: (0, jnp.minimum(i, nt - 1), 0))
    prev = lambda w_: pl.BlockSpec((nb, tt, w_), lambda i: (0, jnp.maximum(i - 1, 0), 0))
    return pl.pallas_call(
        functools.partial(_mixer_kernel, nb=nb, tt=tt, c=c),
        grid=(nt + 1,),
        in_specs=[cur(D_MODEL), _const_spec((1, D_MODEL)), _const_spec((D_MODEL, IN_PAD)),
                  _const_spec((nb, SUBLANES, QKV_WIDTH)), _const_spec((nb, DN_HEADS, DN_HEAD_DIM, DN_HEAD_DIM)),
                  _const_spec((SUBLANES, QKV_WIDTH)), _const_spec((SUBLANES, LANES)),
                  _const_spec((1, DN_HEAD_DIM))],
        out_specs=(cur(S5_WIDTH), prev(DN_WIDTH),
                   pl.BlockSpec((nb, DN_HEADS, DN_HEAD_DIM, DN_HEAD_DIM), lambda i: (0, 0, 0, 0)),
                   pl.BlockSpec((nb, SUBLANES, QKV_WIDTH), lambda i: (0, 0, 0))),
        out_shape=(jax.ShapeDtypeStruct((nb, l, S5_WIDTH), F32),
                   jax.ShapeDtypeStruct((nb, l, DN_WIDTH), BF16),
                   jax.ShapeDtypeStruct((nb, DN_HEADS, DN_HEAD_DIM, DN_HEAD_DIM), F32),
                   jax.ShapeDtypeStruct((nb, SUBLANES, QKV_WIDTH), F32)),
        scratch_shapes=[pltpu.VMEM((nb, tt, IN_PAD), F32), pltpu.VMEM((nb, tt, IN_PAD), F32),
                        pltpu.VMEM((nb, SUBLANES, QKV_WIDTH), F32)],
        compiler_params=_params(1),
        name="mixer",
    )(x, g, w, cache8, s0, cw, gp, ng)


def _s5c_kernel(u_ref, x0re_ref, x0im_ref, pq_ref, qq_ref, mq_ref, tab_ref, dv_ref, gw_ref, gb_ref, ng_ref,
                o_ref, fre_ref, fim_ref, xr_s, xi_s, car_s, u_s, o_s, *, rows, seg):
    t = pl.program_id(1)
    carry = seg == SUBLANES
    ncol = S5_WIDTH // LANES
    for k in range(ncol):
        u_s[k] = u_ref[0, :, k * LANES:(k + 1) * LANES]
    us = [jnp.concatenate([u_s[k, pl.ds(s, rows, stride=S5_SLOTS), :] for k in range(ncol)], axis=1)
          for s in range(S5_SLOTS)]
    ubs = [a.astype(BF16) for a in us]
    blk = 4 * S5_GROUP
    ql = QUAD_LANES

    if carry:
        @pl.when(t == 0)
        def _():
            car_s[0:1, :] = x0re_ref[0]
            car_s[1:2, :] = x0im_ref[0]

    uq = [jnp.concatenate([ubs[s][:, n * blk:(n + 1) * blk] for s in range(S5_SLOTS)], axis=1)
          for n in range(S5_QUADS)]
    for n in range(S5_QUADS):
        inc = _dot(uq[n], pq_ref[n])
        xr_s[:, n * ql:(n + 1) * ql] = inc[:, :ql]
        xi_s[:, n * ql:(n + 1) * ql] = inc[:, ql:]

    seg_row = lax.broadcasted_iota(jnp.int32, (SUBLANES, S5_LANES), 0) & (seg - 1)

    def block_body(rb, cin):
        r0 = pl.multiple_of(rb * SUBLANES, SUBLANES)
        if carry:
            c_re, c_im = cin
        else:
            c_re = x0re_ref[0, pl.ds(r0, SUBLANES), :]
            c_im = x0im_ref[0, pl.ds(r0, SUBLANES), :]
        xr = xr_s[pl.ds(r0, SUBLANES), :]
        xi = xi_s[pl.ds(r0, SUBLANES), :]
        for lvl, d in enumerate(_scan_levels(seg)):
            m_re = tab_ref[16 + 16 * lvl:24 + 16 * lvl, :]
            m_im = tab_ref[24 + 16 * lvl:32 + 16 * lvl, :]
            sr = pltpu.roll(xr, d, axis=0)
            si = pltpu.roll(xi, d, axis=0)
            xr, xi = xr + (m_re * sr - m_im * si), xi + (m_re * si + m_im * sr)
        p_re = tab_ref[0:8, :]
        p_im = tab_ref[8:16, :]
        xr, xi = xr + (p_re * c_re - p_im * c_im), xi + (p_re * c_im + p_im * c_re)
        xr_s[pl.ds(r0, SUBLANES), :] = jnp.where(seg_row == 0, c_re, pltpu.roll(xr, 1, axis=0))
        xi_s[pl.ds(r0, SUBLANES), :] = jnp.where(seg_row == 0, c_im, pltpu.roll(xi, 1, axis=0))
        if carry:
            return xr[SUBLANES - 1:, :], xi[SUBLANES - 1:, :]
        fre_ref[0, pl.ds(r0, SUBLANES), :] = xr
        fim_ref[0, pl.ds(r0, SUBLANES), :] = xi
        return cin

    if carry:
        c_re, c_im = lax.fori_loop(0, rows // SUBLANES, block_body, (car_s[0:1, :], car_s[1:2, :]))
        car_s[0:1, :] = c_re
        car_s[1:2, :] = c_im
        fre_ref[0] = c_re
        fim_ref[0] = c_im
    else:
        lax.fori_loop(0, rows // SUBLANES, block_body, 0)

    ys = []
    for n in range(S5_QUADS):
        xs = jnp.concatenate([xr_s[:, n * ql:(n + 1) * ql], xi_s[:, n * ql:(n + 1) * ql]], axis=1).astype(BF16)
        ys.append(_dot(uq[n], mq_ref[n]) + _dot(xs, qq_ref[n]))
    for s in range(S5_SLOTS):
        y = jnp.concatenate([ys[n][:, s * blk:(s + 1) * blk] for n in range(S5_QUADS)], axis=1)
        y = jax.nn.gelu(y + dv_ref[...] * us[s]).astype(BF16)
        gl = _dot(y, gw_ref[...]) + gb_ref[...]
        o = gl[:, :S5_WIDTH] * jax.nn.sigmoid(gl[:, S5_WIDTH:])
        o = _rms(o, ng_ref[...])
        for k in range(ncol):
            o_s[k, pl.ds(s, rows, stride=S5_SLOTS), :] = o[:, k * LANES:(k + 1) * LANES]
    o_ref[0] = jnp.concatenate([o_s[k] for k in range(ncol)], axis=1).astype(BF16)


def _s5_chunked(u, x0re, x0im, pq, qq, mq, tabc, dv, gw, gb, ng, rows, seg):
    b, l, _ = u.shape
    tt = rows * S5_SLOTS
    seq = pl.BlockSpec((1, tt, S5_WIDTH), lambda i, j: (i, j, 0))
    if seg == SUBLANES:
        st = pl.BlockSpec((1, 1, S5_LANES), lambda i, j: (i, 0, 0))
    else:
        st = pl.BlockSpec((1, rows, S5_LANES), lambda i, j: (i, j, 0))
    return pl.pallas_call(
        functools.partial(_s5c_kernel, rows=rows, seg=seg),
        grid=(b, l // tt),
        in_specs=[seq, st, st,
                  _const_spec((S5_QUADS, QUAD_LANES, 2 * QUAD_LANES)),
                  _const_spec((S5_QUADS, 2 * QUAD_LANES, QUAD_LANES)),
                  _const_spec((S5_QUADS, QUAD_LANES, QUAD_LANES)), _const_spec((64, S5_LANES)),
                  _const_spec((1, S5_WIDTH)), _const_spec((S5_WIDTH, 2 * S5_WIDTH)),
                  _const_spec((1, 2 * S5_WIDTH)), _const_spec((1, S5_WIDTH))],
        out_specs=(seq, st, st),
        out_shape=(jax.ShapeDtypeStruct((b, l, S5_WIDTH), BF16),
                   jax.ShapeDtypeStruct(x0re.shape, F32),
                   jax.ShapeDtypeStruct(x0re.shape, F32)),
        scratch_shapes=[pltpu.VMEM((rows, S5_LANES), F32), pltpu.VMEM((rows, S5_LANES), F32),
                        pltpu.VMEM((SUBLANES, S5_LANES), F32),
                        pltpu.VMEM((S5_WIDTH // LANES, tt, LANES), F32),
                        pltpu.VMEM((S5_WIDTH // LANES, tt, LANES), F32)],
        compiler_params=_params(2),
        name="s5_chunked",
    )(u, x0re, x0im, pq, qq, mq, tabc, dv, gw, gb, ng)


def _ffn_kernel(x_ref, odn_ref, os5_ref, prev_ref, woa_ref, wob_ref, n2_ref, wug_ref, wuv_ref,
                fcw_ref, fcb_ref, wd_ref, fg_ref, y_ref, tail_ref, halo_s, *, tm, ls, carry):
    t = pl.program_id(1)
    x1 = x_ref[0] + _dot(odn_ref[0], woa_ref[...]) + _dot(os5_ref[0], wob_ref[...])
    h2 = _rms(x1, n2_ref[...]).astype(BF16)
    if carry:
        @pl.when(t == 0)
        def _():
            halo_s[...] = prev_ref[0]

    down = None
    for lo in range(0, D_FF, FF_CHUNK):
        hi = min(lo + FF_CHUNK, D_FF)
        gate = _dot(h2, wug_ref[:, lo:hi])
        val = _dot(h2, wuv_ref[:, lo:hi])
        if carry:
            xx = jnp.concatenate([halo_s[:, lo:hi], gate], axis=0)
            g2 = xx[SUBLANES - 2:SUBLANES - 2 + tm]
            g1 = xx[SUBLANES - 1:SUBLANES - 1 + tm]
            halo_s[:, lo:hi] = gate[tm - SUBLANES:, :]
            tail_ref[0, :, lo:hi] = gate[tm - SUBLANES:, :]
        else:
            r = lax.broadcasted_iota(jnp.int32, (tm, hi - lo), 0) & (ls - 1)
            prev = prev_ref[0, :, lo:hi]
            g1 = jnp.where(r < 1, pltpu.roll(prev, tm - 1, axis=0), pltpu.roll(gate, 1, axis=0))
            g2 = jnp.where(r < 2, prev, pltpu.roll(gate, 2, axis=0))
            tail_ref[0, :, lo:hi] = gate
        cw = fcw_ref[:, lo:hi]
        pre = g2 * cw[0:1] + g1 * cw[1:2] + gate * cw[2:3] + fcb_ref[:, lo:hi]
        act = pre * jax.nn.sigmoid(pre) * val
        part = _dot(act.astype(BF16), wd_ref[lo:hi, :])
        down = part if down is None else down + part
    y_ref[0] = _rms(x1 + down, fg_ref[...])


def _ffn(x, odn, os5, prev, woa, wob, n2, wug, wuv, fcw, fcb, wd, fg, tm, ls, carry):
    b, l, _ = x.shape
    seq = lambda w_: pl.BlockSpec((1, tm, w_), lambda i, j: (i, j, 0))
    if carry:
        prev_spec = pl.BlockSpec((1, SUBLANES, D_FF), lambda i, j: (i, 0, 0))
        tail_spec = pl.BlockSpec((1, SUBLANES, D_FF), lambda i, j: (i, 0, 0))
        tail_shape = jax.ShapeDtypeStruct((b, SUBLANES, D_FF), F32)
    else:
        prev_spec = seq(D_FF)
        tail_spec = seq(D_FF)
        tail_shape = jax.ShapeDtypeStruct((b, l, D_FF), F32)
    return pl.pallas_call(
        functools.partial(_ffn_kernel, tm=tm, ls=ls, carry=carry),
        grid=(b, l // tm),
        in_specs=[seq(D_MODEL), seq(DN_WIDTH), seq(S5_WIDTH), prev_spec,
                  _const_spec((DN_WIDTH, D_MODEL)), _const_spec((S5_WIDTH, D_MODEL)),
                  _const_spec((1, D_MODEL)), _const_spec((D_MODEL, D_FF)), _const_spec((D_MODEL, D_FF)),
                  _const_spec((SUBLANES, D_FF)), _const_spec((1, D_FF)), _const_spec((D_FF, D_MODEL)),
                  _const_spec((1, D_MODEL))],
        out_specs=(seq(D_MODEL), tail_spec),
        out_shape=(jax.ShapeDtypeStruct((b, l, D_MODEL), F32), tail_shape),
        scratch_shapes=[pltpu.VMEM((SUBLANES, D_FF), F32)],
        compiler_params=_params(2),
        name="ffn",
    )(x, odn, os5, prev, woa, wob, n2, wug, wuv, fcw, fcb, wd, fg)


def _pad_rows_top(a, rows):
    return jnp.pad(a, ((0, 0), (rows - a.shape[1], 0), (0, 0)))


def _quad_blocks(m):
    same = jnp.eye(4, dtype=bool)[None, :, None, :, None]
    blocks = jnp.where(same, m.reshape(S5_QUADS, 4, S5_GROUP, 1, S5_STATE), 0.0)
    return blocks.reshape(S5_QUADS, 4 * S5_GROUP, QUAD_LANES)


def _trunk(x, conv_dn, s_dn, s5_re, s5_im, conv_ffn, w, prompt):
    b, l, _ = x.shape
    n = b * l
    c = CHUNK if l % CHUNK == 0 else l
    assert l >= 2 * SUBLANES, "sequence shorter than two row tiles"
    cache8 = _pad_rows_top(conv_dn, SUBLANES)
    if prompt:
        u, o_dn, s_dn_new, tail = _mixer(x, w['n1'], w['w_in'], cache8, s_dn, w['dn_cw'], w['dn_gp'], w['dn_g'],
                                         tt=min(l, 128), c=c)
        conv_dn_new = tail[:, SUBLANES - (DN_CONV - 1):]
    else:
        qkv, z, u, ba = _in_proj(x.reshape(n, D_MODEL), w['n1'], w['w_in'], min(n, 512))
        qkv = qkv.reshape(b, l, QKV_WIDTH)
        o_dn, s_dn_new = _deltanet(qkv, z.reshape(b, l, DN_WIDTH), ba.reshape(b, l, LANES), cache8, s_dn,
                                   w['dn_cw'], w['dn_gp'], w['dn_g'], nb=min(b, 8), tt=l, c=c)
        conv_dn_new = qkv[:, l - (DN_CONV - 1):]

    s5_args = (w['pq'], w['qq'], w['mq'], w['tab_long' if prompt else 'tab_short'], w['dv'], w['glu_w'],
               w['glu_b'], w['s5_g'])
    if prompt:
        x0re = s5_re.reshape(b, 1, S5_LANES)
        x0im = s5_im.reshape(b, 1, S5_LANES)
        o_s5, fre, fim = _s5_chunked(u.reshape(b, l, S5_WIDTH), x0re, x0im, *s5_args,
                                     rows=min(l // S5_SLOTS, 256), seg=SUBLANES)
    else:
        seg = l // S5_SLOTS
        rep = lambda s: jnp.repeat(s.reshape(b, S5_LANES), seg, axis=0).reshape(1, b * seg, S5_LANES)
        o_s5, fre, fim = _s5_chunked(u.reshape(1, n, S5_WIDTH), rep(s5_re), rep(s5_im), *s5_args,
                                     rows=b * seg, seg=seg)
        o_s5 = o_s5.reshape(b, l, S5_WIDTH)
        fre = fre.reshape(b, seg, S5_LANES)[:, seg - 1]
        fim = fim.reshape(b, seg, S5_LANES)[:, seg - 1]
    s5_re_new = fre.reshape(b, S5_GROUPS, S5_STATE)
    s5_im_new = fim.reshape(b, S5_GROUPS, S5_STATE)

    ffn_args = (w['w_out_a'], w['w_out_b'], w['n2'], w['w_up_g'], w['w_up_v'], w['fcw'], w['fcb'],
                w['w_down'], w['fg'])
    if prompt:
        prev = _pad_rows_top(conv_ffn, SUBLANES)
        y, tail = _ffn(x, o_dn, o_s5, prev, *ffn_args, tm=512, ls=512, carry=True)
        conv_ffn_new = tail[:, SUBLANES - (FFN_CONV - 1):]
    else:
        prev = jnp.pad(conv_ffn, ((0, 0), (0, l - (FFN_CONV - 1)), (0, 0))).reshape(1, n, D_FF)
        y, tail = _ffn(x.reshape(1, n, D_MODEL), o_dn.reshape(1, n, DN_WIDTH), o_s5.reshape(1, n, S5_WIDTH),
                       prev, *ffn_args, tm=min(n, 128), ls=l, carry=False)
        y = y.reshape(b, l, D_MODEL)
        conv_ffn_new = tail.reshape(b, l, D_FF)[:, l - (FFN_CONV - 1):]
    return y, (conv_dn_new[None], s_dn_new[None], s5_re_new[None], s5_im_new[None], conv_ffn_new[None])


def kernel(x_prompt, x_sample, cache_dn_conv, state_dn, state_s5_re, state_s5_im, cache_ffn_conv, norm1_g, w_in, dn_conv_w, dn_A_log, dn_dt_bias, dn_norm_g, s5_A_re, s5_A_im, s5_log_dt, s5_B_re, s5_B_im, s5_C_re, s5_C_im, s5_D, s5_glu_w, s5_glu_b, s5_norm_g, w_out, norm2_g, w_up, ffn_conv_w, ffn_conv_b, w_down, final_norm_g):
    assert w_in.shape[0] == 1, "single-layer trunk"
    o1 = QKV_WIDTH
    o2 = o1 + DN_WIDTH
    o4 = o2 + 2 * DN_HEADS
    wi = w_in[0]
    w_in_r = jnp.concatenate(
        [wi[:, :o2], wi[:, o4:], wi[:, o2:o4], jnp.zeros((D_MODEL, LANES - 2 * DN_HEADS), wi.dtype)],
        axis=1).astype(BF16)
    lane_pad = lambda v: jnp.pad(v, (BA_LANE_G, LANES - BA_LANE_G - DN_HEADS))
    ldt_lanes = jnp.repeat(s5_log_dt[0], S5_STATE)
    seg_short = x_sample.shape[1] // S5_SLOTS
    assert x_sample.shape[1] % S5_SLOTS == 0 and seg_short in (1, 2, 4), "sample sequences of 4, 8 or 16 rows"
    lane_row = lambda v: v.reshape(1, S5_LANES)
    tab_long, tab_short = [_s5_prep(lane_row(s5_A_re[0]), lane_row(s5_A_im[0]), lane_row(ldt_lanes), seg=s)
                           for s in (SUBLANES, seg_short)]
    quad_row = lambda v: v.reshape(S5_QUADS, 1, QUAD_LANES)
    pq, qq, mq = _s5_prep_quads(quad_row(s5_A_re[0]), quad_row(s5_A_im[0]), quad_row(ldt_lanes),
                                _quad_blocks(s5_B_re[0].transpose(0, 2, 1)), _quad_blocks(s5_B_im[0].transpose(0, 2, 1)),
                                _quad_blocks(s5_C_re[0]), _quad_blocks(s5_C_im[0]))
    w = {
        'n1': norm1_g, 'w_in': w_in_r,
        'dn_cw': jnp.pad(dn_conv_w[0], ((0, SUBLANES - DN_CONV), (0, 0))),
        'dn_gp': jnp.pad(jnp.stack([lane_pad(dn_A_log[0]), lane_pad(dn_dt_bias[0])]), ((0, SUBLANES - 2), (0, 0))),
        'dn_g': dn_norm_g,
        'tab_long': tab_long, 'tab_short': tab_short, 'pq': pq, 'qq': qq, 'mq': mq,
        'dv': s5_D, 'glu_w': s5_glu_w[0].astype(BF16), 'glu_b': s5_glu_b, 's5_g': s5_norm_g,
        'w_out_a': w_out[0, :DN_WIDTH].astype(BF16), 'w_out_b': w_out[0, DN_WIDTH:].astype(BF16),
        'n2': norm2_g, 'w_up_g': w_up[0, :, :D_FF].astype(BF16), 'w_up_v': w_up[0, :, D_FF:].astype(BF16),
        'fcw': jnp.pad(ffn_conv_w[0], ((0, SUBLANES - FFN_CONV), (0, 0))), 'fcb': ffn_conv_b,
        'w_down': w_down[0].astype(BF16), 'fg': final_norm_g.reshape(1, D_MODEL),
    }
    bp = x_prompt.shape[0]
    zeros = lambda *s: jnp.zeros(s, F32)
    y_p, st_p = _trunk(x_prompt, zeros(bp, DN_CONV - 1, QKV_WIDTH), zeros(bp, DN_HEADS, DN_HEAD_DIM, DN_HEAD_DIM),
                       zeros(bp, S5_GROUPS, S5_STATE), zeros(bp, S5_GROUPS, S5_STATE),
                       zeros(bp, FFN_CONV - 1, D_FF), w, prompt=True)
    y_s, st_s = _trunk(x_sample, cache_dn_conv[0], state_dn[0], state_s5_re[0], state_s5_im[0],
                       cache_ffn_conv[0], w, prompt=False)
    return (y_p, y_s) + st_p + st_s
```

```python
import functools

import jax
import jax.numpy as jnp
from jax import lax
from jax.experimental import pallas as pl
from jax.experimental.pallas import tpu as pltpu

F32 = jnp.float32
BF16 = jnp.bfloat16
EPS = 1e-6

D_MODEL = 1024
DN_HEADS = 4
DN_HEAD_DIM = 128
DN_WIDTH = DN_HEADS * DN_HEAD_DIM
DN_CONV = 4
QKV_WIDTH = 3 * DN_WIDTH
S5_WIDTH = D_MODEL - DN_WIDTH
S5_GROUP = 16
S5_GROUPS = S5_WIDTH // S5_GROUP
S5_STATE = 64
S5_LANES = S5_GROUPS * S5_STATE
S5_SLOTS = 8
S5_QUADS = S5_GROUPS // 4
QUAD_LANES = 4 * S5_STATE
QUAD_IN = S5_SLOTS * 4 * S5_GROUP
D_FF = 2816
FFN_CONV = 3
CHUNK = 64

SUBLANES = 8
LANES = 128
BA_LANE_G = DN_HEADS
IN_PAD = QKV_WIDTH + DN_WIDTH + S5_WIDTH + LANES
MXU_DIM = 256
FF_CHUNK = 6 * MXU_DIM

VMEM_LIMIT = 56 * 1024 * 1024


def _dot(a, b):
    return jnp.dot(a, b, preferred_element_type=F32)


def _dot_tn(a, b):
    return lax.dot_general(a, b, (((0,), (0,)), ((), ())), preferred_element_type=F32)


def _split_bf16(a):
    hi = a.astype(BF16)
    lo = (a - hi.astype(F32)).astype(BF16)
    return hi, lo


def _bmm(a, b):
    return lax.dot_general(a, b, (((2,), (1,)), ((0,), (0,))), preferred_element_type=F32)


def _bmm_nt(a, b):
    return lax.dot_general(a, b, (((2,), (2,)), ((0,), (0,))), preferred_element_type=F32)


def _bmm_nt3(a, b):
    ah, al = _split_bf16(a)
    bh, bl = _split_bf16(b)
    return _bmm_nt(ah, bh) + (_bmm_nt(al, bh) + _bmm_nt(ah, bl))


def _rms(x, g):
    return x * lax.rsqrt(jnp.mean(x * x, axis=-1, keepdims=True) + EPS) * g


def _const_spec(shape):
    nd = len(shape)
    return pl.BlockSpec(shape, lambda *_: (0,) * nd, pipeline_mode=pl.Buffered(1))


def _params(n_axes):
    return pltpu.CompilerParams(dimension_semantics=("arbitrary",) * n_axes,
                                vmem_limit_bytes=VMEM_LIMIT)


def _cmul(a, b):
    return a[0] * b[0] - a[1] * b[1], a[0] * b[1] + a[1] * b[0]


def _zoh(are, aim, ldt):
    dt = jnp.exp(ldt)
    mag = jnp.exp(are * dt)
    ang = aim * dt
    lr = mag * jnp.cos(ang)
    li = mag * jnp.sin(ang)
    den = are * are + aim * aim
    f_re = ((lr - 1.0) * are + li * aim) / den
    f_im = (li * are - (lr - 1.0) * aim) / den
    return (lr, li), (f_re, f_im)


def _write_scan_tables(tab_ref, step, seg):
    pw = [step]
    for _ in range(seg - 1):
        pw.append(_cmul(pw[-1], step))
    shape = (SUBLANES, step[0].shape[-1])
    row = lax.broadcasted_iota(jnp.int32, shape, 0) & (seg - 1)
    zero = jnp.zeros(shape, F32)
    pre, pim = zero, zero
    for r in range(seg):
        pre = jnp.where(row == r, pw[r][0], pre)
        pim = jnp.where(row == r, pw[r][1], pim)
    tab_ref[0:8, :] = pre
    tab_ref[8:16, :] = pim
    for lvl, d in enumerate(_scan_levels(seg)):
        tab_ref[16 + 16 * lvl:24 + 16 * lvl, :] = jnp.where(row >= d, pw[d - 1][0], zero)
        tab_ref[24 + 16 * lvl:32 + 16 * lvl, :] = jnp.where(row >= d, pw[d - 1][1], zero)


def _scan_levels(seg):
    return [d for d in (1, 2, 4) if d < seg]


def _s5prep_kernel(are_ref, aim_ref, ldt_ref, tab_ref, *, seg):
    lam, _ = _zoh(are_ref[...], aim_ref[...], ldt_ref[...])
    lam_c = lam
    for _ in range(S5_SLOTS - 1):
        lam_c = _cmul(lam_c, lam)
    _write_scan_tables(tab_ref, lam_c, seg)


def _s5_prep(are, aim, ldt, seg):
    return pl.pallas_call(
        functools.partial(_s5prep_kernel, seg=seg),
        out_shape=jax.ShapeDtypeStruct((64, S5_LANES), F32),
        compiler_params=pltpu.CompilerParams(vmem_limit_bytes=VMEM_LIMIT),
        name="s5_prep",
    )(are, aim, ldt)


def _s5prepq_kernel(are_ref, aim_ref, ldt_ref, bre_ref, bim_ref, cre_ref, cim_ref, pq_ref, qq_ref, mq_ref):
    lam, f = _zoh(are_ref[...], aim_ref[...], ldt_ref[...])
    bb = _cmul(f, (bre_ref[...], bim_ref[...]))
    ct = (cre_ref[...], cim_ref[...])
    one = (jnp.ones_like(lam[0]), jnp.zeros_like(lam[0]))
    pw = [one]
    for _ in range(S5_SLOTS):
        pw.append(_cmul(pw[-1], lam))
    blk = 4 * S5_GROUP
    for s in range(S5_SLOTS):
        p_re, p_im = _cmul(pw[S5_SLOTS - 1 - s], bb)
        pq_ref[:, s * blk:(s + 1) * blk, :QUAD_LANES] = p_re.astype(BF16)
        pq_ref[:, s * blk:(s + 1) * blk, QUAD_LANES:] = p_im.astype(BF16)
    cl = [_cmul(pw[e], ct) for e in range(S5_SLOTS + 1)]
    zero = jnp.zeros_like(cl[0][0])
    qt_re = jnp.concatenate([cl[t + 1][0] for t in range(S5_SLOTS)], axis=1)
    qt_im = jnp.concatenate([-cl[t + 1][1] for t in range(S5_SLOTS)], axis=1)
    for n in range(S5_QUADS):
        qq_ref[n, :QUAD_LANES, :] = qt_re[n].T.astype(BF16)
        qq_ref[n, QUAD_LANES:, :] = qt_im[n].T.astype(BF16)
    for s in range(S5_SLOTS):
        wide_re = jnp.concatenate([cl[t - s][0] if t >= s else zero for t in range(S5_SLOTS)], axis=1)
        wide_im = jnp.concatenate([cl[t - s][1] if t >= s else zero for t in range(S5_SLOTS)], axis=1)
        m = _bmm_nt3(bb[0], wide_re) - _bmm_nt3(bb[1], wide_im)
        mq_ref[:, s * blk:(s + 1) * blk, :] = m.astype(BF16)


def _s5_prep_quads(are_q, aim_q, ldt_q, bre_q, bim_q, cre_q, cim_q):
    w = lambda r, c_: jax.ShapeDtypeStruct((S5_QUADS, r, c_), BF16)
    return pl.pallas_call(
        _s5prepq_kernel,
        out_shape=(w(QUAD_IN, 2 * QUAD_LANES), w(2 * QUAD_LANES, QUAD_IN), w(QUAD_IN, QUAD_IN)),
        compiler_params=pltpu.CompilerParams(vmem_limit_bytes=VMEM_LIMIT),
        name="s5_prep_quads",
    )(are_q, aim_q, ldt_q, bre_q, bim_q, cre_q, cim_q)


def _inproj_kernel(x_ref, g_ref, w_ref, qkv_ref, z_ref, u_ref, ba_ref):
    h = _rms(x_ref[...], g_ref[...]).astype(BF16)
    o1 = QKV_WIDTH
    o2 = o1 + DN_WIDTH
    o3 = o2 + S5_WIDTH
    qkv_ref[...] = _dot(h, w_ref[:, :o1])
    z_ref[...] = _dot(h, w_ref[:, o1:o2])
    u_ref[...] = _dot(h, w_ref[:, o2:o3])
    ba_ref[...] = _dot(h, w_ref[:, o3:])


def _in_proj(x2d, g, w, tm):
    n = x2d.shape[0]
    row = lambda w_: pl.BlockSpec((tm, w_), lambda i: (i, 0))
    return pl.pallas_call(
        _inproj_kernel,
        grid=(n // tm,),
        in_specs=[row(D_MODEL), _const_spec((1, D_MODEL)), _const_spec((D_MODEL, IN_PAD))],
        out_specs=(row(QKV_WIDTH), row(DN_WIDTH), row(S5_WIDTH), row(LANES)),
        out_shape=(jax.ShapeDtypeStruct((n, QKV_WIDTH), F32),
                   jax.ShapeDtypeStruct((n, DN_WIDTH), F32),
                   jax.ShapeDtypeStruct((n, S5_WIDTH), F32),
                   jax.ShapeDtypeStruct((n, LANES), F32)),
        compiler_params=_params(1),
        name="in_proj",
    )(x2d, g, w)


def _stack_heads(a):
    dh = DN_HEAD_DIM
    return jnp.concatenate([a[:, :, h * dh:(h + 1) * dh] for h in range(DN_HEADS)], axis=1)


def _unstack_heads(a, c):
    return jnp.concatenate([a[:, h * c:(h + 1) * c, :] for h in range(DN_HEADS)], axis=2)


def _dn_kernel(qkv_ref, z_ref, ba_ref, cache_ref, s0_ref, cw_ref, gp_ref, ng_ref, o_ref, state_ref, halo_s,
               *, nb, tt, c):
    first = pl.program_id(1) == 0
    stash = _dn_phase_a(first, qkv_ref, z_ref, ba_ref, cache_ref, cw_ref, gp_ref, halo_s, nb=nb, tt=tt, c=c)
    for step in _dn_phase_b(first, stash, s0_ref, ng_ref, o_ref, state_ref, nb=nb, tt=tt, c=c):
        step()


def _dn_phase_a(first, qkv_ref, z_ref, ba_ref, cache_ref, cw_ref, gp_ref, halo_s, *, nb, tt, c, side=()):
    nh = DN_HEADS
    dh = DN_HEAD_DIM
    nck = tt // c
    npb = nb * nck
    hc = nh * c

    @pl.when(first)
    def _():
        halo_s[...] = cache_ref[...]

    side = list(side)

    def side_work(n=1):
        for _ in range(n):
            if side:
                side.pop(0)()

    stack = _stack_heads
    unstack = functools.partial(_unstack_heads, c=c)

    first_row = lax.broadcasted_iota(jnp.int32, (SUBLANES, MXU_DIM), 0) == 0

    def shift_rows(y, carried):
        rolled = pltpu.roll(y, 1, axis=1)
        top = jnp.where(first_row[None], carried, rolled[:, :SUBLANES])
        return jnp.concatenate([top, rolled[:, SUBLANES:]], axis=1)

    conv_parts = []
    for lo in range(0, QKV_WIDTH, MXU_DIM):
        side_work(2)
        hi = lo + MXU_DIM
        cw = cw_ref[:, lo:hi]
        x = qkv_ref[:, :, lo:hi]
        prev = halo_s[:, :, lo:hi]
        acc = x * cw[0:1]
        carried = prev[:, SUBLANES - 1:] * cw[0:1]
        for j in range(1, DN_CONV):
            acc = shift_rows(acc, carried) + x * cw[j:j + 1]
            if j < DN_CONV - 1:
                carried = prev[:, SUBLANES - 1 - j:SUBLANES - j] * cw[0:1]
                for i in range(1, j + 1):
                    carried = carried + prev[:, SUBLANES - 1 - j + i:SUBLANES - j + i] * cw[i:i + 1]
        halo_s[:, :, lo:hi] = x[:, tt - SUBLANES:]
        conv_parts.append((acc * jax.nn.sigmoid(acc)).reshape(npb, c, hi - lo))
    per = DN_WIDTH // MXU_DIM
    q, k, v = [stack(jnp.concatenate(conv_parts[i * per:(i + 1) * per], axis=2)) for i in range(3)]
    side_work()
    q = q * (lax.rsqrt(jnp.sum(q * q, axis=-1, keepdims=True) + EPS) * (dh ** -0.5))
    k = k * lax.rsqrt(jnp.sum(k * k, axis=-1, keepdims=True) + EPS)

    side_work()
    ba = ba_ref[...].reshape(nb * tt, LANES)
    beta = jax.nn.sigmoid(ba).reshape(npb, c, LANES)
    g2 = -jnp.exp(gp_ref[0:1, :]) * jax.nn.softplus(ba + gp_ref[1:2, :])
    row_in_chunk = lax.broadcasted_iota(jnp.int32, (nb * tt, LANES), 0) & (c - 1)
    d = 1
    while d < c:
        g2 = g2 + jnp.where(row_in_chunk >= d, pltpu.roll(g2, d, axis=0), 0.0)
        d *= 2
    g = g2.reshape(npb, c, LANES)

    g_cols = [g[:, :, BA_LANE_G + h:BA_LANE_G + h + 1] for h in range(nh)]
    b_cols = [beta[:, :, h:h + 1] for h in range(nh)]
    g_col = jnp.concatenate(g_cols, axis=1)
    b_col = jnp.concatenate(b_cols, axis=1)
    g_last = jnp.concatenate([jnp.broadcast_to(gc[:, c - 1:c, :], (npb, c, 1)) for gc in g_cols], axis=1)

    hp = min(nh, LANES // c)
    pieces = []
    for h0 in range(0, nh, hp):
        slab = jnp.concatenate(
            [(g2 if hh == 0 else pltpu.roll(g2, LANES - hh, axis=1)).reshape(npb, c, LANES)
             for hh in range(h0, h0 + hp)], axis=1)
        if hp * c < LANES:
            slab = jnp.concatenate([slab, jnp.zeros((npb, LANES - hp * c, LANES), F32)], axis=1)
        rows = [slab[p].T[BA_LANE_G:BA_LANE_G + 1, :hp * c] for p in range(npb)]
        pieces.append(jnp.stack(rows, axis=0))
    g_row = jnp.concatenate(pieces, axis=2)

    ri = lax.broadcasted_iota(jnp.int32, (c, hc), 0)
    lane = lax.broadcasted_iota(jnp.int32, (c, hc), 1)
    cj = lane & (c - 1)
    causal = (ri >= cj)[None]
    strict = (ri > cj)[None]

    def cat_from_cols(cols):
        out = jnp.broadcast_to(cols[nh - 1], (npb, c, hc))
        for h in range(nh - 2, -1, -1):
            out = jnp.where((lane < (h + 1) * c)[None], jnp.broadcast_to(cols[h], (npb, c, hc)), out)
        return out

    decay = jnp.exp(jnp.where(causal, cat_from_cols(g_cols) - g_row, -jnp.inf))

    side_work()
    kb = k.astype(BF16)
    br = lax.broadcasted_iota(jnp.int32, (hc, nh * dh), 0)
    bl = lax.broadcasted_iota(jnp.int32, (hc, nh * dh), 1)
    head_of_row = sum(jnp.where(br >= h * c, 1, 0) for h in range(1, nh))
    head_of_lane = sum(jnp.where(bl >= h * dh, 1, 0) for h in range(1, nh))
    k_bd = jnp.where((head_of_row == head_of_lane)[None], jnp.concatenate([kb] * nh, axis=2), 0.0)
    qk_lhs = jnp.concatenate([unstack(q), unstack(k)], axis=1).astype(BF16)
    qkk = _bmm_nt(qk_lhs, k_bd)
    qk = qkk[:, :c]
    kk = qkk[:, c:]

    sr = lax.broadcasted_iota(jnp.int32, (hc, hc), 0)
    sl = lax.broadcasted_iota(jnp.int32, (hc, hc), 1)
    shift = c.bit_length() - 1
    same_head = ((sr >> shift) == (sl >> shift))[None]

    def bd(m):
        return jnp.where(same_head, jnp.concatenate([m] * nh, axis=1), 0.0)

    lm = jnp.where(strict, cat_from_cols(b_cols) * kk * decay, 0.0)
    tinv = jnp.where((ri == cj)[None], 1.0, 0.0) - jnp.where(((ri >> 1) == (cj >> 1))[None], lm, 0.0)
    s = 2
    while s < c:
        sh = s.bit_length()
        lower_left = ((ri >> sh) == (cj >> sh)) & ((ri & s) != 0) & ((cj & s) == 0)
        side_work()
        a_off = jnp.where(lower_left[None], lm, 0.0).astype(BF16)
        xm = _bmm(tinv.astype(BF16), bd(a_off))
        tinv = tinv - _bmm(xm.astype(BF16), bd(tinv.astype(BF16)))
        s *= 2

    e_g = jnp.exp(g_col)
    rhs = jnp.concatenate([v * b_col, k * (b_col * e_g)], axis=2).astype(BF16)
    sol = _bmm(bd(tinv.astype(BF16)), rhs)
    sol_v = sol[:, :, :dh]
    sol_k = sol[:, :, dh:]
    w_qe = jnp.concatenate([sol_k, q * e_g], axis=2).astype(BF16)
    kd = (k * jnp.exp(g_last - g_col)).astype(BF16)
    qkd_bd = bd((qk * decay).astype(BF16))
    s_decay = jnp.exp(g_last)
    zs = stack(z_ref[...].reshape(npb, c, DN_WIDTH))
    side_work(len(side))
    return sol_v, w_qe, kd, qkd_bd, s_decay, zs


def _dn_phase_b(first, stash, s0_ref, ng_ref, o_ref, state_ref, *, nb, tt, c):
    sol_v, w_qe, kd, qkd_bd, s_decay, zs = stash
    nh = DN_HEADS
    dh = DN_HEAD_DIM
    nck = tt // c
    outs = {}
    pending = {}

    def init():
        @pl.when(first)
        def _():
            state_ref[...] = s0_ref[...]

    def outputs(ck):
        def run():
            for b in range(nb):
                p = b * nck + ck
                ws, qs = [], []
                states = [state_ref[b, h] for h in range(nh)]
                for h in range(nh):
                    sb = states[h].astype(BF16)
                    rows = w_qe[p, h * c:(h + 1) * c, :]
                    ws.append(_dot(rows[:, :dh], sb))
                    qs.append(_dot(rows[:, dh:], sb))
                u = sol_v[p] - jnp.concatenate(ws, axis=0)
                ub = u.astype(BF16)
                outs[p] = jnp.concatenate(qs, axis=0) + _dot(qkd_bd[p], ub)
                pending[p] = (states, ub)
        return run

    def update(ck):
        def run():
            for b in range(nb):
                p = b * nck + ck
                states, ub = pending.pop(p)
                for h in range(nh):
                    r0 = h * c
                    state_ref[b, h] = (states[h] * s_decay[p, r0:r0 + 1, :]
                                       + _dot_tn(kd[p, r0:r0 + c, :], ub[r0:r0 + c, :]))
        return run

    def finish():
        o = jnp.stack([outs[p] for p in range(nb * nck)], axis=0)
        z = zs[...]
        o = _rms(o, ng_ref[...]) * (z * jax.nn.sigmoid(z))
        o_ref[...] = _unstack_heads(o, c).reshape(nb, tt, DN_WIDTH).astype(BF16)

    return [init] + [f(ck) for ck in range(nck) for f in (outputs, update)] + [finish]


def _deltanet(qkv, z, ba, cache8, s0, cw, gp, ng, nb, tt, c):
    b, l, _ = qkv.shape
    seq = lambda w_: pl.BlockSpec((nb, tt, w_), lambda i, j: (i, j, 0))
    per_b = lambda *s: pl.BlockSpec((nb,) + s, lambda i, j: (i,) + (0,) * len(s))
    return pl.pallas_call(
        functools.partial(_dn_kernel, nb=nb, tt=tt, c=c),
        grid=(b // nb, l // tt),
        in_specs=[seq(QKV_WIDTH), seq(DN_WIDTH), seq(LANES), per_b(SUBLANES, QKV_WIDTH),
                  per_b(DN_HEADS, DN_HEAD_DIM, DN_HEAD_DIM),
                  _const_spec((SUBLANES, QKV_WIDTH)), _const_spec((SUBLANES, LANES)),
                  _const_spec((1, DN_HEAD_DIM))],
        out_specs=(seq(DN_WIDTH), per_b(DN_HEADS, DN_HEAD_DIM, DN_HEAD_DIM)),
        out_shape=(jax.ShapeDtypeStruct((b, l, DN_WIDTH), BF16),
                   jax.ShapeDtypeStruct((b, DN_HEADS, DN_HEAD_DIM, DN_HEAD_DIM), F32)),
        scratch_shapes=[pltpu.VMEM((nb, SUBLANES, QKV_WIDTH), F32)],
        compiler_params=_params(2),
        name="deltanet",
    )(qkv, z, ba, cache8, s0, cw, gp, ng)


def _mixer_kernel(x_ref, g_ref, w_ref, cache_ref, s0_ref, cw_ref, gp_ref, ng_ref,
                  u_ref, o_ref, state_ref, tail_ref, p_cur, p_nxt, halo_s, *, nb, tt, c):
    i = pl.program_id(0)
    o1 = QKV_WIDTH
    o2 = o1 + DN_WIDTH
    o3 = o2 + S5_WIDTH

    @pl.when(i == 0)
    def _():
        p_cur[...] = jnp.zeros(p_cur.shape, F32)

    h = _rms(x_ref[...].reshape(nb * tt, D_MODEL), g_ref[...]).astype(BF16)

    def project(lo, hi):
        def run():
            res = _dot(h, w_ref[:, lo:hi]).reshape(nb, tt, hi - lo)
            if o2 <= lo < o3:
                u_ref[:, :, lo - o2:hi - o2] = res
            else:
                p_nxt[:, :, lo:hi] = res
        return run

    side = [project(lo, min(lo + MXU_DIM, IN_PAD)) for lo in range(0, IN_PAD, MXU_DIM)]
    first = i <= 1
    stash = _dn_phase_a(first, p_cur.at[:, :, 0:o1], p_cur.at[:, :, o1:o2], p_cur.at[:, :, o3:IN_PAD],
                        cache_ref, cw_ref, gp_ref, halo_s, nb=nb, tt=tt, c=c, side=side)
    for step in _dn_phase_b(first, stash, s0_ref, ng_ref, o_ref, state_ref, nb=nb, tt=tt, c=c):
        step()
    tail_ref[...] = halo_s[...]
    for lo, hi in ((0, o1), (o1, o2), (o3, IN_PAD)):
        p_cur[:, :, lo:hi] = p_nxt[:, :, lo:hi]


def _mixer(x, g, w, cache8, s0, cw, gp, ng, tt, c):
    nb, l, _ = x.shape
    nt = l // tt
    cur = lambda w_: pl.BlockSpec((nb, tt, w_), lambda i: (0, jnp.minimum(i, nt - 1), 0))
    prev = lambda w_: pl.BlockSpec((nb, tt, w_), lambda i: (0, jnp.maximum(i - 1, 0), 0))
    return pl.pallas_call(
        functools.partial(_mixer_kernel, nb=nb, tt=tt, c=c),
        grid=(nt + 1,),
        in_specs=[cur(D_MODEL), _const_spec((1, D_MODEL)), _const_spec((D_MODEL, IN_PAD)),
                  _const_spec((nb, SUBLANES, QKV_WIDTH)), _const_spec((nb, DN_HEADS, DN_HEAD_DIM, DN_HEAD_DIM)),
                  _const_spec((SUBLANES, QKV_WIDTH)), _const_spec((SUBLANES, LANES)),
                  _const_spec((1, DN_HEAD_DIM))],
        out_specs=(cur(S5_WIDTH), prev(DN_WIDTH),
                   pl.BlockSpec((nb, DN_HEADS, DN_HEAD_DIM, DN_HEAD_DIM), lambda i: (0, 0, 0, 0)),
                   pl.BlockSpec((nb, SUBLANES, QKV_WIDTH), lambda i: (0, 0, 0))),
        out_shape=(jax.ShapeDtypeStruct((nb, l, S5_WIDTH), F32),
                   jax.ShapeDtypeStruct((nb, l, DN_WIDTH), BF16),
                   jax.ShapeDtypeStruct((nb, DN_HEADS, DN_HEAD_DIM, DN_HEAD_DIM), F32),
                   jax.ShapeDtypeStruct((nb, SUBLANES, QKV_WIDTH), F32)),
        scratch_shapes=[pltpu.VMEM((nb, tt, IN_PAD), F32), pltpu.VMEM((nb, tt, IN_PAD), F32),
                        pltpu.VMEM((nb, SUBLANES, QKV_WIDTH), F32)],
        compiler_params=_params(1),
        name="mixer",
    )(x, g, w, cache8, s0, cw, gp, ng)


def _s5c_kernel(u_ref, x0re_ref, x0im_ref, pq_ref, qq_ref, mq_ref, tab_ref, dv_ref, gw_ref, gb_ref, ng_ref,
                o_ref, fre_ref, fim_ref, xr_s, xi_s, car_s, u_s, o_s, *, rows, seg):
    t = pl.program_id(1)
    carry = seg == SUBLANES
    ncol = S5_WIDTH // LANES
    for k in range(ncol):
        u_s[k] = u_ref[0, :, k * LANES:(k + 1) * LANES]
    us = [jnp.concatenate([u_s[k, pl.ds(s, rows, stride=S5_SLOTS), :] for k in range(ncol)], axis=1)
          for s in range(S5_SLOTS)]
    ubs = [a.astype(BF16) for a in us]
    blk = 4 * S5_GROUP
    ql = QUAD_LANES

    if carry:
        @pl.when(t == 0)
        def _():
            car_s[0:1, :] = x0re_ref[0]
            car_s[1:2, :] = x0im_ref[0]

    uq = [jnp.concatenate([ubs[s][:, n * blk:(n + 1) * blk] for s in range(S5_SLOTS)], axis=1)
          for n in range(S5_QUADS)]
    for n in range(S5_QUADS):
        inc = _dot(uq[n], pq_ref[n])
        xr_s[:, n * ql:(n + 1) * ql] = inc[:, :ql]
        xi_s[:, n * ql:(n + 1) * ql] = inc[:, ql:]

    seg_row = lax.broadcasted_iota(jnp.int32, (SUBLANES, S5_LANES), 0) & (seg - 1)

    def block_body(rb, cin):
        r0 = pl.multiple_of(rb * SUBLANES, SUBLANES)
        if carry:
            c_re, c_im = cin
        else:
            c_re = x0re_ref[0, pl.ds(r0, SUBLANES), :]
            c_im = x0im_ref[0, pl.ds(r0, SUBLANES), :]
        xr = xr_s[pl.ds(r0, SUBLANES), :]
        xi = xi_s[pl.ds(r0, SUBLANES), :]
        for lvl, d in enumerate(_scan_levels(seg)):
            m_re = tab_ref[16 + 16 * lvl:24 + 16 * lvl, :]
            m_im = tab_ref[24 + 16 * lvl:32 + 16 * lvl, :]
            sr = pltpu.roll(xr, d, axis=0)
            si = pltpu.roll(xi, d, axis=0)
            xr, xi = xr + (m_re * sr - m_im * si), xi + (m_re * si + m_im * sr)
        p_re = tab_ref[0:8, :]
        p_im = tab_ref[8:16, :]
        xr, xi = xr + (p_re * c_re - p_im * c_im), xi + (p_re * c_im + p_im * c_re)
        xr_s[pl.ds(r0, SUBLANES), :] = jnp.where(seg_row == 0, c_re, pltpu.roll(xr, 1, axis=0))
        xi_s[pl.ds(r0, SUBLANES), :] = jnp.where(seg_row == 0, c_im, pltpu.roll(xi, 1, axis=0))
        if carry:
            return xr[SUBLANES - 1:, :], xi[SUBLANES - 1:, :]
        fre_ref[0, pl.ds(r0, SUBLANES), :] = xr
        fim_ref[0, pl.ds(r0, SUBLANES), :] = xi
        return cin

    if carry:
        c_re, c_im = lax.fori_loop(0, rows // SUBLANES, block_body, (car_s[0:1, :], car_s[1:2, :]))
        car_s[0:1, :] = c_re
        car_s[1:2, :] = c_im
        fre_ref[0] = c_re
        fim_ref[0] = c_im
    else:
        lax.fori_loop(0, rows // SUBLANES, block_body, 0)

    ys = []
    for n in range(S5_QUADS):
        xs = jnp.concatenate([xr_s[:, n * ql:(n + 1) * ql], xi_s[:, n * ql:(n + 1) * ql]], axis=1).astype(BF16)
        half = QUAD_IN // 2
        intra = jnp.concatenate([_dot(uq[n][:, :half], mq_ref[n, :half, :half]), _dot(uq[n], mq_ref[n, :, half:])],
                                axis=1)
        ys.append(intra + _dot(xs, qq_ref[n]))
    for s in range(S5_SLOTS):
        y = jnp.concatenate([ys[n][:, s * blk:(s + 1) * blk] for n in range(S5_QUADS)], axis=1)
        y = jax.nn.gelu(y + dv_ref[...] * us[s]).astype(BF16)
        gl = _dot(y, gw_ref[...]) + gb_ref[...]
        o = gl[:, :S5_WIDTH] * jax.nn.sigmoid(gl[:, S5_WIDTH:])
        o = _rms(o, ng_ref[...])
        for k in range(ncol):
            o_s[k, pl.ds(s, rows, stride=S5_SLOTS), :] = o[:, k * LANES:(k + 1) * LANES]
    o_ref[0] = jnp.concatenate([o_s[k] for k in range(ncol)], axis=1).astype(BF16)


def _s5_chunked(u, x0re, x0im, pq, qq, mq, tabc, dv, gw, gb, ng, rows, seg):
    b, l, _ = u.shape
    tt = rows * S5_SLOTS
    seq = pl.BlockSpec((1, tt, S5_WIDTH), lambda i, j: (i, j, 0))
    if seg == SUBLANES:
        st = pl.BlockSpec((1, 1, S5_LANES), lambda i, j: (i, 0, 0))
    else:
        st = pl.BlockSpec((1, rows, S5_LANES), lambda i, j: (i, j, 0))
    return pl.pallas_call(
        functools.partial(_s5c_kernel, rows=rows, seg=seg),
        grid=(b, l // tt),
        in_specs=[seq, st, st,
                  _const_spec((S5_QUADS, QUAD_IN, 2 * QUAD_LANES)),
                  _const_spec((S5_QUADS, 2 * QUAD_LANES, QUAD_IN)),
                  _const_spec((S5_QUADS, QUAD_IN, QUAD_IN)), _const_spec((64, S5_LANES)),
                  _const_spec((1, S5_WIDTH)), _const_spec((S5_WIDTH, 2 * S5_WIDTH)),
                  _const_spec((1, 2 * S5_WIDTH)), _const_spec((1, S5_WIDTH))],
        out_specs=(seq, st, st),
        out_shape=(jax.ShapeDtypeStruct((b, l, S5_WIDTH), BF16),
                   jax.ShapeDtypeStruct(x0re.shape, F32),
                   jax.ShapeDtypeStruct(x0re.shape, F32)),
        scratch_shapes=[pltpu.VMEM((rows, S5_LANES), F32), pltpu.VMEM((rows, S5_LANES), F32),
                        pltpu.VMEM((SUBLANES, S5_LANES), F32),
                        pltpu.VMEM((S5_WIDTH // LANES, tt, LANES), F32),
                        pltpu.VMEM((S5_WIDTH // LANES, tt, LANES), F32)],
        compiler_params=_params(2),
        name="s5_chunked",
    )(u, x0re, x0im, pq, qq, mq, tabc, dv, gw, gb, ng)


def _ffn_kernel(x_ref, odn_ref, os5_ref, prev_ref, woa_ref, wob_ref, n2_ref, wu_ref,
                fcw_ref, fcb_ref, wd_ref, fg_ref, y_ref, tail_ref, halo_s, *, tm, ls, carry):
    t = pl.program_id(1)
    x1 = x_ref[0] + _dot(odn_ref[0], woa_ref[...]) + _dot(os5_ref[0], wob_ref[...])
    h2 = _rms(x1, n2_ref[...]).astype(BF16)
    if carry:
        @pl.when(t == 0)
        def _():
            halo_s[...] = prev_ref[0]

    down = None
    for lo in range(0, D_FF, FF_CHUNK):
        hi = min(lo + FF_CHUNK, D_FF)
        gate = _dot(h2, wu_ref[:, lo:hi])
        val = _dot(h2, wu_ref[:, D_FF + lo:D_FF + hi])
        if carry:
            xx = jnp.concatenate([halo_s[:, lo:hi], gate], axis=0)
            g2 = xx[SUBLANES - 2:SUBLANES - 2 + tm]
            g1 = xx[SUBLANES - 1:SUBLANES - 1 + tm]
            halo_s[:, lo:hi] = gate[tm - SUBLANES:, :]
            tail_ref[0, :, lo:hi] = gate[tm - SUBLANES:, :]
        else:
            r = lax.broadcasted_iota(jnp.int32, (tm, hi - lo), 0) & (ls - 1)
            prev = prev_ref[0, :, lo:hi]
            g1 = jnp.where(r < 1, pltpu.roll(prev, tm - 1, axis=0), pltpu.roll(gate, 1, axis=0))
            g2 = jnp.where(r < 2, prev, pltpu.roll(gate, 2, axis=0))
            tail_ref[0, :, lo:hi] = gate
        cw = fcw_ref[:, lo:hi]
        pre = g2 * cw[0:1] + g1 * cw[1:2] + gate * cw[2:3] + fcb_ref[:, lo:hi]
        act = pre * jax.nn.sigmoid(pre) * val
        part = _dot(act.astype(BF16), wd_ref[lo:hi, :])
        down = part if down is None else down + part
    y_ref[0] = _rms(x1 + down, fg_ref[...])


def _ffn(x, odn, os5, prev, woa, wob, n2, wu, fcw, fcb, wd, fg, tm, ls, carry):
    b, l, _ = x.shape
    seq = lambda w_: pl.BlockSpec((1, tm, w_), lambda i, j: (i, j, 0))
    if carry:
        prev_spec = pl.BlockSpec((1, SUBLANES, D_FF), lambda i, j: (i, 0, 0))
        tail_spec = pl.BlockSpec((1, SUBLANES, D_FF), lambda i, j: (i, 0, 0))
        tail_shape = jax.ShapeDtypeStruct((b, SUBLANES, D_FF), F32)
    else:
        prev_spec = seq(D_FF)
        tail_spec = seq(D_FF)
        tail_shape = jax.ShapeDtypeStruct((b, l, D_FF), F32)
    return pl.pallas_call(
        functools.partial(_ffn_kernel, tm=tm, ls=ls, carry=carry),
        grid=(b, l // tm),
        in_specs=[seq(D_MODEL), seq(DN_WIDTH), seq(S5_WIDTH), prev_spec,
                  _const_spec((DN_WIDTH, D_MODEL)), _const_spec((S5_WIDTH, D_MODEL)),
                  _const_spec((1, D_MODEL)), _const_spec((D_MODEL, 2 * D_FF)),
                  _const_spec((SUBLANES, D_FF)), _const_spec((1, D_FF)), _const_spec((D_FF, D_MODEL)),
                  _const_spec((1, D_MODEL))],
        out_specs=(seq(D_MODEL), tail_spec),
        out_shape=(jax.ShapeDtypeStruct((b, l, D_MODEL), F32), tail_shape),
        scratch_shapes=[pltpu.VMEM((SUBLANES, D_FF), F32)],
        compiler_params=_params(2),
        name="ffn",
    )(x, odn, os5, prev, woa, wob, n2, wu, fcw, fcb, wd, fg)


def _pad_rows_top(a, rows):
    return jnp.pad(a, ((0, 0), (rows - a.shape[1], 0), (0, 0)))


def _quad_blocks(m):
    same = jnp.eye(4, dtype=bool)[None, :, None, :, None]
    blocks = jnp.where(same, m.reshape(S5_QUADS, 4, S5_GROUP, 1, S5_STATE), 0.0)
    return blocks.reshape(S5_QUADS, 4 * S5_GROUP, QUAD_LANES)


def _trunk(x, conv_dn, s_dn, s5_re, s5_im, conv_ffn, w, prompt):
    b, l, _ = x.shape
    n = b * l
    c = CHUNK if l % CHUNK == 0 else l
    assert l >= 2 * SUBLANES, "sequence shorter than two row tiles"
    cache8 = _pad_rows_top(conv_dn, SUBLANES)
    if prompt:
        u, o_dn, s_dn_new, tail = _mixer(x, w['n1'], w['w_in'], cache8, s_dn, w['dn_cw'], w['dn_gp'], w['dn_g'],
                                         tt=min(l, 128), c=c)
        conv_dn_new = tail[:, SUBLANES - (DN_CONV - 1):]
    else:
        qkv, z, u, ba = _in_proj(x.reshape(n, D_MODEL), w['n1'], w['w_in'], min(n, 512))
        qkv = qkv.reshape(b, l, QKV_WIDTH)
        o_dn, s_dn_new = _deltanet(qkv, z.reshape(b, l, DN_WIDTH), ba.reshape(b, l, LANES), cache8, s_dn,
                                   w['dn_cw'], w['dn_gp'], w['dn_g'], nb=min(b, 8), tt=l, c=c)
        conv_dn_new = qkv[:, l - (DN_CONV - 1):]

    s5_args = (w['pq'], w['qq'], w['mq'], w['tab_long' if prompt else 'tab_short'], w['dv'], w['glu_w'],
               w['glu_b'], w['s5_g'])
    if prompt:
        x0re = s5_re.reshape(b, 1, S5_LANES)
        x0im = s5_im.reshape(b, 1, S5_LANES)
        o_s5, fre, fim = _s5_chunked(u.reshape(b, l, S5_WIDTH), x0re, x0im, *s5_args,
                                     rows=min(l // S5_SLOTS, 128), seg=SUBLANES)
    else:
        seg = l // S5_SLOTS
        rep = lambda s: jnp.repeat(s.reshape(b, S5_LANES), seg, axis=0).reshape(1, b * seg, S5_LANES)
        o_s5, fre, fim = _s5_chunked(u.reshape(1, n, S5_WIDTH), rep(s5_re), rep(s5_im), *s5_args,
                                     rows=b * seg, seg=seg)
        o_s5 = o_s5.reshape(b, l, S5_WIDTH)
        fre = fre.reshape(b, seg, S5_LANES)[:, seg - 1]
        fim = fim.reshape(b, seg, S5_LANES)[:, seg - 1]
    s5_re_new = fre.reshape(b, S5_GROUPS, S5_STATE)
    s5_im_new = fim.reshape(b, S5_GROUPS, S5_STATE)

    ffn_args = (w['w_out_a'], w['w_out_b'], w['n2'], w['w_up'], w['fcw'], w['fcb'],
                w['w_down'], w['fg'])
    if prompt:
        prev = _pad_rows_top(conv_ffn, SUBLANES)
        y, tail = _ffn(x, o_dn, o_s5, prev, *ffn_args, tm=512, ls=512, carry=True)
        conv_ffn_new = tail[:, SUBLANES - (FFN_CONV - 1):]
    else:
        prev = jnp.pad(conv_ffn, ((0, 0), (0, l - (FFN_CONV - 1)), (0, 0))).reshape(1, n, D_FF)
        y, tail = _ffn(x.reshape(1, n, D_MODEL), o_dn.reshape(1, n, DN_WIDTH), o_s5.reshape(1, n, S5_WIDTH),
                       prev, *ffn_args, tm=min(n, 128), ls=l, carry=False)
        y = y.reshape(b, l, D_MODEL)
        conv_ffn_new = tail.reshape(b, l, D_FF)[:, l - (FFN_CONV - 1):]
    return y, (conv_dn_new[None], s_dn_new[None], s5_re_new[None], s5_im_new[None], conv_ffn_new[None])


def kernel(x_prompt, x_sample, cache_dn_conv, state_dn, state_s5_re, state_s5_im, cache_ffn_conv, norm1_g, w_in, dn_conv_w, dn_A_log, dn_dt_bias, dn_norm_g, s5_A_re, s5_A_im, s5_log_dt, s5_B_re, s5_B_im, s5_C_re, s5_C_im, s5_D, s5_glu_w, s5_glu_b, s5_norm_g, w_out, norm2_g, w_up, ffn_conv_w, ffn_conv_b, w_down, final_norm_g):
    assert w_in.shape[0] == 1, "single-layer trunk"
    o1 = QKV_WIDTH
    o2 = o1 + DN_WIDTH
    o4 = o2 + 2 * DN_HEADS
    wi = w_in[0]
    w_in_r = jnp.concatenate(
        [wi[:, :o2], wi[:, o4:], wi[:, o2:o4], jnp.zeros((D_MODEL, LANES - 2 * DN_HEADS), wi.dtype)],
        axis=1).astype(BF16)
    lane_pad = lambda v: jnp.pad(v, (BA_LANE_G, LANES - BA_LANE_G - DN_HEADS))
    ldt_lanes = jnp.repeat(s5_log_dt[0], S5_STATE)
    seg_short = x_sample.shape[1] // S5_SLOTS
    assert x_sample.shape[1] % S5_SLOTS == 0 and seg_short in (1, 2, 4), "sample sequences of 4, 8 or 16 rows"
    lane_row = lambda v: v.reshape(1, S5_LANES)
    tab_long, tab_short = [_s5_prep(lane_row(s5_A_re[0]), lane_row(s5_A_im[0]), lane_row(ldt_lanes), seg=s)
                           for s in (SUBLANES, seg_short)]
    quad_row = lambda v: v.reshape(S5_QUADS, 1, QUAD_LANES)
    pq, qq, mq = _s5_prep_quads(quad_row(s5_A_re[0]), quad_row(s5_A_im[0]), quad_row(ldt_lanes),
                                _quad_blocks(s5_B_re[0].transpose(0, 2, 1)), _quad_blocks(s5_B_im[0].transpose(0, 2, 1)),
                                _quad_blocks(s5_C_re[0]), _quad_blocks(s5_C_im[0]))
    w = {
        'n1': norm1_g, 'w_in': w_in_r,
        'dn_cw': jnp.pad(dn_conv_w[0], ((0, SUBLANES - DN_CONV), (0, 0))),
        'dn_gp': jnp.pad(jnp.stack([lane_pad(dn_A_log[0]), lane_pad(dn_dt_bias[0])]), ((0, SUBLANES - 2), (0, 0))),
        'dn_g': dn_norm_g,
        'tab_long': tab_long, 'tab_short': tab_short, 'pq': pq, 'qq': qq, 'mq': mq,
        'dv': s5_D, 'glu_w': s5_glu_w[0].astype(BF16), 'glu_b': s5_glu_b, 's5_g': s5_norm_g,
        'w_out_a': w_out[0, :DN_WIDTH].astype(BF16), 'w_out_b': w_out[0, DN_WIDTH:].astype(BF16),
        'n2': norm2_g, 'w_up': w_up[0].astype(BF16),
        'fcw': jnp.pad(ffn_conv_w[0], ((0, SUBLANES - FFN_CONV), (0, 0))), 'fcb': ffn_conv_b,
        'w_down': w_down[0].astype(BF16), 'fg': final_norm_g.reshape(1, D_MODEL),
    }
    bp = x_prompt.shape[0]
    zeros = lambda *s: jnp.zeros(s, F32)
    y_p, st_p = _trunk(x_prompt, zeros(bp, DN_CONV - 1, QKV_WIDTH), zeros(bp, DN_HEADS, DN_HEAD_DIM, DN_HEAD_DIM),
                       zeros(bp, S5_GROUPS, S5_STATE), zeros(bp, S5_GROUPS, S5_STATE),
                       zeros(bp, FFN_CONV - 1, D_FF), w, prompt=True)
    y_s, st_s = _trunk(x_sample, cache_dn_conv[0], state_dn[0], state_s5_re[0], state_s5_im[0],
                       cache_ffn_conv[0], w, prompt=False)
    return (y_p, y_s) + st_p + st_s
```

```python
import functools

import jax
import jax.numpy as jnp
from jax import lax
from jax.experimental import pallas as pl
from jax.experimental.pallas import tpu as pltpu

F32 = jnp.float32
BF16 = jnp.bfloat16
EPS = 1e-6

D_MODEL = 1024
DN_HEADS = 4
DN_HEAD_DIM = 128
DN_WIDTH = DN_HEADS * DN_HEAD_DIM
DN_CONV = 4
QKV_WIDTH = 3 * DN_WIDTH
S5_WIDTH = D_MODEL - DN_WIDTH
S5_GROUP = 16
S5_GROUPS = S5_WIDTH // S5_GROUP
S5_STATE = 64
S5_LANES = S5_GROUPS * S5_STATE
S5_SLOTS = 8
S5_QUADS = S5_GROUPS // 4
QUAD_LANES = 4 * S5_STATE
QUAD_IN = S5_SLOTS * 4 * S5_GROUP
D_FF = 2816
FFN_CONV = 3
CHUNK = 64

SUBLANES = 8
LANES = 128
BA_LANE_G = DN_HEADS
IN_PAD = QKV_WIDTH + DN_WIDTH + S5_WIDTH + LANES
MXU_DIM = 256
FF_CHUNK = 6 * MXU_DIM

VMEM_LIMIT = 56 * 1024 * 1024


def _dot(a, b):
    return jnp.dot(a, b, preferred_element_type=F32)


def _dot_tn(a, b):
    return lax.dot_general(a, b, (((0,), (0,)), ((), ())), preferred_element_type=F32)


def _split_bf16(a):
    hi = a.astype(BF16)
    lo = (a - hi.astype(F32)).astype(BF16)
    return hi, lo


def _bmm(a, b):
    return lax.dot_general(a, b, (((2,), (1,)), ((0,), (0,))), preferred_element_type=F32)


def _bmm_nt(a, b):
    return lax.dot_general(a, b, (((2,), (2,)), ((0,), (0,))), preferred_element_type=F32)


def _bmm_nt3(a, b):
    ah, al = _split_bf16(a)
    bh, bl = _split_bf16(b)
    return _bmm_nt(ah, bh) + (_bmm_nt(al, bh) + _bmm_nt(ah, bl))


def _rms(x, g):
    return x * lax.rsqrt(jnp.mean(x * x, axis=-1, keepdims=True) + EPS) * g


def _const_spec(shape):
    nd = len(shape)
    return pl.BlockSpec(shape, lambda *_: (0,) * nd, pipeline_mode=pl.Buffered(1))


def _params(n_axes):
    return pltpu.CompilerParams(dimension_semantics=("arbitrary",) * n_axes,
                                vmem_limit_bytes=VMEM_LIMIT)


def _cmul(a, b):
    return a[0] * b[0] - a[1] * b[1], a[0] * b[1] + a[1] * b[0]


def _zoh(are, aim, ldt):
    dt = jnp.exp(ldt)
    mag = jnp.exp(are * dt)
    ang = aim * dt
    lr = mag * jnp.cos(ang)
    li = mag * jnp.sin(ang)
    den = are * are + aim * aim
    f_re = ((lr - 1.0) * are + li * aim) / den
    f_im = (li * are - (lr - 1.0) * aim) / den
    return (lr, li), (f_re, f_im)


def _write_scan_tables(tab_ref, step, seg):
    pw = [step]
    for _ in range(seg - 1):
        pw.append(_cmul(pw[-1], step))
    shape = (SUBLANES, step[0].shape[-1])
    row = lax.broadcasted_iota(jnp.int32, shape, 0) & (seg - 1)
    zero = jnp.zeros(shape, F32)
    pre, pim = zero, zero
    for r in range(seg):
        pre = jnp.where(row == r, pw[r][0], pre)
        pim = jnp.where(row == r, pw[r][1], pim)
    tab_ref[...] = jnp.zeros(tab_ref.shape, F32)
    tab_ref[0:8, :] = pre
    tab_ref[8:16, :] = pim
    for lvl, d in enumerate(_scan_levels(seg)):
        tab_ref[16 + 16 * lvl:24 + 16 * lvl, :] = jnp.where(row >= d, pw[d - 1][0], zero)
        tab_ref[24 + 16 * lvl:32 + 16 * lvl, :] = jnp.where(row >= d, pw[d - 1][1], zero)


def _scan_levels(seg):
    return [d for d in (1, 2, 4) if d < seg]


def _s5prep_kernel(are_ref, aim_ref, ldt_ref, tab_ref, *, seg):
    lam, _ = _zoh(are_ref[...], aim_ref[...], ldt_ref[...])
    lam_c = lam
    for _ in range(S5_SLOTS - 1):
        lam_c = _cmul(lam_c, lam)
    _write_scan_tables(tab_ref, lam_c, seg)


def _s5_prep(are, aim, ldt, seg):
    return pl.pallas_call(
        functools.partial(_s5prep_kernel, seg=seg),
        out_shape=jax.ShapeDtypeStruct((64, S5_LANES), F32),
        compiler_params=pltpu.CompilerParams(vmem_limit_bytes=VMEM_LIMIT),
        name="s5_prep",
    )(are, aim, ldt)


def _s5prepq_kernel(are_ref, aim_ref, ldt_ref, bre_ref, bim_ref, cre_ref, cim_ref, pq_ref, qq_ref, mq_ref):
    lam, f = _zoh(are_ref[...], aim_ref[...], ldt_ref[...])
    bb = _cmul(f, (bre_ref[...], bim_ref[...]))
    ct = (cre_ref[...], cim_ref[...])
    one = (jnp.ones_like(lam[0]), jnp.zeros_like(lam[0]))
    pw = [one]
    for _ in range(S5_SLOTS):
        pw.append(_cmul(pw[-1], lam))
    blk = 4 * S5_GROUP
    for s in range(S5_SLOTS):
        p_re, p_im = _cmul(pw[S5_SLOTS - 1 - s], bb)
        pq_ref[:, s * blk:(s + 1) * blk, :QUAD_LANES] = p_re.astype(BF16)
        pq_ref[:, s * blk:(s + 1) * blk, QUAD_LANES:] = p_im.astype(BF16)
    cl = [_cmul(pw[e], ct) for e in range(S5_SLOTS + 1)]
    qt_re = jnp.concatenate([cl[t + 1][0] for t in range(S5_SLOTS)], axis=1)
    qt_im = jnp.concatenate([-cl[t + 1][1] for t in range(S5_SLOTS)], axis=1)
    for n in range(S5_QUADS):
        qq_ref[n, :QUAD_LANES, :] = qt_re[n].T.astype(BF16)
        qq_ref[n, QUAD_LANES:, :] = qt_im[n].T.astype(BF16)
    lag_re = jnp.concatenate([cl[tau][0] for tau in range(S5_SLOTS)], axis=1)
    lag_im = jnp.concatenate([cl[tau][1] for tau in range(S5_SLOTS)], axis=1)
    lags = _bmm_nt3(bb[0], lag_re) - _bmm_nt3(bb[1], lag_im)
    for s in range(S5_SLOTS):
        m = lags if s == 0 else jnp.concatenate(
            [jnp.zeros((S5_QUADS, blk, s * blk), F32), lags[:, :, :QUAD_IN - s * blk]], axis=2)
        mq_ref[:, s * blk:(s + 1) * blk, :] = m.astype(BF16)


def _s5_prep_quads(are_q, aim_q, ldt_q, bre_q, bim_q, cre_q, cim_q):
    w = lambda r, c_: jax.ShapeDtypeStruct((S5_QUADS, r, c_), BF16)
    return pl.pallas_call(
        _s5prepq_kernel,
        out_shape=(w(QUAD_IN, 2 * QUAD_LANES), w(2 * QUAD_LANES, QUAD_IN), w(QUAD_IN, QUAD_IN)),
        compiler_params=pltpu.CompilerParams(vmem_limit_bytes=VMEM_LIMIT),
        name="s5_prep_quads",
    )(are_q, aim_q, ldt_q, bre_q, bim_q, cre_q, cim_q)


def _inproj_kernel(x_ref, g_ref, w_ref, qkv_ref, z_ref, u_ref, ba_ref):
    h = _rms(x_ref[...], g_ref[...]).astype(BF16)
    o1 = QKV_WIDTH
    o2 = o1 + DN_WIDTH
    o3 = o2 + S5_WIDTH
    qkv_ref[...] = _dot(h, w_ref[:, :o1])
    z_ref[...] = _dot(h, w_ref[:, o1:o2])
    u_ref[...] = _dot(h, w_ref[:, o2:o3])
    ba_ref[...] = _dot(h, w_ref[:, o3:])


def _in_proj(x2d, g, w, tm):
    n = x2d.shape[0]
    row = lambda w_: pl.BlockSpec((tm, w_), lambda i: (i, 0))
    return pl.pallas_call(
        _inproj_kernel,
        grid=(n // tm,),
        in_specs=[row(D_MODEL), _const_spec((1, D_MODEL)), _const_spec((D_MODEL, IN_PAD))],
        out_specs=(row(QKV_WIDTH), row(DN_WIDTH), row(S5_WIDTH), row(LANES)),
        out_shape=(jax.ShapeDtypeStruct((n, QKV_WIDTH), F32),
                   jax.ShapeDtypeStruct((n, DN_WIDTH), F32),
                   jax.ShapeDtypeStruct((n, S5_WIDTH), F32),
                   jax.ShapeDtypeStruct((n, LANES), F32)),
        compiler_params=_params(1),
        name="in_proj",
    )(x2d, g, w)


def _stack_heads(a):
    dh = DN_HEAD_DIM
    return jnp.concatenate([a[:, :, h * dh:(h + 1) * dh] for h in range(DN_HEADS)], axis=1)


def _unstack_heads(a, c):
    return jnp.concatenate([a[:, h * c:(h + 1) * c, :] for h in range(DN_HEADS)], axis=2)


def _dn_kernel(qkv_ref, z_ref, ba_ref, cache_ref, s0_ref, cw_ref, gp_ref, ng_ref, o_ref, state_ref, halo_s,
               *, nb, tt, c):
    _dn_restart(pl.program_id(1) == 0, cache_ref, s0_ref, halo_s, state_ref)
    stash = _dn_phase_a(qkv_ref, z_ref, ba_ref, cw_ref, gp_ref, halo_s, nb=nb, tt=tt, c=c)
    for step in _dn_phase_b(stash, ng_ref, o_ref, state_ref, nb=nb, tt=tt, c=c):
        step()


def _dn_restart(first, cache_ref, s0_ref, halo_s, state_ref):
    @pl.when(first)
    def _():
        halo_s[...] = cache_ref[...]
        state_ref[...] = s0_ref[...]


def _dn_phase_a(qkv_ref, z_ref, ba_ref, cw_ref, gp_ref, halo_s, *, nb, tt, c, side=()):
    nh = DN_HEADS
    dh = DN_HEAD_DIM
    nck = tt // c
    npb = nb * nck
    hc = nh * c
    side = list(side)

    def side_work(n=1):
        for _ in range(n):
            if side:
                side.pop(0)()

    stack = _stack_heads
    unstack = functools.partial(_unstack_heads, c=c)

    first_row = lax.broadcasted_iota(jnp.int32, (SUBLANES, MXU_DIM), 0) == 0

    def shift_rows(y, carried):
        rolled = pltpu.roll(y, 1, axis=1)
        top = jnp.where(first_row[None], carried, rolled[:, :SUBLANES])
        return jnp.concatenate([top, rolled[:, SUBLANES:]], axis=1)

    conv_parts = []
    for lo in range(0, QKV_WIDTH, MXU_DIM):
        side_work()
        hi = lo + MXU_DIM
        cw = cw_ref[:, lo:hi]
        x = qkv_ref[:, :, lo:hi]
        prev = halo_s[:, :, lo:hi]
        acc = x * cw[0:1]
        carried = prev[:, SUBLANES - 1:] * cw[0:1]
        for j in range(1, DN_CONV):
            acc = shift_rows(acc, carried) + x * cw[j:j + 1]
            if j < DN_CONV - 1:
                carried = prev[:, SUBLANES - 1 - j:SUBLANES - j] * cw[0:1]
                for i in range(1, j + 1):
                    carried = carried + prev[:, SUBLANES - 1 - j + i:SUBLANES - j + i] * cw[i:i + 1]
        halo_s[:, :, lo:hi] = x[:, tt - SUBLANES:]
        conv_parts.append((acc * jax.nn.sigmoid(acc)).reshape(npb, c, hi - lo))
    per = DN_WIDTH // MXU_DIM
    q, k, v = [stack(jnp.concatenate(conv_parts[i * per:(i + 1) * per], axis=2)) for i in range(3)]
    side_work()
    q = q * (lax.rsqrt(jnp.sum(q * q, axis=-1, keepdims=True) + EPS) * (dh ** -0.5))
    k = k * lax.rsqrt(jnp.sum(k * k, axis=-1, keepdims=True) + EPS)

    side_work()
    ba = ba_ref[...].reshape(nb * tt, LANES)
    beta = jax.nn.sigmoid(ba).reshape(npb, c, LANES)
    g2 = -jnp.exp(gp_ref[0:1, :]) * jax.nn.softplus(ba + gp_ref[1:2, :])
    row_in_chunk = lax.broadcasted_iota(jnp.int32, (nb * tt, LANES), 0) & (c - 1)
    d = 1
    while d < c:
        g2 = g2 + jnp.where(row_in_chunk >= d, pltpu.roll(g2, d, axis=0), 0.0)
        d *= 2
    g = g2.reshape(npb, c, LANES)

    g_cols = [g[:, :, BA_LANE_G + h:BA_LANE_G + h + 1] for h in range(nh)]
    b_cols = [beta[:, :, h:h + 1] for h in range(nh)]
    g_col = jnp.concatenate(g_cols, axis=1)
    b_col = jnp.concatenate(b_cols, axis=1)
    g_last = jnp.concatenate([jnp.broadcast_to(gc[:, c - 1:c, :], (npb, c, 1)) for gc in g_cols], axis=1)

    hp = min(nh, LANES // c)
    pieces = []
    for h0 in range(0, nh, hp):
        slab = jnp.concatenate(
            [(g2 if hh == 0 else pltpu.roll(g2, LANES - hh, axis=1)).reshape(npb, c, LANES)
             for hh in range(h0, h0 + hp)], axis=1)
        if hp * c < LANES:
            slab = jnp.concatenate([slab, jnp.zeros((npb, LANES - hp * c, LANES), F32)], axis=1)
        rows = [slab[p].T[BA_LANE_G:BA_LANE_G + 1, :hp * c] for p in range(npb)]
        pieces.append(jnp.stack(rows, axis=0))
    g_row = jnp.concatenate(pieces, axis=2)

    ri = lax.broadcasted_iota(jnp.int32, (c, hc), 0)
    lane = lax.broadcasted_iota(jnp.int32, (c, hc), 1)
    cj = lane & (c - 1)
    causal = (ri >= cj)[None]
    strict = (ri > cj)[None]

    def cat_from_cols(cols):
        out = jnp.broadcast_to(cols[nh - 1], (npb, c, hc))
        for h in range(nh - 2, -1, -1):
            out = jnp.where((lane < (h + 1) * c)[None], jnp.broadcast_to(cols[h], (npb, c, hc)), out)
        return out

    decay = jnp.exp(jnp.where(causal, cat_from_cols(g_cols) - g_row, -jnp.inf))

    side_work()
    kb = k.astype(BF16)
    br = lax.broadcasted_iota(jnp.int32, (hc, nh * dh), 0)
    bl = lax.broadcasted_iota(jnp.int32, (hc, nh * dh), 1)
    head_of_row = sum(jnp.where(br >= h * c, 1, 0) for h in range(1, nh))
    head_of_lane = sum(jnp.where(bl >= h * dh, 1, 0) for h in range(1, nh))
    k_bd = jnp.where((head_of_row == head_of_lane)[None], jnp.concatenate([kb] * nh, axis=2), 0.0)
    qk_lhs = jnp.concatenate([unstack(q), unstack(k)], axis=1).astype(BF16)
    qkk = _bmm_nt(qk_lhs, k_bd)
    qk = qkk[:, :c]
    kk = qkk[:, c:]

    sr = lax.broadcasted_iota(jnp.int32, (hc, hc), 0)
    sl = lax.broadcasted_iota(jnp.int32, (hc, hc), 1)
    shift = c.bit_length() - 1
    same_head = ((sr >> shift) == (sl >> shift))[None]

    def bd(m):
        return jnp.where(same_head, jnp.concatenate([m] * nh, axis=1), 0.0)

    lm = jnp.where(strict, cat_from_cols(b_cols) * kk * decay, 0.0)
    tinv = jnp.where((ri == cj)[None], 1.0, 0.0) - jnp.where(((ri >> 1) == (cj >> 1))[None], lm, 0.0)
    s = 2
    while s < c:
        sh = s.bit_length()
        lower_left = ((ri >> sh) == (cj >> sh)) & ((ri & s) != 0) & ((cj & s) == 0)
        side_work()
        a_off = jnp.where(lower_left[None], lm, 0.0).astype(BF16)
        xm = _bmm(tinv.astype(BF16), bd(a_off))
        tinv = tinv - _bmm(xm.astype(BF16), bd(tinv.astype(BF16)))
        s *= 2

    e_g = jnp.exp(g_col)
    rhs = jnp.concatenate([v * b_col, k * (b_col * e_g)], axis=2).astype(BF16)
    sol = _bmm(bd(tinv.astype(BF16)), rhs)
    sol_v = sol[:, :, :dh]
    sol_k = sol[:, :, dh:]
    w_qe = jnp.concatenate([sol_k, q * e_g], axis=2).astype(BF16)
    kd = (k * jnp.exp(g_last - g_col)).astype(BF16)
    qkd_bd = bd((qk * decay).astype(BF16))
    s_decay = jnp.exp(g_last)
    side_work(len(side))
    return sol_v, w_qe, kd, qkd_bd, s_decay, lambda: stack(z_ref[...].reshape(npb, c, DN_WIDTH))


def _dn_phase_b(stash, ng_ref, o_ref, state_ref, *, nb, tt, c):
    sol_v, w_qe, kd, qkd_bd, s_decay, zs = stash
    nh = DN_HEADS
    dh = DN_HEAD_DIM
    nck = tt // c
    outs = {}
    pending = {}

    def outputs(b, ck):
        def run():
            p = b * nck + ck
            ws, qs = [], []
            states = [state_ref[b, h] for h in range(nh)]
            for h in range(nh):
                sb = states[h].astype(BF16)
                rows = w_qe[p, h * c:(h + 1) * c, :]
                ws.append(_dot(rows[:, :dh], sb))
                qs.append(_dot(rows[:, dh:], sb))
            u = sol_v[p] - jnp.concatenate(ws, axis=0)
            ub = u.astype(BF16)
            outs[p] = jnp.concatenate(qs, axis=0) + _dot(qkd_bd[p], ub)
            pending[p] = (states, ub)
        return run

    def update(b, ck):
        def run():
            p = b * nck + ck
            states, ub = pending.pop(p)
            for h in range(nh):
                r0 = h * c
                state_ref[b, h] = (states[h] * s_decay[p, r0:r0 + 1, :]
                                   + _dot_tn(kd[p, r0:r0 + c, :], ub[r0:r0 + c, :]))
        return run

    def finish():
        o = jnp.stack([outs[p] for p in range(nb * nck)], axis=0)
        z = zs()
        o = _rms(o, ng_ref[...]) * (z * jax.nn.sigmoid(z))
        o_ref[...] = _unstack_heads(o, c).reshape(nb, tt, DN_WIDTH).astype(BF16)

    return [f(b, ck) for b in range(nb) for ck in range(nck) for f in (outputs, update)] + [finish]


def _deltanet(qkv, z, ba, cache8, s0, cw, gp, ng, nb, tt, c):
    b, l, _ = qkv.shape
    seq = lambda w_: pl.BlockSpec((nb, tt, w_), lambda i, j: (i, j, 0))
    per_b = lambda *s: pl.BlockSpec((nb,) + s, lambda i, j: (i,) + (0,) * len(s))
    return pl.pallas_call(
        functools.partial(_dn_kernel, nb=nb, tt=tt, c=c),
        grid=(b // nb, l // tt),
        in_specs=[seq(QKV_WIDTH), seq(DN_WIDTH), seq(LANES), per_b(SUBLANES, QKV_WIDTH),
                  per_b(DN_HEADS, DN_HEAD_DIM, DN_HEAD_DIM),
                  _const_spec((SUBLANES, QKV_WIDTH)), _const_spec((SUBLANES, LANES)),
                  _const_spec((1, DN_HEAD_DIM))],
        out_specs=(seq(DN_WIDTH), per_b(DN_HEADS, DN_HEAD_DIM, DN_HEAD_DIM)),
        out_shape=(jax.ShapeDtypeStruct((b, l, DN_WIDTH), BF16),
                   jax.ShapeDtypeStruct((b, DN_HEADS, DN_HEAD_DIM, DN_HEAD_DIM), F32)),
        scratch_shapes=[pltpu.VMEM((nb, SUBLANES, QKV_WIDTH), F32)],
        compiler_params=_params(2),
        name="deltanet",
    )(qkv, z, ba, cache8, s0, cw, gp, ng)


def _mixer_kernel(x_ref, g_ref, w_ref, cache_ref, s0_ref, cw_ref, gp_ref, ng_ref,
                  u_ref, o_ref, state_ref, tail_ref, p_cur, p_nxt, halo_s, *, nb, tt, c):
    i = pl.program_id(0)
    o1 = QKV_WIDTH
    o2 = o1 + DN_WIDTH
    o3 = o2 + S5_WIDTH

    @pl.when(i == 0)
    def _():
        p_cur[...] = jnp.zeros(p_cur.shape, F32)

    _dn_restart(i <= 1, cache_ref, s0_ref, halo_s, state_ref)

    h = _rms(x_ref[...].reshape(nb * tt, D_MODEL), g_ref[...]).astype(BF16)

    def project(lo, hi):
        def run():
            res = _dot(h, w_ref[:, lo:hi]).reshape(nb, tt, hi - lo)
            if o2 <= lo < o3:
                u_ref[:, :, lo - o2:hi - o2] = res
            else:
                p_nxt[:, :, lo:hi] = res
        return run

    side = [project(lo, min(lo + MXU_DIM, IN_PAD)) for lo in range(0, IN_PAD, MXU_DIM)]
    stash = _dn_phase_a(p_cur.at[:, :, 0:o1], p_cur.at[:, :, o1:o2], p_cur.at[:, :, o3:IN_PAD],
                        cw_ref, gp_ref, halo_s, nb=nb, tt=tt, c=c, side=side)
    for step in _dn_phase_b(stash, ng_ref, o_ref, state_ref, nb=nb, tt=tt, c=c):
        step()
    tail_ref[...] = halo_s[...]
    for lo, hi in ((0, o1), (o1, o2), (o3, IN_PAD)):
        p_cur[:, :, lo:hi] = p_nxt[:, :, lo:hi]


def _mixer(x, g, w, cache8, s0, cw, gp, ng, tt, c):
    nb, l, _ = x.shape
    nt = l // tt
    cur = lambda w_: pl.BlockSpec((nb, tt, w_), lambda i: (0, jnp.minimum(i, nt - 1), 0))
    prev = lambda w_: pl.BlockSpec((nb, tt, w_), lambda i: (0, jnp.maximum(i - 1, 0), 0))
    return pl.pallas_call(
        functools.partial(_mixer_kernel, nb=nb, tt=tt, c=c),
        grid=(nt + 1,),
        in_specs=[cur(D_MODEL), _const_spec((1, D_MODEL)), _const_spec((D_MODEL, IN_PAD)),
                  _const_spec((nb, SUBLANES, QKV_WIDTH)), _const_spec((nb, DN_HEADS, DN_HEAD_DIM, DN_HEAD_DIM)),
                  _const_spec((SUBLANES, QKV_WIDTH)), _const_spec((SUBLANES, LANES)),
                  _const_spec((1, DN_HEAD_DIM))],
        out_specs=(cur(S5_WIDTH), prev(DN_WIDTH),
                   pl.BlockSpec((nb, DN_HEADS, DN_HEAD_DIM, DN_HEAD_DIM), lambda i: (0, 0, 0, 0)),
                   pl.BlockSpec((nb, SUBLANES, QKV_WIDTH), lambda i: (0, 0, 0))),
        out_shape=(jax.ShapeDtypeStruct((nb, l, S5_WIDTH), F32),
                   jax.ShapeDtypeStruct((nb, l, DN_WIDTH), BF16),
                   jax.ShapeDtypeStruct((nb, DN_HEADS, DN_HEAD_DIM, DN_HEAD_DIM), F32),
                   jax.ShapeDtypeStruct((nb, SUBLANES, QKV_WIDTH), F32)),
        scratch_shapes=[pltpu.VMEM((nb, tt, IN_PAD), F32), pltpu.VMEM((nb, tt, IN_PAD), F32),
                        pltpu.VMEM((nb, SUBLANES, QKV_WIDTH), F32)],
        compiler_params=_params(1),
        name="mixer",
    )(x, g, w, cache8, s0, cw, gp, ng)


def _s5c_kernel(u_ref, x0re_ref, x0im_ref, pq_ref, qq_ref, mq_ref, tab_ref, dv_ref, gw_ref, gb_ref, ng_ref,
                o_ref, fre_ref, fim_ref, xr_s, xi_s, car_s, u_s, o_s, *, rows, seg):
    carry = seg == SUBLANES
    if carry:
        @pl.when(pl.program_id(1) == 0)
        def _():
            car_s[0:1, :] = x0re_ref[0]
            car_s[1:2, :] = x0im_ref[0]

    ncol = S5_WIDTH // LANES
    for k in range(ncol):
        u_s[k] = u_ref[0, :, k * LANES:(k + 1) * LANES]
    us = [jnp.concatenate([u_s[k, pl.ds(s, rows, stride=S5_SLOTS), :] for k in range(ncol)], axis=1)
          for s in range(S5_SLOTS)]
    ubs = [a.astype(BF16) for a in us]
    blk = 4 * S5_GROUP
    ql = QUAD_LANES
    uq =[jnp.concatenate([ubs[s][:, n * blk:(n + 1) * blk] for s in range(S5_SLOTS)], axis=1)
          for n in range(S5_QUADS)]
    for n in range(S5_QUADS):
        inc = _dot(uq[n], pq_ref[n])
        xr_s[:, n * ql:(n + 1) * ql] = inc[:, :ql]
        xi_s[:, n * ql:(n + 1) * ql] = inc[:, ql:]

    seg_row = lax.broadcasted_iota(jnp.int32, (SUBLANES, S5_LANES), 0) & (seg - 1)

    def block_body(rb, cin):
        r0 = pl.multiple_of(rb * SUBLANES, SUBLANES)
        if carry:
            c_re, c_im = cin
        else:
            c_re = x0re_ref[0, pl.ds(r0, SUBLANES), :]
            c_im = x0im_ref[0, pl.ds(r0, SUBLANES), :]
        xr = xr_s[pl.ds(r0, SUBLANES), :]
        xi = xi_s[pl.ds(r0, SUBLANES), :]
        for lvl, d in enumerate(_scan_levels(seg)):
            m_re = tab_ref[16 + 16 * lvl:24 + 16 * lvl, :]
            m_im = tab_ref[24 + 16 * lvl:32 + 16 * lvl, :]
            sr = pltpu.roll(xr, d, axis=0)
            si = pltpu.roll(xi, d, axis=0)
            xr, xi = xr + (m_re * sr - m_im * si), xi + (m_re * si + m_im * sr)
        p_re = tab_ref[0:8, :]
        p_im = tab_ref[8:16, :]
        xr, xi = xr + (p_re * c_re - p_im * c_im), xi + (p_re * c_im + p_im * c_re)
        xr_s[pl.ds(r0, SUBLANES), :] = jnp.where(seg_row == 0, c_re, pltpu.roll(xr, 1, axis=0))
        xi_s[pl.ds(r0, SUBLANES), :] = jnp.where(seg_row == 0, c_im, pltpu.roll(xi, 1, axis=0))
        if carry:
            return xr[SUBLANES - 1:, :], xi[SUBLANES - 1:, :]
        fre_ref[0, pl.ds(r0, SUBLANES), :] = xr
        fim_ref[0, pl.ds(r0, SUBLANES), :] = xi
        return cin

    if carry:
        c_re, c_im = lax.fori_loop(0, rows // SUBLANES, block_body, (car_s[0:1, :], car_s[1:2, :]))
        car_s[0:1, :] = c_re
        car_s[1:2, :] = c_im
        fre_ref[0] = c_re
        fim_ref[0] = c_im
    else:
        lax.fori_loop(0, rows // SUBLANES, block_body, 0)

    ys = []
    for n in range(S5_QUADS):
        xs = jnp.concatenate([xr_s[:, n * ql:(n + 1) * ql], xi_s[:, n * ql:(n + 1) * ql]], axis=1).astype(BF16)
        half = QUAD_IN // 2
        intra = jnp.concatenate([_dot(uq[n][:, :half], mq_ref[n, :half, :half]), _dot(uq[n], mq_ref[n, :, half:])],
                                axis=1)
        ys.append(intra + _dot(xs, qq_ref[n]))
    for s in range(S5_SLOTS):
        y = jnp.concatenate([ys[n][:, s * blk:(s + 1) * blk] for n in range(S5_QUADS)], axis=1)
        y = jax.nn.gelu(y + dv_ref[...] * us[s]).astype(BF16)
        gl = _dot(y, gw_ref[...]) + gb_ref[...]
        o = gl[:, :S5_WIDTH] * jax.nn.sigmoid(gl[:, S5_WIDTH:])
        o = _rms(o, ng_ref[...])
        for k in range(ncol):
            o_s[k, pl.ds(s, rows, stride=S5_SLOTS), :] = o[:, k * LANES:(k + 1) * LANES]
    o_ref[0] = jnp.concatenate([o_s[k] for k in range(ncol)], axis=1).astype(BF16)


def _s5_chunked(u, x0re, x0im, pq, qq, mq, tabc, dv, gw, gb, ng, rows, seg):
    b, l, _ = u.shape
    tt = rows * S5_SLOTS
    seq = pl.BlockSpec((1, tt, S5_WIDTH), lambda i, j: (i, j, 0))
    if seg == SUBLANES:
        st = pl.BlockSpec((1, 1, S5_LANES), lambda i, j: (i, 0, 0))
    else:
        st = pl.BlockSpec((1, rows, S5_LANES), lambda i, j: (i, j, 0))
    return pl.pallas_call(
        functools.partial(_s5c_kernel, rows=rows, seg=seg),
        grid=(b, l // tt),
        in_specs=[seq, st, st,
                  _const_spec((S5_QUADS, QUAD_IN, 2 * QUAD_LANES)),
                  _const_spec((S5_QUADS, 2 * QUAD_LANES, QUAD_IN)),
                  _const_spec((S5_QUADS, QUAD_IN, QUAD_IN)), _const_spec((64, S5_LANES)),
                  _const_spec((1, S5_WIDTH)), _const_spec((S5_WIDTH, 2 * S5_WIDTH)),
                  _const_spec((1, 2 * S5_WIDTH)), _const_spec((1, S5_WIDTH))],
        out_specs=(seq, st, st),
        out_shape=(jax.ShapeDtypeStruct((b, l, S5_WIDTH), BF16),
                   jax.ShapeDtypeStruct(x0re.shape, F32),
                   jax.ShapeDtypeStruct(x0re.shape, F32)),
        scratch_shapes=[pltpu.VMEM((rows, S5_LANES), F32), pltpu.VMEM((rows, S5_LANES), F32),
                        pltpu.VMEM((SUBLANES, S5_LANES), F32),
                        pltpu.VMEM((S5_WIDTH // LANES, tt, LANES), F32),
                        pltpu.VMEM((S5_WIDTH // LANES, tt, LANES), F32)],
        compiler_params=_params(2),
        name="s5_chunked",
    )(u, x0re, x0im, pq, qq, mq, tabc, dv, gw, gb, ng)


def _ffn_kernel(x_ref, odn_ref, os5_ref, prev_ref, woa_ref, wob_ref, n2_ref, wu_ref,
                fcw_ref, fcb_ref, wd_ref, fg_ref, y_ref, tail_ref, halo_s, *, tm, ls, carry):
    if carry:
        @pl.when(pl.program_id(1) == 0)
        def _():
            halo_s[...] = prev_ref[0]

    x1 = x_ref[0] + _dot(odn_ref[0], woa_ref[...]) + _dot(os5_ref[0], wob_ref[...])
    h2 = _rms(x1, n2_ref[...]).astype(BF16)
    down = None
    for lo in range(0, D_FF, FF_CHUNK):
        hi = min(lo + FF_CHUNK, D_FF)
        gate = _dot(h2, wu_ref[:, lo:hi])
        val = _dot(h2, wu_ref[:, D_FF + lo:D_FF + hi])
        if carry:
            xx = jnp.concatenate([halo_s[:, lo:hi], gate], axis=0)
            g2 = xx[SUBLANES - 2:SUBLANES - 2 + tm]
            g1 = xx[SUBLANES - 1:SUBLANES - 1 + tm]
            halo_s[:, lo:hi] = gate[tm - SUBLANES:, :]
            tail_ref[0, :, lo:hi] = gate[tm - SUBLANES:, :]
        else:
            r = lax.broadcasted_iota(jnp.int32, (tm, hi - lo), 0) & (ls - 1)
            prev = prev_ref[0, :, lo:hi]
            g1 = jnp.where(r < 1, pltpu.roll(prev, tm - 1, axis=0), pltpu.roll(gate, 1, axis=0))
            g2 = jnp.where(r < 2, prev, pltpu.roll(gate, 2, axis=0))
            tail_ref[0, :, lo:hi] = gate
        cw = fcw_ref[:, lo:hi]
        pre = g2 * cw[0:1] + g1 * cw[1:2] + gate * cw[2:3] + fcb_ref[:, lo:hi]
        act = pre * jax.nn.sigmoid(pre) * val
        part = _dot(act.astype(BF16), wd_ref[lo:hi, :])
        down = part if down is None else down + part
    y_ref[0] = _rms(x1 + down, fg_ref[...])


def _ffn(x, odn, os5, prev, woa, wob, n2, wu, fcw, fcb, wd, fg, tm, ls, carry):
    b, l, _ = x.shape
    seq = lambda w_: pl.BlockSpec((1, tm, w_), lambda i, j: (i, j, 0))
    if carry:
        prev_spec = pl.BlockSpec((1, SUBLANES, D_FF), lambda i, j: (i, 0, 0))
        tail_spec = pl.BlockSpec((1, SUBLANES, D_FF), lambda i, j: (i, 0, 0))
        tail_shape = jax.ShapeDtypeStruct((b, SUBLANES, D_FF), F32)
    else:
        prev_spec = seq(D_FF)
        tail_spec = seq(D_FF)
        tail_shape = jax.ShapeDtypeStruct((b, l, D_FF), F32)
    return pl.pallas_call(
        functools.partial(_ffn_kernel, tm=tm, ls=ls, carry=carry),
        grid=(b, l // tm),
        in_specs=[seq(D_MODEL), seq(DN_WIDTH), seq(S5_WIDTH), prev_spec,
                  _const_spec((DN_WIDTH, D_MODEL)), _const_spec((S5_WIDTH, D_MODEL)),
                  _const_spec((1, D_MODEL)), _const_spec((D_MODEL, 2 * D_FF)),
                  _const_spec((SUBLANES, D_FF)), _const_spec((1, D_FF)), _const_spec((D_FF, D_MODEL)),
                  _const_spec((1, D_MODEL))],
        out_specs=(seq(D_MODEL), tail_spec),
        out_shape=(jax.ShapeDtypeStruct((b, l, D_MODEL), F32), tail_shape),
        scratch_shapes=[pltpu.VMEM((SUBLANES, D_FF), F32)],
        compiler_params=_params(2),
        name="ffn",
    )(x, odn, os5, prev, woa, wob, n2, wu, fcw, fcb, wd, fg)


def _pad_rows_top(a, rows):
    return jnp.pad(a, ((0, 0), (rows - a.shape[1], 0), (0, 0)))


def _quad_blocks(m):
    same = jnp.eye(4, dtype=bool)[None, :, None, :, None]
    blocks = jnp.where(same, m.reshape(S5_QUADS, 4, S5_GROUP, 1, S5_STATE), 0.0)
    return blocks.reshape(S5_QUADS, 4 * S5_GROUP, QUAD_LANES)


def _trunk(x, conv_dn, s_dn, s5_re, s5_im, conv_ffn, w, prompt):
    b, l, _ = x.shape
    n = b * l
    c = CHUNK if l % CHUNK == 0 else l
    assert l >= 2 * SUBLANES, "sequence shorter than two row tiles"
    cache8 = _pad_rows_top(conv_dn, SUBLANES)
    if prompt:
        u, o_dn, s_dn_new, tail = _mixer(x, w['n1'], w['w_in'], cache8, s_dn, w['dn_cw'], w['dn_gp'], w['dn_g'],
                                         tt=min(l, 128), c=c)
        conv_dn_new = tail[:, SUBLANES - (DN_CONV - 1):]
    else:
        qkv, z, u, ba = _in_proj(x.reshape(n, D_MODEL), w['n1'], w['w_in'], min(n, 512))
        qkv = qkv.reshape(b, l, QKV_WIDTH)
        o_dn, s_dn_new = _deltanet(qkv, z.reshape(b, l, DN_WIDTH), ba.reshape(b, l, LANES), cache8, s_dn,
                                   w['dn_cw'], w['dn_gp'], w['dn_g'], nb=min(b, 8), tt=l, c=c)
        conv_dn_new = qkv[:, l - (DN_CONV - 1):]

    s5_args = (w['pq'], w['qq'], w['mq'], w['tab_long' if prompt else 'tab_short'], w['dv'], w['glu_w'],
               w['glu_b'], w['s5_g'])
    if prompt:
        x0re = s5_re.reshape(b, 1, S5_LANES)
        x0im = s5_im.reshape(b, 1, S5_LANES)
        o_s5, fre, fim = _s5_chunked(u.reshape(b, l, S5_WIDTH), x0re, x0im, *s5_args,
                                     rows=min(l // S5_SLOTS, 128), seg=SUBLANES)
    else:
        seg = l // S5_SLOTS
        rep = lambda s: jnp.repeat(s.reshape(b, S5_LANES), seg, axis=0).reshape(1, b * seg, S5_LANES)
        o_s5, fre, fim = _s5_chunked(u.reshape(1, n, S5_WIDTH), rep(s5_re), rep(s5_im), *s5_args,
                                     rows=b * seg, seg=seg)
        o_s5 = o_s5.reshape(b, l, S5_WIDTH)
        fre = fre.reshape(b, seg, S5_LANES)[:, seg - 1]
        fim = fim.reshape(b, seg, S5_LANES)[:, seg - 1]
    s5_re_new = fre.reshape(b, S5_GROUPS, S5_STATE)
    s5_im_new = fim.reshape(b, S5_GROUPS, S5_STATE)

    ffn_args = (w['w_out_a'], w['w_out_b'], w['n2'], w['w_up'], w['fcw'], w['fcb'],
                w['w_down'], w['fg'])
    if prompt:
        prev = _pad_rows_top(conv_ffn, SUBLANES)
        y, tail = _ffn(x, o_dn, o_s5, prev, *ffn_args, tm=512, ls=512, carry=True)
        conv_ffn_new = tail[:, SUBLANES - (FFN_CONV - 1):]
    else:
        prev = jnp.pad(conv_ffn, ((0, 0), (0, l - (FFN_CONV - 1)), (0, 0))).reshape(1, n, D_FF)
        y, tail = _ffn(x.reshape(1, n, D_MODEL), o_dn.reshape(1, n, DN_WIDTH), o_s5.reshape(1, n, S5_WIDTH),
                       prev, *ffn_args, tm=min(n, 128), ls=l, carry=False)
        y = y.reshape(b, l, D_MODEL)
        conv_ffn_new = tail.reshape(b, l, D_FF)[:, l - (FFN_CONV - 1):]
    return y, (conv_dn_new[None], s_dn_new[None], s5_re_new[None], s5_im_new[None], conv_ffn_new[None])


def kernel(x_prompt, x_sample, cache_dn_conv, state_dn, state_s5_re, state_s5_im, cache_ffn_conv, norm1_g, w_in, dn_conv_w, dn_A_log, dn_dt_bias, dn_norm_g, s5_A_re, s5_A_im, s5_log_dt, s5_B_re, s5_B_im, s5_C_re, s5_C_im, s5_D, s5_glu_w, s5_glu_b, s5_norm_g, w_out, norm2_g, w_up, ffn_conv_w, ffn_conv_b, w_down, final_norm_g):
    assert w_in.shape[0] == 1, "single-layer trunk"
    o1 = QKV_WIDTH
    o2 = o1 + DN_WIDTH
    o4 = o2 + 2 * DN_HEADS
    wi = w_in[0]
    w_in_r = jnp.concatenate(
        [wi[:, :o2], wi[:, o4:], wi[:, o2:o4], jnp.zeros((D_MODEL, LANES - 2 * DN_HEADS), wi.dtype)],
        axis=1).astype(BF16)
    lane_pad = lambda v: jnp.pad(v, (BA_LANE_G, LANES - BA_LANE_G - DN_HEADS))
    ldt_lanes = jnp.repeat(s5_log_dt[0], S5_STATE)
    seg_short = x_sample.shape[1] // S5_SLOTS
    assert x_sample.shape[1] % S5_SLOTS == 0 and seg_short in (1, 2, 4), "sample sequences of 4, 8 or 16 rows"
    lane_row = lambda v: v.reshape(1, S5_LANES)
    tab_long, tab_short = [_s5_prep(lane_row(s5_A_re[0]), lane_row(s5_A_im[0]), lane_row(ldt_lanes), seg=s)
                           for s in (SUBLANES, seg_short)]
    quad_row = lambda v: v.reshape(S5_QUADS, 1, QUAD_LANES)
    pq, qq, mq = _s5_prep_quads(quad_row(s5_A_re[0]), quad_row(s5_A_im[0]), quad_row(ldt_lanes),
                                _quad_blocks(s5_B_re[0].transpose(0, 2, 1)), _quad_blocks(s5_B_im[0].transpose(0, 2, 1)),
                                _quad_blocks(s5_C_re[0]), _quad_blocks(s5_C_im[0]))
    w = {
        'n1': norm1_g, 'w_in': w_in_r,
        'dn_cw': jnp.pad(dn_conv_w[0], ((0, SUBLANES - DN_CONV), (0, 0))),
        'dn_gp': jnp.pad(jnp.stack([lane_pad(dn_A_log[0]), lane_pad(dn_dt_bias[0])]), ((0, SUBLANES - 2), (0, 0))),
        'dn_g': dn_norm_g,
        'tab_long': tab_long, 'tab_short': tab_short, 'pq': pq, 'qq': qq, 'mq': mq,
        'dv': s5_D, 'glu_w': s5_glu_w[0].astype(BF16), 'glu_b': s5_glu_b, 's5_g': s5_norm_g,
        'w_out_a': w_out[0, :DN_WIDTH].astype(BF16), 'w_out_b': w_out[0, DN_WIDTH:].astype(BF16),
        'n2': norm2_g, 'w_up': w_up[0].astype(BF16),
        'fcw': jnp.pad(ffn_conv_w[0], ((0, SUBLANES - FFN_CONV), (0, 0))), 'fcb': ffn_conv_b,
        'w_down': w_down[0].astype(BF16), 'fg': final_norm_g.reshape(1, D_MODEL),
    }
    bp = x_prompt.shape[0]
    zeros = lambda *s: jnp.zeros(s, F32)
    y_p, st_p = _trunk(x_prompt, zeros(bp, DN_CONV - 1, QKV_WIDTH), zeros(bp, DN_HEADS, DN_HEAD_DIM, DN_HEAD_DIM),
                       zeros(bp, S5_GROUPS, S5_STATE), zeros(bp, S5_GROUPS, S5_STATE),
                       zeros(bp, FFN_CONV - 1, D_FF), w, prompt=True)
    y_s, st_s = _trunk(x_sample, cache_dn_conv[0], state_dn[0], state_s5_re[0], state_s5_im[0],
                       cache_ffn_conv[0], w, prompt=False)
    return (y_p, y_s) + st_p + st_s
```

```python
import functools

import jax
import jax.numpy as jnp
from jax import lax
from jax.experimental import pallas as pl
from jax.experimental.pallas import tpu as pltpu

F32 = jnp.float32
BF16 = jnp.bfloat16
EPS = 1e-6

D_MODEL = 1024
DN_HEADS = 4
DN_HEAD_DIM = 128
DN_WIDTH = DN_HEADS * DN_HEAD_DIM
DN_CONV = 4
QKV_WIDTH = 3 * DN_WIDTH
S5_WIDTH = D_MODEL - DN_WIDTH
S5_GROUP = 16
S5_GROUPS = S5_WIDTH // S5_GROUP
S5_STATE = 64
S5_LANES = S5_GROUPS * S5_STATE
S5_SLOTS = 8
S5_QUADS = S5_GROUPS // 4
QUAD_LANES = 4 * S5_STATE
QUAD_IN = S5_SLOTS * 4 * S5_GROUP
D_FF = 2816
FFN_CONV = 3
CHUNK = 64

SUBLANES = 8
LANES = 128
BA_LANE_G = DN_HEADS
IN_PAD = QKV_WIDTH + DN_WIDTH + S5_WIDTH + LANES
MXU_DIM = 256
FF_CHUNK = 6 * MXU_DIM

VMEM_LIMIT = 56 * 1024 * 1024


def _dot(a, b):
    return jnp.dot(a, b, preferred_element_type=F32)


def _dot_tn(a, b):
    return lax.dot_general(a, b, (((0,), (0,)), ((), ())), preferred_element_type=F32)


def _split_bf16(a):
    hi = a.astype(BF16)
    lo = (a - hi.astype(F32)).astype(BF16)
    return hi, lo


def _bmm(a, b):
    return lax.dot_general(a, b, (((2,), (1,)), ((0,), (0,))), preferred_element_type=F32)


def _bmm_nt(a, b):
    return lax.dot_general(a, b, (((2,), (2,)), ((0,), (0,))), preferred_element_type=F32)


def _bmm_nt3(a, b):
    ah, al = _split_bf16(a)
    bh, bl = _split_bf16(b)
    return _bmm_nt(ah, bh) + (_bmm_nt(al, bh) + _bmm_nt(ah, bl))


def _rms(x, g):
    return x * lax.rsqrt(jnp.mean(x * x, axis=-1, keepdims=True) + EPS) * g


def _const_spec(shape):
    nd = len(shape)
    return pl.BlockSpec(shape, lambda *_: (0,) * nd, pipeline_mode=pl.Buffered(1))


def _params(n_axes):
    return pltpu.CompilerParams(dimension_semantics=("arbitrary",) * n_axes,
                                vmem_limit_bytes=VMEM_LIMIT)


def _cmul(a, b):
    return a[0] * b[0] - a[1] * b[1], a[0] * b[1] + a[1] * b[0]


def _zoh(are, aim, ldt):
    dt = jnp.exp(ldt)
    mag = jnp.exp(are * dt)
    ang = aim * dt
    lr = mag * jnp.cos(ang)
    li = mag * jnp.sin(ang)
    den = are * are + aim * aim
    f_re = ((lr - 1.0) * are + li * aim) / den
    f_im = (li * are - (lr - 1.0) * aim) / den
    return (lr, li), (f_re, f_im)


def _write_scan_tables(tab_ref, step, seg):
    pw = [step]
    for _ in range(seg - 1):
        pw.append(_cmul(pw[-1], step))
    shape = (SUBLANES, step[0].shape[-1])
    row = lax.broadcasted_iota(jnp.int32, shape, 0) & (seg - 1)
    zero = jnp.zeros(shape, F32)
    pre, pim = zero, zero
    for r in range(seg):
        pre = jnp.where(row == r, pw[r][0], pre)
        pim = jnp.where(row == r, pw[r][1], pim)
    tab_ref[...] = jnp.zeros(tab_ref.shape, F32)
    tab_ref[0:8, :] = pre
    tab_ref[8:16, :] = pim
    for lvl, d in enumerate(_scan_levels(seg)):
        tab_ref[16 + 16 * lvl:24 + 16 * lvl, :] = jnp.where(row >= d, pw[d - 1][0], zero)
        tab_ref[24 + 16 * lvl:32 + 16 * lvl, :] = jnp.where(row >= d, pw[d - 1][1], zero)


def _scan_levels(seg):
    return [d for d in (1, 2, 4) if d < seg]


def _s5prep_kernel(are_ref, aim_ref, ldt_ref, tab_ref, *, seg):
    lam, _ = _zoh(are_ref[...], aim_ref[...], ldt_ref[...])
    lam_c = lam
    for _ in range(S5_SLOTS - 1):
        lam_c = _cmul(lam_c, lam)
    _write_scan_tables(tab_ref, lam_c, seg)


def _s5_prep(are, aim, ldt, seg):
    return pl.pallas_call(
        functools.partial(_s5prep_kernel, seg=seg),
        out_shape=jax.ShapeDtypeStruct((64, S5_LANES), F32),
        compiler_params=pltpu.CompilerParams(vmem_limit_bytes=VMEM_LIMIT),
        name="s5_prep",
    )(are, aim, ldt)


def _s5prepq_kernel(are_ref, aim_ref, ldt_ref, bre_ref, bim_ref, cre_ref, cim_ref, pq_ref, qq_ref, mq_ref):
    lam, f = _zoh(are_ref[...], aim_ref[...], ldt_ref[...])
    bb = _cmul(f, (bre_ref[...], bim_ref[...]))
    ct = (cre_ref[...], cim_ref[...])
    one = (jnp.ones_like(lam[0]), jnp.zeros_like(lam[0]))
    pw = [one]
    for _ in range(S5_SLOTS):
        pw.append(_cmul(pw[-1], lam))
    blk = 4 * S5_GROUP
    for s in range(S5_SLOTS):
        p_re, p_im = _cmul(pw[S5_SLOTS - 1 - s], bb)
        pq_ref[:, s * blk:(s + 1) * blk, :QUAD_LANES] = p_re.astype(BF16)
        pq_ref[:, s * blk:(s + 1) * blk, QUAD_LANES:] = p_im.astype(BF16)
    cl = [_cmul(pw[e], ct) for e in range(S5_SLOTS + 1)]
    qt_re = jnp.concatenate([cl[t + 1][0] for t in range(S5_SLOTS)], axis=1)
    qt_im = jnp.concatenate([-cl[t + 1][1] for t in range(S5_SLOTS)], axis=1)
    for n in range(S5_QUADS):
        qq_ref[n, :QUAD_LANES, :] = qt_re[n].T.astype(BF16)
        qq_ref[n, QUAD_LANES:, :] = qt_im[n].T.astype(BF16)
    lag_re = jnp.concatenate([cl[tau][0] for tau in range(S5_SLOTS)], axis=1)
    lag_im = jnp.concatenate([cl[tau][1] for tau in range(S5_SLOTS)], axis=1)
    lags = _bmm_nt3(bb[0], lag_re) - _bmm_nt3(bb[1], lag_im)
    for s in range(S5_SLOTS):
        m = lags if s == 0 else jnp.concatenate(
            [jnp.zeros((S5_QUADS, blk, s * blk), F32), lags[:, :, :QUAD_IN - s * blk]], axis=2)
        mq_ref[:, s * blk:(s + 1) * blk, :] = m.astype(BF16)


def _s5_prep_quads(are_q, aim_q, ldt_q, bre_q, bim_q, cre_q, cim_q):
    w = lambda r, c_: jax.ShapeDtypeStruct((S5_QUADS, r, c_), BF16)
    return pl.pallas_call(
        _s5prepq_kernel,
        out_shape=(w(QUAD_IN, 2 * QUAD_LANES), w(2 * QUAD_LANES, QUAD_IN), w(QUAD_IN, QUAD_IN)),
        compiler_params=pltpu.CompilerParams(vmem_limit_bytes=VMEM_LIMIT),
        name="s5_prep_quads",
    )(are_q, aim_q, ldt_q, bre_q, bim_q, cre_q, cim_q)


def _inproj_kernel(x_ref, g_ref, w_ref, qkv_ref, z_ref, u_ref, ba_ref):
    h = _rms(x_ref[...], g_ref[...]).astype(BF16)
    o1 = QKV_WIDTH
    o2 = o1 + DN_WIDTH
    o3 = o2 + S5_WIDTH
    qkv_ref[...] = _dot(h, w_ref[:, :o1])
    z_ref[...] = _dot(h, w_ref[:, o1:o2])
    u_ref[...] = _dot(h, w_ref[:, o2:o3])
    ba_ref[...] = _dot(h, w_ref[:, o3:])


def _in_proj(x2d, g, w, tm):
    n = x2d.shape[0]
    row = lambda w_: pl.BlockSpec((tm, w_), lambda i: (i, 0))
    return pl.pallas_call(
        _inproj_kernel,
        grid=(n // tm,),
        in_specs=[row(D_MODEL), _const_spec((1, D_MODEL)), _const_spec((D_MODEL, IN_PAD))],
        out_specs=(row(QKV_WIDTH), row(DN_WIDTH), row(S5_WIDTH), row(LANES)),
        out_shape=(jax.ShapeDtypeStruct((n, QKV_WIDTH), F32),
                   jax.ShapeDtypeStruct((n, DN_WIDTH), F32),
                   jax.ShapeDtypeStruct((n, S5_WIDTH), F32),
                   jax.ShapeDtypeStruct((n, LANES), F32)),
        compiler_params=_params(1),
        name="in_proj",
    )(x2d, g, w)


def _stack_heads(a):
    dh = DN_HEAD_DIM
    return jnp.concatenate([a[:, :, h * dh:(h + 1) * dh] for h in range(DN_HEADS)], axis=1)


def _unstack_heads(a, c):
    return jnp.concatenate([a[:, h * c:(h + 1) * c, :] for h in range(DN_HEADS)], axis=2)


def _dn_kernel(qkv_ref, z_ref, ba_ref, cache_ref, s0_ref, cw_ref, gp_ref, ng_ref, o_ref, state_ref, halo_s,
               *, nb, tt, c):
    _dn_restart(pl.program_id(1) == 0, cache_ref, s0_ref, halo_s, state_ref)
    stash = _dn_phase_a(qkv_ref, z_ref, ba_ref, cw_ref, gp_ref, halo_s, nb=nb, tt=tt, c=c)
    for step in _dn_phase_b(stash, ng_ref, o_ref, state_ref, nb=nb, tt=tt, c=c):
        step()


def _dn_restart(first, cache_ref, s0_ref, halo_s, state_ref):
    @pl.when(first)
    def _():
        halo_s[...] = cache_ref[...]
        state_ref[...] = s0_ref[...]


def _dn_phase_a(qkv_ref, z_ref, ba_ref, cw_ref, gp_ref, halo_s, *, nb, tt, c, side=()):
    nh = DN_HEADS
    dh = DN_HEAD_DIM
    nck = tt // c
    npb = nb * nck
    hc = nh * c
    side = list(side)

    def side_work(n=1):
        for _ in range(n):
            if side:
                side.pop(0)()

    stack = _stack_heads
    unstack = functools.partial(_unstack_heads, c=c)

    first_row = lax.broadcasted_iota(jnp.int32, (SUBLANES, MXU_DIM), 0) == 0

    def shift_rows(y, carried):
        rolled = pltpu.roll(y, 1, axis=1)
        top = jnp.where(first_row[None], carried, rolled[:, :SUBLANES])
        return jnp.concatenate([top, rolled[:, SUBLANES:]], axis=1)

    conv_parts = []
    for lo in range(0, QKV_WIDTH, MXU_DIM):
        side_work()
        hi = lo + MXU_DIM
        cw = cw_ref[:, lo:hi]
        x = qkv_ref[:, :, lo:hi]
        prev = halo_s[:, :, lo:hi]
        acc = x * cw[0:1]
        carried = prev[:, SUBLANES - 1:] * cw[0:1]
        for j in range(1, DN_CONV):
            acc = shift_rows(acc, carried) + x * cw[j:j + 1]
            if j < DN_CONV - 1:
                carried = prev[:, SUBLANES - 1 - j:SUBLANES - j] * cw[0:1]
                for i in range(1, j + 1):
                    carried = carried + prev[:, SUBLANES - 1 - j + i:SUBLANES - j + i] * cw[i:i + 1]
        halo_s[:, :, lo:hi] = x[:, tt - SUBLANES:]
        conv_parts.append((acc * jax.nn.sigmoid(acc)).reshape(npb, c, hi - lo))
    per = DN_WIDTH // MXU_DIM
    q, k, v = [stack(jnp.concatenate(conv_parts[i * per:(i + 1) * per], axis=2)) for i in range(3)]
    side_work()
    q = q * (lax.rsqrt(jnp.sum(q * q, axis=-1, keepdims=True) + EPS) * (dh ** -0.5))
    k = k * lax.rsqrt(jnp.sum(k * k, axis=-1, keepdims=True) + EPS)

    side_work()
    ba = ba_ref[...].reshape(nb * tt, LANES)
    beta = jax.nn.sigmoid(ba).reshape(npb, c, LANES)
    g2 = -jnp.exp(gp_ref[0:1, :]) * jax.nn.softplus(ba + gp_ref[1:2, :])
    row_in_chunk = lax.broadcasted_iota(jnp.int32, (nb * tt, LANES), 0) & (c - 1)
    d = 1
    while d < c:
        g2 = g2 + jnp.where(row_in_chunk >= d, pltpu.roll(g2, d, axis=0), 0.0)
        d *= 2
    g = g2.reshape(npb, c, LANES)

    g_cols = [g[:, :, BA_LANE_G + h:BA_LANE_G + h + 1] for h in range(nh)]
    b_cols = [beta[:, :, h:h + 1] for h in range(nh)]
    g_col = jnp.concatenate(g_cols, axis=1)
    b_col = jnp.concatenate(b_cols, axis=1)
    g_last = jnp.concatenate([jnp.broadcast_to(gc[:, c - 1:c, :], (npb, c, 1)) for gc in g_cols], axis=1)

    hp = min(nh, LANES // c)
    pieces = []
    for h0 in range(0, nh, hp):
        slab = jnp.concatenate(
            [(g2 if hh == 0 else pltpu.roll(g2, LANES - hh, axis=1)).reshape(npb, c, LANES)
             for hh in range(h0, h0 + hp)], axis=1)
        if hp * c < LANES:
            slab = jnp.concatenate([slab, jnp.zeros((npb, LANES - hp * c, LANES), F32)], axis=1)
        rows = [slab[p].T[BA_LANE_G:BA_LANE_G + 1, :hp * c] for p in range(npb)]
        pieces.append(jnp.stack(rows, axis=0))
    g_row = jnp.concatenate(pieces, axis=2)

    ri = lax.broadcasted_iota(jnp.int32, (c, hc), 0)
    lane = lax.broadcasted_iota(jnp.int32, (c, hc), 1)
    cj = lane & (c - 1)
    causal = (ri >= cj)[None]
    strict = (ri > cj)[None]

    def cat_from_cols(cols):
        out = jnp.broadcast_to(cols[nh - 1], (npb, c, hc))
        for h in range(nh - 2, -1, -1):
            out = jnp.where((lane < (h + 1) * c)[None], jnp.broadcast_to(cols[h], (npb, c, hc)), out)
        return out

    decay = jnp.exp(jnp.where(causal, cat_from_cols(g_cols) - g_row, -jnp.inf))

    side_work()
    kb = k.astype(BF16)
    br = lax.broadcasted_iota(jnp.int32, (hc, nh * dh), 0)
    bl = lax.broadcasted_iota(jnp.int32, (hc, nh * dh), 1)
    head_of_row = sum(jnp.where(br >= h * c, 1, 0) for h in range(1, nh))
    head_of_lane = sum(jnp.where(bl >= h * dh, 1, 0) for h in range(1, nh))
    k_bd = jnp.where((head_of_row == head_of_lane)[None], jnp.concatenate([kb] * nh, axis=2), 0.0)
    qk_lhs = jnp.concatenate([unstack(q), unstack(k)], axis=1).astype(BF16)
    qkk = _bmm_nt(qk_lhs, k_bd)
    qk = qkk[:, :c]
    kk = qkk[:, c:]

    sr = lax.broadcasted_iota(jnp.int32, (hc, hc), 0)
    sl = lax.broadcasted_iota(jnp.int32, (hc, hc), 1)
    shift = c.bit_length() - 1
    same_head = ((sr >> shift) == (sl >> shift))[None]

    def bd(m):
        return jnp.where(same_head, jnp.concatenate([m] * nh, axis=1), 0.0)

    lm = jnp.where(strict, cat_from_cols(b_cols) * kk * decay, 0.0)
    tinv = jnp.where((ri == cj)[None], 1.0, 0.0) - jnp.where(((ri >> 1) == (cj >> 1))[None], lm, 0.0)
    s = 2
    while s < c:
        sh = s.bit_length()
        lower_left = ((ri >> sh) == (cj >> sh)) & ((ri & s) != 0) & ((cj & s) == 0)
        side_work()
        a_off = jnp.where(lower_left[None], lm, 0.0).astype(BF16)
        xm = _bmm(tinv.astype(BF16), bd(a_off))
        tinv = tinv - _bmm(xm.astype(BF16), bd(tinv.astype(BF16)))
        s *= 2

    e_g = jnp.exp(g_col)
    rhs = jnp.concatenate([v * b_col, k * (b_col * e_g)], axis=2).astype(BF16)
    sol = _bmm(bd(tinv.astype(BF16)), rhs)
    sol_v = sol[:, :, :dh]
    sol_k = sol[:, :, dh:]
    qe = q * e_g
    w_qe = jnp.concatenate([part[:, h * c:(h + 1) * c] for h in range(nh) for part in (sol_k, qe)],
                           axis=1).astype(BF16)
    kd = (k * jnp.exp(g_last - g_col)).astype(BF16)
    qkd_bd = bd((qk * decay).astype(BF16))
    s_decay = jnp.exp(g_last)
    side_work(len(side))
    return sol_v, w_qe, kd, qkd_bd, s_decay, lambda: stack(z_ref[...].reshape(npb, c, DN_WIDTH))


def _dn_phase_b(stash, ng_ref, o_ref, state_ref, *, nb, tt, c):
    sol_v, w_qe, kd, qkd_bd, s_decay, zs = stash
    nh = DN_HEADS
    dh = DN_HEAD_DIM
    nck = tt // c
    outs = {}
    pending = {}

    def outputs(ck):
        def run():
            prods = {}
            for b in range(nb):
                p = b * nck + ck
                states = [state_ref[b, h] for h in range(nh)]
                prods[b] = [_dot(w_qe[p, h * 2 * c:(h + 1) * 2 * c, :], states[h].astype(BF16)) for h in range(nh)]
                pending[p] = states
            for b in range(nb):
                p = b * nck + ck
                u = sol_v[p] - jnp.concatenate([r[:c] for r in prods[b]], axis=0)
                ub = u.astype(BF16)
                outs[p] = jnp.concatenate([r[c:] for r in prods[b]], axis=0) + _dot(qkd_bd[p], ub)
                pending[p] = (pending[p], ub)
        return run

    def update(ck):
        def run():
            for b in range(nb):
                p = b * nck + ck
                states, ub = pending.pop(p)
                for h in range(nh):
                    r0 = h * c
                    state_ref[b, h] = (states[h] * s_decay[p, r0:r0 + 1, :]
                                       + _dot_tn(kd[p, r0:r0 + c, :], ub[r0:r0 + c, :]))
        return run

    def finish():
        o = jnp.stack([outs[p] for p in range(nb * nck)], axis=0)
        z = zs()
        o = _rms(o, ng_ref[...]) * (z * jax.nn.sigmoid(z))
        o_ref[...] = _unstack_heads(o, c).reshape(nb, tt, DN_WIDTH).astype(BF16)

    return [f(ck) for ck in range(nck) for f in (outputs, update)] + [finish]


def _deltanet(qkv, z, ba, cache8, s0, cw, gp, ng, nb, tt, c):
    b, l, _ = qkv.shape
    seq = lambda w_: pl.BlockSpec((nb, tt, w_), lambda i, j: (i, j, 0))
    per_b = lambda *s: pl.BlockSpec((nb,) + s, lambda i, j: (i,) + (0,) * len(s))
    return pl.pallas_call(
        functools.partial(_dn_kernel, nb=nb, tt=tt, c=c),
        grid=(b // nb, l // tt),
        in_specs=[seq(QKV_WIDTH), seq(DN_WIDTH), seq(LANES), per_b(SUBLANES, QKV_WIDTH),
                  per_b(DN_HEADS, DN_HEAD_DIM, DN_HEAD_DIM),
                  _const_spec((SUBLANES, QKV_WIDTH)), _const_spec((SUBLANES, LANES)),
                  _const_spec((1, DN_HEAD_DIM))],
        out_specs=(seq(DN_WIDTH), per_b(DN_HEADS, DN_HEAD_DIM, DN_HEAD_DIM)),
        out_shape=(jax.ShapeDtypeStruct((b, l, DN_WIDTH), BF16),
                   jax.ShapeDtypeStruct((b, DN_HEADS, DN_HEAD_DIM, DN_HEAD_DIM), F32)),
        scratch_shapes=[pltpu.VMEM((nb, SUBLANES, QKV_WIDTH), F32)],
        compiler_params=_params(2),
        name="deltanet",
    )(qkv, z, ba, cache8, s0, cw, gp, ng)


def _mixer_kernel(x_ref, g_ref, w_ref, cache_ref, s0_ref, cw_ref, gp_ref, ng_ref,
                  u_ref, o_ref, state_ref, tail_ref, p_cur, p_nxt, halo_s, *, nb, tt, c):
    i = pl.program_id(0)
    o1 = QKV_WIDTH
    o2 = o1 + DN_WIDTH
    o3 = o2 + S5_WIDTH

    @pl.when(i == 0)
    def _():
        p_cur[...] = jnp.zeros(p_cur.shape, F32)

    _dn_restart(i <= 1, cache_ref, s0_ref, halo_s, state_ref)

    h = _rms(x_ref[...].reshape(nb * tt, D_MODEL), g_ref[...]).astype(BF16)

    def project(lo, hi):
        def run():
            res = _dot(h, w_ref[:, lo:hi]).reshape(nb, tt, hi - lo)
            if o2 <= lo < o3:
                u_ref[:, :, lo - o2:hi - o2] = res
            else:
                p_nxt[:, :, lo:hi] = res
        return run

    side = [project(lo, min(lo + MXU_DIM, IN_PAD)) for lo in range(0, IN_PAD, MXU_DIM)]
    stash = _dn_phase_a(p_cur.at[:, :, 0:o1], p_cur.at[:, :, o1:o2], p_cur.at[:, :, o3:IN_PAD],
                        cw_ref, gp_ref, halo_s, nb=nb, tt=tt, c=c, side=side)
    for step in _dn_phase_b(stash, ng_ref, o_ref, state_ref, nb=nb, tt=tt, c=c):
        step()
    tail_ref[...] = halo_s[...]
    for lo, hi in ((0, o1), (o1, o2), (o3, IN_PAD)):
        p_cur[:, :, lo:hi] = p_nxt[:, :, lo:hi]


def _mixer(x, g, w, cache8, s0, cw, gp, ng, tt, c):
    nb, l, _ = x.shape
    nt = l // tt
    cur = lambda w_: pl.BlockSpec((nb, tt, w_), lambda i: (0, jnp.minimum(i, nt - 1), 0))
    prev = lambda w_: pl.BlockSpec((nb, tt, w_), lambda i: (0, jnp.maximum(i - 1, 0), 0))
    return pl.pallas_call(
        functools.partial(_mixer_kernel, nb=nb, tt=tt, c=c),
        grid=(nt + 1,),
        in_specs=[cur(D_MODEL), _const_spec((1, D_MODEL)), _const_spec((D_MODEL, IN_PAD)),
                  _const_spec((nb, SUBLANES, QKV_WIDTH)), _const_spec((nb, DN_HEADS, DN_HEAD_DIM, DN_HEAD_DIM)),
                  _const_spec((SUBLANES, QKV_WIDTH)), _const_spec((SUBLANES, LANES)),
                  _const_spec((1, DN_HEAD_DIM))],
        out_specs=(cur(S5_WIDTH), prev(DN_WIDTH),
                   pl.BlockSpec((nb, DN_HEADS, DN_HEAD_DIM, DN_HEAD_DIM), lambda i: (0, 0, 0, 0)),
                   pl.BlockSpec((nb, SUBLANES, QKV_WIDTH), lambda i: (0, 0, 0))),
        out_shape=(jax.ShapeDtypeStruct((nb, l, S5_WIDTH), F32),
                   jax.ShapeDtypeStruct((nb, l, DN_WIDTH), BF16),
                   jax.ShapeDtypeStruct((nb, DN_HEADS, DN_HEAD_DIM, DN_HEAD_DIM), F32),
                   jax.ShapeDtypeStruct((nb, SUBLANES, QKV_WIDTH), F32)),
        scratch_shapes=[pltpu.VMEM((nb, tt, IN_PAD), F32), pltpu.VMEM((nb, tt, IN_PAD), F32),
                        pltpu.VMEM((nb, SUBLANES, QKV_WIDTH), F32)],
        compiler_params=_params(1),
        name="mixer",
    )(x, g, w, cache8, s0, cw, gp, ng)


def _s5c_kernel(u_ref, x0re_ref, x0im_ref, pq_ref, qq_ref, mq_ref, tab_ref, dv_ref, gw_ref, gb_ref, ng_ref,
                o_ref, fre_ref, fim_ref, xr_s, xi_s, car_s, u_s, o_s, *, rows, seg):
    carry = seg == SUBLANES
    if carry:
        @pl.when(pl.program_id(1) == 0)
        def _():
            car_s[0:1, :] = x0re_ref[0]
            car_s[1:2, :] = x0im_ref[0]

    ncol = S5_WIDTH // LANES
    for k in range(ncol):
        u_s[k] = u_ref[0, :, k * LANES:(k + 1) * LANES]
    us = [jnp.concatenate([u_s[k, pl.ds(s, rows, stride=S5_SLOTS), :] for k in range(ncol)], axis=1)
          for s in range(S5_SLOTS)]
    ubs = [a.astype(BF16) for a in us]
    blk = 4 * S5_GROUP
    ql = QUAD_LANES
    uq =[jnp.concatenate([ubs[s][:, n * blk:(n + 1) * blk] for s in range(S5_SLOTS)], axis=1)
          for n in range(S5_QUADS)]
    for n in range(S5_QUADS):
        inc = _dot(uq[n], pq_ref[n])
        xr_s[:, n * ql:(n + 1) * ql] = inc[:, :ql]
        xi_s[:, n * ql:(n + 1) * ql] = inc[:, ql:]

    seg_row = lax.broadcasted_iota(jnp.int32, (SUBLANES, S5_LANES), 0) & (seg - 1)

    def block_body(rb, cin):
        r0 = pl.multiple_of(rb * SUBLANES, SUBLANES)
        if carry:
            c_re, c_im = cin
        else:
            c_re = x0re_ref[0, pl.ds(r0, SUBLANES), :]
            c_im = x0im_ref[0, pl.ds(r0, SUBLANES), :]
        xr = xr_s[pl.ds(r0, SUBLANES), :]
        xi = xi_s[pl.ds(r0, SUBLANES), :]
        for lvl, d in enumerate(_scan_levels(seg)):
            m_re = tab_ref[16 + 16 * lvl:24 + 16 * lvl, :]
            m_im = tab_ref[24 + 16 * lvl:32 + 16 * lvl, :]
            sr = pltpu.roll(xr, d, axis=0)
            si = pltpu.roll(xi, d, axis=0)
            xr, xi = xr + (m_re * sr - m_im * si), xi + (m_re * si + m_im * sr)
        p_re = tab_ref[0:8, :]
        p_im = tab_ref[8:16, :]
        xr, xi = xr + (p_re * c_re - p_im * c_im), xi + (p_re * c_im + p_im * c_re)
        xr_s[pl.ds(r0, SUBLANES), :] = jnp.where(seg_row == 0, c_re, pltpu.roll(xr, 1, axis=0))
        xi_s[pl.ds(r0, SUBLANES), :] = jnp.where(seg_row == 0, c_im, pltpu.roll(xi, 1, axis=0))
        if carry:
            return xr[SUBLANES - 1:, :], xi[SUBLANES - 1:, :]
        fre_ref[0, pl.ds(r0, SUBLANES), :] = xr
        fim_ref[0, pl.ds(r0, SUBLANES), :] = xi
        return cin

    if carry:
        c_re, c_im = lax.fori_loop(0, rows // SUBLANES, block_body, (car_s[0:1, :], car_s[1:2, :]))
        car_s[0:1, :] = c_re
        car_s[1:2, :] = c_im
        fre_ref[0] = c_re
        fim_ref[0] = c_im
    else:
        lax.fori_loop(0, rows // SUBLANES, block_body, 0)

    ys = []
    for n in range(S5_QUADS):
        xs = jnp.concatenate([xr_s[:, n * ql:(n + 1) * ql], xi_s[:, n * ql:(n + 1) * ql]], axis=1).astype(BF16)
        half = QUAD_IN // 2
        intra = jnp.concatenate([_dot(uq[n][:, :half], mq_ref[n, :half, :half]), _dot(uq[n], mq_ref[n, :, half:])],
                                axis=1)
        ys.append(intra + _dot(xs, qq_ref[n]))
    for s in range(S5_SLOTS):
        y = jnp.concatenate([ys[n][:, s * blk:(s + 1) * blk] for n in range(S5_QUADS)], axis=1)
        y = jax.nn.gelu(y + dv_ref[...] * us[s]).astype(BF16)
        gl = _dot(y, gw_ref[...]) + gb_ref[...]
        o = gl[:, :S5_WIDTH] * jax.nn.sigmoid(gl[:, S5_WIDTH:])
        o = _rms(o, ng_ref[...])
        for k in range(ncol):
            o_s[k, pl.ds(s, rows, stride=S5_SLOTS), :] = o[:, k * LANES:(k + 1) * LANES]
    o_ref[0] = jnp.concatenate([o_s[k] for k in range(ncol)], axis=1).astype(BF16)


def _s5_chunked(u, x0re, x0im, pq, qq, mq, tabc, dv, gw, gb, ng, rows, seg):
    b, l, _ = u.shape
    tt = rows * S5_SLOTS
    seq = pl.BlockSpec((1, tt, S5_WIDTH), lambda i, j: (i, j, 0))
    if seg == SUBLANES:
        st = pl.BlockSpec((1, 1, S5_LANES), lambda i, j: (i, 0, 0))
    else:
        st = pl.BlockSpec((1, rows, S5_LANES), lambda i, j: (i, j, 0))
    return pl.pallas_call(
        functools.partial(_s5c_kernel, rows=rows, seg=seg),
        grid=(b, l // tt),
        in_specs=[seq, st, st,
                  _const_spec((S5_QUADS, QUAD_IN, 2 * QUAD_LANES)),
                  _const_spec((S5_QUADS, 2 * QUAD_LANES, QUAD_IN)),
                  _const_spec((S5_QUADS, QUAD_IN, QUAD_IN)), _const_spec((64, S5_LANES)),
                  _const_spec((1, S5_WIDTH)), _const_spec((S5_WIDTH, 2 * S5_WIDTH)),
                  _const_spec((1, 2 * S5_WIDTH)), _const_spec((1, S5_WIDTH))],
        out_specs=(seq, st, st),
        out_shape=(jax.ShapeDtypeStruct((b, l, S5_WIDTH), BF16),
                   jax.ShapeDtypeStruct(x0re.shape, F32),
                   jax.ShapeDtypeStruct(x0re.shape, F32)),
        scratch_shapes=[pltpu.VMEM((rows, S5_LANES), F32), pltpu.VMEM((rows, S5_LANES), F32),
                        pltpu.VMEM((SUBLANES, S5_LANES), F32),
                        pltpu.VMEM((S5_WIDTH // LANES, tt, LANES), F32),
                        pltpu.VMEM((S5_WIDTH // LANES, tt, LANES), F32)],
        compiler_params=_params(2),
        name="s5_chunked",
    )(u, x0re, x0im, pq, qq, mq, tabc, dv, gw, gb, ng)


def _ffn_kernel(x_ref, odn_ref, os5_ref, prev_ref, woa_ref, wob_ref, n2_ref, wu_ref,
                fcw_ref, fcb_ref, wd_ref, fg_ref, y_ref, tail_ref, halo_s, *, tm, ls, carry):
    if carry:
        @pl.when(pl.program_id(1) == 0)
        def _():
            halo_s[...] = prev_ref[0]

    x1 = x_ref[0] + _dot(odn_ref[0], woa_ref[...]) + _dot(os5_ref[0], wob_ref[...])
    h2 = _rms(x1, n2_ref[...]).astype(BF16)
    down = None
    for lo in range(0, D_FF, FF_CHUNK):
        hi = min(lo + FF_CHUNK, D_FF)
        gate = _dot(h2, wu_ref[:, lo:hi])
        val = _dot(h2, wu_ref[:, D_FF + lo:D_FF + hi])
        if carry:
            xx = jnp.concatenate([halo_s[:, lo:hi], gate], axis=0)
            g2 = xx[SUBLANES - 2:SUBLANES - 2 + tm]
            g1 = xx[SUBLANES - 1:SUBLANES - 1 + tm]
            halo_s[:, lo:hi] = gate[tm - SUBLANES:, :]
            tail_ref[0, :, lo:hi] = gate[tm - SUBLANES:, :]
        else:
            r = lax.broadcasted_iota(jnp.int32, (tm, hi - lo), 0) & (ls - 1)
            prev = prev_ref[0, :, lo:hi]
            g1 = jnp.where(r < 1, pltpu.roll(prev, tm - 1, axis=0), pltpu.roll(gate, 1, axis=0))
            g2 = jnp.where(r < 2, prev, pltpu.roll(gate, 2, axis=0))
            tail_ref[0, :, lo:hi] = gate
        cw = fcw_ref[:, lo:hi]
        pre = g2 * cw[0:1] + g1 * cw[1:2] + gate * cw[2:3] + fcb_ref[:, lo:hi]
        act = pre * jax.nn.sigmoid(pre) * val
        part = _dot(act.astype(BF16), wd_ref[lo:hi, :])
        down = part if down is None else down + part
    y_ref[0] = _rms(x1 + down, fg_ref[...])


def _ffn(x, odn, os5, prev, woa, wob, n2, wu, fcw, fcb, wd, fg, tm, ls, carry):
    b, l, _ = x.shape
    seq = lambda w_: pl.BlockSpec((1, tm, w_), lambda i, j: (i, j, 0))
    if carry:
        prev_spec = pl.BlockSpec((1, SUBLANES, D_FF), lambda i, j: (i, 0, 0))
        tail_spec = pl.BlockSpec((1, SUBLANES, D_FF), lambda i, j: (i, 0, 0))
        tail_shape = jax.ShapeDtypeStruct((b, SUBLANES, D_FF), F32)
    else:
        prev_spec = seq(D_FF)
        tail_spec = seq(D_FF)
        tail_shape = jax.ShapeDtypeStruct((b, l, D_FF), F32)
    return pl.pallas_call(
        functools.partial(_ffn_kernel, tm=tm, ls=ls, carry=carry),
        grid=(b, l // tm),
        in_specs=[seq(D_MODEL), seq(DN_WIDTH), seq(S5_WIDTH), prev_spec,
                  _const_spec((DN_WIDTH, D_MODEL)), _const_spec((S5_WIDTH, D_MODEL)),
                  _const_spec((1, D_MODEL)), _const_spec((D_MODEL, 2 * D_FF)),
                  _const_spec((SUBLANES, D_FF)), _const_spec((1, D_FF)), _const_spec((D_FF, D_MODEL)),
                  _const_spec((1, D_MODEL))],
        out_specs=(seq(D_MODEL), tail_spec),
        out_shape=(jax.ShapeDtypeStruct((b, l, D_MODEL), F32), tail_shape),
        scratch_shapes=[pltpu.VMEM((SUBLANES, D_FF), F32)],
        compiler_params=_params(2),
        name="ffn",
    )(x, odn, os5, prev, woa, wob, n2, wu, fcw, fcb, wd, fg)


def _pad_rows_top(a, rows):
    return jnp.pad(a, ((0, 0), (rows - a.shape[1], 0), (0, 0)))


def _quad_blocks(m):
    same = jnp.eye(4, dtype=bool)[None, :, None, :, None]
    blocks = jnp.where(same, m.reshape(S5_QUADS, 4, S5_GROUP, 1, S5_STATE), 0.0)
    return blocks.reshape(S5_QUADS, 4 * S5_GROUP, QUAD_LANES)


def _trunk(x, conv_dn, s_dn, s5_re, s5_im, conv_ffn, w, prompt):
    b, l, _ = x.shape
    n = b * l
    c = CHUNK if l % CHUNK == 0 else l
    assert l >= 2 * SUBLANES, "sequence shorter than two row tiles"
    cache8 = _pad_rows_top(conv_dn, SUBLANES)
    if prompt:
        u, o_dn, s_dn_new, tail = _mixer(x, w['n1'], w['w_in'], cache8, s_dn, w['dn_cw'], w['dn_gp'], w['dn_g'],
                                         tt=min(l, 128), c=c)
        conv_dn_new = tail[:, SUBLANES - (DN_CONV - 1):]
    else:
        qkv, z, u, ba = _in_proj(x.reshape(n, D_MODEL), w['n1'], w['w_in'], min(n, 512))
        qkv = qkv.reshape(b, l, QKV_WIDTH)
        o_dn, s_dn_new = _deltanet(qkv, z.reshape(b, l, DN_WIDTH), ba.reshape(b, l, LANES), cache8, s_dn,
                                   w['dn_cw'], w['dn_gp'], w['dn_g'], nb=min(b, 16), tt=l, c=c)
        conv_dn_new = qkv[:, l - (DN_CONV - 1):]

    s5_args = (w['pq'], w['qq'], w['mq'], w['tab_long' if prompt else 'tab_short'], w['dv'], w['glu_w'],
               w['glu_b'], w['s5_g'])
    if prompt:
        x0re = s5_re.reshape(b, 1, S5_LANES)
        x0im = s5_im.reshape(b, 1, S5_LANES)
        o_s5, fre, fim = _s5_chunked(u.reshape(b, l, S5_WIDTH), x0re, x0im, *s5_args,
                                     rows=min(l // S5_SLOTS, 128), seg=SUBLANES)
    else:
        seg = l // S5_SLOTS
        rep = lambda s: jnp.repeat(s.reshape(b, S5_LANES), seg, axis=0).reshape(1, b * seg, S5_LANES)
        o_s5, fre, fim = _s5_chunked(u.reshape(1, n, S5_WIDTH), rep(s5_re), rep(s5_im), *s5_args,
                                     rows=b * seg, seg=seg)
        o_s5 = o_s5.reshape(b, l, S5_WIDTH)
        fre = fre.reshape(b, seg, S5_LANES)[:, seg - 1]
        fim = fim.reshape(b, seg, S5_LANES)[:, seg - 1]
    s5_re_new = fre.reshape(b, S5_GROUPS, S5_STATE)
    s5_im_new = fim.reshape(b, S5_GROUPS, S5_STATE)

    ffn_args = (w['w_out_a'], w['w_out_b'], w['n2'], w['w_up'], w['fcw'], w['fcb'],
                w['w_down'], w['fg'])
    if prompt:
        prev = _pad_rows_top(conv_ffn, SUBLANES)
        y, tail = _ffn(x, o_dn, o_s5, prev, *ffn_args, tm=512, ls=512, carry=True)
        conv_ffn_new = tail[:, SUBLANES - (FFN_CONV - 1):]
    else:
        prev = jnp.pad(conv_ffn, ((0, 0), (0, l - (FFN_CONV - 1)), (0, 0))).reshape(1, n, D_FF)
        y, tail = _ffn(x.reshape(1, n, D_MODEL), o_dn.reshape(1, n, DN_WIDTH), o_s5.reshape(1, n, S5_WIDTH),
                       prev, *ffn_args, tm=min(n, 256), ls=l, carry=False)
        y = y.reshape(b, l, D_MODEL)
        conv_ffn_new = tail.reshape(b, l, D_FF)[:, l - (FFN_CONV - 1):]
    return y, (conv_dn_new[None], s_dn_new[None], s5_re_new[None], s5_im_new[None], conv_ffn_new[None])


def kernel(x_prompt, x_sample, cache_dn_conv, state_dn, state_s5_re, state_s5_im, cache_ffn_conv, norm1_g, w_in, dn_conv_w, dn_A_log, dn_dt_bias, dn_norm_g, s5_A_re, s5_A_im, s5_log_dt, s5_B_re, s5_B_im, s5_C_re, s5_C_im, s5_D, s5_glu_w, s5_glu_b, s5_norm_g, w_out, norm2_g, w_up, ffn_conv_w, ffn_conv_b, w_down, final_norm_g):
    assert w_in.shape[0] == 1, "single-layer trunk"
    o1 = QKV_WIDTH
    o2 = o1 + DN_WIDTH
    o4 = o2 + 2 * DN_HEADS
    wi = w_in[0]
    w_in_r = jnp.concatenate(
        [wi[:, :o2], wi[:, o4:], wi[:, o2:o4], jnp.zeros((D_MODEL, LANES - 2 * DN_HEADS), wi.dtype)],
        axis=1).astype(BF16)
    lane_pad = lambda v: jnp.pad(v, (BA_LANE_G, LANES - BA_LANE_G - DN_HEADS))
    ldt_lanes = jnp.repeat(s5_log_dt[0], S5_STATE)
    seg_short = x_sample.shape[1] // S5_SLOTS
    assert x_sample.shape[1] % S5_SLOTS == 0 and seg_short in (1, 2, 4), "sample sequences of 4, 8 or 16 rows"
    lane_row = lambda v: v.reshape(1, S5_LANES)
    tab_long, tab_short = [_s5_prep(lane_row(s5_A_re[0]), lane_row(s5_A_im[0]), lane_row(ldt_lanes), seg=s)
                           for s in (SUBLANES, seg_short)]
    quad_row = lambda v: v.reshape(S5_QUADS, 1, QUAD_LANES)
    pq, qq, mq = _s5_prep_quads(quad_row(s5_A_re[0]), quad_row(s5_A_im[0]), quad_row(ldt_lanes),
                                _quad_blocks(s5_B_re[0].transpose(0, 2, 1)), _quad_blocks(s5_B_im[0].transpose(0, 2, 1)),
                                _quad_blocks(s5_C_re[0]), _quad_blocks(s5_C_im[0]))
    w = {
        'n1': norm1_g, 'w_in': w_in_r,
        'dn_cw': jnp.pad(dn_conv_w[0], ((0, SUBLANES - DN_CONV), (0, 0))),
        'dn_gp': jnp.pad(jnp.stack([lane_pad(dn_A_log[0]), lane_pad(dn_dt_bias[0])]), ((0, SUBLANES - 2), (0, 0))),
        'dn_g': dn_norm_g,
        'tab_long': tab_long, 'tab_short': tab_short, 'pq': pq, 'qq': qq, 'mq': mq,
        'dv': s5_D, 'glu_w': s5_glu_w[0].astype(BF16), 'glu_b': s5_glu_b, 's5_g': s5_norm_g,
        'w_out_a': w_out[0, :DN_WIDTH].astype(BF16), 'w_out_b': w_out[0, DN_WIDTH:].astype(BF16),
        'n2': norm2_g, 'w_up': w_up[0].astype(BF16),
        'fcw': jnp.pad(ffn_conv_w[0], ((0, SUBLANES - FFN_CONV), (0, 0))), 'fcb': ffn_conv_b,
        'w_down': w_down[0].astype(BF16), 'fg': final_norm_g.reshape(1, D_MODEL),
    }
    bp = x_prompt.shape[0]
    zeros = lambda *s: jnp.zeros(s, F32)
    y_p, st_p = _trunk(x_prompt, zeros(bp, DN_CONV - 1, QKV_WIDTH), zeros(bp, DN_HEADS, DN_HEAD_DIM, DN_HEAD_DIM),
                       zeros(bp, S5_GROUPS, S5_STATE), zeros(bp, S5_GROUPS, S5_STATE),
                       zeros(bp, FFN_CONV - 1, D_FF), w, prompt=True)
    y_s, st_s = _trunk(x_sample, cache_dn_conv[0], state_dn[0], state_s5_re[0], state_s5_im[0],
                       cache_ffn_conv[0], w, prompt=False)
    return (y_p, y_s) + st_p + st_s
```

```python
import functools

import jax
import jax.numpy as jnp
from jax import lax
from jax.experimental import pallas as pl
from jax.experimental.pallas import tpu as pltpu

F32 = jnp.float32
BF16 = jnp.bfloat16
EPS = 1e-6

D_MODEL = 1024
DN_HEADS = 4
DN_HEAD_DIM = 128
DN_WIDTH = DN_HEADS * DN_HEAD_DIM
DN_CONV = 4
QKV_WIDTH = 3 * DN_WIDTH
S5_WIDTH = D_MODEL - DN_WIDTH
S5_GROUP = 16
S5_GROUPS = S5_WIDTH // S5_GROUP
S5_STATE = 64
S5_LANES = S5_GROUPS * S5_STATE
S5_SLOTS = 8
S5_QUADS = S5_GROUPS // 4
QUAD_LANES = 4 * S5_STATE
QUAD_IN = S5_SLOTS * 4 * S5_GROUP
D_FF = 2816
FFN_CONV = 3
CHUNK = 64

SUBLANES = 8
LANES = 128
BA_LANE_G = DN_HEADS
IN_PAD = QKV_WIDTH + DN_WIDTH + S5_WIDTH + LANES
MXU_DIM = 256
FF_CHUNK = 6 * MXU_DIM

VMEM_LIMIT = 56 * 1024 * 1024


def _dot(a, b):
    return jnp.dot(a, b, preferred_element_type=F32)


def _dot_tn(a, b):
    return lax.dot_general(a, b, (((0,), (0,)), ((), ())), preferred_element_type=F32)


def _split_bf16(a):
    hi = a.astype(BF16)
    lo = (a - hi.astype(F32)).astype(BF16)
    return hi, lo


def _bmm(a, b):
    return lax.dot_general(a, b, (((2,), (1,)), ((0,), (0,))), preferred_element_type=F32)


def _bmm_nt(a, b):
    return lax.dot_general(a, b, (((2,), (2,)), ((0,), (0,))), preferred_element_type=F32)


def _bmm_nt3(a, b):
    ah, al = _split_bf16(a)
    bh, bl = _split_bf16(b)
    return _bmm_nt(ah, bh) + (_bmm_nt(al, bh) + _bmm_nt(ah, bl))


def _rms(x, g):
    return x * lax.rsqrt(jnp.mean(x * x, axis=-1, keepdims=True) + EPS) * g


def _const_spec(shape):
    nd = len(shape)
    return pl.BlockSpec(shape, lambda *_: (0,) * nd, pipeline_mode=pl.Buffered(1))


def _params(n_axes):
    return pltpu.CompilerParams(dimension_semantics=("arbitrary",) * n_axes,
                                vmem_limit_bytes=VMEM_LIMIT)


def _cmul(a, b):
    return a[0] * b[0] - a[1] * b[1], a[0] * b[1] + a[1] * b[0]


def _zoh(are, aim, ldt):
    dt = jnp.exp(ldt)
    mag = jnp.exp(are * dt)
    ang = aim * dt
    lr = mag * jnp.cos(ang)
    li = mag * jnp.sin(ang)
    den = are * are + aim * aim
    f_re = ((lr - 1.0) * are + li * aim) / den
    f_im = (li * are - (lr - 1.0) * aim) / den
    return (lr, li), (f_re, f_im)


def _write_scan_tables(tab_ref, step, seg):
    pw = [step]
    for _ in range(seg - 1):
        pw.append(_cmul(pw[-1], step))
    shape = (SUBLANES, step[0].shape[-1])
    row = lax.broadcasted_iota(jnp.int32, shape, 0) & (seg - 1)
    zero = jnp.zeros(shape, F32)
    pre, pim = zero, zero
    for r in range(seg):
        pre = jnp.where(row == r, pw[r][0], pre)
        pim = jnp.where(row == r, pw[r][1], pim)
    tab_ref[...] = jnp.zeros(tab_ref.shape, F32)
    tab_ref[0:8, :] = pre
    tab_ref[8:16, :] = pim
    for lvl, d in enumerate(_scan_levels(seg)):
        tab_ref[16 + 16 * lvl:24 + 16 * lvl, :] = jnp.where(row >= d, pw[d - 1][0], zero)
        tab_ref[24 + 16 * lvl:32 + 16 * lvl, :] = jnp.where(row >= d, pw[d - 1][1], zero)


def _scan_levels(seg):
    return [d for d in (1, 2, 4) if d < seg]


def _s5prep_kernel(a_ref, tab_ref, *, seg):
    lam, _ = _zoh(a_ref[0], a_ref[1], a_ref[2])
    lam_c = lam
    for _ in range(S5_SLOTS - 1):
        lam_c = _cmul(lam_c, lam)
    _write_scan_tables(tab_ref, lam_c, seg)


def _s5_prep(a_lanes, seg):
    return pl.pallas_call(
        functools.partial(_s5prep_kernel, seg=seg),
        out_shape=jax.ShapeDtypeStruct((64, S5_LANES), F32),
        compiler_params=pltpu.CompilerParams(vmem_limit_bytes=VMEM_LIMIT),
        name="s5_prep",
    )(a_lanes)


def _s5prepq_kernel(a_ref, bc_ref, pq_ref, qq_ref, mq_ref):
    lam, f = _zoh(a_ref[0], a_ref[1], a_ref[2])
    bb = _cmul(f, (bc_ref[0], bc_ref[1]))
    ct = (bc_ref[2], bc_ref[3])
    one = (jnp.ones_like(lam[0]), jnp.zeros_like(lam[0]))
    pw = [one]
    for _ in range(S5_SLOTS):
        pw.append(_cmul(pw[-1], lam))
    blk = 4 * S5_GROUP
    for s in range(S5_SLOTS):
        p_re, p_im = _cmul(pw[S5_SLOTS - 1 - s], bb)
        pq_ref[:, s * blk:(s + 1) * blk, :QUAD_LANES] = p_re.astype(BF16)
        pq_ref[:, s * blk:(s + 1) * blk, QUAD_LANES:] = p_im.astype(BF16)
    cl = [_cmul(pw[e], ct) for e in range(S5_SLOTS + 1)]
    qt_re = jnp.concatenate([cl[t + 1][0] for t in range(S5_SLOTS)], axis=1)
    qt_im = jnp.concatenate([-cl[t + 1][1] for t in range(S5_SLOTS)], axis=1)
    for n in range(S5_QUADS):
        qq_ref[n, :QUAD_LANES, :] = qt_re[n].T.astype(BF16)
        qq_ref[n, QUAD_LANES:, :] = qt_im[n].T.astype(BF16)
    lag_re = jnp.concatenate([cl[tau][0] for tau in range(S5_SLOTS)], axis=1)
    lag_im = jnp.concatenate([cl[tau][1] for tau in range(S5_SLOTS)], axis=1)
    lags = _bmm_nt3(bb[0], lag_re) - _bmm_nt3(bb[1], lag_im)
    for s in range(S5_SLOTS):
        m = lags if s == 0 else jnp.concatenate(
            [jnp.zeros((S5_QUADS, blk, s * blk), F32), lags[:, :, :QUAD_IN - s * blk]], axis=2)
        mq_ref[:, s * blk:(s + 1) * blk, :] = m.astype(BF16)


def _s5_prep_quads(a_quads, bc_quads):
    w = lambda r, c_: jax.ShapeDtypeStruct((S5_QUADS, r, c_), BF16)
    return pl.pallas_call(
        _s5prepq_kernel,
        out_shape=(w(QUAD_IN, 2 * QUAD_LANES), w(2 * QUAD_LANES, QUAD_IN), w(QUAD_IN, QUAD_IN)),
        compiler_params=pltpu.CompilerParams(vmem_limit_bytes=VMEM_LIMIT),
        name="s5_prep_quads",
    )(a_quads, bc_quads)


IN_MAIN = QKV_WIDTH + DN_WIDTH


def _w_cols(wm_ref, wt_ref, lo, hi):
    return wm_ref[:, lo:hi] if hi <= IN_MAIN else wt_ref[:, lo - IN_MAIN:hi - IN_MAIN]


def _inproj_kernel(x_ref, g_ref, wm_ref, wt_ref, qkv_ref, z_ref, u_ref, ba_ref):
    h = _rms(x_ref[...], g_ref[...]).astype(BF16)
    o1 = QKV_WIDTH
    o2 = o1 + DN_WIDTH
    o3 = o2 + S5_WIDTH
    qkv_ref[...] = _dot(h, _w_cols(wm_ref, wt_ref, 0, o1))
    z_ref[...] = _dot(h, _w_cols(wm_ref, wt_ref, o1, o2))
    u_ref[...] = _dot(h, _w_cols(wm_ref, wt_ref, o2, o3))
    ba_ref[...] = _dot(h, _w_cols(wm_ref, wt_ref, o3, IN_PAD))


def _in_proj(x2d, g, wm, wt, tm):
    n = x2d.shape[0]
    row = lambda w_: pl.BlockSpec((tm, w_), lambda i: (i, 0))
    return pl.pallas_call(
        _inproj_kernel,
        grid=(n // tm,),
        in_specs=[row(D_MODEL), _const_spec((1, D_MODEL)), _const_spec((D_MODEL, IN_MAIN)),
                  _const_spec((D_MODEL, IN_PAD - IN_MAIN))],
        out_specs=(row(QKV_WIDTH), row(DN_WIDTH), row(S5_WIDTH), row(LANES)),
        out_shape=(jax.ShapeDtypeStruct((n, QKV_WIDTH), F32),
                   jax.ShapeDtypeStruct((n, DN_WIDTH), F32),
                   jax.ShapeDtypeStruct((n, S5_WIDTH), F32),
                   jax.ShapeDtypeStruct((n, LANES), F32)),
        compiler_params=_params(1),
        name="in_proj",
    )(x2d, g, wm, wt)


def _stack_heads(a):
    dh = DN_HEAD_DIM
    return jnp.concatenate([a[:, :, h * dh:(h + 1) * dh] for h in range(DN_HEADS)], axis=1)


def _unstack_heads(a, c):
    return jnp.concatenate([a[:, h * c:(h + 1) * c, :] for h in range(DN_HEADS)], axis=2)


def _dn_kernel(qkv_ref, z_ref, ba_ref, cache_ref, s0_ref, cw_ref, gp_ref, ng_ref, o_ref, state_ref, halo_s,
               *, nb, tt, c):
    _dn_restart(pl.program_id(1) == 0, cache_ref, s0_ref, halo_s, state_ref)
    stash = _dn_phase_a(qkv_ref, z_ref, ba_ref, cw_ref, gp_ref, halo_s, nb=nb, tt=tt, c=c)
    for step in _dn_phase_b(stash, ng_ref, o_ref, state_ref, nb=nb, tt=tt, c=c):
        step()


def _dn_restart(first, cache_ref, s0_ref, halo_s, state_ref):
    @pl.when(first)
    def _():
        halo_s[...] = cache_ref[...]
        state_ref[...] = s0_ref[...]


def _dn_phase_a(qkv_ref, z_ref, ba_ref, cw_ref, gp_ref, halo_s, *, nb, tt, c, side=()):
    nh = DN_HEADS
    dh = DN_HEAD_DIM
    nck = tt // c
    npb = nb * nck
    hc = nh * c
    side = list(side)

    def side_work(n=1):
        for _ in range(n):
            if side:
                side.pop(0)()

    stack = _stack_heads
    unstack = functools.partial(_unstack_heads, c=c)

    first_row = lax.broadcasted_iota(jnp.int32, (SUBLANES, MXU_DIM), 0) == 0

    def shift_rows(y, carried):
        rolled = pltpu.roll(y, 1, axis=1)
        top = jnp.where(first_row[None], carried, rolled[:, :SUBLANES])
        return jnp.concatenate([top, rolled[:, SUBLANES:]], axis=1)

    conv_parts = []
    for lo in range(0, QKV_WIDTH, MXU_DIM):
        side_work()
        hi = lo + MXU_DIM
        cw = cw_ref[:, lo:hi]
        x = qkv_ref[:, :, lo:hi]
        prev = halo_s[:, :, lo:hi]
        acc = x * cw[0:1]
        carried = prev[:, SUBLANES - 1:] * cw[0:1]
        for j in range(1, DN_CONV):
            acc = shift_rows(acc, carried) + x * cw[j:j + 1]
            if j < DN_CONV - 1:
                carried = prev[:, SUBLANES - 1 - j:SUBLANES - j] * cw[0:1]
                for i in range(1, j + 1):
                    carried = carried + prev[:, SUBLANES - 1 - j + i:SUBLANES - j + i] * cw[i:i + 1]
        halo_s[:, :, lo:hi] = x[:, tt - SUBLANES:]
        conv_parts.append((acc * jax.nn.sigmoid(acc)).reshape(npb, c, hi - lo))
    per = DN_WIDTH // MXU_DIM
    q, k, v = [stack(jnp.concatenate(conv_parts[i * per:(i + 1) * per], axis=2)) for i in range(3)]
    side_work()
    q = q * (lax.rsqrt(jnp.sum(q * q, axis=-1, keepdims=True) + EPS) * (dh ** -0.5))
    k = k * lax.rsqrt(jnp.sum(k * k, axis=-1, keepdims=True) + EPS)

    side_work()
    ba = ba_ref[...].reshape(nb * tt, LANES)
    beta = jax.nn.sigmoid(ba).reshape(npb, c, LANES)
    g2 = -jnp.exp(gp_ref[0:1, :]) * jax.nn.softplus(ba + gp_ref[1:2, :])
    row_in_chunk = lax.broadcasted_iota(jnp.int32, (nb * tt, LANES), 0) & (c - 1)
    d = 1
    while d < c:
        g2 = g2 + jnp.where(row_in_chunk >= d, pltpu.roll(g2, d, axis=0), 0.0)
        d *= 2
    g = g2.reshape(npb, c, LANES)

    g_cols = [g[:, :, BA_LANE_G + h:BA_LANE_G + h + 1] for h in range(nh)]
    b_cols = [beta[:, :, h:h + 1] for h in range(nh)]
    g_col = jnp.concatenate(g_cols, axis=1)
    b_col = jnp.concatenate(b_cols, axis=1)
    g_last = jnp.concatenate([jnp.broadcast_to(gc[:, c - 1:c, :], (npb, c, 1)) for gc in g_cols], axis=1)

    hp = min(nh, LANES // c)
    pieces = []
    for h0 in range(0, nh, hp):
        slab = jnp.concatenate(
            [(g2 if hh == 0 else pltpu.roll(g2, LANES - hh, axis=1)).reshape(npb, c, LANES)
             for hh in range(h0, h0 + hp)], axis=1)
        if hp * c < LANES:
            slab = jnp.concatenate([slab, jnp.zeros((npb, LANES - hp * c, LANES), F32)], axis=1)
        rows = [slab[p].T[BA_LANE_G:BA_LANE_G + 1, :hp * c] for p in range(npb)]
        pieces.append(jnp.stack(rows, axis=0))
    g_row = jnp.concatenate(pieces, axis=2)

    ri = lax.broadcasted_iota(jnp.int32, (c, hc), 0)
    lane = lax.broadcasted_iota(jnp.int32, (c, hc), 1)
    cj = lane & (c - 1)
    causal = (ri >= cj)[None]
    strict = (ri > cj)[None]

    def cat_from_cols(cols):
        out = jnp.broadcast_to(cols[nh - 1], (npb, c, hc))
        for h in range(nh - 2, -1, -1):
            out = jnp.where((lane < (h + 1) * c)[None], jnp.broadcast_to(cols[h], (npb, c, hc)), out)
        return out

    decay = jnp.exp(jnp.where(causal, cat_from_cols(g_cols) - g_row, -jnp.inf))

    side_work()
    kb = k.astype(BF16)
    br = lax.broadcasted_iota(jnp.int32, (hc, nh * dh), 0)
    bl = lax.broadcasted_iota(jnp.int32, (hc, nh * dh), 1)
    head_of_row = sum(jnp.where(br >= h * c, 1, 0) for h in range(1, nh))
    head_of_lane = sum(jnp.where(bl >= h * dh, 1, 0) for h in range(1, nh))
    k_bd = jnp.where((head_of_row == head_of_lane)[None], jnp.concatenate([kb] * nh, axis=2), 0.0)
    qk_lhs = jnp.concatenate([unstack(q), unstack(k)], axis=1).astype(BF16)
    qkk = _bmm_nt(qk_lhs, k_bd)
    qk = qkk[:, :c]
    kk = qkk[:, c:]

    sr = lax.broadcasted_iota(jnp.int32, (hc, hc), 0)
    sl = lax.broadcasted_iota(jnp.int32, (hc, hc), 1)
    shift = c.bit_length() - 1
    same_head = ((sr >> shift) == (sl >> shift))[None]

    def bd(m):
        return jnp.where(same_head, jnp.concatenate([m] * nh, axis=1), 0.0)

    lm = jnp.where(strict, cat_from_cols(b_cols) * kk * decay, 0.0)
    tinv = jnp.where((ri == cj)[None], 1.0, 0.0) - jnp.where(((ri >> 1) == (cj >> 1))[None], lm, 0.0)
    s = 2
    while s < c:
        sh = s.bit_length()
        lower_left = ((ri >> sh) == (cj >> sh)) & ((ri & s) != 0) & ((cj & s) == 0)
        side_work()
        a_off = jnp.where(lower_left[None], lm, 0.0).astype(BF16)
        xm = _bmm(tinv.astype(BF16), bd(a_off))
        tinv = tinv - _bmm(xm.astype(BF16), bd(tinv.astype(BF16)))
        s *= 2

    e_g = jnp.exp(g_col)
    rhs = jnp.concatenate([v * b_col, k * (b_col * e_g)], axis=2).astype(BF16)
    sol = _bmm(bd(tinv.astype(BF16)), rhs)
    sol_v = sol[:, :, :dh]
    sol_k = sol[:, :, dh:]
    qe = q * e_g
    w_qe = jnp.concatenate([part[:, h * c:(h + 1) * c] for h in range(nh) for part in (sol_k, qe)],
                           axis=1).astype(BF16)
    kd = (k * jnp.exp(g_last - g_col)).astype(BF16)
    qkd_bd = bd((qk * decay).astype(BF16))
    s_decay = jnp.exp(g_last)
    side_work(len(side))
    return sol_v, w_qe, kd, qkd_bd, s_decay, lambda: stack(z_ref[...].reshape(npb, c, DN_WIDTH))


def _dn_phase_b(stash, ng_ref, o_ref, state_ref, *, nb, tt, c):
    sol_v, w_qe, kd, qkd_bd, s_decay, zs = stash
    nh = DN_HEADS
    dh = DN_HEAD_DIM
    nck = tt // c
    outs = {}
    pending = {}

    def outputs(ck):
        def run():
            prods = {}
            for b in range(nb):
                p = b * nck + ck
                states = [state_ref[b, h] for h in range(nh)]
                prods[b] = [_dot(w_qe[p, h * 2 * c:(h + 1) * 2 * c, :], states[h].astype(BF16)) for h in range(nh)]
                pending[p] = states
            for b in range(nb):
                p = b * nck + ck
                u = sol_v[p] - jnp.concatenate([r[:c] for r in prods[b]], axis=0)
                ub = u.astype(BF16)
                outs[p] = jnp.concatenate([r[c:] for r in prods[b]], axis=0) + _dot(qkd_bd[p], ub)
                pending[p] = (pending[p], ub)
        return run

    def update(ck):
        def run():
            for b in range(nb):
                p = b * nck + ck
                states, ub = pending.pop(p)
                for h in range(nh):
                    r0 = h * c
                    state_ref[b, h] = (states[h] * s_decay[p, r0:r0 + 1, :]
                                       + _dot_tn(kd[p, r0:r0 + c, :], ub[r0:r0 + c, :]))
        return run

    def finish():
        o = jnp.stack([outs[p] for p in range(nb * nck)], axis=0)
        z = zs()
        o = _rms(o, ng_ref[...]) * (z * jax.nn.sigmoid(z))
        o_ref[...] = _unstack_heads(o, c).reshape(nb, tt, DN_WIDTH).astype(BF16)

    return [f(ck) for ck in range(nck) for f in (outputs, update)] + [finish]


def _deltanet(qkv, z, ba, cache8, s0, cw, gp, ng, nb, tt, c):
    b, l, _ = qkv.shape
    seq = lambda w_: pl.BlockSpec((nb, tt, w_), lambda i, j: (i, j, 0))
    per_b = lambda *s: pl.BlockSpec((nb,) + s, lambda i, j: (i,) + (0,) * len(s))
    return pl.pallas_call(
        functools.partial(_dn_kernel, nb=nb, tt=tt, c=c),
        grid=(b // nb, l // tt),
        in_specs=[seq(QKV_WIDTH), seq(DN_WIDTH), seq(LANES), per_b(SUBLANES, QKV_WIDTH),
                  per_b(DN_HEADS, DN_HEAD_DIM, DN_HEAD_DIM),
                  _const_spec((SUBLANES, QKV_WIDTH)), _const_spec((SUBLANES, LANES)),
                  _const_spec((1, DN_HEAD_DIM))],
        out_specs=(seq(DN_WIDTH), per_b(DN_HEADS, DN_HEAD_DIM, DN_HEAD_DIM)),
        out_shape=(jax.ShapeDtypeStruct((b, l, DN_WIDTH), BF16),
                   jax.ShapeDtypeStruct((b, DN_HEADS, DN_HEAD_DIM, DN_HEAD_DIM), F32)),
        scratch_shapes=[pltpu.VMEM((nb, SUBLANES, QKV_WIDTH), F32)],
        compiler_params=_params(2),
        name="deltanet",
    )(qkv, z, ba, cache8, s0, cw, gp, ng)


def _mixer_kernel(x_ref, g_ref, wm_ref, wt_ref, cache_ref, s0_ref, cw_ref, gp_ref, ng_ref,
                  u_ref, o_ref, state_ref, tail_ref, p_cur, p_nxt, halo_s, *, nb, tt, c):
    i = pl.program_id(0)
    o1 = QKV_WIDTH
    o2 = o1 + DN_WIDTH
    o3 = o2 + S5_WIDTH

    @pl.when(i == 0)
    def _():
        p_cur[...] = jnp.zeros(p_cur.shape, F32)

    _dn_restart(i <= 1, cache_ref, s0_ref, halo_s, state_ref)

    h = _rms(x_ref[...].reshape(nb * tt, D_MODEL), g_ref[...]).astype(BF16)

    def project(lo, hi):
        def run():
            res = _dot(h, _w_cols(wm_ref, wt_ref, lo, hi)).reshape(nb, tt, hi - lo)
            if o2 <= lo < o3:
                u_ref[:, :, lo - o2:hi - o2] = res
            else:
                p_nxt[:, :, lo:hi] = res
        return run

    side = [project(lo, min(lo + MXU_DIM, IN_PAD)) for lo in range(0, IN_PAD, MXU_DIM)]
    stash = _dn_phase_a(p_cur.at[:, :, 0:o1], p_cur.at[:, :, o1:o2], p_cur.at[:, :, o3:IN_PAD],
                        cw_ref, gp_ref, halo_s, nb=nb, tt=tt, c=c, side=side)
    for step in _dn_phase_b(stash, ng_ref, o_ref, state_ref, nb=nb, tt=tt, c=c):
        step()
    tail_ref[...] = halo_s[...]
    for lo, hi in ((0, o1), (o1, o2), (o3, IN_PAD)):
        p_cur[:, :, lo:hi] = p_nxt[:, :, lo:hi]


def _mixer(x, g, wm, wt, cache8, s0, cw, gp, ng, tt, c):
    nb, l, _ = x.shape
    nt = l // tt
    cur = lambda w_: pl.BlockSpec((nb, tt, w_), lambda i: (0, jnp.minimum(i, nt - 1), 0))
    prev = lambda w_: pl.BlockSpec((nb, tt, w_), lambda i: (0, jnp.maximum(i - 1, 0), 0))
    return pl.pallas_call(
        functools.partial(_mixer_kernel, nb=nb, tt=tt, c=c),
        grid=(nt + 1,),
        in_specs=[cur(D_MODEL), _const_spec((1, D_MODEL)), _const_spec((D_MODEL, IN_MAIN)),
                  _const_spec((D_MODEL, IN_PAD - IN_MAIN)), _const_spec((nb, SUBLANES, QKV_WIDTH)), _const_spec((nb, DN_HEADS, DN_HEAD_DIM, DN_HEAD_DIM)),
                  _const_spec((SUBLANES, QKV_WIDTH)), _const_spec((SUBLANES, LANES)),
                  _const_spec((1, DN_HEAD_DIM))],
        out_specs=(cur(S5_WIDTH), prev(DN_WIDTH),
                   pl.BlockSpec((nb, DN_HEADS, DN_HEAD_DIM, DN_HEAD_DIM), lambda i: (0, 0, 0, 0)),
                   pl.BlockSpec((nb, SUBLANES, QKV_WIDTH), lambda i: (0, 0, 0))),
        out_shape=(jax.ShapeDtypeStruct((nb, l, S5_WIDTH), F32),
                   jax.ShapeDtypeStruct((nb, l, DN_WIDTH), BF16),
                   jax.ShapeDtypeStruct((nb, DN_HEADS, DN_HEAD_DIM, DN_HEAD_DIM), F32),
                   jax.ShapeDtypeStruct((nb, SUBLANES, QKV_WIDTH), F32)),
        scratch_shapes=[pltpu.VMEM((nb, tt, IN_PAD), F32), pltpu.VMEM((nb, tt, IN_PAD), F32),
                        pltpu.VMEM((nb, SUBLANES, QKV_WIDTH), F32)],
        compiler_params=_params(1),
        name="mixer",
    )(x, g, wm, wt, cache8, s0, cw, gp, ng)


def _s5c_kernel(u_ref, x0re_ref, x0im_ref, pq_ref, qq_ref, mq_ref, tab_ref, dv_ref, gw_ref, gb_ref, ng_ref,
                o_ref, fre_ref, fim_ref, xr_s, xi_s, car_s, u_s, o_s, *, rows, seg):
    carry = seg == SUBLANES
    if carry:
        @pl.when(pl.program_id(1) == 0)
        def _():
            car_s[0:1, :] = x0re_ref[0]
            car_s[1:2, :] = x0im_ref[0]

    ncol = S5_WIDTH // LANES
    for k in range(ncol):
        u_s[k] = u_ref[0, :, k * LANES:(k + 1) * LANES]
    us = [jnp.concatenate([u_s[k, pl.ds(s, rows, stride=S5_SLOTS), :] for k in range(ncol)], axis=1)
          for s in range(S5_SLOTS)]
    ubs = [a.astype(BF16) for a in us]
    blk = 4 * S5_GROUP
    ql = QUAD_LANES
    uq =[jnp.concatenate([ubs[s][:, n * blk:(n + 1) * blk] for s in range(S5_SLOTS)], axis=1)
          for n in range(S5_QUADS)]
    for n in range(S5_QUADS):
        inc = _dot(uq[n], pq_ref[n])
        xr_s[:, n * ql:(n + 1) * ql] = inc[:, :ql]
        xi_s[:, n * ql:(n + 1) * ql] = inc[:, ql:]

    seg_row = lax.broadcasted_iota(jnp.int32, (SUBLANES, S5_LANES), 0) & (seg - 1)

    def block_body(rb, cin):
        r0 = pl.multiple_of(rb * SUBLANES, SUBLANES)
        if carry:
            c_re, c_im = cin
        else:
            c_re = x0re_ref[0, pl.ds(r0, SUBLANES), :]
            c_im = x0im_ref[0, pl.ds(r0, SUBLANES), :]
        xr = xr_s[pl.ds(r0, SUBLANES), :]
        xi = xi_s[pl.ds(r0, SUBLANES), :]
        for lvl, d in enumerate(_scan_levels(seg)):
            m_re = tab_ref[16 + 16 * lvl:24 + 16 * lvl, :]
            m_im = tab_ref[24 + 16 * lvl:32 + 16 * lvl, :]
            sr = pltpu.roll(xr, d, axis=0)
            si = pltpu.roll(xi, d, axis=0)
            xr, xi = xr + (m_re * sr - m_im * si), xi + (m_re * si + m_im * sr)
        p_re = tab_ref[0:8, :]
        p_im = tab_ref[8:16, :]
        xr, xi = xr + (p_re * c_re - p_im * c_im), xi + (p_re * c_im + p_im * c_re)
        xr_s[pl.ds(r0, SUBLANES), :] = jnp.where(seg_row == 0, c_re, pltpu.roll(xr, 1, axis=0))
        xi_s[pl.ds(r0, SUBLANES), :] = jnp.where(seg_row == 0, c_im, pltpu.roll(xi, 1, axis=0))
        if carry:
            return xr[SUBLANES - 1:, :], xi[SUBLANES - 1:, :]
        fre_ref[0, pl.ds(r0, SUBLANES), :] = xr
        fim_ref[0, pl.ds(r0, SUBLANES), :] = xi
        return cin

    if carry:
        c_re, c_im = lax.fori_loop(0, rows // SUBLANES, block_body, (car_s[0:1, :], car_s[1:2, :]))
        car_s[0:1, :] = c_re
        car_s[1:2, :] = c_im
        fre_ref[0] = c_re
        fim_ref[0] = c_im
    else:
        lax.fori_loop(0, rows // SUBLANES, block_body, 0)

    ys = []
    for n in range(S5_QUADS):
        xs = jnp.concatenate([xr_s[:, n * ql:(n + 1) * ql], xi_s[:, n * ql:(n + 1) * ql]], axis=1).astype(BF16)
        half = QUAD_IN // 2
        intra = jnp.concatenate([_dot(uq[n][:, :half], mq_ref[n, :half, :half]), _dot(uq[n], mq_ref[n, :, half:])],
                                axis=1)
        ys.append(intra + _dot(xs, qq_ref[n]))
    for s in range(S5_SLOTS):
        y = jnp.concatenate([ys[n][:, s * blk:(s + 1) * blk] for n in range(S5_QUADS)], axis=1)
        y = jax.nn.gelu(y + dv_ref[...] * us[s]).astype(BF16)
        gl = _dot(y, gw_ref[...]) + gb_ref[...]
        o = gl[:, :S5_WIDTH] * jax.nn.sigmoid(gl[:, S5_WIDTH:])
        o = _rms(o, ng_ref[...])
        for k in range(ncol):
            o_s[k, pl.ds(s, rows, stride=S5_SLOTS), :] = o[:, k * LANES:(k + 1) * LANES]
    o_ref[0] = jnp.concatenate([o_s[k] for k in range(ncol)], axis=1).astype(BF16)


def _s5_chunked(u, x0re, x0im, pq, qq, mq, tabc, dv, gw, gb, ng, rows, seg):
    b, l, _ = u.shape
    tt = rows * S5_SLOTS
    seq = pl.BlockSpec((1, tt, S5_WIDTH), lambda i, j: (i, j, 0))
    if seg == SUBLANES:
        st = pl.BlockSpec((1, 1, S5_LANES), lambda i, j: (i, 0, 0))
    else:
        st = pl.BlockSpec((1, rows, S5_LANES), lambda i, j: (i, j, 0))
    return pl.pallas_call(
        functools.partial(_s5c_kernel, rows=rows, seg=seg),
        grid=(b, l // tt),
        in_specs=[seq, st, st,
                  _const_spec((S5_QUADS, QUAD_IN, 2 * QUAD_LANES)),
                  _const_spec((S5_QUADS, 2 * QUAD_LANES, QUAD_IN)),
                  _const_spec((S5_QUADS, QUAD_IN, QUAD_IN)), _const_spec((64, S5_LANES)),
                  _const_spec((1, S5_WIDTH)), _const_spec((S5_WIDTH, 2 * S5_WIDTH)),
                  _const_spec((1, 2 * S5_WIDTH)), _const_spec((1, S5_WIDTH))],
        out_specs=(seq, st, st),
        out_shape=(jax.ShapeDtypeStruct((b, l, S5_WIDTH), BF16),
                   jax.ShapeDtypeStruct(x0re.shape, F32),
                   jax.ShapeDtypeStruct(x0re.shape, F32)),
        scratch_shapes=[pltpu.VMEM((rows, S5_LANES), F32), pltpu.VMEM((rows, S5_LANES), F32),
                        pltpu.VMEM((SUBLANES, S5_LANES), F32),
                        pltpu.VMEM((S5_WIDTH // LANES, tt, LANES), F32),
                        pltpu.VMEM((S5_WIDTH // LANES, tt, LANES), F32)],
        compiler_params=_params(2),
        name="s5_chunked",
    )(u, x0re, x0im, pq, qq, mq, tabc, dv, gw, gb, ng)


def _ffn_kernel(x_ref, odn_ref, os5_ref, prev_ref, woa_ref, wob_ref, n2_ref, wu_ref,
                fcw_ref, fcb_ref, wd_ref, fg_ref, y_ref, tail_ref, halo_s, *, tm, ls, carry):
    if carry:
        @pl.when(pl.program_id(1) == 0)
        def _():
            halo_s[...] = prev_ref[0]

    x1 = x_ref[0] + _dot(odn_ref[0], woa_ref[...]) + _dot(os5_ref[0], wob_ref[...])
    h2 = _rms(x1, n2_ref[...]).astype(BF16)
    down = None
    for lo in range(0, D_FF, FF_CHUNK):
        hi = min(lo + FF_CHUNK, D_FF)
        gate = _dot(h2, wu_ref[:, lo:hi])
        val = _dot(h2, wu_ref[:, D_FF + lo:D_FF + hi])
        if carry:
            xx = jnp.concatenate([halo_s[:, lo:hi], gate], axis=0)
            g2 = xx[SUBLANES - 2:SUBLANES - 2 + tm]
            g1 = xx[SUBLANES - 1:SUBLANES - 1 + tm]
            halo_s[:, lo:hi] = gate[tm - SUBLANES:, :]
            tail_ref[0, :, lo:hi] = gate[tm - SUBLANES:, :]
        else:
            r = lax.broadcasted_iota(jnp.int32, (tm, hi - lo), 0) & (ls - 1)
            prev = prev_ref[0, :, lo:hi]
            g1 = jnp.where(r < 1, pltpu.roll(prev, tm - 1, axis=0), pltpu.roll(gate, 1, axis=0))
            g2 = jnp.where(r < 2, prev, pltpu.roll(gate, 2, axis=0))
            tail_ref[0, :, lo:hi] = gate
        cw = fcw_ref[:, lo:hi]
        pre = g2 * cw[0:1] + g1 * cw[1:2] + gate * cw[2:3] + fcb_ref[:, lo:hi]
        act = pre * jax.nn.sigmoid(pre) * val
        part = _dot(act.astype(BF16), wd_ref[lo:hi, :])
        down = part if down is None else down + part
    y_ref[0] = _rms(x1 + down, fg_ref[...])


def _ffn(x, odn, os5, prev, woa, wob, n2, wu, fcw, fcb, wd, fg, tm, ls, carry):
    b, l, _ = x.shape
    seq = lambda w_: pl.BlockSpec((1, tm, w_), lambda i, j: (i, j, 0))
    if carry:
        prev_spec = pl.BlockSpec((1, SUBLANES, D_FF), lambda i, j: (i, 0, 0))
        tail_spec = pl.BlockSpec((1, SUBLANES, D_FF), lambda i, j: (i, 0, 0))
        tail_shape = jax.ShapeDtypeStruct((b, SUBLANES, D_FF), F32)
    else:
        prev_spec = seq(D_FF)
        tail_spec = seq(D_FF)
        tail_shape = jax.ShapeDtypeStruct((b, l, D_FF), F32)
    return pl.pallas_call(
        functools.partial(_ffn_kernel, tm=tm, ls=ls, carry=carry),
        grid=(b, l // tm),
        in_specs=[seq(D_MODEL), seq(DN_WIDTH), seq(S5_WIDTH), prev_spec,
                  _const_spec((DN_WIDTH, D_MODEL)), _const_spec((S5_WIDTH, D_MODEL)),
                  _const_spec((1, D_MODEL)), _const_spec((D_MODEL, 2 * D_FF)),
                  _const_spec((SUBLANES, D_FF)), _const_spec((1, D_FF)), _const_spec((D_FF, D_MODEL)),
                  _const_spec((1, D_MODEL))],
        out_specs=(seq(D_MODEL), tail_spec),
        out_shape=(jax.ShapeDtypeStruct((b, l, D_MODEL), F32), tail_shape),
        scratch_shapes=[pltpu.VMEM((SUBLANES, D_FF), F32)],
        compiler_params=_params(2),
        name="ffn",
    )(x, odn, os5, prev, woa, wob, n2, wu, fcw, fcb, wd, fg)


def _pad_rows_top(a, rows):
    return jnp.pad(a, ((0, 0), (rows - a.shape[1], 0), (0, 0)))


def _quad_blocks(m):
    n = m.shape[0]
    same = jnp.eye(4, dtype=bool)[:, None, :, None]
    blocks = jnp.where(same, m.reshape(n, S5_QUADS, 4, S5_GROUP, 1, S5_STATE), 0.0)
    return blocks.reshape(n, S5_QUADS, 4 * S5_GROUP, QUAD_LANES)


def _trunk(x, conv_dn, s_dn, s5_re, s5_im, conv_ffn, w, prompt):
    b, l, _ = x.shape
    n = b * l
    c = CHUNK if l % CHUNK == 0 else l
    assert l >= 2 * SUBLANES, "sequence shorter than two row tiles"
    cache8 = _pad_rows_top(conv_dn, SUBLANES)
    if prompt:
        u, o_dn, s_dn_new, tail = _mixer(x, w['n1'], w['w_in_main'], w['w_in_tail'], cache8, s_dn,
                                         w['dn_cw'], w['dn_gp'], w['dn_g'],
                                         tt=min(l, 128), c=c)
        conv_dn_new = tail[:, SUBLANES - (DN_CONV - 1):]
    else:
        qkv, z, u, ba = _in_proj(x.reshape(n, D_MODEL), w['n1'], w['w_in_main'], w['w_in_tail'], min(n, 512))
        qkv = qkv.reshape(b, l, QKV_WIDTH)
        o_dn, s_dn_new = _deltanet(qkv, z.reshape(b, l, DN_WIDTH), ba.reshape(b, l, LANES), cache8, s_dn,
                                   w['dn_cw'], w['dn_gp'], w['dn_g'], nb=min(b, 16), tt=l, c=c)
        conv_dn_new = qkv[:, l - (DN_CONV - 1):]

    s5_args = (w['pq'], w['qq'], w['mq'], w['tab_long' if prompt else 'tab_short'], w['dv'], w['glu_w'],
               w['glu_b'], w['s5_g'])
    if prompt:
        x0re = s5_re.reshape(b, 1, S5_LANES)
        x0im = s5_im.reshape(b, 1, S5_LANES)
        o_s5, fre, fim = _s5_chunked(u.reshape(b, l, S5_WIDTH), x0re, x0im, *s5_args,
                                     rows=min(l // S5_SLOTS, 128), seg=SUBLANES)
    else:
        seg = l // S5_SLOTS
        rep = lambda s: jnp.repeat(s.reshape(b, S5_LANES), seg, axis=0).reshape(1, b * seg, S5_LANES)
        o_s5, fre, fim = _s5_chunked(u.reshape(1, n, S5_WIDTH), rep(s5_re), rep(s5_im), *s5_args,
                                     rows=b * seg, seg=seg)
        o_s5 = o_s5.reshape(b, l, S5_WIDTH)
        fre = fre.reshape(b, seg, S5_LANES)[:, seg - 1]
        fim = fim.reshape(b, seg, S5_LANES)[:, seg - 1]
    s5_re_new = fre.reshape(b, S5_GROUPS, S5_STATE)
    s5_im_new = fim.reshape(b, S5_GROUPS, S5_STATE)

    ffn_args = (w['w_out_a'], w['w_out_b'], w['n2'], w['w_up'], w['fcw'], w['fcb'],
                w['w_down'], w['fg'])
    if prompt:
        prev = _pad_rows_top(conv_ffn, SUBLANES)
        y, tail = _ffn(x, o_dn, o_s5, prev, *ffn_args, tm=512, ls=512, carry=True)
        conv_ffn_new = tail[:, SUBLANES - (FFN_CONV - 1):]
    else:
        prev = jnp.pad(conv_ffn, ((0, 0), (0, l - (FFN_CONV - 1)), (0, 0))).reshape(1, n, D_FF)
        y, tail = _ffn(x.reshape(1, n, D_MODEL), o_dn.reshape(1, n, DN_WIDTH), o_s5.reshape(1, n, S5_WIDTH),
                       prev, *ffn_args, tm=min(n, 256), ls=l, carry=False)
        y = y.reshape(b, l, D_MODEL)
        conv_ffn_new = tail.reshape(b, l, D_FF)[:, l - (FFN_CONV - 1):]
    return y, (conv_dn_new[None], s_dn_new[None], s5_re_new[None], s5_im_new[None], conv_ffn_new[None])


def kernel(x_prompt, x_sample, cache_dn_conv, state_dn, state_s5_re, state_s5_im, cache_ffn_conv, norm1_g, w_in, dn_conv_w, dn_A_log, dn_dt_bias, dn_norm_g, s5_A_re, s5_A_im, s5_log_dt, s5_B_re, s5_B_im, s5_C_re, s5_C_im, s5_D, s5_glu_w, s5_glu_b, s5_norm_g, w_out, norm2_g, w_up, ffn_conv_w, ffn_conv_b, w_down, final_norm_g):
    assert w_in.shape[0] == 1, "single-layer trunk"
    o1 = QKV_WIDTH
    o2 = o1 + DN_WIDTH
    o4 = o2 + 2 * DN_HEADS
    wi = w_in[0]
    w_in_main = wi[:, :o2].astype(BF16)
    w_in_tail = jnp.concatenate(
        [wi[:, o4:], wi[:, o2:o4], jnp.zeros((D_MODEL, LANES - 2 * DN_HEADS), wi.dtype)], axis=1).astype(BF16)
    lane_pad = lambda v: jnp.pad(v, (BA_LANE_G, LANES - BA_LANE_G - DN_HEADS))
    seg_short = x_sample.shape[1] // S5_SLOTS
    assert x_sample.shape[1] % S5_SLOTS == 0 and seg_short in (1, 2, 4), "sample sequences of 8, 16 or 32 rows"
    a3 = jnp.concatenate([s5_A_re, s5_A_im, jnp.broadcast_to(s5_log_dt[:, :, None], s5_A_re.shape)], axis=0)
    a_lanes = a3.reshape(3, 1, S5_LANES)
    tab_long, tab_short = [_s5_prep(a_lanes, seg=s) for s in (SUBLANES, seg_short)]
    bc = jnp.concatenate([s5_B_re.transpose(0, 1, 3, 2), s5_B_im.transpose(0, 1, 3, 2), s5_C_re, s5_C_im], axis=0)
    pq, qq, mq = _s5_prep_quads(a3.reshape(3, S5_QUADS, 1, QUAD_LANES), _quad_blocks(bc))
    w = {
        'n1': norm1_g, 'w_in_main': w_in_main, 'w_in_tail': w_in_tail,
        'dn_cw': jnp.pad(dn_conv_w[0], ((0, SUBLANES - DN_CONV), (0, 0))),
        'dn_gp': jnp.pad(jnp.stack([lane_pad(dn_A_log[0]), lane_pad(dn_dt_bias[0])]), ((0, SUBLANES - 2), (0, 0))),
        'dn_g': dn_norm_g,
        'tab_long': tab_long, 'tab_short': tab_short, 'pq': pq, 'qq': qq, 'mq': mq,
        'dv': s5_D, 'glu_w': s5_glu_w[0].astype(BF16), 'glu_b': s5_glu_b, 's5_g': s5_norm_g,
        'w_out_a': w_out[0, :DN_WIDTH].astype(BF16), 'w_out_b': w_out[0, DN_WIDTH:].astype(BF16),
        'n2': norm2_g, 'w_up': w_up[0].astype(BF16),
        'fcw': jnp.pad(ffn_conv_w[0], ((0, SUBLANES - FFN_CONV), (0, 0))), 'fcb': ffn_conv_b,
        'w_down': w_down[0].astype(BF16), 'fg': final_norm_g.reshape(1, D_MODEL),
    }
    bp = x_prompt.shape[0]
    zeros = lambda *s: jnp.zeros(s, F32)
    y_p, st_p = _trunk(x_prompt, zeros(bp, DN_CONV - 1, QKV_WIDTH), zeros(bp, DN_HEADS, DN_HEAD_DIM, DN_HEAD_DIM),
                       zeros(bp, S5_GROUPS, S5_STATE), zeros(bp, S5_GROUPS, S5_STATE),
                       zeros(bp, FFN_CONV - 1, D_FF), w, prompt=True)
    y_s, st_s = _trunk(x_sample, cache_dn_conv[0], state_dn[0], state_s5_re[0], state_s5_im[0],
                       cache_ffn_conv[0], w, prompt=False)
    return (y_p, y_s) + st_p + st_s
```

```python
import functools

import jax
import jax.numpy as jnp
from jax import lax
from jax.experimental import pallas as pl
from jax.experimental.pallas import tpu as pltpu

F32 = jnp.float32
BF16 = jnp.bfloat16
EPS = 1e-6

D_MODEL = 1024
DN_HEADS = 4
DN_HEAD_DIM = 128
DN_WIDTH = DN_HEADS * DN_HEAD_DIM
DN_CONV = 4
QKV_WIDTH = 3 * DN_WIDTH
S5_WIDTH = D_MODEL - DN_WIDTH
S5_GROUP = 16
S5_GROUPS = S5_WIDTH // S5_GROUP
S5_STATE = 64
S5_LANES = S5_GROUPS * S5_STATE
S5_SLOTS = 8
S5_QUADS = S5_GROUPS // 4
QUAD_LANES = 4 * S5_STATE
QUAD_IN = S5_SLOTS * 4 * S5_GROUP
D_FF = 2816
FFN_CONV = 3
CHUNK = 64

SUBLANES = 8
LANES = 128
BA_LANE_G = DN_HEADS
IN_MAIN = QKV_WIDTH + DN_WIDTH
IN_PAD = IN_MAIN + S5_WIDTH + LANES
MXU_DIM = 256
FF_CHUNK = 6 * MXU_DIM

DN_ROWS_LONG = 2 * CHUNK
DN_SEQS_SHORT = 16
S5_CHUNKS_LONG = 256
FFN_ROWS_LONG = 512
FFN_ROWS_SHORT = 256
INPROJ_ROWS = 512

VMEM_LIMIT = 56 * 1024 * 1024


def _dot(a, b):
    return jnp.dot(a, b, preferred_element_type=F32)


def _dot_tn(a, b):
    return lax.dot_general(a, b, (((0,), (0,)), ((), ())), preferred_element_type=F32)


def _split_bf16(a):
    hi = a.astype(BF16)
    lo = (a - hi.astype(F32)).astype(BF16)
    return hi, lo


def _bmm(a, b):
    return lax.dot_general(a, b, (((2,), (1,)), ((0,), (0,))), preferred_element_type=F32)


def _bmm_nt(a, b):
    return lax.dot_general(a, b, (((2,), (2,)), ((0,), (0,))), preferred_element_type=F32)


def _bmm_nt3(a, b):
    ah, al = _split_bf16(a)
    bh, bl = _split_bf16(b)
    return _bmm_nt(ah, bh) + (_bmm_nt(al, bh) + _bmm_nt(ah, bl))


def _rms(x, g):
    return x * lax.rsqrt(jnp.mean(x * x, axis=-1, keepdims=True) + EPS) * g


def _const_spec(shape):
    nd = len(shape)
    return pl.BlockSpec(shape, lambda *_: (0,) * nd, pipeline_mode=pl.Buffered(1))


def _params(n_axes):
    return pltpu.CompilerParams(dimension_semantics=("arbitrary",) * n_axes,
                                vmem_limit_bytes=VMEM_LIMIT)


def _cmul(a, b):
    return a[0] * b[0] - a[1] * b[1], a[0] * b[1] + a[1] * b[0]


def _zoh(are, aim, ldt):
    dt = jnp.exp(ldt)
    mag = jnp.exp(are * dt)
    ang = aim * dt
    lr = mag * jnp.cos(ang)
    li = mag * jnp.sin(ang)
    den = are * are + aim * aim
    f_re = ((lr - 1.0) * are + li * aim) / den
    f_im = (li * are - (lr - 1.0) * aim) / den
    return (lr, li), (f_re, f_im)


def _write_scan_tables(tab_ref, step, seg):
    pw = [step]
    for _ in range(seg - 1):
        pw.append(_cmul(pw[-1], step))
    shape = (SUBLANES, step[0].shape[-1])
    row = lax.broadcasted_iota(jnp.int32, shape, 0) & (seg - 1)
    zero = jnp.zeros(shape, F32)
    pre, pim = zero, zero
    for r in range(seg):
        pre = jnp.where(row == r, pw[r][0], pre)
        pim = jnp.where(row == r, pw[r][1], pim)
    tab_ref[...] = jnp.zeros(tab_ref.shape, F32)
    tab_ref[0:8, :] = pre
    tab_ref[8:16, :] = pim
    for lvl, d in enumerate(_scan_levels(seg)):
        tab_ref[16 + 16 * lvl:24 + 16 * lvl, :] = jnp.where(row >= d, pw[d - 1][0], zero)
        tab_ref[24 + 16 * lvl:32 + 16 * lvl, :] = jnp.where(row >= d, pw[d - 1][1], zero)


def _scan_levels(seg):
    return [d for d in (1, 2, 4) if d < seg]


def _s5prep_kernel(a_ref, tab_ref, *, seg):
    lam, _ = _zoh(a_ref[0], a_ref[1], a_ref[2])
    lam_c = lam
    for _ in range(S5_SLOTS - 1):
        lam_c = _cmul(lam_c, lam)
    _write_scan_tables(tab_ref, lam_c, seg)


def _s5_prep(a_lanes, seg):
    return pl.pallas_call(
        functools.partial(_s5prep_kernel, seg=seg),
        out_shape=jax.ShapeDtypeStruct((64, S5_LANES), F32),
        compiler_params=pltpu.CompilerParams(vmem_limit_bytes=VMEM_LIMIT),
        name="s5_prep",
    )(a_lanes)


def _s5prepq_kernel(a_ref, bc_ref, pq_ref, qq_ref, mq_ref):
    lam, f = _zoh(a_ref[0], a_ref[1], a_ref[2])
    bb = _cmul(f, (bc_ref[0], bc_ref[1]))
    ct = (bc_ref[2], bc_ref[3])
    one = (jnp.ones_like(lam[0]), jnp.zeros_like(lam[0]))
    pw = [one]
    for _ in range(S5_SLOTS):
        pw.append(_cmul(pw[-1], lam))
    blk = 4 * S5_GROUP
    for s in range(S5_SLOTS):
        p_re, p_im = _cmul(pw[S5_SLOTS - 1 - s], bb)
        pq_ref[:, s * blk:(s + 1) * blk, :QUAD_LANES] = p_re.astype(BF16)
        pq_ref[:, s * blk:(s + 1) * blk, QUAD_LANES:] = p_im.astype(BF16)
    cl = [_cmul(pw[e], ct) for e in range(S5_SLOTS + 1)]
    qt_re = jnp.concatenate([cl[t + 1][0] for t in range(S5_SLOTS)], axis=1)
    qt_im = jnp.concatenate([-cl[t + 1][1] for t in range(S5_SLOTS)], axis=1)
    for n in range(S5_QUADS):
        qq_ref[n, :QUAD_LANES, :] = qt_re[n].T.astype(BF16)
        qq_ref[n, QUAD_LANES:, :] = qt_im[n].T.astype(BF16)
    lag_re = jnp.concatenate([cl[tau][0] for tau in range(S5_SLOTS)], axis=1)
    lag_im = jnp.concatenate([cl[tau][1] for tau in range(S5_SLOTS)], axis=1)
    lags = _bmm_nt3(bb[0], lag_re) - _bmm_nt3(bb[1], lag_im)
    for s in range(S5_SLOTS):
        m = lags if s == 0 else jnp.concatenate(
            [jnp.zeros((S5_QUADS, blk, s * blk), F32), lags[:, :, :QUAD_IN - s * blk]], axis=2)
        mq_ref[:, s * blk:(s + 1) * blk, :] = m.astype(BF16)


def _s5_prep_quads(a_quads, bc_quads):
    w = lambda r, c_: jax.ShapeDtypeStruct((S5_QUADS, r, c_), BF16)
    return pl.pallas_call(
        _s5prepq_kernel,
        out_shape=(w(QUAD_IN, 2 * QUAD_LANES), w(2 * QUAD_LANES, QUAD_IN), w(QUAD_IN, QUAD_IN)),
        compiler_params=pltpu.CompilerParams(vmem_limit_bytes=VMEM_LIMIT),
        name="s5_prep_quads",
    )(a_quads, bc_quads)


def _w_cols(wm_ref, wt_ref, lo, hi):
    return wm_ref[:, lo:hi] if hi <= IN_MAIN else wt_ref[:, lo - IN_MAIN:hi - IN_MAIN]


def _inproj_kernel(x_ref, g_ref, wm_ref, wt_ref, qkv_ref, z_ref, u_ref, ba_ref):
    h = _rms(x_ref[...], g_ref[...]).astype(BF16)
    o1 = QKV_WIDTH
    o2 = o1 + DN_WIDTH
    o3 = o2 + S5_WIDTH
    qkv_ref[...] = _dot(h, _w_cols(wm_ref, wt_ref, 0, o1))
    z_ref[...] = _dot(h, _w_cols(wm_ref, wt_ref, o1, o2))
    u_ref[...] = _dot(h, _w_cols(wm_ref, wt_ref, o2, o3))
    ba_ref[...] = _dot(h, _w_cols(wm_ref, wt_ref, o3, IN_PAD))


def _in_proj(x2d, g, wm, wt, tm):
    n = x2d.shape[0]
    row = lambda w_: pl.BlockSpec((tm, w_), lambda i: (i, 0))
    return pl.pallas_call(
        _inproj_kernel,
        grid=(n // tm,),
        in_specs=[row(D_MODEL), _const_spec((1, D_MODEL)), _const_spec((D_MODEL, IN_MAIN)),
                  _const_spec((D_MODEL, IN_PAD - IN_MAIN))],
        out_specs=(row(QKV_WIDTH), row(DN_WIDTH), row(S5_WIDTH), row(LANES)),
        out_shape=(jax.ShapeDtypeStruct((n, QKV_WIDTH), F32),
                   jax.ShapeDtypeStruct((n, DN_WIDTH), F32),
                   jax.ShapeDtypeStruct((n, S5_WIDTH), F32),
                   jax.ShapeDtypeStruct((n, LANES), F32)),
        compiler_params=_params(1),
        name="in_proj",
    )(x2d, g, wm, wt)


def _stack_heads(a):
    dh = DN_HEAD_DIM
    return jnp.concatenate([a[:, :, h * dh:(h + 1) * dh] for h in range(DN_HEADS)], axis=1)


def _unstack_heads(a, c):
    return jnp.concatenate([a[:, h * c:(h + 1) * c, :] for h in range(DN_HEADS)], axis=2)


def _dn_kernel(qkv_ref, z_ref, ba_ref, cache_ref, s0_ref, cw_ref, gp_ref, ng_ref, o_ref, state_ref, halo_s,
               *, nb, tt, c):
    _dn_restart(pl.program_id(1) == 0, cache_ref, s0_ref, halo_s, state_ref)
    stash = _dn_phase_a(qkv_ref, z_ref, ba_ref, cw_ref, gp_ref, halo_s, nb=nb, tt=tt, c=c)
    for step in _dn_phase_b(stash, ng_ref, o_ref, state_ref, nb=nb, tt=tt, c=c):
        step()


def _dn_restart(first, cache_ref, s0_ref, halo_s, state_ref):
    @pl.when(first)
    def _():
        halo_s[...] = cache_ref[...]
        state_ref[...] = s0_ref[...]


def _dn_phase_a(qkv_ref, z_ref, ba_ref, cw_ref, gp_ref, halo_s, *, nb, tt, c, side=()):
    nh = DN_HEADS
    dh = DN_HEAD_DIM
    nck = tt // c
    npb = nb * nck
    hc = nh * c
    side = list(side)

    def side_work(n=1):
        for _ in range(n):
            if side:
                side.pop(0)()

    stack = _stack_heads
    unstack = functools.partial(_unstack_heads, c=c)

    first_row = lax.broadcasted_iota(jnp.int32, (SUBLANES, MXU_DIM), 0) == 0

    def shift_rows(y, carried):
        rolled = pltpu.roll(y, 1, axis=1)
        top = jnp.where(first_row[None], carried, rolled[:, :SUBLANES])
        return jnp.concatenate([top, rolled[:, SUBLANES:]], axis=1)

    conv_parts = []
    for lo in range(0, QKV_WIDTH, MXU_DIM):
        side_work()
        hi = lo + MXU_DIM
        cw = cw_ref[:, lo:hi]
        x = qkv_ref[:, :, lo:hi]
        prev = halo_s[:, :, lo:hi]
        acc = x * cw[0:1]
        carried = prev[:, SUBLANES - 1:] * cw[0:1]
        for j in range(1, DN_CONV):
            acc = shift_rows(acc, carried) + x * cw[j:j + 1]
            if j < DN_CONV - 1:
                carried = prev[:, SUBLANES - 1 - j:SUBLANES - j] * cw[0:1]
                for i in range(1, j + 1):
                    carried = carried + prev[:, SUBLANES - 1 - j + i:SUBLANES - j + i] * cw[i:i + 1]
        halo_s[:, :, lo:hi] = x[:, tt - SUBLANES:]
        conv_parts.append((acc * jax.nn.sigmoid(acc)).reshape(npb, c, hi - lo))
    per = DN_WIDTH // MXU_DIM
    q, k, v = [stack(jnp.concatenate(conv_parts[i * per:(i + 1) * per], axis=2)) for i in range(3)]
    side_work()
    q = q * (lax.rsqrt(jnp.sum(q * q, axis=-1, keepdims=True) + EPS) * (dh ** -0.5))
    k = k * lax.rsqrt(jnp.sum(k * k, axis=-1, keepdims=True) + EPS)

    side_work()
    ba = ba_ref[...].reshape(nb * tt, LANES)
    beta = jax.nn.sigmoid(ba).reshape(npb, c, LANES)
    g2 = -jnp.exp(gp_ref[0:1, :]) * jax.nn.softplus(ba + gp_ref[1:2, :])
    row_in_chunk = lax.broadcasted_iota(jnp.int32, (nb * tt, LANES), 0) & (c - 1)
    d = 1
    while d < c:
        g2 = g2 + jnp.where(row_in_chunk >= d, pltpu.roll(g2, d, axis=0), 0.0)
        d *= 2
    g = g2.reshape(npb, c, LANES)

    g_cols = [g[:, :, BA_LANE_G + h:BA_LANE_G + h + 1] for h in range(nh)]
    b_cols = [beta[:, :, h:h + 1] for h in range(nh)]
    g_col = jnp.concatenate(g_cols, axis=1)
    b_col = jnp.concatenate(b_cols, axis=1)
    g_last = jnp.concatenate([jnp.broadcast_to(gc[:, c - 1:c, :], (npb, c, 1)) for gc in g_cols], axis=1)

    hp = min(nh, LANES // c)
    pieces = []
    for h0 in range(0, nh, hp):
        slab = jnp.concatenate(
            [(g2 if hh == 0 else pltpu.roll(g2, LANES - hh, axis=1)).reshape(npb, c, LANES)
             for hh in range(h0, h0 + hp)], axis=1)
        if hp * c < LANES:
            slab = jnp.concatenate([slab, jnp.zeros((npb, LANES - hp * c, LANES), F32)], axis=1)
        rows = [slab[p].T[BA_LANE_G:BA_LANE_G + 1, :hp * c] for p in range(npb)]
        pieces.append(jnp.stack(rows, axis=0))
    g_row = jnp.concatenate(pieces, axis=2)

    ri = lax.broadcasted_iota(jnp.int32, (c, hc), 0)
    lane = lax.broadcasted_iota(jnp.int32, (c, hc), 1)
    cj = lane & (c - 1)
    causal = (ri >= cj)[None]
    strict = (ri > cj)[None]

    def cat_from_cols(cols):
        out = jnp.broadcast_to(cols[nh - 1], (npb, c, hc))
        for h in range(nh - 2, -1, -1):
            out = jnp.where((lane < (h + 1) * c)[None], jnp.broadcast_to(cols[h], (npb, c, hc)), out)
        return out

    decay = jnp.exp(jnp.where(causal, cat_from_cols(g_cols) - g_row, -jnp.inf))

    side_work()
    kb = k.astype(BF16)
    br = lax.broadcasted_iota(jnp.int32, (hc, nh * dh), 0)
    bl = lax.broadcasted_iota(jnp.int32, (hc, nh * dh), 1)
    head_of_row = sum(jnp.where(br >= h * c, 1, 0) for h in range(1, nh))
    head_of_lane = sum(jnp.where(bl >= h * dh, 1, 0) for h in range(1, nh))
    k_bd = jnp.where((head_of_row == head_of_lane)[None], jnp.concatenate([kb] * nh, axis=2), 0.0)
    qk_lhs = jnp.concatenate([unstack(q), unstack(k)], axis=1).astype(BF16)
    qkk = _bmm_nt(qk_lhs, k_bd)
    qk = qkk[:, :c]
    kk = qkk[:, c:]

    sr = lax.broadcasted_iota(jnp.int32, (hc, hc), 0)
    sl = lax.broadcasted_iota(jnp.int32, (hc, hc), 1)
    shift = c.bit_length() - 1
    same_head = ((sr >> shift) == (sl >> shift))[None]

    def bd(m):
        return jnp.where(same_head, jnp.concatenate([m] * nh, axis=1), 0.0)

    lm = jnp.where(strict, cat_from_cols(b_cols) * kk * decay, 0.0)
    tinv = jnp.where((ri == cj)[None], 1.0, 0.0) - jnp.where(((ri >> 1) == (cj >> 1))[None], lm, 0.0)
    s = 2
    while s < c:
        sh = s.bit_length()
        lower_left = ((ri >> sh) == (cj >> sh)) & ((ri & s) != 0) & ((cj & s) == 0)
        side_work()
        a_off = jnp.where(lower_left[None], lm, 0.0).astype(BF16)
        xm = _bmm(tinv.astype(BF16), bd(a_off))
        tinv = tinv - _bmm(xm.astype(BF16), bd(tinv.astype(BF16)))
        s *= 2

    e_g = jnp.exp(g_col)
    rhs = jnp.concatenate([v * b_col, k * (b_col * e_g)], axis=2).astype(BF16)
    sol = _bmm(bd(tinv.astype(BF16)), rhs)
    sol_v = sol[:, :, :dh]
    sol_k = sol[:, :, dh:]
    qe = q * e_g
    w_qe = jnp.concatenate([part[:, h * c:(h + 1) * c] for h in range(nh) for part in (sol_k, qe)],
                           axis=1).astype(BF16)
    kd = (k * jnp.exp(g_last - g_col)).astype(BF16)
    qkd_bd = bd((qk * decay).astype(BF16))
    s_decay = jnp.exp(g_last)
    side_work(len(side))
    return sol_v, w_qe, kd, qkd_bd, s_decay, lambda: stack(z_ref[...].reshape(npb, c, DN_WIDTH))


def _dn_phase_b(stash, ng_ref, o_ref, state_ref, *, nb, tt, c):
    sol_v, w_qe, kd, qkd_bd, s_decay, zs = stash
    nh = DN_HEADS
    dh = DN_HEAD_DIM
    nck = tt // c
    outs = {}
    pending = {}

    def outputs(ck):
        def run():
            prods = {}
            for b in range(nb):
                p = b * nck + ck
                states = [state_ref[b, h] for h in range(nh)]
                prods[b] = [_dot(w_qe[p, h * 2 * c:(h + 1) * 2 * c, :], states[h].astype(BF16)) for h in range(nh)]
                pending[p] = states
            for b in range(nb):
                p = b * nck + ck
                u = sol_v[p] - jnp.concatenate([r[:c] for r in prods[b]], axis=0)
                ub = u.astype(BF16)
                outs[p] = jnp.concatenate([r[c:] for r in prods[b]], axis=0) + _dot(qkd_bd[p], ub)
                pending[p] = (pending[p], ub)
        return run

    def update(ck):
        def run():
            for b in range(nb):
                p = b * nck + ck
                states, ub = pending.pop(p)
                for h in range(nh):
                    r0 = h * c
                    state_ref[b, h] = (states[h] * s_decay[p, r0:r0 + 1, :]
                                       + _dot_tn(kd[p, r0:r0 + c, :], ub[r0:r0 + c, :]))
        return run

    def finish():
        o = jnp.stack([outs[p] for p in range(nb * nck)], axis=0)
        z = zs()
        o = _rms(o, ng_ref[...]) * (z * jax.nn.sigmoid(z))
        o_ref[...] = _unstack_heads(o, c).reshape(nb, tt, DN_WIDTH).astype(BF16)

    return [f(ck) for ck in range(nck) for f in (outputs, update)] + [finish]


def _deltanet(qkv, z, ba, cache8, s0, cw, gp, ng, nb, tt, c):
    b, l, _ = qkv.shape
    seq = lambda w_: pl.BlockSpec((nb, tt, w_), lambda i, j: (i, j, 0))
    per_b = lambda *s: pl.BlockSpec((nb,) + s, lambda i, j: (i,) + (0,) * len(s))
    return pl.pallas_call(
        functools.partial(_dn_kernel, nb=nb, tt=tt, c=c),
        grid=(b // nb, l // tt),
        in_specs=[seq(QKV_WIDTH), seq(DN_WIDTH), seq(LANES), per_b(SUBLANES, QKV_WIDTH),
                  per_b(DN_HEADS, DN_HEAD_DIM, DN_HEAD_DIM),
                  _const_spec((SUBLANES, QKV_WIDTH)), _const_spec((SUBLANES, LANES)),
                  _const_spec((1, DN_HEAD_DIM))],
        out_specs=(seq(DN_WIDTH), per_b(DN_HEADS, DN_HEAD_DIM, DN_HEAD_DIM)),
        out_shape=(jax.ShapeDtypeStruct((b, l, DN_WIDTH), BF16),
                   jax.ShapeDtypeStruct((b, DN_HEADS, DN_HEAD_DIM, DN_HEAD_DIM), F32)),
        scratch_shapes=[pltpu.VMEM((nb, SUBLANES, QKV_WIDTH), F32)],
        compiler_params=_params(2),
        name="deltanet",
    )(qkv, z, ba, cache8, s0, cw, gp, ng)


def _mixer_kernel(x_ref, g_ref, wm_ref, wt_ref, cache_ref, s0_ref, cw_ref, gp_ref, ng_ref,
                  u_ref, o_ref, state_ref, tail_ref, p_cur, p_nxt, halo_s, *, nb, tt, c):
    i = pl.program_id(0)
    o1 = QKV_WIDTH
    o2 = o1 + DN_WIDTH
    o3 = o2 + S5_WIDTH

    @pl.when(i == 0)
    def _():
        p_cur[...] = jnp.zeros(p_cur.shape, F32)

    _dn_restart(i <= 1, cache_ref, s0_ref, halo_s, state_ref)

    h = _rms(x_ref[...].reshape(nb * tt, D_MODEL), g_ref[...]).astype(BF16)

    def project(lo, hi):
        def run():
            res = _dot(h, _w_cols(wm_ref, wt_ref, lo, hi)).reshape(nb, tt, hi - lo)
            if o2 <= lo < o3:
                u_ref[:, :, lo - o2:hi - o2] = res
            else:
                p_nxt[:, :, lo:hi] = res
        return run

    side = [project(lo, min(lo + MXU_DIM, IN_PAD)) for lo in range(0, IN_PAD, MXU_DIM)]
    stash = _dn_phase_a(p_cur.at[:, :, 0:o1], p_cur.at[:, :, o1:o2], p_cur.at[:, :, o3:IN_PAD],
                        cw_ref, gp_ref, halo_s, nb=nb, tt=tt, c=c, side=side)
    for step in _dn_phase_b(stash, ng_ref, o_ref, state_ref, nb=nb, tt=tt, c=c):
        step()
    tail_ref[...] = halo_s[...]
    for lo, hi in ((0, o1), (o1, o2), (o3, IN_PAD)):
        p_cur[:, :, lo:hi] = p_nxt[:, :, lo:hi]


def _mixer(x, g, wm, wt, cache8, s0, cw, gp, ng, tt, c):
    nb, l, _ = x.shape
    nt = l // tt
    cur = lambda w_: pl.BlockSpec((nb, tt, w_), lambda i: (0, jnp.minimum(i, nt - 1), 0))
    prev = lambda w_: pl.BlockSpec((nb, tt, w_), lambda i: (0, jnp.maximum(i - 1, 0), 0))
    return pl.pallas_call(
        functools.partial(_mixer_kernel, nb=nb, tt=tt, c=c),
        grid=(nt + 1,),
        in_specs=[cur(D_MODEL), _const_spec((1, D_MODEL)), _const_spec((D_MODEL, IN_MAIN)),
                  _const_spec((D_MODEL, IN_PAD - IN_MAIN)), _const_spec((nb, SUBLANES, QKV_WIDTH)), _const_spec((nb, DN_HEADS, DN_HEAD_DIM, DN_HEAD_DIM)),
                  _const_spec((SUBLANES, QKV_WIDTH)), _const_spec((SUBLANES, LANES)),
                  _const_spec((1, DN_HEAD_DIM))],
        out_specs=(cur(S5_WIDTH), prev(DN_WIDTH),
                   pl.BlockSpec((nb, DN_HEADS, DN_HEAD_DIM, DN_HEAD_DIM), lambda i: (0, 0, 0, 0)),
                   pl.BlockSpec((nb, SUBLANES, QKV_WIDTH), lambda i: (0, 0, 0))),
        out_shape=(jax.ShapeDtypeStruct((nb, l, S5_WIDTH), F32),
                   jax.ShapeDtypeStruct((nb, l, DN_WIDTH), BF16),
                   jax.ShapeDtypeStruct((nb, DN_HEADS, DN_HEAD_DIM, DN_HEAD_DIM), F32),
                   jax.ShapeDtypeStruct((nb, SUBLANES, QKV_WIDTH), F32)),
        scratch_shapes=[pltpu.VMEM((nb, tt, IN_PAD), F32), pltpu.VMEM((nb, tt, IN_PAD), F32),
                        pltpu.VMEM((nb, SUBLANES, QKV_WIDTH), F32)],
        compiler_params=_params(1),
        name="mixer",
    )(x, g, wm, wt, cache8, s0, cw, gp, ng)


def _s5c_kernel(u_ref, x0re_ref, x0im_ref, pq_ref, qq_ref, mq_ref, tab_ref, dv_ref, gw_ref, gb_ref, ng_ref,
                o_ref, fre_ref, fim_ref, xr_s, xi_s, car_s, u_s, o_s, *, rows, seg):
    carry = seg == SUBLANES
    if carry:
        @pl.when(pl.program_id(1) == 0)
        def _():
            car_s[0:1, :] = x0re_ref[0]
            car_s[1:2, :] = x0im_ref[0]

    ncol = S5_WIDTH // LANES
    for k in range(ncol):
        u_s[k] = u_ref[0, :, k * LANES:(k + 1) * LANES]
    us = [jnp.concatenate([u_s[k, pl.ds(s, rows, stride=S5_SLOTS), :] for k in range(ncol)], axis=1)
          for s in range(S5_SLOTS)]
    ubs = [a.astype(BF16) for a in us]
    blk = 4 * S5_GROUP
    ql = QUAD_LANES
    uq =[jnp.concatenate([ubs[s][:, n * blk:(n + 1) * blk] for s in range(S5_SLOTS)], axis=1)
          for n in range(S5_QUADS)]
    for n in range(S5_QUADS):
        inc = _dot(uq[n], pq_ref[n])
        xr_s[:, n * ql:(n + 1) * ql] = inc[:, :ql]
        xi_s[:, n * ql:(n + 1) * ql] = inc[:, ql:]

    seg_row = lax.broadcasted_iota(jnp.int32, (SUBLANES, S5_LANES), 0) & (seg - 1)

    def block_body(rb, cin):
        r0 = pl.multiple_of(rb * SUBLANES, SUBLANES)
        if carry:
            c_re, c_im = cin
        else:
            c_re = x0re_ref[0, pl.ds(r0, SUBLANES), :]
            c_im = x0im_ref[0, pl.ds(r0, SUBLANES), :]
        xr = xr_s[pl.ds(r0, SUBLANES), :]
        xi = xi_s[pl.ds(r0, SUBLANES), :]
        for lvl, d in enumerate(_scan_levels(seg)):
            m_re = tab_ref[16 + 16 * lvl:24 + 16 * lvl, :]
            m_im = tab_ref[24 + 16 * lvl:32 + 16 * lvl, :]
            sr = pltpu.roll(xr, d, axis=0)
            si = pltpu.roll(xi, d, axis=0)
            xr, xi = xr + (m_re * sr - m_im * si), xi + (m_re * si + m_im * sr)
        p_re = tab_ref[0:8, :]
        p_im = tab_ref[8:16, :]
        xr, xi = xr + (p_re * c_re - p_im * c_im), xi + (p_re * c_im + p_im * c_re)
        xr_s[pl.ds(r0, SUBLANES), :] = jnp.where(seg_row == 0, c_re, pltpu.roll(xr, 1, axis=0))
        xi_s[pl.ds(r0, SUBLANES), :] = jnp.where(seg_row == 0, c_im, pltpu.roll(xi, 1, axis=0))
        if carry:
            return xr[SUBLANES - 1:, :], xi[SUBLANES - 1:, :]
        fre_ref[0, pl.ds(r0, SUBLANES), :] = xr
        fim_ref[0, pl.ds(r0, SUBLANES), :] = xi
        return cin

    if carry:
        c_re, c_im = lax.fori_loop(0, rows // SUBLANES, block_body, (car_s[0:1, :], car_s[1:2, :]))
        car_s[0:1, :] = c_re
        car_s[1:2, :] = c_im
        fre_ref[0] = c_re
        fim_ref[0] = c_im
    else:
        lax.fori_loop(0, rows // SUBLANES, block_body, 0)

    ys = []
    for n in range(S5_QUADS):
        xs = jnp.concatenate([xr_s[:, n * ql:(n + 1) * ql], xi_s[:, n * ql:(n + 1) * ql]], axis=1).astype(BF16)
        half = QUAD_IN // 2
        intra = jnp.concatenate([_dot(uq[n][:, :half], mq_ref[n, :half, :half]), _dot(uq[n], mq_ref[n, :, half:])],
                                axis=1)
        ys.append(intra + _dot(xs, qq_ref[n]))
    for s in range(S5_SLOTS):
        y = jnp.concatenate([ys[n][:, s * blk:(s + 1) * blk] for n in range(S5_QUADS)], axis=1)
        y = jax.nn.gelu(y + dv_ref[...] * us[s]).astype(BF16)
        gl = _dot(y, gw_ref[...]) + gb_ref[...]
        o = gl[:, :S5_WIDTH] * jax.nn.sigmoid(gl[:, S5_WIDTH:])
        o = _rms(o, ng_ref[...])
        for k in range(ncol):
            o_s[k, pl.ds(s, rows, stride=S5_SLOTS), :] = o[:, k * LANES:(k + 1) * LANES]
    o_ref[0] = jnp.concatenate([o_s[k] for k in range(ncol)], axis=1).astype(BF16)


def _s5_chunked(u, x0re, x0im, pq, qq, mq, tabc, dv, gw, gb, ng, rows, seg):
    b, l, _ = u.shape
    tt = rows * S5_SLOTS
    seq = pl.BlockSpec((1, tt, S5_WIDTH), lambda i, j: (i, j, 0))
    if seg == SUBLANES:
        st = pl.BlockSpec((1, 1, S5_LANES), lambda i, j: (i, 0, 0))
    else:
        st = pl.BlockSpec((1, rows, S5_LANES), lambda i, j: (i, j, 0))
    return pl.pallas_call(
        functools.partial(_s5c_kernel, rows=rows, seg=seg),
        grid=(b, l // tt),
        in_specs=[seq, st, st,
                  _const_spec((S5_QUADS, QUAD_IN, 2 * QUAD_LANES)),
                  _const_spec((S5_QUADS, 2 * QUAD_LANES, QUAD_IN)),
                  _const_spec((S5_QUADS, QUAD_IN, QUAD_IN)), _const_spec((64, S5_LANES)),
                  _const_spec((1, S5_WIDTH)), _const_spec((S5_WIDTH, 2 * S5_WIDTH)),
                  _const_spec((1, 2 * S5_WIDTH)), _const_spec((1, S5_WIDTH))],
        out_specs=(seq, st, st),
        out_shape=(jax.ShapeDtypeStruct((b, l, S5_WIDTH), BF16),
                   jax.ShapeDtypeStruct(x0re.shape, F32),
                   jax.ShapeDtypeStruct(x0re.shape, F32)),
        scratch_shapes=[pltpu.VMEM((rows, S5_LANES), F32), pltpu.VMEM((rows, S5_LANES), F32),
                        pltpu.VMEM((SUBLANES, S5_LANES), F32),
                        pltpu.VMEM((S5_WIDTH // LANES, tt, LANES), F32),
                        pltpu.VMEM((S5_WIDTH // LANES, tt, LANES), F32)],
        compiler_params=_params(2),
        name="s5_chunked",
    )(u, x0re, x0im, pq, qq, mq, tabc, dv, gw, gb, ng)


def _ffn_kernel(x_ref, odn_ref, os5_ref, prev_ref, woa_ref, wob_ref, n2_ref, wu_ref,
                fcw_ref, fcb_ref, wd_ref, fg_ref, y_ref, tail_ref, halo_s, *, tm, ls, carry):
    if carry:
        @pl.when(pl.program_id(1) == 0)
        def _():
            halo_s[...] = prev_ref[0]

    x1 = x_ref[0] + _dot(odn_ref[0], woa_ref[...]) + _dot(os5_ref[0], wob_ref[...])
    h2 = _rms(x1, n2_ref[...]).astype(BF16)
    down = None
    for lo in range(0, D_FF, FF_CHUNK):
        hi = min(lo + FF_CHUNK, D_FF)
        gate = _dot(h2, wu_ref[:, lo:hi])
        val = _dot(h2, wu_ref[:, D_FF + lo:D_FF + hi])
        if carry:
            xx = jnp.concatenate([halo_s[:, lo:hi], gate], axis=0)
            g2 = xx[SUBLANES - 2:SUBLANES - 2 + tm]
            g1 = xx[SUBLANES - 1:SUBLANES - 1 + tm]
            halo_s[:, lo:hi] = gate[tm - SUBLANES:, :]
            tail_ref[0, :, lo:hi] = gate[tm - SUBLANES:, :]
        else:
            r = lax.broadcasted_iota(jnp.int32, (tm, hi - lo), 0) & (ls - 1)
            prev = prev_ref[0, :, lo:hi]
            g1 = jnp.where(r < 1, pltpu.roll(prev, tm - 1, axis=0), pltpu.roll(gate, 1, axis=0))
            g2 = jnp.where(r < 2, prev, pltpu.roll(gate, 2, axis=0))
            tail_ref[0, :, lo:hi] = gate
        cw = fcw_ref[:, lo:hi]
        pre = g2 * cw[0:1] + g1 * cw[1:2] + gate * cw[2:3] + fcb_ref[:, lo:hi]
        act = pre * jax.nn.sigmoid(pre) * val
        part = _dot(act.astype(BF16), wd_ref[lo:hi, :])
        down = part if down is None else down + part
    y_ref[0] = _rms(x1 + down, fg_ref[...])


def _ffn(x, odn, os5, prev, woa, wob, n2, wu, fcw, fcb, wd, fg, tm, ls, carry):
    b, l, _ = x.shape
    seq = lambda w_: pl.BlockSpec((1, tm, w_), lambda i, j: (i, j, 0))
    if carry:
        prev_spec = pl.BlockSpec((1, SUBLANES, D_FF), lambda i, j: (i, 0, 0))
        tail_spec = pl.BlockSpec((1, SUBLANES, D_FF), lambda i, j: (i, 0, 0))
        tail_shape = jax.ShapeDtypeStruct((b, SUBLANES, D_FF), F32)
    else:
        prev_spec = seq(D_FF)
        tail_spec = seq(D_FF)
        tail_shape = jax.ShapeDtypeStruct((b, l, D_FF), F32)
    return pl.pallas_call(
        functools.partial(_ffn_kernel, tm=tm, ls=ls, carry=carry),
        grid=(b, l // tm),
        in_specs=[seq(D_MODEL), seq(DN_WIDTH), seq(S5_WIDTH), prev_spec,
                  _const_spec((DN_WIDTH, D_MODEL)), _const_spec((S5_WIDTH, D_MODEL)),
                  _const_spec((1, D_MODEL)), _const_spec((D_MODEL, 2 * D_FF)),
                  _const_spec((SUBLANES, D_FF)), _const_spec((1, D_FF)), _const_spec((D_FF, D_MODEL)),
                  _const_spec((1, D_MODEL))],
        out_specs=(seq(D_MODEL), tail_spec),
        out_shape=(jax.ShapeDtypeStruct((b, l, D_MODEL), F32), tail_shape),
        scratch_shapes=[pltpu.VMEM((SUBLANES, D_FF), F32)],
        compiler_params=_params(2),
        name="ffn",
    )(x, odn, os5, prev, woa, wob, n2, wu, fcw, fcb, wd, fg)


def _pad_rows_top(a, rows):
    return jnp.pad(a, ((0, 0), (rows - a.shape[1], 0), (0, 0)))


def _quad_blocks(m):
    n = m.shape[0]
    same = jnp.eye(4, dtype=bool)[:, None, :, None]
    blocks = jnp.where(same, m.reshape(n, S5_QUADS, 4, S5_GROUP, 1, S5_STATE), 0.0)
    return blocks.reshape(n, S5_QUADS, 4 * S5_GROUP, QUAD_LANES)


def _trunk(x, conv_dn, s_dn, s5_re, s5_im, conv_ffn, w, prompt):
    b, l, _ = x.shape
    n = b * l
    c = CHUNK if l % CHUNK == 0 else l
    assert l >= 2 * SUBLANES, "sequence shorter than two row tiles"
    cache8 = _pad_rows_top(conv_dn, SUBLANES)
    if prompt:
        u, o_dn, s_dn_new, tail = _mixer(x, w['n1'], w['w_in_main'], w['w_in_tail'], cache8, s_dn,
                                         w['dn_cw'], w['dn_gp'], w['dn_g'],
                                         tt=min(l, DN_ROWS_LONG), c=c)
        conv_dn_new = tail[:, SUBLANES - (DN_CONV - 1):]
    else:
        qkv, z, u, ba = _in_proj(x.reshape(n, D_MODEL), w['n1'], w['w_in_main'], w['w_in_tail'], min(n, INPROJ_ROWS))
        qkv = qkv.reshape(b, l, QKV_WIDTH)
        o_dn, s_dn_new = _deltanet(qkv, z.reshape(b, l, DN_WIDTH), ba.reshape(b, l, LANES), cache8, s_dn,
                                   w['dn_cw'], w['dn_gp'], w['dn_g'], nb=min(b, DN_SEQS_SHORT), tt=l, c=c)
        conv_dn_new = qkv[:, l - (DN_CONV - 1):]

    s5_args = (w['pq'], w['qq'], w['mq'], w['tab_long' if prompt else 'tab_short'], w['dv'], w['glu_w'],
               w['glu_b'], w['s5_g'])
    if prompt:
        x0re = s5_re.reshape(b, 1, S5_LANES)
        x0im = s5_im.reshape(b, 1, S5_LANES)
        o_s5, fre, fim = _s5_chunked(u.reshape(b, l, S5_WIDTH), x0re, x0im, *s5_args,
                                     rows=min(l // S5_SLOTS, S5_CHUNKS_LONG), seg=SUBLANES)
    else:
        seg = l // S5_SLOTS
        rep = lambda s: jnp.repeat(s.reshape(b, S5_LANES), seg, axis=0).reshape(1, b * seg, S5_LANES)
        o_s5, fre, fim = _s5_chunked(u.reshape(1, n, S5_WIDTH), rep(s5_re), rep(s5_im), *s5_args,
                                     rows=b * seg, seg=seg)
        o_s5 = o_s5.reshape(b, l, S5_WIDTH)
        fre = fre.reshape(b, seg, S5_LANES)[:, seg - 1]
        fim = fim.reshape(b, seg, S5_LANES)[:, seg - 1]
    s5_re_new = fre.reshape(b, S5_GROUPS, S5_STATE)
    s5_im_new = fim.reshape(b, S5_GROUPS, S5_STATE)

    ffn_args = (w['w_out_a'], w['w_out_b'], w['n2'], w['w_up'], w['fcw'], w['fcb'],
                w['w_down'], w['fg'])
    if prompt:
        prev = _pad_rows_top(conv_ffn, SUBLANES)
        tm = min(l, FFN_ROWS_LONG)
        y, tail = _ffn(x, o_dn, o_s5, prev, *ffn_args, tm=tm, ls=tm, carry=True)
        conv_ffn_new = tail[:, SUBLANES - (FFN_CONV - 1):]
    else:
        prev = jnp.pad(conv_ffn, ((0, 0), (0, l - (FFN_CONV - 1)), (0, 0))).reshape(1, n, D_FF)
        y, tail = _ffn(x.reshape(1, n, D_MODEL), o_dn.reshape(1, n, DN_WIDTH), o_s5.reshape(1, n, S5_WIDTH),
                       prev, *ffn_args, tm=min(n, FFN_ROWS_SHORT), ls=l, carry=False)
        y = y.reshape(b, l, D_MODEL)
        conv_ffn_new = tail.reshape(b, l, D_FF)[:, l - (FFN_CONV - 1):]
    return y, (conv_dn_new[None], s_dn_new[None], s5_re_new[None], s5_im_new[None], conv_ffn_new[None])


def kernel(x_prompt, x_sample, cache_dn_conv, state_dn, state_s5_re, state_s5_im, cache_ffn_conv, norm1_g, w_in, dn_conv_w, dn_A_log, dn_dt_bias, dn_norm_g, s5_A_re, s5_A_im, s5_log_dt, s5_B_re, s5_B_im, s5_C_re, s5_C_im, s5_D, s5_glu_w, s5_glu_b, s5_norm_g, w_out, norm2_g, w_up, ffn_conv_w, ffn_conv_b, w_down, final_norm_g):
    assert w_in.shape[0] == 1, "single-layer trunk"
    o1 = QKV_WIDTH
    o2 = o1 + DN_WIDTH
    o4 = o2 + 2 * DN_HEADS
    wi = w_in[0]
    w_in_main = wi[:, :o2].astype(BF16)
    w_in_tail = jnp.concatenate(
        [wi[:, o4:], wi[:, o2:o4], jnp.zeros((D_MODEL, LANES - 2 * DN_HEADS), wi.dtype)], axis=1).astype(BF16)
    lane_pad = lambda v: jnp.pad(v, (BA_LANE_G, LANES - BA_LANE_G - DN_HEADS))
    seg_short = x_sample.shape[1] // S5_SLOTS
    assert x_sample.shape[1] % S5_SLOTS == 0 and seg_short in (1, 2, 4), "sample sequences of 8, 16 or 32 rows"
    a3 = jnp.concatenate([s5_A_re, s5_A_im, jnp.broadcast_to(s5_log_dt[:, :, None], s5_A_re.shape)], axis=0)
    a_lanes = a3.reshape(3, 1, S5_LANES)
    tab_long, tab_short = [_s5_prep(a_lanes, seg=s) for s in (SUBLANES, seg_short)]
    bc = jnp.concatenate([s5_B_re.transpose(0, 1, 3, 2), s5_B_im.transpose(0, 1, 3, 2), s5_C_re, s5_C_im], axis=0)
    pq, qq, mq = _s5_prep_quads(a3.reshape(3, S5_QUADS, 1, QUAD_LANES), _quad_blocks(bc))
    w = {
        'n1': norm1_g, 'w_in_main': w_in_main, 'w_in_tail': w_in_tail,
        'dn_cw': jnp.pad(dn_conv_w[0], ((0, SUBLANES - DN_CONV), (0, 0))),
        'dn_gp': jnp.pad(jnp.stack([lane_pad(dn_A_log[0]), lane_pad(dn_dt_bias[0])]), ((0, SUBLANES - 2), (0, 0))),
        'dn_g': dn_norm_g,
        'tab_long': tab_long, 'tab_short': tab_short, 'pq': pq, 'qq': qq, 'mq': mq,
        'dv': s5_D, 'glu_w': s5_glu_w[0].astype(BF16), 'glu_b': s5_glu_b, 's5_g': s5_norm_g,
        'w_out_a': w_out[0, :DN_WIDTH].astype(BF16), 'w_out_b': w_out[0, DN_WIDTH:].astype(BF16),
        'n2': norm2_g, 'w_up': w_up[0].astype(BF16),
        'fcw': jnp.pad(ffn_conv_w[0], ((0, SUBLANES - FFN_CONV), (0, 0))), 'fcb': ffn_conv_b,
        'w_down': w_down[0].astype(BF16), 'fg': final_norm_g.reshape(1, D_MODEL),
    }
    bp = x_prompt.shape[0]
    zeros = lambda *s: jnp.zeros(s, F32)
    y_p, st_p = _trunk(x_prompt, zeros(bp, DN_CONV - 1, QKV_WIDTH), zeros(bp, DN_HEADS, DN_HEAD_DIM, DN_HEAD_DIM),
                       zeros(bp, S5_GROUPS, S5_STATE), zeros(bp, S5_GROUPS, S5_STATE),
                       zeros(bp, FFN_CONV - 1, D_FF), w, prompt=True)
    y_s, st_s = _trunk(x_sample, cache_dn_conv[0], state_dn[0], state_s5_re[0], state_s5_im[0],
                       cache_ffn_conv[0], w, prompt=False)
    return (y_p, y_s) + st_p + st_s
```

```python
import functools

import jax
import jax.numpy as jnp
from jax import lax
from jax.experimental import pallas as pl
from jax.experimental.pallas import tpu as pltpu

F32 = jnp.float32
BF16 = jnp.bfloat16
EPS = 1e-6

D_MODEL = 1024
DN_HEADS = 4
DN_HEAD_DIM = 128
DN_WIDTH = DN_HEADS * DN_HEAD_DIM
DN_CONV = 4
QKV_WIDTH = 3 * DN_WIDTH
S5_WIDTH = D_MODEL - DN_WIDTH
S5_GROUP = 16
S5_GROUPS = S5_WIDTH // S5_GROUP
S5_STATE = 64
S5_LANES = S5_GROUPS * S5_STATE
S5_SLOTS = 8
S5_QUADS = S5_GROUPS // 4
QUAD_LANES = 4 * S5_STATE
QUAD_IN = S5_SLOTS * 4 * S5_GROUP
D_FF = 2816
FFN_CONV = 3
CHUNK = 64

SUBLANES = 8
LANES = 128
BA_LANE_G = DN_HEADS
IN_MAIN = QKV_WIDTH + DN_WIDTH
IN_PAD = IN_MAIN + S5_WIDTH + LANES
MXU_DIM = 256
FF_CHUNK = 6 * MXU_DIM

DN_ROWS_LONG = 2 * CHUNK
DN_SEQS_SHORT = 16
S5_CHUNKS_LONG = 256
FFN_ROWS_LONG = 512
FFN_ROWS_SHORT = 256
INPROJ_ROWS = 512

VMEM_LIMIT = 56 * 1024 * 1024


def _dot(a, b):
    return jnp.dot(a, b, preferred_element_type=F32)


def _dot_tn(a, b):
    return lax.dot_general(a, b, (((0,), (0,)), ((), ())), preferred_element_type=F32)


def _split_bf16(a):
    hi = a.astype(BF16)
    lo = (a - hi.astype(F32)).astype(BF16)
    return hi, lo


def _bmm(a, b):
    return lax.dot_general(a, b, (((2,), (1,)), ((0,), (0,))), preferred_element_type=F32)


def _bmm_nt(a, b):
    return lax.dot_general(a, b, (((2,), (2,)), ((0,), (0,))), preferred_element_type=F32)


def _bmm_nt3(a, b):
    ah, al = _split_bf16(a)
    bh, bl = _split_bf16(b)
    return _bmm_nt(ah, bh) + (_bmm_nt(al, bh) + _bmm_nt(ah, bl))


def _rms(x, g):
    return x * lax.rsqrt(jnp.mean(x * x, axis=-1, keepdims=True) + EPS) * g


def _const_spec(shape):
    nd = len(shape)
    return pl.BlockSpec(shape, lambda *_: (0,) * nd, pipeline_mode=pl.Buffered(1))


def _params(n_axes):
    return pltpu.CompilerParams(dimension_semantics=("arbitrary",) * n_axes,
                                vmem_limit_bytes=VMEM_LIMIT)


def _cmul(a, b):
    return a[0] * b[0] - a[1] * b[1], a[0] * b[1] + a[1] * b[0]


def _zoh(are, aim, ldt):
    dt = jnp.exp(ldt)
    mag = jnp.exp(are * dt)
    ang = aim * dt
    lr = mag * jnp.cos(ang)
    li = mag * jnp.sin(ang)
    den = are * are + aim * aim
    f_re = ((lr - 1.0) * are + li * aim) / den
    f_im = (li * are - (lr - 1.0) * aim) / den
    return (lr, li), (f_re, f_im)


def _write_scan_tables(tab_ref, step, seg):
    pw = [step]
    for _ in range(seg - 1):
        pw.append(_cmul(pw[-1], step))
    shape = (SUBLANES, step[0].shape[-1])
    row = lax.broadcasted_iota(jnp.int32, shape, 0) & (seg - 1)
    zero = jnp.zeros(shape, F32)
    pre, pim = zero, zero
    for r in range(seg):
        pre = jnp.where(row == r, pw[r][0], pre)
        pim = jnp.where(row == r, pw[r][1], pim)
    tab_ref[...] = jnp.zeros(tab_ref.shape, F32)
    tab_ref[0:8, :] = pre
    tab_ref[8:16, :] = pim
    for lvl, d in enumerate(_scan_levels(seg)):
        tab_ref[16 + 16 * lvl:24 + 16 * lvl, :] = jnp.where(row >= d, pw[d - 1][0], zero)
        tab_ref[24 + 16 * lvl:32 + 16 * lvl, :] = jnp.where(row >= d, pw[d - 1][1], zero)


def _scan_levels(seg):
    return [d for d in (1, 2, 4) if d < seg]


def _s5prep_kernel(a_ref, tab_ref, *, seg):
    lam, _ = _zoh(a_ref[0], a_ref[1], a_ref[2])
    lam_c = lam
    for _ in range(S5_SLOTS - 1):
        lam_c = _cmul(lam_c, lam)
    _write_scan_tables(tab_ref, lam_c, seg)


def _s5_prep(a_lanes, seg):
    return pl.pallas_call(
        functools.partial(_s5prep_kernel, seg=seg),
        out_shape=jax.ShapeDtypeStruct((64, S5_LANES), F32),
        compiler_params=pltpu.CompilerParams(vmem_limit_bytes=VMEM_LIMIT),
        name="s5_prep",
    )(a_lanes)


def _s5prepq_kernel(a_ref, bc_ref, pq_ref, qq_ref, mq_ref):
    lam, f = _zoh(a_ref[0], a_ref[1], a_ref[2])
    bb = _cmul(f, (bc_ref[0], bc_ref[1]))
    ct = (bc_ref[2], bc_ref[3])
    one = (jnp.ones_like(lam[0]), jnp.zeros_like(lam[0]))
    pw = [one]
    for _ in range(S5_SLOTS):
        pw.append(_cmul(pw[-1], lam))
    blk = 4 * S5_GROUP
    for s in range(S5_SLOTS):
        p_re, p_im = _cmul(pw[S5_SLOTS - 1 - s], bb)
        pq_ref[:, s * blk:(s + 1) * blk, :QUAD_LANES] = p_re.astype(BF16)
        pq_ref[:, s * blk:(s + 1) * blk, QUAD_LANES:] = p_im.astype(BF16)
    cl = [_cmul(pw[e], ct) for e in range(S5_SLOTS + 1)]
    qt_re = jnp.concatenate([cl[t + 1][0] for t in range(S5_SLOTS)], axis=1)
    qt_im = jnp.concatenate([-cl[t + 1][1] for t in range(S5_SLOTS)], axis=1)
    for n in range(S5_QUADS):
        qq_ref[n, :QUAD_LANES, :] = qt_re[n].T.astype(BF16)
        qq_ref[n, QUAD_LANES:, :] = qt_im[n].T.astype(BF16)
    lag_re = jnp.concatenate([cl[tau][0] for tau in range(S5_SLOTS)], axis=1)
    lag_im = jnp.concatenate([cl[tau][1] for tau in range(S5_SLOTS)], axis=1)
    lags = _bmm_nt3(bb[0], lag_re) - _bmm_nt3(bb[1], lag_im)
    for s in range(S5_SLOTS):
        m = lags if s == 0 else jnp.concatenate(
            [jnp.zeros((S5_QUADS, blk, s * blk), F32), lags[:, :, :QUAD_IN - s * blk]], axis=2)
        mq_ref[:, s * blk:(s + 1) * blk, :] = m.astype(BF16)


def _s5_prep_quads(a_quads, bc_quads):
    w = lambda r, c_: jax.ShapeDtypeStruct((S5_QUADS, r, c_), BF16)
    return pl.pallas_call(
        _s5prepq_kernel,
        out_shape=(w(QUAD_IN, 2 * QUAD_LANES), w(2 * QUAD_LANES, QUAD_IN), w(QUAD_IN, QUAD_IN)),
        compiler_params=pltpu.CompilerParams(vmem_limit_bytes=VMEM_LIMIT),
        name="s5_prep_quads",
    )(a_quads, bc_quads)


def _w_cols(wm_ref, wt_ref, lo, hi):
    return wm_ref[:, lo:hi] if hi <= IN_MAIN else wt_ref[:, lo - IN_MAIN:hi - IN_MAIN]


def _inproj_kernel(x_ref, g_ref, wm_ref, wt_ref, qkv_ref, z_ref, u_ref, ba_ref):
    h = _rms(x_ref[...], g_ref[...]).astype(BF16)
    o1 = QKV_WIDTH
    o2 = o1 + DN_WIDTH
    o3 = o2 + S5_WIDTH
    qkv_ref[...] = _dot(h, _w_cols(wm_ref, wt_ref, 0, o1))
    z_ref[...] = _dot(h, _w_cols(wm_ref, wt_ref, o1, o2))
    u_ref[...] = _dot(h, _w_cols(wm_ref, wt_ref, o2, o3))
    ba_ref[...] = _dot(h, _w_cols(wm_ref, wt_ref, o3, IN_PAD))


def _in_proj(x2d, g, wm, wt, tm):
    n = x2d.shape[0]
    row = lambda w_: pl.BlockSpec((tm, w_), lambda i: (i, 0))
    return pl.pallas_call(
        _inproj_kernel,
        grid=(n // tm,),
        in_specs=[row(D_MODEL), _const_spec((1, D_MODEL)), _const_spec((D_MODEL, IN_MAIN)),
                  _const_spec((D_MODEL, IN_PAD - IN_MAIN))],
        out_specs=(row(QKV_WIDTH), row(DN_WIDTH), row(S5_WIDTH), row(LANES)),
        out_shape=(jax.ShapeDtypeStruct((n, QKV_WIDTH), F32),
                   jax.ShapeDtypeStruct((n, DN_WIDTH), F32),
                   jax.ShapeDtypeStruct((n, S5_WIDTH), F32),
                   jax.ShapeDtypeStruct((n, LANES), F32)),
        compiler_params=_params(1),
        name="in_proj",
    )(x2d, g, wm, wt)


def _stack_heads(a):
    dh = DN_HEAD_DIM
    return jnp.concatenate([a[:, :, h * dh:(h + 1) * dh] for h in range(DN_HEADS)], axis=1)


def _unstack_heads(a, c):
    return jnp.concatenate([a[:, h * c:(h + 1) * c, :] for h in range(DN_HEADS)], axis=2)


def _dn_kernel(qkv_ref, z_ref, ba_ref, cache_ref, s0_ref, cw_ref, gp_ref, ng_ref, o_ref, state_ref, halo_s,
               *, nb, tt, c):
    _dn_restart(pl.program_id(1) == 0, cache_ref, s0_ref, halo_s, state_ref)
    stash = _dn_phase_a(qkv_ref, z_ref, ba_ref, cw_ref, gp_ref, halo_s, nb=nb, tt=tt, c=c)
    for step in _dn_phase_b(stash, ng_ref, o_ref, state_ref, nb=nb, tt=tt, c=c):
        step()


def _dn_restart(first, cache_ref, s0_ref, halo_s, state_ref):
    @pl.when(first)
    def _():
        halo_s[...] = cache_ref[...]
        state_ref[...] = s0_ref[...]


def _dn_phase_a(qkv_ref, z_ref, ba_ref, cw_ref, gp_ref, halo_s, *, nb, tt, c, side=()):
    nh = DN_HEADS
    dh = DN_HEAD_DIM
    nck = tt // c
    npb = nb * nck
    hc = nh * c
    side = list(side)

    def side_work(n=1):
        for _ in range(n):
            if side:
                side.pop(0)()

    stack = _stack_heads
    unstack = functools.partial(_unstack_heads, c=c)

    assert DN_CONV == 4
    row8 = lax.broadcasted_iota(jnp.int32, (SUBLANES, MXU_DIM), 0)

    def shift_rows(y, before, d):
        rolled = pltpu.roll(y, d, axis=1)
        top = jnp.where((row8 < d)[None], pltpu.roll(before, d, axis=1), rolled[:, :SUBLANES])
        return jnp.concatenate([top, rolled[:, SUBLANES:]], axis=1)

    conv_parts = []
    for lo in range(0, QKV_WIDTH, MXU_DIM):
        side_work()
        hi = lo + MXU_DIM
        cw = cw_ref[:, lo:hi]
        x = qkv_ref[:, :, lo:hi]
        prev = halo_s[:, :, lo:hi]
        x1 = shift_rows(x, prev, 1)
        pair = x * cw[1:2] + x1 * cw[0:1]
        pair_prev = prev * cw[1:2] + pltpu.roll(prev, 1, axis=1) * cw[0:1]
        acc = (x * cw[3:4] + x1 * cw[2:3]) + shift_rows(pair, pair_prev, 2)
        halo_s[:, :, lo:hi] = x[:, tt - SUBLANES:]
        conv_parts.append((acc * jax.nn.sigmoid(acc)).reshape(npb, c, hi - lo))
    per = DN_WIDTH // MXU_DIM
    q, k, v = [stack(jnp.concatenate(conv_parts[i * per:(i + 1) * per], axis=2)) for i in range(3)]
    side_work()
    q = q * (lax.rsqrt(jnp.sum(q * q, axis=-1, keepdims=True) + EPS) * (dh ** -0.5))
    k = k * lax.rsqrt(jnp.sum(k * k, axis=-1, keepdims=True) + EPS)

    side_work()
    ba = ba_ref[...].reshape(nb * tt, LANES)
    beta = jax.nn.sigmoid(ba).reshape(npb, c, LANES)
    g2 = -jnp.exp(gp_ref[0:1, :]) * jax.nn.softplus(ba + gp_ref[1:2, :])
    row_in_chunk = lax.broadcasted_iota(jnp.int32, (nb * tt, LANES), 0) & (c - 1)
    d = 1
    while d < c:
        g2 = g2 + jnp.where(row_in_chunk >= d, pltpu.roll(g2, d, axis=0), 0.0)
        d *= 2
    g = g2.reshape(npb, c, LANES)

    g_cols = [g[:, :, BA_LANE_G + h:BA_LANE_G + h + 1] for h in range(nh)]
    b_cols = [beta[:, :, h:h + 1] for h in range(nh)]
    g_col = jnp.concatenate(g_cols, axis=1)
    b_col = jnp.concatenate(b_cols, axis=1)
    g_last = jnp.concatenate([jnp.broadcast_to(gc[:, c - 1:c, :], (npb, c, 1)) for gc in g_cols], axis=1)

    hp = min(nh, LANES // c)
    pieces = []
    for h0 in range(0, nh, hp):
        slab = jnp.concatenate(
            [(g2 if hh == 0 else pltpu.roll(g2, LANES - hh, axis=1)).reshape(npb, c, LANES)
             for hh in range(h0, h0 + hp)], axis=1)
        if hp * c < LANES:
            slab = jnp.concatenate([slab, jnp.zeros((npb, LANES - hp * c, LANES), F32)], axis=1)
        rows = [slab[p].T[BA_LANE_G:BA_LANE_G + 1, :hp * c] for p in range(npb)]
        pieces.append(jnp.stack(rows, axis=0))
    g_row = jnp.concatenate(pieces, axis=2)

    ri = lax.broadcasted_iota(jnp.int32, (c, hc), 0)
    lane = lax.broadcasted_iota(jnp.int32, (c, hc), 1)
    cj = lane & (c - 1)
    causal = (ri >= cj)[None]
    strict = (ri > cj)[None]

    def cat_from_cols(cols):
        out = jnp.broadcast_to(cols[nh - 1], (npb, c, hc))
        for h in range(nh - 2, -1, -1):
            out = jnp.where((lane < (h + 1) * c)[None], jnp.broadcast_to(cols[h], (npb, c, hc)), out)
        return out

    decay = jnp.exp(jnp.where(causal, cat_from_cols(g_cols) - g_row, -jnp.inf))

    side_work()
    kb = k.astype(BF16)
    br = lax.broadcasted_iota(jnp.int32, (hc, nh * dh), 0)
    bl = lax.broadcasted_iota(jnp.int32, (hc, nh * dh), 1)
    head_of_row = sum(jnp.where(br >= h * c, 1, 0) for h in range(1, nh))
    head_of_lane = sum(jnp.where(bl >= h * dh, 1, 0) for h in range(1, nh))
    k_bd = jnp.where((head_of_row == head_of_lane)[None], jnp.concatenate([kb] * nh, axis=2), 0.0)
    qk_lhs = jnp.concatenate([unstack(q), unstack(k)], axis=1).astype(BF16)
    qkk = _bmm_nt(qk_lhs, k_bd)
    qk = qkk[:, :c]
    kk = qkk[:, c:]

    sr = lax.broadcasted_iota(jnp.int32, (hc, hc), 0)
    sl = lax.broadcasted_iota(jnp.int32, (hc, hc), 1)
    shift = c.bit_length() - 1
    same_head = ((sr >> shift) == (sl >> shift))[None]

    def bd(m):
        return jnp.where(same_head, jnp.concatenate([m] * nh, axis=1), 0.0)

    lm = jnp.where(strict, cat_from_cols(b_cols) * kk * decay, 0.0)
    tinv = jnp.where((ri == cj)[None], 1.0, 0.0) - jnp.where(((ri >> 1) == (cj >> 1))[None], lm, 0.0)
    s = 2
    while s < c:
        sh = s.bit_length()
        lower_left = ((ri >> sh) == (cj >> sh)) & ((ri & s) != 0) & ((cj & s) == 0)
        side_work()
        a_off = jnp.where(lower_left[None], lm, 0.0).astype(BF16)
        xm = _bmm(tinv.astype(BF16), bd(a_off))
        tinv = tinv - _bmm(xm.astype(BF16), bd(tinv.astype(BF16)))
        s *= 2

    e_g = jnp.exp(g_col)
    rhs = jnp.concatenate([v * b_col, k * (b_col * e_g)], axis=2).astype(BF16)
    sol = _bmm(bd(tinv.astype(BF16)), rhs)
    sol_v = sol[:, :, :dh]
    sol_k = sol[:, :, dh:]
    qe = q * e_g
    w_qe = jnp.concatenate([part[:, h * c:(h + 1) * c] for h in range(nh) for part in (sol_k, qe)],
                           axis=1).astype(BF16)
    kd = (k * jnp.exp(g_last - g_col)).astype(BF16)
    qkd_bd = bd((qk * decay).astype(BF16))
    s_decay = jnp.exp(g_last)
    side_work(len(side))
    return sol_v, w_qe, kd, qkd_bd, s_decay, lambda: stack(z_ref[...].reshape(npb, c, DN_WIDTH))


def _dn_phase_b(stash, ng_ref, o_ref, state_ref, *, nb, tt, c):
    sol_v, w_qe, kd, qkd_bd, s_decay, zs = stash
    nh = DN_HEADS
    dh = DN_HEAD_DIM
    nck = tt // c
    outs = {}
    pending = {}

    def outputs(ck):
        def run():
            prods = {}
            for b in range(nb):
                p = b * nck + ck
                states = [state_ref[b, h] for h in range(nh)]
                prods[b] = [_dot(w_qe[p, h * 2 * c:(h + 1) * 2 * c, :], states[h].astype(BF16)) for h in range(nh)]
                pending[p] = states
            for b in range(nb):
                p = b * nck + ck
                u = sol_v[p] - jnp.concatenate([r[:c] for r in prods[b]], axis=0)
                ub = u.astype(BF16)
                outs[p] = jnp.concatenate([r[c:] for r in prods[b]], axis=0) + _dot(qkd_bd[p], ub)
                pending[p] = (pending[p], ub)
        return run

    def update(ck):
        def run():
            for b in range(nb):
                p = b * nck + ck
                states, ub = pending.pop(p)
                for h in range(nh):
                    r0 = h * c
                    state_ref[b, h] = (states[h] * s_decay[p, r0:r0 + 1, :]
                                       + _dot_tn(kd[p, r0:r0 + c, :], ub[r0:r0 + c, :]))
        return run

    def finish():
        o = jnp.stack([outs[p] for p in range(nb * nck)], axis=0)
        z = zs()
        o = _rms(o, ng_ref[...]) * (z * jax.nn.sigmoid(z))
        o_ref[...] = _unstack_heads(o, c).reshape(nb, tt, DN_WIDTH).astype(BF16)

    return [f(ck) for ck in range(nck) for f in (outputs, update)] + [finish]


def _deltanet(qkv, z, ba, cache8, s0, cw, gp, ng, nb, tt, c):
    b, l, _ = qkv.shape
    seq = lambda w_: pl.BlockSpec((nb, tt, w_), lambda i, j: (i, j, 0))
    per_b = lambda *s: pl.BlockSpec((nb,) + s, lambda i, j: (i,) + (0,) * len(s))
    return pl.pallas_call(
        functools.partial(_dn_kernel, nb=nb, tt=tt, c=c),
        grid=(b // nb, l // tt),
        in_specs=[seq(QKV_WIDTH), seq(DN_WIDTH), seq(LANES), per_b(SUBLANES, QKV_WIDTH),
                  per_b(DN_HEADS, DN_HEAD_DIM, DN_HEAD_DIM),
                  _const_spec((SUBLANES, QKV_WIDTH)), _const_spec((SUBLANES, LANES)),
                  _const_spec((1, DN_HEAD_DIM))],
        out_specs=(seq(DN_WIDTH), per_b(DN_HEADS, DN_HEAD_DIM, DN_HEAD_DIM)),
        out_shape=(jax.ShapeDtypeStruct((b, l, DN_WIDTH), BF16),
                   jax.ShapeDtypeStruct((b, DN_HEADS, DN_HEAD_DIM, DN_HEAD_DIM), F32)),
        scratch_shapes=[pltpu.VMEM((nb, SUBLANES, QKV_WIDTH), F32)],
        compiler_params=_params(2),
        name="deltanet",
    )(qkv, z, ba, cache8, s0, cw, gp, ng)


def _mixer_kernel(x_ref, g_ref, wm_ref, wt_ref, cache_ref, s0_ref, cw_ref, gp_ref, ng_ref,
                  u_ref, o_ref, state_ref, tail_ref, p_cur, p_nxt, halo_s, *, nb, tt, c):
    i = pl.program_id(0)
    o1 = QKV_WIDTH
    o2 = o1 + DN_WIDTH
    o3 = o2 + S5_WIDTH

    @pl.when(i == 0)
    def _():
        p_cur[...] = jnp.zeros(p_cur.shape, F32)

    _dn_restart(i <= 1, cache_ref, s0_ref, halo_s, state_ref)

    h = _rms(x_ref[...].reshape(nb * tt, D_MODEL), g_ref[...]).astype(BF16)

    def project(lo, hi):
        def run():
            res = _dot(h, _w_cols(wm_ref, wt_ref, lo, hi)).reshape(nb, tt, hi - lo)
            if o2 <= lo < o3:
                u_ref[:, :, lo - o2:hi - o2] = res
            else:
                p_nxt[:, :, lo:hi] = res
        return run

    side = [project(lo, min(lo + MXU_DIM, IN_PAD)) for lo in range(0, IN_PAD, MXU_DIM)]
    stash = _dn_phase_a(p_cur.at[:, :, 0:o1], p_cur.at[:, :, o1:o2], p_cur.at[:, :, o3:IN_PAD],
                        cw_ref, gp_ref, halo_s, nb=nb, tt=tt, c=c, side=side)
    for step in _dn_phase_b(stash, ng_ref, o_ref, state_ref, nb=nb, tt=tt, c=c):
        step()
    tail_ref[...] = halo_s[...]
    for lo, hi in ((0, o1), (o1, o2), (o3, IN_PAD)):
        p_cur[:, :, lo:hi] = p_nxt[:, :, lo:hi]


def _mixer(x, g, wm, wt, cache8, s0, cw, gp, ng, tt, c):
    nb, l, _ = x.shape
    nt = l // tt
    cur = lambda w_: pl.BlockSpec((nb, tt, w_), lambda i: (0, jnp.minimum(i, nt - 1), 0))
    prev = lambda w_: pl.BlockSpec((nb, tt, w_), lambda i: (0, jnp.maximum(i - 1, 0), 0))
    return pl.pallas_call(
        functools.partial(_mixer_kernel, nb=nb, tt=tt, c=c),
        grid=(nt + 1,),
        in_specs=[cur(D_MODEL), _const_spec((1, D_MODEL)), _const_spec((D_MODEL, IN_MAIN)),
                  _const_spec((D_MODEL, IN_PAD - IN_MAIN)), _const_spec((nb, SUBLANES, QKV_WIDTH)), _const_spec((nb, DN_HEADS, DN_HEAD_DIM, DN_HEAD_DIM)),
                  _const_spec((SUBLANES, QKV_WIDTH)), _const_spec((SUBLANES, LANES)),
                  _const_spec((1, DN_HEAD_DIM))],
        out_specs=(cur(S5_WIDTH), prev(DN_WIDTH),
                   pl.BlockSpec((nb, DN_HEADS, DN_HEAD_DIM, DN_HEAD_DIM), lambda i: (0, 0, 0, 0)),
                   pl.BlockSpec((nb, SUBLANES, QKV_WIDTH), lambda i: (0, 0, 0))),
        out_shape=(jax.ShapeDtypeStruct((nb, l, S5_WIDTH), F32),
                   jax.ShapeDtypeStruct((nb, l, DN_WIDTH), BF16),
                   jax.ShapeDtypeStruct((nb, DN_HEADS, DN_HEAD_DIM, DN_HEAD_DIM), F32),
                   jax.ShapeDtypeStruct((nb, SUBLANES, QKV_WIDTH), F32)),
        scratch_shapes=[pltpu.VMEM((nb, tt, IN_PAD), F32), pltpu.VMEM((nb, tt, IN_PAD), F32),
                        pltpu.VMEM((nb, SUBLANES, QKV_WIDTH), F32)],
        compiler_params=_params(1),
        name="mixer",
    )(x, g, wm, wt, cache8, s0, cw, gp, ng)


def _s5c_kernel(u_ref, x0re_ref, x0im_ref, pq_ref, qq_ref, mq_ref, tab_ref, dv_ref, gw_ref, gb_ref, ng_ref,
                o_ref, fre_ref, fim_ref, xr_s, xi_s, car_s, u_s, o_s, *, rows, seg):
    carry = seg == SUBLANES
    if carry:
        @pl.when(pl.program_id(1) == 0)
        def _():
            car_s[0:1, :] = x0re_ref[0]
            car_s[1:2, :] = x0im_ref[0]

    ncol = S5_WIDTH // LANES
    for k in range(ncol):
        u_s[k] = u_ref[0, :, k * LANES:(k + 1) * LANES]
    us = [jnp.concatenate([u_s[k, pl.ds(s, rows, stride=S5_SLOTS), :] for k in range(ncol)], axis=1)
          for s in range(S5_SLOTS)]
    ubs = [a.astype(BF16) for a in us]
    blk = 4 * S5_GROUP
    ql = QUAD_LANES
    uq =[jnp.concatenate([ubs[s][:, n * blk:(n + 1) * blk] for s in range(S5_SLOTS)], axis=1)
          for n in range(S5_QUADS)]
    for n in range(S5_QUADS):
        inc = _dot(uq[n], pq_ref[n])
        xr_s[:, n * ql:(n + 1) * ql] = inc[:, :ql]
        xi_s[:, n * ql:(n + 1) * ql] = inc[:, ql:]

    seg_row = lax.broadcasted_iota(jnp.int32, (SUBLANES, S5_LANES), 0) & (seg - 1)

    def block_body(rb, cin):
        r0 = pl.multiple_of(rb * SUBLANES, SUBLANES)
        if carry:
            c_re, c_im = cin
        else:
            c_re = x0re_ref[0, pl.ds(r0, SUBLANES), :]
            c_im = x0im_ref[0, pl.ds(r0, SUBLANES), :]
        xr = xr_s[pl.ds(r0, SUBLANES), :]
        xi = xi_s[pl.ds(r0, SUBLANES), :]
        for lvl, d in enumerate(_scan_levels(seg)):
            m_re = tab_ref[16 + 16 * lvl:24 + 16 * lvl, :]
            m_im = tab_ref[24 + 16 * lvl:32 + 16 * lvl, :]
            sr = pltpu.roll(xr, d, axis=0)
            si = pltpu.roll(xi, d, axis=0)
            xr, xi = xr + (m_re * sr - m_im * si), xi + (m_re * si + m_im * sr)
        p_re = tab_ref[0:8, :]
        p_im = tab_ref[8:16, :]
        xr, xi = xr + (p_re * c_re - p_im * c_im), xi + (p_re * c_im + p_im * c_re)
        xr_s[pl.ds(r0, SUBLANES), :] = jnp.where(seg_row == 0, c_re, pltpu.roll(xr, 1, axis=0))
        xi_s[pl.ds(r0, SUBLANES), :] = jnp.where(seg_row == 0, c_im, pltpu.roll(xi, 1, axis=0))
        if carry:
            return xr[SUBLANES - 1:, :], xi[SUBLANES - 1:, :]
        fre_ref[0, pl.ds(r0, SUBLANES), :] = xr
        fim_ref[0, pl.ds(r0, SUBLANES), :] = xi
        return cin

    if carry:
        c_re, c_im = lax.fori_loop(0, rows // SUBLANES, block_body, (car_s[0:1, :], car_s[1:2, :]))
        car_s[0:1, :] = c_re
        car_s[1:2, :] = c_im
        fre_ref[0] = c_re
        fim_ref[0] = c_im
    else:
        lax.fori_loop(0, rows // SUBLANES, block_body, 0)

    ys = []
    for n in range(S5_QUADS):
        xs = jnp.concatenate([xr_s[:, n * ql:(n + 1) * ql], xi_s[:, n * ql:(n + 1) * ql]], axis=1).astype(BF16)
        half = QUAD_IN // 2
        intra = jnp.concatenate([_dot(uq[n][:, :half], mq_ref[n, :half, :half]), _dot(uq[n], mq_ref[n, :, half:])],
                                axis=1)
        ys.append(intra + _dot(xs, qq_ref[n]))
    for s in range(S5_SLOTS):
        y = jnp.concatenate([ys[n][:, s * blk:(s + 1) * blk] for n in range(S5_QUADS)], axis=1)
        y = jax.nn.gelu(y + dv_ref[...] * us[s]).astype(BF16)
        gl = _dot(y, gw_ref[...]) + gb_ref[...]
        o = gl[:, :S5_WIDTH] * jax.nn.sigmoid(gl[:, S5_WIDTH:])
        o = _rms(o, ng_ref[...])
        for k in range(ncol):
            o_s[k, pl.ds(s, rows, stride=S5_SLOTS), :] = o[:, k * LANES:(k + 1) * LANES]
    o_ref[0] = jnp.concatenate([o_s[k] for k in range(ncol)], axis=1).astype(BF16)


def _s5_chunked(u, x0re, x0im, pq, qq, mq, tabc, dv, gw, gb, ng, rows, seg):
    b, l, _ = u.shape
    tt = rows * S5_SLOTS
    seq = pl.BlockSpec((1, tt, S5_WIDTH), lambda i, j: (i, j, 0))
    if seg == SUBLANES:
        st = pl.BlockSpec((1, 1, S5_LANES), lambda i, j: (i, 0, 0))
    else:
        st = pl.BlockSpec((1, rows, S5_LANES), lambda i, j: (i, j, 0))
    return pl.pallas_call(
        functools.partial(_s5c_kernel, rows=rows, seg=seg),
        grid=(b, l // tt),
        in_specs=[seq, st, st,
                  _const_spec((S5_QUADS, QUAD_IN, 2 * QUAD_LANES)),
                  _const_spec((S5_QUADS, 2 * QUAD_LANES, QUAD_IN)),
                  _const_spec((S5_QUADS, QUAD_IN, QUAD_IN)), _const_spec((64, S5_LANES)),
                  _const_spec((1, S5_WIDTH)), _const_spec((S5_WIDTH, 2 * S5_WIDTH)),
                  _const_spec((1, 2 * S5_WIDTH)), _const_spec((1, S5_WIDTH))],
        out_specs=(seq, st, st),
        out_shape=(jax.ShapeDtypeStruct((b, l, S5_WIDTH), BF16),
                   jax.ShapeDtypeStruct(x0re.shape, F32),
                   jax.ShapeDtypeStruct(x0re.shape, F32)),
        scratch_shapes=[pltpu.VMEM((rows, S5_LANES), F32), pltpu.VMEM((rows, S5_LANES), F32),
                        pltpu.VMEM((SUBLANES, S5_LANES), F32),
                        pltpu.VMEM((S5_WIDTH // LANES, tt, LANES), F32),
                        pltpu.VMEM((S5_WIDTH // LANES, tt, LANES), F32)],
        compiler_params=_params(2),
        name="s5_chunked",
    )(u, x0re, x0im, pq, qq, mq, tabc, dv, gw, gb, ng)


def _ffn_kernel(x_ref, odn_ref, os5_ref, prev_ref, woa_ref, wob_ref, n2_ref, wu_ref,
                fcw_ref, fcb_ref, wd_ref, fg_ref, y_ref, tail_ref, halo_s, *, tm, ls, carry):
    if carry:
        @pl.when(pl.program_id(1) == 0)
        def _():
            halo_s[...] = prev_ref[0]

    x1 = x_ref[0] + _dot(odn_ref[0], woa_ref[...]) + _dot(os5_ref[0], wob_ref[...])
    h2 = _rms(x1, n2_ref[...]).astype(BF16)
    down = None
    for lo in range(0, D_FF, FF_CHUNK):
        hi = min(lo + FF_CHUNK, D_FF)
        gate = _dot(h2, wu_ref[:, lo:hi])
        val = _dot(h2, wu_ref[:, D_FF + lo:D_FF + hi])
        if carry:
            xx = jnp.concatenate([halo_s[:, lo:hi], gate], axis=0)
            g2 = xx[SUBLANES - 2:SUBLANES - 2 + tm]
            g1 = xx[SUBLANES - 1:SUBLANES - 1 + tm]
            halo_s[:, lo:hi] = gate[tm - SUBLANES:, :]
            tail_ref[0, :, lo:hi] = gate[tm - SUBLANES:, :]
        else:
            r = lax.broadcasted_iota(jnp.int32, (tm, hi - lo), 0) & (ls - 1)
            prev = prev_ref[0, :, lo:hi]
            g1 = jnp.where(r < 1, pltpu.roll(prev, tm - 1, axis=0), pltpu.roll(gate, 1, axis=0))
            g2 = jnp.where(r < 2, prev, pltpu.roll(gate, 2, axis=0))
            tail_ref[0, :, lo:hi] = gate
        cw = fcw_ref[:, lo:hi]
        pre = g2 * cw[0:1] + g1 * cw[1:2] + gate * cw[2:3] + fcb_ref[:, lo:hi]
        act = pre * jax.nn.sigmoid(pre) * val
        part = _dot(act.astype(BF16), wd_ref[lo:hi, :])
        down = part if down is None else down + part
    y_ref[0] = _rms(x1 + down, fg_ref[...])


def _ffn(x, odn, os5, prev, woa, wob, n2, wu, fcw, fcb, wd, fg, tm, ls, carry):
    b, l, _ = x.shape
    seq = lambda w_: pl.BlockSpec((1, tm, w_), lambda i, j: (i, j, 0))
    if carry:
        prev_spec = pl.BlockSpec((1, SUBLANES, D_FF), lambda i, j: (i, 0, 0))
        tail_spec = pl.BlockSpec((1, SUBLANES, D_FF), lambda i, j: (i, 0, 0))
        tail_shape = jax.ShapeDtypeStruct((b, SUBLANES, D_FF), F32)
    else:
        prev_spec = seq(D_FF)
        tail_spec = seq(D_FF)
        tail_shape = jax.ShapeDtypeStruct((b, l, D_FF), F32)
    return pl.pallas_call(
        functools.partial(_ffn_kernel, tm=tm, ls=ls, carry=carry),
        grid=(b, l // tm),
        in_specs=[seq(D_MODEL), seq(DN_WIDTH), seq(S5_WIDTH), prev_spec,
                  _const_spec((DN_WIDTH, D_MODEL)), _const_spec((S5_WIDTH, D_MODEL)),
                  _const_spec((1, D_MODEL)), _const_spec((D_MODEL, 2 * D_FF)),
                  _const_spec((SUBLANES, D_FF)), _const_spec((1, D_FF)), _const_spec((D_FF, D_MODEL)),
                  _const_spec((1, D_MODEL))],
        out_specs=(seq(D_MODEL), tail_spec),
        out_shape=(jax.ShapeDtypeStruct((b, l, D_MODEL), F32), tail_shape),
        scratch_shapes=[pltpu.VMEM((SUBLANES, D_FF), F32)],
        compiler_params=_params(2),
        name="ffn",
    )(x, odn, os5, prev, woa, wob, n2, wu, fcw, fcb, wd, fg)


def _pad_rows_top(a, rows):
    return jnp.pad(a, ((0, 0), (rows - a.shape[1], 0), (0, 0)))


def _quad_blocks(m):
    n = m.shape[0]
    same = jnp.eye(4, dtype=bool)[:, None, :, None]
    blocks = jnp.where(same, m.reshape(n, S5_QUADS, 4, S5_GROUP, 1, S5_STATE), 0.0)
    return blocks.reshape(n, S5_QUADS, 4 * S5_GROUP, QUAD_LANES)


def _trunk(x, conv_dn, s_dn, s5_re, s5_im, conv_ffn, w, prompt):
    b, l, _ = x.shape
    n = b * l
    c = CHUNK if l % CHUNK == 0 else l
    assert l >= 2 * SUBLANES, "sequence shorter than two row tiles"
    cache8 = _pad_rows_top(conv_dn, SUBLANES)
    if prompt:
        u, o_dn, s_dn_new, tail = _mixer(x, w['n1'], w['w_in_main'], w['w_in_tail'], cache8, s_dn,
                                         w['dn_cw'], w['dn_gp'], w['dn_g'],
                                         tt=min(l, DN_ROWS_LONG), c=c)
        conv_dn_new = tail[:, SUBLANES - (DN_CONV - 1):]
    else:
        qkv, z, u, ba = _in_proj(x.reshape(n, D_MODEL), w['n1'], w['w_in_main'], w['w_in_tail'], min(n, INPROJ_ROWS))
        qkv = qkv.reshape(b, l, QKV_WIDTH)
        o_dn, s_dn_new = _deltanet(qkv, z.reshape(b, l, DN_WIDTH), ba.reshape(b, l, LANES), cache8, s_dn,
                                   w['dn_cw'], w['dn_gp'], w['dn_g'], nb=min(b, DN_SEQS_SHORT), tt=l, c=c)
        conv_dn_new = qkv[:, l - (DN_CONV - 1):]

    s5_args = (w['pq'], w['qq'], w['mq'], w['tab_long' if prompt else 'tab_short'], w['dv'], w['glu_w'],
               w['glu_b'], w['s5_g'])
    if prompt:
        x0re = s5_re.reshape(b, 1, S5_LANES)
        x0im = s5_im.reshape(b, 1, S5_LANES)
        o_s5, fre, fim = _s5_chunked(u.reshape(b, l, S5_WIDTH), x0re, x0im, *s5_args,
                                     rows=min(l // S5_SLOTS, S5_CHUNKS_LONG), seg=SUBLANES)
    else:
        seg = l // S5_SLOTS
        rep = lambda s: jnp.repeat(s.reshape(b, S5_LANES), seg, axis=0).reshape(1, b * seg, S5_LANES)
        o_s5, fre, fim = _s5_chunked(u.reshape(1, n, S5_WIDTH), rep(s5_re), rep(s5_im), *s5_args,
                                     rows=b * seg, seg=seg)
        o_s5 = o_s5.reshape(b, l, S5_WIDTH)
        fre = fre.reshape(b, seg, S5_LANES)[:, seg - 1]
        fim = fim.reshape(b, seg, S5_LANES)[:, seg - 1]
    s5_re_new = fre.reshape(b, S5_GROUPS, S5_STATE)
    s5_im_new = fim.reshape(b, S5_GROUPS, S5_STATE)

    ffn_args = (w['w_out_a'], w['w_out_b'], w['n2'], w['w_up'], w['fcw'], w['fcb'],
                w['w_down'], w['fg'])
    if prompt:
        prev = _pad_rows_top(conv_ffn, SUBLANES)
        tm = min(l, FFN_ROWS_LONG)
        y, tail = _ffn(x, o_dn, o_s5, prev, *ffn_args, tm=tm, ls=tm, carry=True)
        conv_ffn_new = tail[:, SUBLANES - (FFN_CONV - 1):]
    else:
        prev = jnp.pad(conv_ffn, ((0, 0), (0, l - (FFN_CONV - 1)), (0, 0))).reshape(1, n, D_FF)
        y, tail = _ffn(x.reshape(1, n, D_MODEL), o_dn.reshape(1, n, DN_WIDTH), o_s5.reshape(1, n, S5_WIDTH),
                       prev, *ffn_args, tm=min(n, FFN_ROWS_SHORT), ls=l, carry=False)
        y = y.reshape(b, l, D_MODEL)
        conv_ffn_new = tail.reshape(b, l, D_FF)[:, l - (FFN_CONV - 1):]
    return y, (conv_dn_new[None], s_dn_new[None], s5_re_new[None], s5_im_new[None], conv_ffn_new[None])


def kernel(x_prompt, x_sample, cache_dn_conv, state_dn, state_s5_re, state_s5_im, cache_ffn_conv, norm1_g, w_in, dn_conv_w, dn_A_log, dn_dt_bias, dn_norm_g, s5_A_re, s5_A_im, s5_log_dt, s5_B_re, s5_B_im, s5_C_re, s5_C_im, s5_D, s5_glu_w, s5_glu_b, s5_norm_g, w_out, norm2_g, w_up, ffn_conv_w, ffn_conv_b, w_down, final_norm_g):
    assert w_in.shape[0] == 1, "single-layer trunk"
    o1 = QKV_WIDTH
    o2 = o1 + DN_WIDTH
    o4 = o2 + 2 * DN_HEADS
    wi = w_in[0]
    w_in_main = wi[:, :o2].astype(BF16)
    w_in_tail = jnp.concatenate(
        [wi[:, o4:], wi[:, o2:o4], jnp.zeros((D_MODEL, LANES - 2 * DN_HEADS), wi.dtype)], axis=1).astype(BF16)
    lane_pad = lambda v: jnp.pad(v, (BA_LANE_G, LANES - BA_LANE_G - DN_HEADS))
    seg_short = x_sample.shape[1] // S5_SLOTS
    assert x_sample.shape[1] % S5_SLOTS == 0 and seg_short in (1, 2, 4), "sample sequences of 8, 16 or 32 rows"
    a3 = jnp.concatenate([s5_A_re, s5_A_im, jnp.broadcast_to(s5_log_dt[:, :, None], s5_A_re.shape)], axis=0)
    a_lanes = a3.reshape(3, 1, S5_LANES)
    tab_long, tab_short = [_s5_prep(a_lanes, seg=s) for s in (SUBLANES, seg_short)]
    bc = jnp.concatenate([s5_B_re.transpose(0, 1, 3, 2), s5_B_im.transpose(0, 1, 3, 2), s5_C_re, s5_C_im], axis=0)
    pq, qq, mq = _s5_prep_quads(a3.reshape(3, S5_QUADS, 1, QUAD_LANES), _quad_blocks(bc))
    w = {
        'n1': norm1_g, 'w_in_main': w_in_main, 'w_in_tail': w_in_tail,
        'dn_cw': jnp.pad(dn_conv_w[0], ((0, SUBLANES - DN_CONV), (0, 0))),
        'dn_gp': jnp.pad(jnp.stack([lane_pad(dn_A_log[0]), lane_pad(dn_dt_bias[0])]), ((0, SUBLANES - 2), (0, 0))),
        'dn_g': dn_norm_g,
        'tab_long': tab_long, 'tab_short': tab_short, 'pq': pq, 'qq': qq, 'mq': mq,
        'dv': s5_D, 'glu_w': s5_glu_w[0].astype(BF16), 'glu_b': s5_glu_b, 's5_g': s5_norm_g,
        'w_out_a': w_out[0, :DN_WIDTH].astype(BF16), 'w_out_b': w_out[0, DN_WIDTH:].astype(BF16),
        'n2': norm2_g, 'w_up': w_up[0].astype(BF16),
        'fcw': jnp.pad(ffn_conv_w[0], ((0, SUBLANES - FFN_CONV), (0, 0))), 'fcb': ffn_conv_b,
        'w_down': w_down[0].astype(BF16), 'fg': final_norm_g.reshape(1, D_MODEL),
    }
    bp = x_prompt.shape[0]
    zeros = lambda *s: jnp.zeros(s, F32)
    y_p, st_p = _trunk(x_prompt, zeros(bp, DN_CONV - 1, QKV_WIDTH), zeros(bp, DN_HEADS, DN_HEAD_DIM, DN_HEAD_DIM),
                       zeros(bp, S5_GROUPS, S5_STATE), zeros(bp, S5_GROUPS, S5_STATE),
                       zeros(bp, FFN_CONV - 1, D_FF), w, prompt=True)
    y_s, st_s = _trunk(x_sample, cache_dn_conv[0], state_dn[0], state_s5_re[0], state_s5_im[0],
                       cache_ffn_conv[0], w, prompt=False)
    return (y_p, y_s) + st_p + st_s
```

```python
import functools

import jax
import jax.numpy as jnp
from jax import lax
from jax.experimental import pallas as pl
from jax.experimental.pallas import tpu as pltpu

F32 = jnp.float32
BF16 = jnp.bfloat16
EPS = 1e-6

D_MODEL = 1024
DN_HEADS = 4
DN_HEAD_DIM = 128
DN_WIDTH = DN_HEADS * DN_HEAD_DIM
DN_CONV = 4
QKV_WIDTH = 3 * DN_WIDTH
S5_WIDTH = D_MODEL - DN_WIDTH
S5_GROUP = 16
S5_GROUPS = S5_WIDTH // S5_GROUP
S5_STATE = 64
S5_LANES = S5_GROUPS * S5_STATE
S5_SLOTS = 8
S5_QUADS = S5_GROUPS // 4
QUAD_LANES = 4 * S5_STATE
QUAD_IN = S5_SLOTS * 4 * S5_GROUP
D_FF = 2816
FFN_CONV = 3
CHUNK = 64

SUBLANES = 8
LANES = 128
BA_LANE_G = DN_HEADS
IN_MAIN = QKV_WIDTH + DN_WIDTH
IN_PAD = IN_MAIN + S5_WIDTH + LANES
MXU_DIM = 256
FF_CHUNK = 6 * MXU_DIM

DN_ROWS_LONG = 2 * CHUNK
DN_SEQS_SHORT = 16
S5_CHUNKS_LONG = 256
FFN_ROWS_LONG = 512
FFN_ROWS_SHORT = 256
INPROJ_ROWS = 512

VMEM_LIMIT = 56 * 1024 * 1024


def _dot(a, b):
    return jnp.dot(a, b, preferred_element_type=F32)


def _dot_tn(a, b):
    return lax.dot_general(a, b, (((0,), (0,)), ((), ())), preferred_element_type=F32)


def _split_bf16(a):
    hi = a.astype(BF16)
    lo = (a - hi.astype(F32)).astype(BF16)
    return hi, lo


def _bmm(a, b):
    return lax.dot_general(a, b, (((2,), (1,)), ((0,), (0,))), preferred_element_type=F32)


def _bmm_nt(a, b):
    return lax.dot_general(a, b, (((2,), (2,)), ((0,), (0,))), preferred_element_type=F32)


def _bmm_nt3(a, b):
    ah, al = _split_bf16(a)
    bh, bl = _split_bf16(b)
    return _bmm_nt(ah, bh) + (_bmm_nt(al, bh) + _bmm_nt(ah, bl))


def _rms(x, g):
    return x * lax.rsqrt(jnp.mean(x * x, axis=-1, keepdims=True) + EPS) * g


def _const_spec(shape):
    nd = len(shape)
    return pl.BlockSpec(shape, lambda *_: (0,) * nd, pipeline_mode=pl.Buffered(1))


def _params(n_axes):
    return pltpu.CompilerParams(dimension_semantics=("arbitrary",) * n_axes,
                                vmem_limit_bytes=VMEM_LIMIT)


def _cmul(a, b):
    return a[0] * b[0] - a[1] * b[1], a[0] * b[1] + a[1] * b[0]


def _zoh(are, aim, ldt):
    dt = jnp.exp(ldt)
    mag = jnp.exp(are * dt)
    ang = aim * dt
    lr = mag * jnp.cos(ang)
    li = mag * jnp.sin(ang)
    den = are * are + aim * aim
    f_re = ((lr - 1.0) * are + li * aim) / den
    f_im = (li * are - (lr - 1.0) * aim) / den
    return (lr, li), (f_re, f_im)


def _write_scan_tables(tab_ref, step, seg):
    pw = [step]
    for _ in range(seg - 1):
        pw.append(_cmul(pw[-1], step))
    shape = (SUBLANES, step[0].shape[-1])
    row = lax.broadcasted_iota(jnp.int32, shape, 0) & (seg - 1)
    zero = jnp.zeros(shape, F32)
    pre, pim = zero, zero
    for r in range(seg):
        pre = jnp.where(row == r, pw[r][0], pre)
        pim = jnp.where(row == r, pw[r][1], pim)
    tab_ref[...] = jnp.zeros(tab_ref.shape, F32)
    tab_ref[0:8, :] = pre
    tab_ref[8:16, :] = pim
    for lvl, d in enumerate(_scan_levels(seg)):
        tab_ref[16 + 16 * lvl:24 + 16 * lvl, :] = jnp.where(row >= d, pw[d - 1][0], zero)
        tab_ref[24 + 16 * lvl:32 + 16 * lvl, :] = jnp.where(row >= d, pw[d - 1][1], zero)


def _scan_levels(seg):
    return [d for d in (1, 2, 4) if d < seg]


def _s5prep_kernel(a_ref, tab_ref, *, seg):
    lam, _ = _zoh(a_ref[0], a_ref[1], a_ref[2])
    lam_c = lam
    for _ in range(S5_SLOTS - 1):
        lam_c = _cmul(lam_c, lam)
    _write_scan_tables(tab_ref, lam_c, seg)


def _s5_prep(a_lanes, seg):
    return pl.pallas_call(
        functools.partial(_s5prep_kernel, seg=seg),
        out_shape=jax.ShapeDtypeStruct((64, S5_LANES), F32),
        compiler_params=pltpu.CompilerParams(vmem_limit_bytes=VMEM_LIMIT),
        name="s5_prep",
    )(a_lanes)


def _s5prepq_kernel(a_ref, bc_ref, pq_ref, qq_ref, mq_ref):
    lam, f = _zoh(a_ref[0], a_ref[1], a_ref[2])
    bb = _cmul(f, (bc_ref[0], bc_ref[1]))
    ct = (bc_ref[2], bc_ref[3])
    one = (jnp.ones_like(lam[0]), jnp.zeros_like(lam[0]))
    pw = [one]
    for _ in range(S5_SLOTS):
        pw.append(_cmul(pw[-1], lam))
    blk = 4 * S5_GROUP
    for s in range(S5_SLOTS):
        p_re, p_im = _cmul(pw[S5_SLOTS - 1 - s], bb)
        pq_ref[:, s * blk:(s + 1) * blk, :QUAD_LANES] = p_re.astype(BF16)
        pq_ref[:, s * blk:(s + 1) * blk, QUAD_LANES:] = p_im.astype(BF16)
    cl = [_cmul(pw[e], ct) for e in range(S5_SLOTS + 1)]
    qt_re = jnp.concatenate([cl[t + 1][0] for t in range(S5_SLOTS)], axis=1)
    qt_im = jnp.concatenate([-cl[t + 1][1] for t in range(S5_SLOTS)], axis=1)
    for n in range(S5_QUADS):
        qq_ref[n, :QUAD_LANES, :] = qt_re[n].T.astype(BF16)
        qq_ref[n, QUAD_LANES:, :] = qt_im[n].T.astype(BF16)
    lag_re = jnp.concatenate([cl[tau][0] for tau in range(S5_SLOTS)], axis=1)
    lag_im = jnp.concatenate([cl[tau][1] for tau in range(S5_SLOTS)], axis=1)
    lags = _bmm_nt3(bb[0], lag_re) - _bmm_nt3(bb[1], lag_im)
    for s in range(S5_SLOTS):
        m = lags if s == 0 else jnp.concatenate(
            [jnp.zeros((S5_QUADS, blk, s * blk), F32), lags[:, :, :QUAD_IN - s * blk]], axis=2)
        mq_ref[:, s * blk:(s + 1) * blk, :] = m.astype(BF16)


def _s5_prep_quads(a_quads, bc_quads):
    w = lambda r, c_: jax.ShapeDtypeStruct((S5_QUADS, r, c_), BF16)
    return pl.pallas_call(
        _s5prepq_kernel,
        out_shape=(w(QUAD_IN, 2 * QUAD_LANES), w(2 * QUAD_LANES, QUAD_IN), w(QUAD_IN, QUAD_IN)),
        compiler_params=pltpu.CompilerParams(vmem_limit_bytes=VMEM_LIMIT),
        name="s5_prep_quads",
    )(a_quads, bc_quads)


def _w_cols(wm_ref, wt_ref, lo, hi):
    return wm_ref[:, lo:hi] if hi <= IN_MAIN else wt_ref[:, lo - IN_MAIN:hi - IN_MAIN]


def _inproj_kernel(x_ref, g_ref, wm_ref, wt_ref, qkv_ref, z_ref, u_ref, ba_ref):
    h = _rms(x_ref[...], g_ref[...]).astype(BF16)
    o1 = QKV_WIDTH
    o2 = o1 + DN_WIDTH
    o3 = o2 + S5_WIDTH
    qkv_ref[...] = _dot(h, _w_cols(wm_ref, wt_ref, 0, o1))
    z_ref[...] = _dot(h, _w_cols(wm_ref, wt_ref, o1, o2))
    u_ref[...] = _dot(h, _w_cols(wm_ref, wt_ref, o2, o3))
    ba_ref[...] = _dot(h, _w_cols(wm_ref, wt_ref, o3, IN_PAD))


def _in_proj(x2d, g, wm, wt, tm):
    n = x2d.shape[0]
    row = lambda w_: pl.BlockSpec((tm, w_), lambda i: (i, 0))
    return pl.pallas_call(
        _inproj_kernel,
        grid=(n // tm,),
        in_specs=[row(D_MODEL), _const_spec((1, D_MODEL)), _const_spec((D_MODEL, IN_MAIN)),
                  _const_spec((D_MODEL, IN_PAD - IN_MAIN))],
        out_specs=(row(QKV_WIDTH), row(DN_WIDTH), row(S5_WIDTH), row(LANES)),
        out_shape=(jax.ShapeDtypeStruct((n, QKV_WIDTH), F32),
                   jax.ShapeDtypeStruct((n, DN_WIDTH), F32),
                   jax.ShapeDtypeStruct((n, S5_WIDTH), F32),
                   jax.ShapeDtypeStruct((n, LANES), F32)),
        compiler_params=_params(1),
        name="in_proj",
    )(x2d, g, wm, wt)


def _stack_heads(a):
    dh = DN_HEAD_DIM
    return jnp.concatenate([a[:, :, h * dh:(h + 1) * dh] for h in range(DN_HEADS)], axis=1)


def _unstack_heads(a, c):
    return jnp.concatenate([a[:, h * c:(h + 1) * c, :] for h in range(DN_HEADS)], axis=2)


def _dn_kernel(qkv_ref, z_ref, ba_ref, cache_ref, s0_ref, cw_ref, gp_ref, ng_ref, o_ref, state_ref, halo_s,
               *, nb, tt, c):
    _dn_restart(pl.program_id(1) == 0, cache_ref, s0_ref, halo_s, state_ref)
    stash = _dn_phase_a(qkv_ref, z_ref, ba_ref, cw_ref, gp_ref, halo_s, nb=nb, tt=tt, c=c)
    for step in _dn_phase_b(stash, ng_ref, o_ref, state_ref, nb=nb, tt=tt, c=c):
        step()


def _dn_restart(first, cache_ref, s0_ref, halo_s, state_ref):
    @pl.when(first)
    def _():
        halo_s[...] = cache_ref[...]
        state_ref[...] = s0_ref[...]


def _dn_phase_a(qkv_ref, z_ref, ba_ref, cw_ref, gp_ref, halo_s, *, nb, tt, c, side=()):
    nh = DN_HEADS
    dh = DN_HEAD_DIM
    nck = tt // c
    npb = nb * nck
    hc = nh * c
    side = list(side)

    def side_work(n=1):
        for _ in range(n):
            if side:
                side.pop(0)()

    stack = _stack_heads
    unstack = functools.partial(_unstack_heads, c=c)

    assert DN_CONV == 4
    piece = DN_WIDTH
    row8 = lax.broadcasted_iota(jnp.int32, (SUBLANES, piece), 0)

    def shift_rows(y, before, d):
        rolled = pltpu.roll(y, d, axis=1)
        top = jnp.where((row8 < d)[None], pltpu.roll(before, d, axis=1), rolled[:, :SUBLANES])
        return jnp.concatenate([top, rolled[:, SUBLANES:]], axis=1)

    conv_parts = []
    for lo in range(0, QKV_WIDTH, piece):
        side_work()
        hi = lo + piece
        cw = cw_ref[:, lo:hi]
        x = qkv_ref[:, :, lo:hi]
        prev = halo_s[:, :, lo:hi]
        x1 = shift_rows(x, prev, 1)
        pair = x * cw[1:2] + x1 * cw[0:1]
        pair_prev = prev * cw[1:2] + pltpu.roll(prev, 1, axis=1) * cw[0:1]
        acc = (x * cw[3:4] + x1 * cw[2:3]) + shift_rows(pair, pair_prev, 2)
        halo_s[:, :, lo:hi] = x[:, tt - SUBLANES:]
        conv_parts.append((acc * jax.nn.sigmoid(acc)).reshape(npb, c, hi - lo))
    act = jnp.concatenate(conv_parts, axis=2)
    q, k, v = [stack(act[:, :, i * DN_WIDTH:(i + 1) * DN_WIDTH]) for i in range(3)]
    side_work()
    q = q * (lax.rsqrt(jnp.sum(q * q, axis=-1, keepdims=True) + EPS) * (dh ** -0.5))
    k = k * lax.rsqrt(jnp.sum(k * k, axis=-1, keepdims=True) + EPS)

    side_work()
    ba = ba_ref[...].reshape(nb * tt, LANES)
    beta = jax.nn.sigmoid(ba).reshape(npb, c, LANES)
    g2 = -jnp.exp(gp_ref[0:1, :]) * jax.nn.softplus(ba + gp_ref[1:2, :])
    row_in_chunk = lax.broadcasted_iota(jnp.int32, (nb * tt, LANES), 0) & (c - 1)
    d = 1
    while d < c:
        g2 = g2 + jnp.where(row_in_chunk >= d, pltpu.roll(g2, d, axis=0), 0.0)
        d *= 2
    g = g2.reshape(npb, c, LANES)

    g_cols = [g[:, :, BA_LANE_G + h:BA_LANE_G + h + 1] for h in range(nh)]
    b_cols = [beta[:, :, h:h + 1] for h in range(nh)]
    g_col = jnp.concatenate(g_cols, axis=1)
    b_col = jnp.concatenate(b_cols, axis=1)
    g_last = jnp.concatenate([jnp.broadcast_to(gc[:, c - 1:c, :], (npb, c, 1)) for gc in g_cols], axis=1)

    hp = min(nh, LANES // c)
    pieces = []
    for h0 in range(0, nh, hp):
        slab = jnp.concatenate(
            [(g2 if hh == 0 else pltpu.roll(g2, LANES - hh, axis=1)).reshape(npb, c, LANES)
             for hh in range(h0, h0 + hp)], axis=1)
        if hp * c < LANES:
            slab = jnp.concatenate([slab, jnp.zeros((npb, LANES - hp * c, LANES), F32)], axis=1)
        rows = [slab[p].T[BA_LANE_G:BA_LANE_G + 1, :hp * c] for p in range(npb)]
        pieces.append(jnp.stack(rows, axis=0))
    g_row = jnp.concatenate(pieces, axis=2)

    ri = lax.broadcasted_iota(jnp.int32, (c, hc), 0)
    lane = lax.broadcasted_iota(jnp.int32, (c, hc), 1)
    cj = lane & (c - 1)
    causal = (ri >= cj)[None]
    strict = (ri > cj)[None]

    def cat_from_cols(cols):
        out = jnp.broadcast_to(cols[nh - 1], (npb, c, hc))
        for h in range(nh - 2, -1, -1):
            out = jnp.where((lane < (h + 1) * c)[None], jnp.broadcast_to(cols[h], (npb, c, hc)), out)
        return out

    decay = jnp.exp(jnp.where(causal, cat_from_cols(g_cols) - g_row, -jnp.inf))

    side_work()
    kb = k.astype(BF16)
    br = lax.broadcasted_iota(jnp.int32, (hc, nh * dh), 0)
    bl = lax.broadcasted_iota(jnp.int32, (hc, nh * dh), 1)
    head_of_row = sum(jnp.where(br >= h * c, 1, 0) for h in range(1, nh))
    head_of_lane = sum(jnp.where(bl >= h * dh, 1, 0) for h in range(1, nh))
    k_bd = jnp.where((head_of_row == head_of_lane)[None], jnp.concatenate([kb] * nh, axis=2), 0.0)
    qk_lhs = jnp.concatenate([unstack(q), unstack(k)], axis=1).astype(BF16)
    qkk = _bmm_nt(qk_lhs, k_bd)
    qk = qkk[:, :c]
    kk = qkk[:, c:]

    sr = lax.broadcasted_iota(jnp.int32, (hc, hc), 0)
    sl = lax.broadcasted_iota(jnp.int32, (hc, hc), 1)
    shift = c.bit_length() - 1
    same_head = ((sr >> shift) == (sl >> shift))[None]

    def bd(m):
        return jnp.where(same_head, jnp.concatenate([m] * nh, axis=1), 0.0)

    lm = jnp.where(strict, cat_from_cols(b_cols) * kk * decay, 0.0)
    tinv = jnp.where((ri == cj)[None], 1.0, 0.0) - jnp.where(((ri >> 1) == (cj >> 1))[None], lm, 0.0)
    s = 2
    while s < c:
        sh = s.bit_length()
        lower_left = ((ri >> sh) == (cj >> sh)) & ((ri & s) != 0) & ((cj & s) == 0)
        side_work()
        a_off = jnp.where(lower_left[None], lm, 0.0).astype(BF16)
        xm = _bmm(tinv.astype(BF16), bd(a_off))
        tinv = tinv - _bmm(xm.astype(BF16), bd(tinv.astype(BF16)))
        s *= 2

    e_g = jnp.exp(g_col)
    rhs = jnp.concatenate([v * b_col, k * (b_col * e_g)], axis=2).astype(BF16)
    sol = _bmm(bd(tinv.astype(BF16)), rhs)
    sol_v = sol[:, :, :dh]
    sol_k = sol[:, :, dh:]
    qe = q * e_g
    w_qe = jnp.concatenate([part[:, h * c:(h + 1) * c] for h in range(nh) for part in (sol_k, qe)],
                           axis=1).astype(BF16)
    kd = (k * jnp.exp(g_last - g_col)).astype(BF16)
    qkd_bd = bd((qk * decay).astype(BF16))
    s_decay = jnp.exp(g_last)
    side_work(len(side))
    return sol_v, w_qe, kd, qkd_bd, s_decay, lambda: stack(z_ref[...].reshape(npb, c, DN_WIDTH))


def _dn_phase_b(stash, ng_ref, o_ref, state_ref, *, nb, tt, c):
    sol_v, w_qe, kd, qkd_bd, s_decay, zs = stash
    nh = DN_HEADS
    dh = DN_HEAD_DIM
    nck = tt // c
    outs = {}
    pending = {}

    def outputs(ck):
        def run():
            prods = {}
            for b in range(nb):
                p = b * nck + ck
                states = [state_ref[b, h] for h in range(nh)]
                prods[b] = [_dot(w_qe[p, h * 2 * c:(h + 1) * 2 * c, :], states[h].astype(BF16)) for h in range(nh)]
                pending[p] = states
            for b in range(nb):
                p = b * nck + ck
                u = sol_v[p] - jnp.concatenate([r[:c] for r in prods[b]], axis=0)
                ub = u.astype(BF16)
                outs[p] = jnp.concatenate([r[c:] for r in prods[b]], axis=0) + _dot(qkd_bd[p], ub)
                pending[p] = (pending[p], ub)
        return run

    def update(ck):
        def run():
            for b in range(nb):
                p = b * nck + ck
                states, ub = pending.pop(p)
                for h in range(nh):
                    r0 = h * c
                    state_ref[b, h] = (states[h] * s_decay[p, r0:r0 + 1, :]
                                       + _dot_tn(kd[p, r0:r0 + c, :], ub[r0:r0 + c, :]))
        return run

    def finish():
        o = jnp.stack([outs[p] for p in range(nb * nck)], axis=0)
        z = zs()
        o = _rms(o, ng_ref[...]) * (z * jax.nn.sigmoid(z))
        o_ref[...] = _unstack_heads(o, c).reshape(nb, tt, DN_WIDTH).astype(BF16)

    return [f(ck) for ck in range(nck) for f in (outputs, update)] + [finish]


def _deltanet(qkv, z, ba, cache8, s0, cw, gp, ng, nb, tt, c):
    b, l, _ = qkv.shape
    seq = lambda w_: pl.BlockSpec((nb, tt, w_), lambda i, j: (i, j, 0))
    per_b = lambda *s: pl.BlockSpec((nb,) + s, lambda i, j: (i,) + (0,) * len(s))
    return pl.pallas_call(
        functools.partial(_dn_kernel, nb=nb, tt=tt, c=c),
        grid=(b // nb, l // tt),
        in_specs=[seq(QKV_WIDTH), seq(DN_WIDTH), seq(LANES), per_b(SUBLANES, QKV_WIDTH),
                  per_b(DN_HEADS, DN_HEAD_DIM, DN_HEAD_DIM),
                  _const_spec((SUBLANES, QKV_WIDTH)), _const_spec((SUBLANES, LANES)),
                  _const_spec((1, DN_HEAD_DIM))],
        out_specs=(seq(DN_WIDTH), per_b(DN_HEADS, DN_HEAD_DIM, DN_HEAD_DIM)),
        out_shape=(jax.ShapeDtypeStruct((b, l, DN_WIDTH), BF16),
                   jax.ShapeDtypeStruct((b, DN_HEADS, DN_HEAD_DIM, DN_HEAD_DIM), F32)),
        scratch_shapes=[pltpu.VMEM((nb, SUBLANES, QKV_WIDTH), F32)],
        compiler_params=_params(2),
        name="deltanet",
    )(qkv, z, ba, cache8, s0, cw, gp, ng)


def _mixer_kernel(x_ref, g_ref, wm_ref, wt_ref, cache_ref, s0_ref, cw_ref, gp_ref, ng_ref,
                  u_ref, o_ref, state_ref, tail_ref, p_cur, p_nxt, halo_s, *, nb, tt, c):
    i = pl.program_id(0)
    o1 = QKV_WIDTH
    o2 = o1 + DN_WIDTH
    o3 = o2 + S5_WIDTH

    @pl.when(i == 0)
    def _():
        p_cur[...] = jnp.zeros(p_cur.shape, F32)

    _dn_restart(i <= 1, cache_ref, s0_ref, halo_s, state_ref)

    h = _rms(x_ref[...].reshape(nb * tt, D_MODEL), g_ref[...]).astype(BF16)

    def project(lo, hi):
        def run():
            res = _dot(h, _w_cols(wm_ref, wt_ref, lo, hi)).reshape(nb, tt, hi - lo)
            if o2 <= lo < o3:
                u_ref[:, :, lo - o2:hi - o2] = res
            else:
                p_nxt[:, :, lo:hi] = res
        return run

    side = [project(lo, min(lo + MXU_DIM, IN_PAD)) for lo in range(0, IN_PAD, MXU_DIM)]
    stash = _dn_phase_a(p_cur.at[:, :, 0:o1], p_cur.at[:, :, o1:o2], p_cur.at[:, :, o3:IN_PAD],
                        cw_ref, gp_ref, halo_s, nb=nb, tt=tt, c=c, side=side)
    for step in _dn_phase_b(stash, ng_ref, o_ref, state_ref, nb=nb, tt=tt, c=c):
        step()
    tail_ref[...] = halo_s[...]
    for lo, hi in ((0, o1), (o1, o2), (o3, IN_PAD)):
        p_cur[:, :, lo:hi] = p_nxt[:, :, lo:hi]


def _mixer(x, g, wm, wt, cache8, s0, cw, gp, ng, tt, c):
    nb, l, _ = x.shape
    nt = l // tt
    cur = lambda w_: pl.BlockSpec((nb, tt, w_), lambda i: (0, jnp.minimum(i, nt - 1), 0))
    prev = lambda w_: pl.BlockSpec((nb, tt, w_), lambda i: (0, jnp.maximum(i - 1, 0), 0))
    return pl.pallas_call(
        functools.partial(_mixer_kernel, nb=nb, tt=tt, c=c),
        grid=(nt + 1,),
        in_specs=[cur(D_MODEL), _const_spec((1, D_MODEL)), _const_spec((D_MODEL, IN_MAIN)),
                  _const_spec((D_MODEL, IN_PAD - IN_MAIN)), _const_spec((nb, SUBLANES, QKV_WIDTH)), _const_spec((nb, DN_HEADS, DN_HEAD_DIM, DN_HEAD_DIM)),
                  _const_spec((SUBLANES, QKV_WIDTH)), _const_spec((SUBLANES, LANES)),
                  _const_spec((1, DN_HEAD_DIM))],
        out_specs=(cur(S5_WIDTH), prev(DN_WIDTH),
                   pl.BlockSpec((nb, DN_HEADS, DN_HEAD_DIM, DN_HEAD_DIM), lambda i: (0, 0, 0, 0)),
                   pl.BlockSpec((nb, SUBLANES, QKV_WIDTH), lambda i: (0, 0, 0))),
        out_shape=(jax.ShapeDtypeStruct((nb, l, S5_WIDTH), F32),
                   jax.ShapeDtypeStruct((nb, l, DN_WIDTH), BF16),
                   jax.ShapeDtypeStruct((nb, DN_HEADS, DN_HEAD_DIM, DN_HEAD_DIM), F32),
                   jax.ShapeDtypeStruct((nb, SUBLANES, QKV_WIDTH), F32)),
        scratch_shapes=[pltpu.VMEM((nb, tt, IN_PAD), F32), pltpu.VMEM((nb, tt, IN_PAD), F32),
                        pltpu.VMEM((nb, SUBLANES, QKV_WIDTH), F32)],
        compiler_params=_params(1),
        name="mixer",
    )(x, g, wm, wt, cache8, s0, cw, gp, ng)


def _s5c_kernel(u_ref, x0re_ref, x0im_ref, pq_ref, qq_ref, mq_ref, tab_ref, dv_ref, gw_ref, gb_ref, ng_ref,
                o_ref, fre_ref, fim_ref, xr_s, xi_s, car_s, u_s, o_s, *, rows, seg):
    carry = seg == SUBLANES
    if carry:
        @pl.when(pl.program_id(1) == 0)
        def _():
            car_s[0:1, :] = x0re_ref[0]
            car_s[1:2, :] = x0im_ref[0]

    ncol = S5_WIDTH // LANES
    for k in range(ncol):
        u_s[k] = u_ref[0, :, k * LANES:(k + 1) * LANES]
    us = [jnp.concatenate([u_s[k, pl.ds(s, rows, stride=S5_SLOTS), :] for k in range(ncol)], axis=1)
          for s in range(S5_SLOTS)]
    ubs = [a.astype(BF16) for a in us]
    blk = 4 * S5_GROUP
    ql = QUAD_LANES
    uq =[jnp.concatenate([ubs[s][:, n * blk:(n + 1) * blk] for s in range(S5_SLOTS)], axis=1)
          for n in range(S5_QUADS)]
    for n in range(S5_QUADS):
        inc = _dot(uq[n], pq_ref[n])
        xr_s[:, n * ql:(n + 1) * ql] = inc[:, :ql]
        xi_s[:, n * ql:(n + 1) * ql] = inc[:, ql:]

    seg_row = lax.broadcasted_iota(jnp.int32, (SUBLANES, S5_LANES), 0) & (seg - 1)

    def block_body(rb, cin):
        r0 = pl.multiple_of(rb * SUBLANES, SUBLANES)
        if carry:
            c_re, c_im = cin
        else:
            c_re = x0re_ref[0, pl.ds(r0, SUBLANES), :]
            c_im = x0im_ref[0, pl.ds(r0, SUBLANES), :]
        xr = xr_s[pl.ds(r0, SUBLANES), :]
        xi = xi_s[pl.ds(r0, SUBLANES), :]
        for lvl, d in enumerate(_scan_levels(seg)):
            m_re = tab_ref[16 + 16 * lvl:24 + 16 * lvl, :]
            m_im = tab_ref[24 + 16 * lvl:32 + 16 * lvl, :]
            sr = pltpu.roll(xr, d, axis=0)
            si = pltpu.roll(xi, d, axis=0)
            xr, xi = xr + (m_re * sr - m_im * si), xi + (m_re * si + m_im * sr)
        p_re = tab_ref[0:8, :]
        p_im = tab_ref[8:16, :]
        xr, xi = xr + (p_re * c_re - p_im * c_im), xi + (p_re * c_im + p_im * c_re)
        xr_s[pl.ds(r0, SUBLANES), :] = jnp.where(seg_row == 0, c_re, pltpu.roll(xr, 1, axis=0))
        xi_s[pl.ds(r0, SUBLANES), :] = jnp.where(seg_row == 0, c_im, pltpu.roll(xi, 1, axis=0))
        if carry:
            return xr[SUBLANES - 1:, :], xi[SUBLANES - 1:, :]
        fre_ref[0, pl.ds(r0, SUBLANES), :] = xr
        fim_ref[0, pl.ds(r0, SUBLANES), :] = xi
        return cin

    if carry:
        c_re, c_im = lax.fori_loop(0, rows // SUBLANES, block_body, (car_s[0:1, :], car_s[1:2, :]))
        car_s[0:1, :] = c_re
        car_s[1:2, :] = c_im
        fre_ref[0] = c_re
        fim_ref[0] = c_im
    else:
        lax.fori_loop(0, rows // SUBLANES, block_body, 0)

    ys = []
    for n in range(S5_QUADS):
        xs = jnp.concatenate([xr_s[:, n * ql:(n + 1) * ql], xi_s[:, n * ql:(n + 1) * ql]], axis=1).astype(BF16)
        half = QUAD_IN // 2
        intra = jnp.concatenate([_dot(uq[n][:, :half], mq_ref[n, :half, :half]), _dot(uq[n], mq_ref[n, :, half:])],
                                axis=1)
        ys.append(intra + _dot(xs, qq_ref[n]))
    for s in range(S5_SLOTS):
        y = jnp.concatenate([ys[n][:, s * blk:(s + 1) * blk] for n in range(S5_QUADS)], axis=1)
        y = jax.nn.gelu(y + dv_ref[...] * us[s]).astype(BF16)
        gl = _dot(y, gw_ref[...]) + gb_ref[...]
        o = gl[:, :S5_WIDTH] * jax.nn.sigmoid(gl[:, S5_WIDTH:])
        o = _rms(o, ng_ref[...])
        for k in range(ncol):
            o_s[k, pl.ds(s, rows, stride=S5_SLOTS), :] = o[:, k * LANES:(k + 1) * LANES]
    o_ref[0] = jnp.concatenate([o_s[k] for k in range(ncol)], axis=1).astype(BF16)


def _s5_chunked(u, x0re, x0im, pq, qq, mq, tabc, dv, gw, gb, ng, rows, seg):
    b, l, _ = u.shape
    tt = rows * S5_SLOTS
    seq = pl.BlockSpec((1, tt, S5_WIDTH), lambda i, j: (i, j, 0))
    if seg == SUBLANES:
        st = pl.BlockSpec((1, 1, S5_LANES), lambda i, j: (i, 0, 0))
    else:
        st = pl.BlockSpec((1, rows, S5_LANES), lambda i, j: (i, j, 0))
    return pl.pallas_call(
        functools.partial(_s5c_kernel, rows=rows, seg=seg),
        grid=(b, l // tt),
        in_specs=[seq, st, st,
                  _const_spec((S5_QUADS, QUAD_IN, 2 * QUAD_LANES)),
                  _const_spec((S5_QUADS, 2 * QUAD_LANES, QUAD_IN)),
                  _const_spec((S5_QUADS, QUAD_IN, QUAD_IN)), _const_spec((64, S5_LANES)),
                  _const_spec((1, S5_WIDTH)), _const_spec((S5_WIDTH, 2 * S5_WIDTH)),
                  _const_spec((1, 2 * S5_WIDTH)), _const_spec((1, S5_WIDTH))],
        out_specs=(seq, st, st),
        out_shape=(jax.ShapeDtypeStruct((b, l, S5_WIDTH), BF16),
                   jax.ShapeDtypeStruct(x0re.shape, F32),
                   jax.ShapeDtypeStruct(x0re.shape, F32)),
        scratch_shapes=[pltpu.VMEM((rows, S5_LANES), F32), pltpu.VMEM((rows, S5_LANES), F32),
                        pltpu.VMEM((SUBLANES, S5_LANES), F32),
                        pltpu.VMEM((S5_WIDTH // LANES, tt, LANES), F32),
                        pltpu.VMEM((S5_WIDTH // LANES, tt, LANES), F32)],
        compiler_params=_params(2),
        name="s5_chunked",
    )(u, x0re, x0im, pq, qq, mq, tabc, dv, gw, gb, ng)


def _ffn_kernel(x_ref, odn_ref, os5_ref, prev_ref, woa_ref, wob_ref, n2_ref, wu_ref,
                fcw_ref, fcb_ref, wd_ref, fg_ref, y_ref, tail_ref, halo_s, *, tm, ls, carry):
    if carry:
        @pl.when(pl.program_id(1) == 0)
        def _():
            halo_s[...] = prev_ref[0]

    x1 = x_ref[0] + _dot(odn_ref[0], woa_ref[...]) + _dot(os5_ref[0], wob_ref[...])
    h2 = _rms(x1, n2_ref[...]).astype(BF16)
    down = None
    for lo in range(0, D_FF, FF_CHUNK):
        hi = min(lo + FF_CHUNK, D_FF)
        gate = _dot(h2, wu_ref[:, lo:hi])
        val = _dot(h2, wu_ref[:, D_FF + lo:D_FF + hi])
        if carry:
            xx = jnp.concatenate([halo_s[:, lo:hi], gate], axis=0)
            g2 = xx[SUBLANES - 2:SUBLANES - 2 + tm]
            g1 = xx[SUBLANES - 1:SUBLANES - 1 + tm]
            halo_s[:, lo:hi] = gate[tm - SUBLANES:, :]
            tail_ref[0, :, lo:hi] = gate[tm - SUBLANES:, :]
        else:
            r = lax.broadcasted_iota(jnp.int32, (tm, hi - lo), 0) & (ls - 1)
            prev = prev_ref[0, :, lo:hi]
            g1 = jnp.where(r < 1, pltpu.roll(prev, tm - 1, axis=0), pltpu.roll(gate, 1, axis=0))
            g2 = jnp.where(r < 2, prev, pltpu.roll(gate, 2, axis=0))
            tail_ref[0, :, lo:hi] = gate
        cw = fcw_ref[:, lo:hi]
        pre = g2 * cw[0:1] + g1 * cw[1:2] + gate * cw[2:3] + fcb_ref[:, lo:hi]
        act = pre * jax.nn.sigmoid(pre) * val
        part = _dot(act.astype(BF16), wd_ref[lo:hi, :])
        down = part if down is None else down + part
    y_ref[0] = _rms(x1 + down, fg_ref[...])


def _ffn(x, odn, os5, prev, woa, wob, n2, wu, fcw, fcb, wd, fg, tm, ls, carry):
    b, l, _ = x.shape
    seq = lambda w_: pl.BlockSpec((1, tm, w_), lambda i, j: (i, j, 0))
    if carry:
        prev_spec = pl.BlockSpec((1, SUBLANES, D_FF), lambda i, j: (i, 0, 0))
        tail_spec = pl.BlockSpec((1, SUBLANES, D_FF), lambda i, j: (i, 0, 0))
        tail_shape = jax.ShapeDtypeStruct((b, SUBLANES, D_FF), F32)
    else:
        prev_spec = seq(D_FF)
        tail_spec = seq(D_FF)
        tail_shape = jax.ShapeDtypeStruct((b, l, D_FF), F32)
    return pl.pallas_call(
        functools.partial(_ffn_kernel, tm=tm, ls=ls, carry=carry),
        grid=(b, l // tm),
        in_specs=[seq(D_MODEL), seq(DN_WIDTH), seq(S5_WIDTH), prev_spec,
                  _const_spec((DN_WIDTH, D_MODEL)), _const_spec((S5_WIDTH, D_MODEL)),
                  _const_spec((1, D_MODEL)), _const_spec((D_MODEL, 2 * D_FF)),
                  _const_spec((SUBLANES, D_FF)), _const_spec((1, D_FF)), _const_spec((D_FF, D_MODEL)),
                  _const_spec((1, D_MODEL))],
        out_specs=(seq(D_MODEL), tail_spec),
        out_shape=(jax.ShapeDtypeStruct((b, l, D_MODEL), F32), tail_shape),
        scratch_shapes=[pltpu.VMEM((SUBLANES, D_FF), F32)],
        compiler_params=_params(2),
        name="ffn",
    )(x, odn, os5, prev, woa, wob, n2, wu, fcw, fcb, wd, fg)


def _pad_rows_top(a, rows):
    return jnp.pad(a, ((0, 0), (rows - a.shape[1], 0), (0, 0)))


def _quad_blocks(m):
    n = m.shape[0]
    same = jnp.eye(4, dtype=bool)[:, None, :, None]
    blocks = jnp.where(same, m.reshape(n, S5_QUADS, 4, S5_GROUP, 1, S5_STATE), 0.0)
    return blocks.reshape(n, S5_QUADS, 4 * S5_GROUP, QUAD_LANES)


def _trunk(x, conv_dn, s_dn, s5_re, s5_im, conv_ffn, w, prompt):
    b, l, _ = x.shape
    n = b * l
    c = CHUNK if l % CHUNK == 0 else l
    assert l >= 2 * SUBLANES, "sequence shorter than two row tiles"
    cache8 = _pad_rows_top(conv_dn, SUBLANES)
    if prompt:
        u, o_dn, s_dn_new, tail = _mixer(x, w['n1'], w['w_in_main'], w['w_in_tail'], cache8, s_dn,
                                         w['dn_cw'], w['dn_gp'], w['dn_g'],
                                         tt=min(l, DN_ROWS_LONG), c=c)
        conv_dn_new = tail[:, SUBLANES - (DN_CONV - 1):]
    else:
        qkv, z, u, ba = _in_proj(x.reshape(n, D_MODEL), w['n1'], w['w_in_main'], w['w_in_tail'], min(n, INPROJ_ROWS))
        qkv = qkv.reshape(b, l, QKV_WIDTH)
        o_dn, s_dn_new = _deltanet(qkv, z.reshape(b, l, DN_WIDTH), ba.reshape(b, l, LANES), cache8, s_dn,
                                   w['dn_cw'], w['dn_gp'], w['dn_g'], nb=min(b, DN_SEQS_SHORT), tt=l, c=c)
        conv_dn_new = qkv[:, l - (DN_CONV - 1):]

    s5_args = (w['pq'], w['qq'], w['mq'], w['tab_long' if prompt else 'tab_short'], w['dv'], w['glu_w'],
               w['glu_b'], w['s5_g'])
    if prompt:
        x0re = s5_re.reshape(b, 1, S5_LANES)
        x0im = s5_im.reshape(b, 1, S5_LANES)
        o_s5, fre, fim = _s5_chunked(u.reshape(b, l, S5_WIDTH), x0re, x0im, *s5_args,
                                     rows=min(l // S5_SLOTS, S5_CHUNKS_LONG), seg=SUBLANES)
    else:
        seg = l // S5_SLOTS
        rep = lambda s: jnp.repeat(s.reshape(b, S5_LANES), seg, axis=0).reshape(1, b * seg, S5_LANES)
        o_s5, fre, fim = _s5_chunked(u.reshape(1, n, S5_WIDTH), rep(s5_re), rep(s5_im), *s5_args,
                                     rows=b * seg, seg=seg)
        o_s5 = o_s5.reshape(b, l, S5_WIDTH)
        fre = fre.reshape(b, seg, S5_LANES)[:, seg - 1]
        fim = fim.reshape(b, seg, S5_LANES)[:, seg - 1]
    s5_re_new = fre.reshape(b, S5_GROUPS, S5_STATE)
    s5_im_new = fim.reshape(b, S5_GROUPS, S5_STATE)

    ffn_args = (w['w_out_a'], w['w_out_b'], w['n2'], w['w_up'], w['fcw'], w['fcb'],
                w['w_down'], w['fg'])
    if prompt:
        prev = _pad_rows_top(conv_ffn, SUBLANES)
        tm = min(l, FFN_ROWS_LONG)
        y, tail = _ffn(x, o_dn, o_s5, prev, *ffn_args, tm=tm, ls=tm, carry=True)
        conv_ffn_new = tail[:, SUBLANES - (FFN_CONV - 1):]
    else:
        prev = jnp.pad(conv_ffn, ((0, 0), (0, l - (FFN_CONV - 1)), (0, 0))).reshape(1, n, D_FF)
        y, tail = _ffn(x.reshape(1, n, D_MODEL), o_dn.reshape(1, n, DN_WIDTH), o_s5.reshape(1, n, S5_WIDTH),
                       prev, *ffn_args, tm=min(n, FFN_ROWS_SHORT), ls=l, carry=False)
        y = y.reshape(b, l, D_MODEL)
        conv_ffn_new = tail.reshape(b, l, D_FF)[:, l - (FFN_CONV - 1):]
    return y, (conv_dn_new[None], s_dn_new[None], s5_re_new[None], s5_im_new[None], conv_ffn_new[None])


def kernel(x_prompt, x_sample, cache_dn_conv, state_dn, state_s5_re, state_s5_im, cache_ffn_conv, norm1_g, w_in, dn_conv_w, dn_A_log, dn_dt_bias, dn_norm_g, s5_A_re, s5_A_im, s5_log_dt, s5_B_re, s5_B_im, s5_C_re, s5_C_im, s5_D, s5_glu_w, s5_glu_b, s5_norm_g, w_out, norm2_g, w_up, ffn_conv_w, ffn_conv_b, w_down, final_norm_g):
    assert w_in.shape[0] == 1, "single-layer trunk"
    o1 = QKV_WIDTH
    o2 = o1 + DN_WIDTH
    o4 = o2 + 2 * DN_HEADS
    wi = w_in[0]
    w_in_main = wi[:, :o2].astype(BF16)
    w_in_tail = jnp.concatenate(
        [wi[:, o4:], wi[:, o2:o4], jnp.zeros((D_MODEL, LANES - 2 * DN_HEADS), wi.dtype)], axis=1).astype(BF16)
    lane_pad = lambda v: jnp.pad(v, (BA_LANE_G, LANES - BA_LANE_G - DN_HEADS))
    seg_short = x_sample.shape[1] // S5_SLOTS
    assert x_sample.shape[1] % S5_SLOTS == 0 and seg_short in (1, 2, 4), "sample sequences of 8, 16 or 32 rows"
    a3 = jnp.concatenate([s5_A_re, s5_A_im, jnp.broadcast_to(s5_log_dt[:, :, None], s5_A_re.shape)], axis=0)
    a_lanes = a3.reshape(3, 1, S5_LANES)
    tab_long, tab_short = [_s5_prep(a_lanes, seg=s) for s in (SUBLANES, seg_short)]
    bc = jnp.concatenate([s5_B_re.transpose(0, 1, 3, 2), s5_B_im.transpose(0, 1, 3, 2), s5_C_re, s5_C_im], axis=0)
    pq, qq, mq = _s5_prep_quads(a3.reshape(3, S5_QUADS, 1, QUAD_LANES), _quad_blocks(bc))
    w = {
        'n1': norm1_g, 'w_in_main': w_in_main, 'w_in_tail': w_in_tail,
        'dn_cw': jnp.pad(dn_conv_w[0], ((0, SUBLANES - DN_CONV), (0, 0))),
        'dn_gp': jnp.pad(jnp.stack([lane_pad(dn_A_log[0]), lane_pad(dn_dt_bias[0])]), ((0, SUBLANES - 2), (0, 0))),
        'dn_g': dn_norm_g,
        'tab_long': tab_long, 'tab_short': tab_short, 'pq': pq, 'qq': qq, 'mq': mq,
        'dv': s5_D, 'glu_w': s5_glu_w[0].astype(BF16), 'glu_b': s5_glu_b, 's5_g': s5_norm_g,
        'w_out_a': w_out[0, :DN_WIDTH].astype(BF16), 'w_out_b': w_out[0, DN_WIDTH:].astype(BF16),
        'n2': norm2_g, 'w_up': w_up[0].astype(BF16),
        'fcw': jnp.pad(ffn_conv_w[0], ((0, SUBLANES - FFN_CONV), (0, 0))), 'fcb': ffn_conv_b,
        'w_down': w_down[0].astype(BF16), 'fg': final_norm_g.reshape(1, D_MODEL),
    }
    bp = x_prompt.shape[0]
    zeros = lambda *s: jnp.zeros(s, F32)
    y_p, st_p = _trunk(x_prompt, zeros(bp, DN_CONV - 1, QKV_WIDTH), zeros(bp, DN_HEADS, DN_HEAD_DIM, DN_HEAD_DIM),
                       zeros(bp, S5_GROUPS, S5_STATE), zeros(bp, S5_GROUPS, S5_STATE),
                       zeros(bp, FFN_CONV - 1, D_FF), w, prompt=True)
    y_s, st_s = _trunk(x_sample, cache_dn_conv[0], state_dn[0], state_s5_re[0], state_s5_im[0],
                       cache_ffn_conv[0], w, prompt=False)
    return (y_p, y_s) + st_p + st_s
```

```python
import functools

import jax
import jax.numpy as jnp
from jax import lax
from jax.experimental import pallas as pl
from jax.experimental.pallas import tpu as pltpu

F32 = jnp.float32
BF16 = jnp.bfloat16
EPS = 1e-6

D_MODEL = 1024
DN_HEADS = 4
DN_HEAD_DIM = 128
DN_WIDTH = DN_HEADS * DN_HEAD_DIM
DN_CONV = 4
QKV_WIDTH = 3 * DN_WIDTH
S5_WIDTH = D_MODEL - DN_WIDTH
S5_GROUP = 16
S5_GROUPS = S5_WIDTH // S5_GROUP
S5_STATE = 64
S5_LANES = S5_GROUPS * S5_STATE
S5_SLOTS = 8
S5_QUADS = S5_GROUPS // 4
QUAD_LANES = 4 * S5_STATE
QUAD_IN = S5_SLOTS * 4 * S5_GROUP
D_FF = 2816
FFN_CONV = 3
CHUNK = 64

SUBLANES = 8
LANES = 128
BA_LANE_G = DN_HEADS
IN_MAIN = QKV_WIDTH + DN_WIDTH
IN_PAD = IN_MAIN + S5_WIDTH + LANES
MXU_DIM = 256
FF_CHUNK = 6 * MXU_DIM

DN_ROWS_LONG = 2 * CHUNK
DN_SEQS_SHORT = 16
S5_CHUNKS_LONG = 256
FFN_ROWS_LONG = 512
FFN_ROWS_SHORT = 256
INPROJ_ROWS = 512

VMEM_LIMIT = 56 * 1024 * 1024


def _dot(a, b):
    return jnp.dot(a, b, preferred_element_type=F32)


def _dot_tn(a, b):
    return lax.dot_general(a, b, (((0,), (0,)), ((), ())), preferred_element_type=F32)


def _split_bf16(a):
    hi = a.astype(BF16)
    lo = (a - hi.astype(F32)).astype(BF16)
    return hi, lo


def _bmm(a, b):
    return lax.dot_general(a, b, (((2,), (1,)), ((0,), (0,))), preferred_element_type=F32)


def _bmm_nt(a, b):
    return lax.dot_general(a, b, (((2,), (2,)), ((0,), (0,))), preferred_element_type=F32)


def _bmm_nt3(a, b):
    ah, al = _split_bf16(a)
    bh, bl = _split_bf16(b)
    return _bmm_nt(ah, bh) + (_bmm_nt(al, bh) + _bmm_nt(ah, bl))


def _rms(x, g):
    return x * lax.rsqrt(jnp.mean(x * x, axis=-1, keepdims=True) + EPS) * g


def _const_spec(shape):
    nd = len(shape)
    return pl.BlockSpec(shape, lambda *_: (0,) * nd, pipeline_mode=pl.Buffered(1))


def _params(n_axes):
    return pltpu.CompilerParams(dimension_semantics=("arbitrary",) * n_axes,
                                vmem_limit_bytes=VMEM_LIMIT)


def _cmul(a, b):
    return a[0] * b[0] - a[1] * b[1], a[0] * b[1] + a[1] * b[0]


def _zoh(are, aim, ldt):
    dt = jnp.exp(ldt)
    mag = jnp.exp(are * dt)
    ang = aim * dt
    lr = mag * jnp.cos(ang)
    li = mag * jnp.sin(ang)
    den = are * are + aim * aim
    f_re = ((lr - 1.0) * are + li * aim) / den
    f_im = (li * are - (lr - 1.0) * aim) / den
    return (lr, li), (f_re, f_im)


def _write_scan_tables(tab_ref, step, seg):
    pw = [step]
    for _ in range(seg - 1):
        pw.append(_cmul(pw[-1], step))
    shape = (SUBLANES, step[0].shape[-1])
    row = lax.broadcasted_iota(jnp.int32, shape, 0) & (seg - 1)
    zero = jnp.zeros(shape, F32)
    pre, pim = zero, zero
    for r in range(seg):
        pre = jnp.where(row == r, pw[r][0], pre)
        pim = jnp.where(row == r, pw[r][1], pim)
    tab_ref[...] = jnp.zeros(tab_ref.shape, F32)
    tab_ref[0:8, :] = pre
    tab_ref[8:16, :] = pim
    for lvl, d in enumerate(_scan_levels(seg)):
        tab_ref[16 + 16 * lvl:24 + 16 * lvl, :] = jnp.where(row >= d, pw[d - 1][0], zero)
        tab_ref[24 + 16 * lvl:32 + 16 * lvl, :] = jnp.where(row >= d, pw[d - 1][1], zero)


def _scan_levels(seg):
    return [d for d in (1, 2, 4) if d < seg]


def _s5prep_kernel(a_ref, tab_ref, *, seg):
    lam, _ = _zoh(a_ref[0], a_ref[1], a_ref[2])
    lam_c = lam
    for _ in range(S5_SLOTS - 1):
        lam_c = _cmul(lam_c, lam)
    _write_scan_tables(tab_ref, lam_c, seg)


def _s5_prep(a_lanes, seg):
    return pl.pallas_call(
        functools.partial(_s5prep_kernel, seg=seg),
        out_shape=jax.ShapeDtypeStruct((64, S5_LANES), F32),
        compiler_params=pltpu.CompilerParams(vmem_limit_bytes=VMEM_LIMIT),
        name="s5_prep",
    )(a_lanes)


def _s5prepq_kernel(a_ref, bc_ref, pq_ref, qq_ref, mq_ref):
    lam, f = _zoh(a_ref[0], a_ref[1], a_ref[2])
    bb = _cmul(f, (bc_ref[0], bc_ref[1]))
    ct = (bc_ref[2], bc_ref[3])
    one = (jnp.ones_like(lam[0]), jnp.zeros_like(lam[0]))
    pw = [one]
    for _ in range(S5_SLOTS):
        pw.append(_cmul(pw[-1], lam))
    blk = 4 * S5_GROUP
    for s in range(S5_SLOTS):
        p_re, p_im = _cmul(pw[S5_SLOTS - 1 - s], bb)
        pq_ref[:, s * blk:(s + 1) * blk, :QUAD_LANES] = p_re.astype(BF16)
        pq_ref[:, s * blk:(s + 1) * blk, QUAD_LANES:] = p_im.astype(BF16)
    cl = [_cmul(pw[e], ct) for e in range(S5_SLOTS + 1)]
    qt_re = jnp.concatenate([cl[t + 1][0] for t in range(S5_SLOTS)], axis=1)
    qt_im = jnp.concatenate([-cl[t + 1][1] for t in range(S5_SLOTS)], axis=1)
    for n in range(S5_QUADS):
        qq_ref[n, :QUAD_LANES, :] = qt_re[n].T.astype(BF16)
        qq_ref[n, QUAD_LANES:, :] = qt_im[n].T.astype(BF16)
    lag_re = jnp.concatenate([cl[tau][0] for tau in range(S5_SLOTS)], axis=1)
    lag_im = jnp.concatenate([cl[tau][1] for tau in range(S5_SLOTS)], axis=1)
    lags = _bmm_nt3(bb[0], lag_re) - _bmm_nt3(bb[1], lag_im)
    for s in range(S5_SLOTS):
        m = lags if s == 0 else jnp.concatenate(
            [jnp.zeros((S5_QUADS, blk, s * blk), F32), lags[:, :, :QUAD_IN - s * blk]], axis=2)
        mq_ref[:, s * blk:(s + 1) * blk, :] = m.astype(BF16)


def _s5_prep_quads(a_quads, bc_quads):
    w = lambda r, c_: jax.ShapeDtypeStruct((S5_QUADS, r, c_), BF16)
    return pl.pallas_call(
        _s5prepq_kernel,
        out_shape=(w(QUAD_IN, 2 * QUAD_LANES), w(2 * QUAD_LANES, QUAD_IN), w(QUAD_IN, QUAD_IN)),
        compiler_params=pltpu.CompilerParams(vmem_limit_bytes=VMEM_LIMIT),
        name="s5_prep_quads",
    )(a_quads, bc_quads)


def _w_cols(wm_ref, wt_ref, lo, hi):
    return wm_ref[:, lo:hi] if hi <= IN_MAIN else wt_ref[:, lo - IN_MAIN:hi - IN_MAIN]


def _inproj_kernel(x_ref, g_ref, wm_ref, wt_ref, qkv_ref, z_ref, u_ref, ba_ref):
    h = _rms(x_ref[...], g_ref[...]).astype(BF16)
    o1 = QKV_WIDTH
    o2 = o1 + DN_WIDTH
    o3 = o2 + S5_WIDTH
    qkv_ref[...] = _dot(h, _w_cols(wm_ref, wt_ref, 0, o1))
    z_ref[...] = _dot(h, _w_cols(wm_ref, wt_ref, o1, o2))
    u_ref[...] = _dot(h, _w_cols(wm_ref, wt_ref, o2, o3))
    ba_ref[...] = _dot(h, _w_cols(wm_ref, wt_ref, o3, IN_PAD))


def _in_proj(x2d, g, wm, wt, tm):
    n = x2d.shape[0]
    row = lambda w_: pl.BlockSpec((tm, w_), lambda i: (i, 0))
    return pl.pallas_call(
        _inproj_kernel,
        grid=(n // tm,),
        in_specs=[row(D_MODEL), _const_spec((1, D_MODEL)), _const_spec((D_MODEL, IN_MAIN)),
                  _const_spec((D_MODEL, IN_PAD - IN_MAIN))],
        out_specs=(row(QKV_WIDTH), row(DN_WIDTH), row(S5_WIDTH), row(LANES)),
        out_shape=(jax.ShapeDtypeStruct((n, QKV_WIDTH), F32),
                   jax.ShapeDtypeStruct((n, DN_WIDTH), F32),
                   jax.ShapeDtypeStruct((n, S5_WIDTH), F32),
                   jax.ShapeDtypeStruct((n, LANES), F32)),
        compiler_params=_params(1),
        name="in_proj",
    )(x2d, g, wm, wt)


def _stack_heads(a):
    dh = DN_HEAD_DIM
    return jnp.concatenate([a[:, :, h * dh:(h + 1) * dh] for h in range(DN_HEADS)], axis=1)


def _unstack_heads(a, c):
    return jnp.concatenate([a[:, h * c:(h + 1) * c, :] for h in range(DN_HEADS)], axis=2)


def _dn_kernel(qkv_ref, z_ref, ba_ref, cache_ref, s0_ref, cw_ref, gp_ref, ng_ref, o_ref, state_ref, halo_s,
               *, nb, tt, c):
    _dn_restart(pl.program_id(1) == 0, cache_ref, s0_ref, halo_s, state_ref)
    stash = _dn_phase_a(qkv_ref, z_ref, ba_ref, cw_ref, gp_ref, halo_s, nb=nb, tt=tt, c=c)
    for step in _dn_phase_b(stash, ng_ref, o_ref, state_ref, nb=nb, tt=tt, c=c):
        step()


def _dn_restart(first, cache_ref, s0_ref, halo_s, state_ref):
    @pl.when(first)
    def _():
        halo_s[...] = cache_ref[...]
        state_ref[...] = s0_ref[...]


def _dn_phase_a(qkv_ref, z_ref, ba_ref, cw_ref, gp_ref, halo_s, *, nb, tt, c, side=()):
    nh = DN_HEADS
    dh = DN_HEAD_DIM
    nck = tt // c
    npb = nb * nck
    hc = nh * c
    side = list(side)

    def side_work(n=1):
        for _ in range(n):
            if side:
                side.pop(0)()

    stack = _stack_heads
    unstack = functools.partial(_unstack_heads, c=c)

    assert DN_CONV == 4
    piece = DN_WIDTH
    row8 = lax.broadcasted_iota(jnp.int32, (SUBLANES, piece), 0)

    def shift_rows(y, before, d):
        rolled = pltpu.roll(y, d, axis=1)
        top = jnp.where((row8 < d)[None], pltpu.roll(before, d, axis=1), rolled[:, :SUBLANES])
        return jnp.concatenate([top, rolled[:, SUBLANES:]], axis=1)

    conv_parts = []
    for lo in range(0, QKV_WIDTH, piece):
        side_work()
        hi = lo + piece
        cw = cw_ref[:, lo:hi]
        x = qkv_ref[:, :, lo:hi]
        prev = halo_s[:, :, lo:hi]
        x1 = shift_rows(x, prev, 1)
        pair = x * cw[1:2] + x1 * cw[0:1]
        pair_prev = prev * cw[1:2] + pltpu.roll(prev, 1, axis=1) * cw[0:1]
        acc = (x * cw[3:4] + x1 * cw[2:3]) + shift_rows(pair, pair_prev, 2)
        halo_s[:, :, lo:hi] = x[:, tt - SUBLANES:]
        conv_parts.append((acc * jax.nn.sigmoid(acc)).reshape(npb, c, hi - lo))
    act = jnp.concatenate(conv_parts, axis=2)
    q, k, v = [stack(act[:, :, i * DN_WIDTH:(i + 1) * DN_WIDTH]) for i in range(3)]
    side_work()
    q = q * (lax.rsqrt(jnp.sum(q * q, axis=-1, keepdims=True) + EPS) * (dh ** -0.5))
    k = k * lax.rsqrt(jnp.sum(k * k, axis=-1, keepdims=True) + EPS)

    side_work()
    ba = ba_ref[...].reshape(nb * tt, LANES)
    beta = jax.nn.sigmoid(ba).reshape(npb, c, LANES)
    g2 = -jnp.exp(gp_ref[0:1, :]) * jax.nn.softplus(ba + gp_ref[1:2, :])
    row_in_chunk = lax.broadcasted_iota(jnp.int32, (nb * tt, LANES), 0) & (c - 1)
    d = 1
    while d < c:
        g2 = g2 + jnp.where(row_in_chunk >= d, pltpu.roll(g2, d, axis=0), 0.0)
        d *= 2
    g = g2.reshape(npb, c, LANES)

    g_cols = [g[:, :, BA_LANE_G + h:BA_LANE_G + h + 1] for h in range(nh)]
    b_cols = [beta[:, :, h:h + 1] for h in range(nh)]
    g_col = jnp.concatenate(g_cols, axis=1)
    b_col = jnp.concatenate(b_cols, axis=1)
    g_last = jnp.concatenate([jnp.broadcast_to(gc[:, c - 1:c, :], (npb, c, 1)) for gc in g_cols], axis=1)

    hp = min(nh, LANES // c)
    pieces = []
    for h0 in range(0, nh, hp):
        slab = jnp.concatenate(
            [(g2 if hh == 0 else pltpu.roll(g2, LANES - hh, axis=1)).reshape(npb, c, LANES)
             for hh in range(h0, h0 + hp)], axis=1)
        if hp * c < LANES:
            slab = jnp.concatenate([slab, jnp.zeros((npb, LANES - hp * c, LANES), F32)], axis=1)
        rows = [slab[p].T[BA_LANE_G:BA_LANE_G + 1, :hp * c] for p in range(npb)]
        pieces.append(jnp.stack(rows, axis=0))
    g_row = jnp.concatenate(pieces, axis=2)

    ri = lax.broadcasted_iota(jnp.int32, (c, hc), 0)
    lane = lax.broadcasted_iota(jnp.int32, (c, hc), 1)
    cj = lane & (c - 1)
    causal = (ri >= cj)[None]
    strict = (ri > cj)[None]

    def cat_from_cols(cols):
        out = jnp.broadcast_to(cols[nh - 1], (npb, c, hc))
        for h in range(nh - 2, -1, -1):
            out = jnp.where((lane < (h + 1) * c)[None], jnp.broadcast_to(cols[h], (npb, c, hc)), out)
        return out

    decay = jnp.exp(jnp.where(causal, cat_from_cols(g_cols) - g_row, -jnp.inf))

    side_work()
    kb = k.astype(BF16)
    br = lax.broadcasted_iota(jnp.int32, (hc, nh * dh), 0)
    bl = lax.broadcasted_iota(jnp.int32, (hc, nh * dh), 1)
    head_of_row = sum(jnp.where(br >= h * c, 1, 0) for h in range(1, nh))
    head_of_lane = sum(jnp.where(bl >= h * dh, 1, 0) for h in range(1, nh))
    k_bd = jnp.where((head_of_row == head_of_lane)[None], jnp.concatenate([kb] * nh, axis=2), 0.0)
    k_beta = k * b_col
    qk_lhs = jnp.concatenate([unstack(q), unstack(k_beta)], axis=1).astype(BF16)
    qkk = _bmm_nt(qk_lhs, k_bd)
    qk = qkk[:, :c]
    kk_beta = qkk[:, c:]

    sr = lax.broadcasted_iota(jnp.int32, (hc, hc), 0)
    sl = lax.broadcasted_iota(jnp.int32, (hc, hc), 1)
    shift = c.bit_length() - 1
    same_head = ((sr >> shift) == (sl >> shift))[None]

    def bd(m):
        return jnp.where(same_head, jnp.concatenate([m] * nh, axis=1), 0.0)

    lm = jnp.where(strict, kk_beta * decay, 0.0)
    tinv = jnp.where((ri == cj)[None], 1.0, 0.0) - jnp.where(((ri >> 1) == (cj >> 1))[None], lm, 0.0)
    s = 2
    while s < c:
        sh = s.bit_length()
        lower_left = ((ri >> sh) == (cj >> sh)) & ((ri & s) != 0) & ((cj & s) == 0)
        side_work()
        a_off = jnp.where(lower_left[None], lm, 0.0).astype(BF16)
        xm = _bmm(tinv.astype(BF16), bd(a_off))
        tinv = tinv - _bmm(xm.astype(BF16), bd(tinv.astype(BF16)))
        s *= 2

    e_g = jnp.exp(g_col)
    rhs = jnp.concatenate([v * b_col, k_beta * e_g], axis=2).astype(BF16)
    sol = _bmm(bd(tinv.astype(BF16)), rhs)
    sol_v = sol[:, :, :dh]
    sol_k = sol[:, :, dh:]
    qe = q * e_g
    w_qe = jnp.concatenate([part[:, h * c:(h + 1) * c] for h in range(nh) for part in (sol_k, qe)],
                           axis=1).astype(BF16)
    kd = (k * jnp.exp(g_last - g_col)).astype(BF16)
    qkd_bd = bd((qk * decay).astype(BF16))
    s_decay = jnp.exp(g_last)
    side_work(len(side))
    return sol_v, w_qe, kd, qkd_bd, s_decay, lambda: stack(z_ref[...].reshape(npb, c, DN_WIDTH))


def _dn_phase_b(stash, ng_ref, o_ref, state_ref, *, nb, tt, c):
    sol_v, w_qe, kd, qkd_bd, s_decay, zs = stash
    nh = DN_HEADS
    dh = DN_HEAD_DIM
    nck = tt // c
    outs = {}
    pending = {}

    def outputs(ck):
        def run():
            prods = {}
            for b in range(nb):
                p = b * nck + ck
                states = [state_ref[b, h] for h in range(nh)]
                prods[b] = [_dot(w_qe[p, h * 2 * c:(h + 1) * 2 * c, :], states[h].astype(BF16)) for h in range(nh)]
                pending[p] = states
            for b in range(nb):
                p = b * nck + ck
                u = sol_v[p] - jnp.concatenate([r[:c] for r in prods[b]], axis=0)
                ub = u.astype(BF16)
                outs[p] = jnp.concatenate([r[c:] for r in prods[b]], axis=0) + _dot(qkd_bd[p], ub)
                pending[p] = (pending[p], ub)
        return run

    def update(ck):
        def run():
            for b in range(nb):
                p = b * nck + ck
                states, ub = pending.pop(p)
                for h in range(nh):
                    r0 = h * c
                    state_ref[b, h] = (states[h] * s_decay[p, r0:r0 + 1, :]
                                       + _dot_tn(kd[p, r0:r0 + c, :], ub[r0:r0 + c, :]))
        return run

    def finish():
        o = jnp.stack([outs[p] for p in range(nb * nck)], axis=0)
        z = zs()
        o = _rms(o, ng_ref[...]) * (z * jax.nn.sigmoid(z))
        o_ref[...] = _unstack_heads(o, c).reshape(nb, tt, DN_WIDTH).astype(BF16)

    return [f(ck) for ck in range(nck) for f in (outputs, update)] + [finish]


def _deltanet(qkv, z, ba, cache8, s0, cw, gp, ng, nb, tt, c):
    b, l, _ = qkv.shape
    seq = lambda w_: pl.BlockSpec((nb, tt, w_), lambda i, j: (i, j, 0))
    per_b = lambda *s: pl.BlockSpec((nb,) + s, lambda i, j: (i,) + (0,) * len(s))
    return pl.pallas_call(
        functools.partial(_dn_kernel, nb=nb, tt=tt, c=c),
        grid=(b // nb, l // tt),
        in_specs=[seq(QKV_WIDTH), seq(DN_WIDTH), seq(LANES), per_b(SUBLANES, QKV_WIDTH),
                  per_b(DN_HEADS, DN_HEAD_DIM, DN_HEAD_DIM),
                  _const_spec((SUBLANES, QKV_WIDTH)), _const_spec((SUBLANES, LANES)),
                  _const_spec((1, DN_HEAD_DIM))],
        out_specs=(seq(DN_WIDTH), per_b(DN_HEADS, DN_HEAD_DIM, DN_HEAD_DIM)),
        out_shape=(jax.ShapeDtypeStruct((b, l, DN_WIDTH), BF16),
                   jax.ShapeDtypeStruct((b, DN_HEADS, DN_HEAD_DIM, DN_HEAD_DIM), F32)),
        scratch_shapes=[pltpu.VMEM((nb, SUBLANES, QKV_WIDTH), F32)],
        compiler_params=_params(2),
        name="deltanet",
    )(qkv, z, ba, cache8, s0, cw, gp, ng)


def _mixer_kernel(x_ref, g_ref, wm_ref, wt_ref, cache_ref, s0_ref, cw_ref, gp_ref, ng_ref,
                  u_ref, o_ref, state_ref, tail_ref, p_cur, p_nxt, halo_s, *, nb, tt, c):
    i = pl.program_id(0)
    o1 = QKV_WIDTH
    o2 = o1 + DN_WIDTH
    o3 = o2 + S5_WIDTH

    @pl.when(i == 0)
    def _():
        p_cur[...] = jnp.zeros(p_cur.shape, F32)

    _dn_restart(i <= 1, cache_ref, s0_ref, halo_s, state_ref)

    h = _rms(x_ref[...].reshape(nb * tt, D_MODEL), g_ref[...]).astype(BF16)

    def project(lo, hi):
        def run():
            res = _dot(h, _w_cols(wm_ref, wt_ref, lo, hi)).reshape(nb, tt, hi - lo)
            if o2 <= lo < o3:
                u_ref[:, :, lo - o2:hi - o2] = res
            else:
                p_nxt[:, :, lo:hi] = res
        return run

    side = [project(lo, min(lo + MXU_DIM, IN_PAD)) for lo in range(0, IN_PAD, MXU_DIM)]
    stash = _dn_phase_a(p_cur.at[:, :, 0:o1], p_cur.at[:, :, o1:o2], p_cur.at[:, :, o3:IN_PAD],
                        cw_ref, gp_ref, halo_s, nb=nb, tt=tt, c=c, side=side)
    for step in _dn_phase_b(stash, ng_ref, o_ref, state_ref, nb=nb, tt=tt, c=c):
        step()
    tail_ref[...] = halo_s[...]
    for lo, hi in ((0, o1), (o1, o2), (o3, IN_PAD)):
        p_cur[:, :, lo:hi] = p_nxt[:, :, lo:hi]


def _mixer(x, g, wm, wt, cache8, s0, cw, gp, ng, tt, c):
    nb, l, _ = x.shape
    nt = l // tt
    cur = lambda w_: pl.BlockSpec((nb, tt, w_), lambda i: (0, jnp.minimum(i, nt - 1), 0))
    prev = lambda w_: pl.BlockSpec((nb, tt, w_), lambda i: (0, jnp.maximum(i - 1, 0), 0))
    return pl.pallas_call(
        functools.partial(_mixer_kernel, nb=nb, tt=tt, c=c),
        grid=(nt + 1,),
        in_specs=[cur(D_MODEL), _const_spec((1, D_MODEL)), _const_spec((D_MODEL, IN_MAIN)),
                  _const_spec((D_MODEL, IN_PAD - IN_MAIN)), _const_spec((nb, SUBLANES, QKV_WIDTH)), _const_spec((nb, DN_HEADS, DN_HEAD_DIM, DN_HEAD_DIM)),
                  _const_spec((SUBLANES, QKV_WIDTH)), _const_spec((SUBLANES, LANES)),
                  _const_spec((1, DN_HEAD_DIM))],
        out_specs=(cur(S5_WIDTH), prev(DN_WIDTH),
                   pl.BlockSpec((nb, DN_HEADS, DN_HEAD_DIM, DN_HEAD_DIM), lambda i: (0, 0, 0, 0)),
                   pl.BlockSpec((nb, SUBLANES, QKV_WIDTH), lambda i: (0, 0, 0))),
        out_shape=(jax.ShapeDtypeStruct((nb, l, S5_WIDTH), F32),
                   jax.ShapeDtypeStruct((nb, l, DN_WIDTH), BF16),
                   jax.ShapeDtypeStruct((nb, DN_HEADS, DN_HEAD_DIM, DN_HEAD_DIM), F32),
                   jax.ShapeDtypeStruct((nb, SUBLANES, QKV_WIDTH), F32)),
        scratch_shapes=[pltpu.VMEM((nb, tt, IN_PAD), F32), pltpu.VMEM((nb, tt, IN_PAD), F32),
                        pltpu.VMEM((nb, SUBLANES, QKV_WIDTH), F32)],
        compiler_params=_params(1),
        name="mixer",
    )(x, g, wm, wt, cache8, s0, cw, gp, ng)


def _s5c_kernel(u_ref, x0re_ref, x0im_ref, pq_ref, qq_ref, mq_ref, tab_ref, dv_ref, gw_ref, gb_ref, ng_ref,
                o_ref, fre_ref, fim_ref, xr_s, xi_s, car_s, u_s, o_s, *, rows, seg):
    carry = seg == SUBLANES
    if carry:
        @pl.when(pl.program_id(1) == 0)
        def _():
            car_s[0:1, :] = x0re_ref[0]
            car_s[1:2, :] = x0im_ref[0]

    ncol = S5_WIDTH // LANES
    for k in range(ncol):
        u_s[k] = u_ref[0, :, k * LANES:(k + 1) * LANES]
    us = [jnp.concatenate([u_s[k, pl.ds(s, rows, stride=S5_SLOTS), :] for k in range(ncol)], axis=1)
          for s in range(S5_SLOTS)]
    ubs = [a.astype(BF16) for a in us]
    blk = 4 * S5_GROUP
    ql = QUAD_LANES
    uq =[jnp.concatenate([ubs[s][:, n * blk:(n + 1) * blk] for s in range(S5_SLOTS)], axis=1)
          for n in range(S5_QUADS)]
    for n in range(S5_QUADS):
        inc = _dot(uq[n], pq_ref[n])
        xr_s[:, n * ql:(n + 1) * ql] = inc[:, :ql]
        xi_s[:, n * ql:(n + 1) * ql] = inc[:, ql:]

    seg_row = lax.broadcasted_iota(jnp.int32, (SUBLANES, S5_LANES), 0) & (seg - 1)

    def block_body(rb, cin):
        r0 = pl.multiple_of(rb * SUBLANES, SUBLANES)
        if carry:
            c_re, c_im = cin
        else:
            c_re = x0re_ref[0, pl.ds(r0, SUBLANES), :]
            c_im = x0im_ref[0, pl.ds(r0, SUBLANES), :]
        xr = xr_s[pl.ds(r0, SUBLANES), :]
        xi = xi_s[pl.ds(r0, SUBLANES), :]
        for lvl, d in enumerate(_scan_levels(seg)):
            m_re = tab_ref[16 + 16 * lvl:24 + 16 * lvl, :]
            m_im = tab_ref[24 + 16 * lvl:32 + 16 * lvl, :]
            sr = pltpu.roll(xr, d, axis=0)
            si = pltpu.roll(xi, d, axis=0)
            xr, xi = xr + (m_re * sr - m_im * si), xi + (m_re * si + m_im * sr)
        p_re = tab_ref[0:8, :]
        p_im = tab_ref[8:16, :]
        xr, xi = xr + (p_re * c_re - p_im * c_im), xi + (p_re * c_im + p_im * c_re)
        xr_s[pl.ds(r0, SUBLANES), :] = jnp.where(seg_row == 0, c_re, pltpu.roll(xr, 1, axis=0))
        xi_s[pl.ds(r0, SUBLANES), :] = jnp.where(seg_row == 0, c_im, pltpu.roll(xi, 1, axis=0))
        if carry:
            return xr[SUBLANES - 1:, :], xi[SUBLANES - 1:, :]
        fre_ref[0, pl.ds(r0, SUBLANES), :] = xr
        fim_ref[0, pl.ds(r0, SUBLANES), :] = xi
        return cin

    if carry:
        c_re, c_im = lax.fori_loop(0, rows // SUBLANES, block_body, (car_s[0:1, :], car_s[1:2, :]))
        car_s[0:1, :] = c_re
        car_s[1:2, :] = c_im
        fre_ref[0] = c_re
        fim_ref[0] = c_im
    else:
        lax.fori_loop(0, rows // SUBLANES, block_body, 0)

    ys = []
    for n in range(S5_QUADS):
        xs = jnp.concatenate([xr_s[:, n * ql:(n + 1) * ql], xi_s[:, n * ql:(n + 1) * ql]], axis=1).astype(BF16)
        half = QUAD_IN // 2
        intra = jnp.concatenate([_dot(uq[n][:, :half], mq_ref[n, :half, :half]), _dot(uq[n], mq_ref[n, :, half:])],
                                axis=1)
        ys.append(intra + _dot(xs, qq_ref[n]))
    for s in range(S5_SLOTS):
        y = jnp.concatenate([ys[n][:, s * blk:(s + 1) * blk] for n in range(S5_QUADS)], axis=1)
        y = jax.nn.gelu(y + dv_ref[...] * us[s]).astype(BF16)
        gl = _dot(y, gw_ref[...]) + gb_ref[...]
        o = gl[:, :S5_WIDTH] * jax.nn.sigmoid(gl[:, S5_WIDTH:])
        o = _rms(o, ng_ref[...])
        for k in range(ncol):
            o_s[k, pl.ds(s, rows, stride=S5_SLOTS), :] = o[:, k * LANES:(k + 1) * LANES]
    o_ref[0] = jnp.concatenate([o_s[k] for k in range(ncol)], axis=1).astype(BF16)


def _s5_chunked(u, x0re, x0im, pq, qq, mq, tabc, dv, gw, gb, ng, rows, seg):
    b, l, _ = u.shape
    tt = rows * S5_SLOTS
    seq = pl.BlockSpec((1, tt, S5_WIDTH), lambda i, j: (i, j, 0))
    if seg == SUBLANES:
        st = pl.BlockSpec((1, 1, S5_LANES), lambda i, j: (i, 0, 0))
    else:
        st = pl.BlockSpec((1, rows, S5_LANES), lambda i, j: (i, j, 0))
    return pl.pallas_call(
        functools.partial(_s5c_kernel, rows=rows, seg=seg),
        grid=(b, l // tt),
        in_specs=[seq, st, st,
                  _const_spec((S5_QUADS, QUAD_IN, 2 * QUAD_LANES)),
                  _const_spec((S5_QUADS, 2 * QUAD_LANES, QUAD_IN)),
                  _const_spec((S5_QUADS, QUAD_IN, QUAD_IN)), _const_spec((64, S5_LANES)),
                  _const_spec((1, S5_WIDTH)), _const_spec((S5_WIDTH, 2 * S5_WIDTH)),
                  _const_spec((1, 2 * S5_WIDTH)), _const_spec((1, S5_WIDTH))],
        out_specs=(seq, st, st),
        out_shape=(jax.ShapeDtypeStruct((b, l, S5_WIDTH), BF16),
                   jax.ShapeDtypeStruct(x0re.shape, F32),
                   jax.ShapeDtypeStruct(x0re.shape, F32)),
        scratch_shapes=[pltpu.VMEM((rows, S5_LANES), F32), pltpu.VMEM((rows, S5_LANES), F32),
                        pltpu.VMEM((SUBLANES, S5_LANES), F32),
                        pltpu.VMEM((S5_WIDTH // LANES, tt, LANES), F32),
                        pltpu.VMEM((S5_WIDTH // LANES, tt, LANES), F32)],
        compiler_params=_params(2),
        name="s5_chunked",
    )(u, x0re, x0im, pq, qq, mq, tabc, dv, gw, gb, ng)


def _ffn_kernel(x_ref, odn_ref, os5_ref, prev_ref, woa_ref, wob_ref, n2_ref, wu_ref,
                fcw_ref, fcb_ref, wd_ref, fg_ref, y_ref, tail_ref, halo_s, *, tm, ls, carry):
    if carry:
        @pl.when(pl.program_id(1) == 0)
        def _():
            halo_s[...] = prev_ref[0]

    x1 = x_ref[0] + _dot(odn_ref[0], woa_ref[...]) + _dot(os5_ref[0], wob_ref[...])
    h2 = _rms(x1, n2_ref[...]).astype(BF16)
    down = None
    for lo in range(0, D_FF, FF_CHUNK):
        hi = min(lo + FF_CHUNK, D_FF)
        gate = _dot(h2, wu_ref[:, lo:hi])
        val = _dot(h2, wu_ref[:, D_FF + lo:D_FF + hi])
        if carry:
            xx = jnp.concatenate([halo_s[:, lo:hi], gate], axis=0)
            g2 = xx[SUBLANES - 2:SUBLANES - 2 + tm]
            g1 = xx[SUBLANES - 1:SUBLANES - 1 + tm]
            halo_s[:, lo:hi] = gate[tm - SUBLANES:, :]
            tail_ref[0, :, lo:hi] = gate[tm - SUBLANES:, :]
        else:
            r = lax.broadcasted_iota(jnp.int32, (tm, hi - lo), 0) & (ls - 1)
            prev = prev_ref[0, :, lo:hi]
            g1 = jnp.where(r < 1, pltpu.roll(prev, tm - 1, axis=0), pltpu.roll(gate, 1, axis=0))
            g2 = jnp.where(r < 2, prev, pltpu.roll(gate, 2, axis=0))
            tail_ref[0, :, lo:hi] = gate
        cw = fcw_ref[:, lo:hi]
        pre = g2 * cw[0:1] + g1 * cw[1:2] + gate * cw[2:3] + fcb_ref[:, lo:hi]
        act = pre * jax.nn.sigmoid(pre) * val
        part = _dot(act.astype(BF16), wd_ref[lo:hi, :])
        down = part if down is None else down + part
    y_ref[0] = _rms(x1 + down, fg_ref[...])


def _ffn(x, odn, os5, prev, woa, wob, n2, wu, fcw, fcb, wd, fg, tm, ls, carry):
    b, l, _ = x.shape
    seq = lambda w_: pl.BlockSpec((1, tm, w_), lambda i, j: (i, j, 0))
    if carry:
        prev_spec = pl.BlockSpec((1, SUBLANES, D_FF), lambda i, j: (i, 0, 0))
        tail_spec = pl.BlockSpec((1, SUBLANES, D_FF), lambda i, j: (i, 0, 0))
        tail_shape = jax.ShapeDtypeStruct((b, SUBLANES, D_FF), F32)
    else:
        prev_spec = seq(D_FF)
        tail_spec = seq(D_FF)
        tail_shape = jax.ShapeDtypeStruct((b, l, D_FF), F32)
    return pl.pallas_call(
        functools.partial(_ffn_kernel, tm=tm, ls=ls, carry=carry),
        grid=(b, l // tm),
        in_specs=[seq(D_MODEL), seq(DN_WIDTH), seq(S5_WIDTH), prev_spec,
                  _const_spec((DN_WIDTH, D_MODEL)), _const_spec((S5_WIDTH, D_MODEL)),
                  _const_spec((1, D_MODEL)), _const_spec((D_MODEL, 2 * D_FF)),
                  _const_spec((SUBLANES, D_FF)), _const_spec((1, D_FF)), _const_spec((D_FF, D_MODEL)),
                  _const_spec((1, D_MODEL))],
        out_specs=(seq(D_MODEL), tail_spec),
        out_shape=(jax.ShapeDtypeStruct((b, l, D_MODEL), F32), tail_shape),
        scratch_shapes=[pltpu.VMEM((SUBLANES, D_FF), F32)],
        compiler_params=_params(2),
        name="ffn",
    )(x, odn, os5, prev, woa, wob, n2, wu, fcw, fcb, wd, fg)


def _pad_rows_top(a, rows):
    return jnp.pad(a, ((0, 0), (rows - a.shape[1], 0), (0, 0)))


def _quad_blocks(m):
    n = m.shape[0]
    same = jnp.eye(4, dtype=bool)[:, None, :, None]
    blocks = jnp.where(same, m.reshape(n, S5_QUADS, 4, S5_GROUP, 1, S5_STATE), 0.0)
    return blocks.reshape(n, S5_QUADS, 4 * S5_GROUP, QUAD_LANES)


def _trunk(x, conv_dn, s_dn, s5_re, s5_im, conv_ffn, w, prompt):
    b, l, _ = x.shape
    n = b * l
    c = CHUNK if l % CHUNK == 0 else l
    assert l >= 2 * SUBLANES, "sequence shorter than two row tiles"
    cache8 = _pad_rows_top(conv_dn, SUBLANES)
    if prompt:
        u, o_dn, s_dn_new, tail = _mixer(x, w['n1'], w['w_in_main'], w['w_in_tail'], cache8, s_dn,
                                         w['dn_cw'], w['dn_gp'], w['dn_g'],
                                         tt=min(l, DN_ROWS_LONG), c=c)
        conv_dn_new = tail[:, SUBLANES - (DN_CONV - 1):]
    else:
        qkv, z, u, ba = _in_proj(x.reshape(n, D_MODEL), w['n1'], w['w_in_main'], w['w_in_tail'], min(n, INPROJ_ROWS))
        qkv = qkv.reshape(b, l, QKV_WIDTH)
        o_dn, s_dn_new = _deltanet(qkv, z.reshape(b, l, DN_WIDTH), ba.reshape(b, l, LANES), cache8, s_dn,
                                   w['dn_cw'], w['dn_gp'], w['dn_g'], nb=min(b, DN_SEQS_SHORT), tt=l, c=c)
        conv_dn_new = qkv[:, l - (DN_CONV - 1):]

    s5_args = (w['pq'], w['qq'], w['mq'], w['tab_long' if prompt else 'tab_short'], w['dv'], w['glu_w'],
               w['glu_b'], w['s5_g'])
    if prompt:
        x0re = s5_re.reshape(b, 1, S5_LANES)
        x0im = s5_im.reshape(b, 1, S5_LANES)
        o_s5, fre, fim = _s5_chunked(u.reshape(b, l, S5_WIDTH), x0re, x0im, *s5_args,
                                     rows=min(l // S5_SLOTS, S5_CHUNKS_LONG), seg=SUBLANES)
    else:
        seg = l // S5_SLOTS
        rep = lambda s: jnp.repeat(s.reshape(b, S5_LANES), seg, axis=0).reshape(1, b * seg, S5_LANES)
        o_s5, fre, fim = _s5_chunked(u.reshape(1, n, S5_WIDTH), rep(s5_re), rep(s5_im), *s5_args,
                                     rows=b * seg, seg=seg)
        o_s5 = o_s5.reshape(b, l, S5_WIDTH)
        fre = fre.reshape(b, seg, S5_LANES)[:, seg - 1]
        fim = fim.reshape(b, seg, S5_LANES)[:, seg - 1]
    s5_re_new = fre.reshape(b, S5_GROUPS, S5_STATE)
    s5_im_new = fim.reshape(b, S5_GROUPS, S5_STATE)

    ffn_args = (w['w_out_a'], w['w_out_b'], w['n2'], w['w_up'], w['fcw'], w['fcb'],
                w['w_down'], w['fg'])
    if prompt:
        prev = _pad_rows_top(conv_ffn, SUBLANES)
        tm = min(l, FFN_ROWS_LONG)
        y, tail = _ffn(x, o_dn, o_s5, prev, *ffn_args, tm=tm, ls=tm, carry=True)
        conv_ffn_new = tail[:, SUBLANES - (FFN_CONV - 1):]
    else:
        prev = jnp.pad(conv_ffn, ((0, 0), (0, l - (FFN_CONV - 1)), (0, 0))).reshape(1, n, D_FF)
        y, tail = _ffn(x.reshape(1, n, D_MODEL), o_dn.reshape(1, n, DN_WIDTH), o_s5.reshape(1, n, S5_WIDTH),
                       prev, *ffn_args, tm=min(n, FFN_ROWS_SHORT), ls=l, carry=False)
        y = y.reshape(b, l, D_MODEL)
        conv_ffn_new = tail.reshape(b, l, D_FF)[:, l - (FFN_CONV - 1):]
    return y, (conv_dn_new[None], s_dn_new[None], s5_re_new[None], s5_im_new[None], conv_ffn_new[None])


def kernel(x_prompt, x_sample, cache_dn_conv, state_dn, state_s5_re, state_s5_im, cache_ffn_conv, norm1_g, w_in, dn_conv_w, dn_A_log, dn_dt_bias, dn_norm_g, s5_A_re, s5_A_im, s5_log_dt, s5_B_re, s5_B_im, s5_C_re, s5_C_im, s5_D, s5_glu_w, s5_glu_b, s5_norm_g, w_out, norm2_g, w_up, ffn_conv_w, ffn_conv_b, w_down, final_norm_g):
    assert w_in.shape[0] == 1, "single-layer trunk"
    o1 = QKV_WIDTH
    o2 = o1 + DN_WIDTH
    o4 = o2 + 2 * DN_HEADS
    wi = w_in[0]
    w_in_main = wi[:, :o2].astype(BF16)
    w_in_tail = jnp.concatenate(
        [wi[:, o4:], wi[:, o2:o4], jnp.zeros((D_MODEL, LANES - 2 * DN_HEADS), wi.dtype)], axis=1).astype(BF16)
    lane_pad = lambda v: jnp.pad(v, (BA_LANE_G, LANES - BA_LANE_G - DN_HEADS))
    seg_short = x_sample.shape[1] // S5_SLOTS
    assert x_sample.shape[1] % S5_SLOTS == 0 and seg_short in (1, 2, 4), "sample sequences of 8, 16 or 32 rows"
    a3 = jnp.concatenate([s5_A_re, s5_A_im, jnp.broadcast_to(s5_log_dt[:, :, None], s5_A_re.shape)], axis=0)
    a_lanes = a3.reshape(3, 1, S5_LANES)
    tab_long, tab_short = [_s5_prep(a_lanes, seg=s) for s in (SUBLANES, seg_short)]
    bc = jnp.concatenate([s5_B_re.transpose(0, 1, 3, 2), s5_B_im.transpose(0, 1, 3, 2), s5_C_re, s5_C_im], axis=0)
    pq, qq, mq = _s5_prep_quads(a3.reshape(3, S5_QUADS, 1, QUAD_LANES), _quad_blocks(bc))
    w = {
        'n1': norm1_g, 'w_in_main': w_in_main, 'w_in_tail': w_in_tail,
        'dn_cw': jnp.pad(dn_conv_w[0], ((0, SUBLANES - DN_CONV), (0, 0))),
        'dn_gp': jnp.pad(jnp.stack([lane_pad(dn_A_log[0]), lane_pad(dn_dt_bias[0])]), ((0, SUBLANES - 2), (0, 0))),
        'dn_g': dn_norm_g,
        'tab_long': tab_long, 'tab_short': tab_short, 'pq': pq, 'qq': qq, 'mq': mq,
        'dv': s5_D, 'glu_w': s5_glu_w[0].astype(BF16), 'glu_b': s5_glu_b, 's5_g': s5_norm_g,
        'w_out_a': w_out[0, :DN_WIDTH].astype(BF16), 'w_out_b': w_out[0, DN_WIDTH:].astype(BF16),
        'n2': norm2_g, 'w_up': w_up[0].astype(BF16),
        'fcw': jnp.pad(ffn_conv_w[0], ((0, SUBLANES - FFN_CONV), (0, 0))), 'fcb': ffn_conv_b,
        'w_down': w_down[0].astype(BF16), 'fg': final_norm_g.reshape(1, D_MODEL),
    }
    bp = x_prompt.shape[0]
    zeros = lambda *s: jnp.zeros(s, F32)
    y_p, st_p = _trunk(x_prompt, zeros(bp, DN_CONV - 1, QKV_WIDTH), zeros(bp, DN_HEADS, DN_HEAD_DIM, DN_HEAD_DIM),
                       zeros(bp, S5_GROUPS, S5_STATE), zeros(bp, S5_GROUPS, S5_STATE),
                       zeros(bp, FFN_CONV - 1, D_FF), w, prompt=True)
    y_s, st_s = _trunk(x_sample, cache_dn_conv[0], state_dn[0], state_s5_re[0], state_s5_im[0],
                       cache_ffn_conv[0], w, prompt=False)
    return (y_p, y_s) + st_p + st_s
```

```python
import functools

import jax
import jax.numpy as jnp
from jax import lax
from jax.experimental import pallas as pl
from jax.experimental.pallas import tpu as pltpu

F32 = jnp.float32
BF16 = jnp.bfloat16
EPS = 1e-6

D_MODEL = 1024
DN_HEADS = 4
DN_HEAD_DIM = 128
DN_WIDTH = DN_HEADS * DN_HEAD_DIM
DN_CONV = 4
QKV_WIDTH = 3 * DN_WIDTH
S5_WIDTH = D_MODEL - DN_WIDTH
S5_GROUP = 16
S5_GROUPS = S5_WIDTH // S5_GROUP
S5_STATE = 64
S5_LANES = S5_GROUPS * S5_STATE
S5_SLOTS = 8
S5_QUADS = S5_GROUPS // 4
QUAD_LANES = 4 * S5_STATE
QUAD_IN = S5_SLOTS * 4 * S5_GROUP
D_FF = 2816
FFN_CONV = 3
CHUNK = 64

SUBLANES = 8
LANES = 128
BA_LANE_G = DN_HEADS
IN_MAIN = QKV_WIDTH + DN_WIDTH
IN_PAD = IN_MAIN + S5_WIDTH + LANES
MXU_DIM = 256
FF_CHUNK = 6 * MXU_DIM

DN_ROWS_LONG = 2 * CHUNK
DN_SEQS_SHORT = 16
S5_CHUNKS_LONG = 256
FFN_ROWS_LONG = 512
FFN_ROWS_SHORT = 256
INPROJ_ROWS = 512

VMEM_LIMIT = 56 * 1024 * 1024


def _dot(a, b):
    return jnp.dot(a, b, preferred_element_type=F32)


def _dot_tn(a, b):
    return lax.dot_general(a, b, (((0,), (0,)), ((), ())), preferred_element_type=F32)


def _split_bf16(a):
    hi = a.astype(BF16)
    lo = (a - hi.astype(F32)).astype(BF16)
    return hi, lo


def _bmm(a, b):
    return lax.dot_general(a, b, (((2,), (1,)), ((0,), (0,))), preferred_element_type=F32)


def _bmm_nt(a, b):
    return lax.dot_general(a, b, (((2,), (2,)), ((0,), (0,))), preferred_element_type=F32)


def _bmm_nt3(a, b):
    ah, al = _split_bf16(a)
    bh, bl = _split_bf16(b)
    return _bmm_nt(ah, bh) + (_bmm_nt(al, bh) + _bmm_nt(ah, bl))


def _rms(x, g):
    return x * lax.rsqrt(jnp.mean(x * x, axis=-1, keepdims=True) + EPS) * g


def _const_spec(shape):
    nd = len(shape)
    return pl.BlockSpec(shape, lambda *_: (0,) * nd, pipeline_mode=pl.Buffered(1))


def _params(n_axes):
    return pltpu.CompilerParams(dimension_semantics=("arbitrary",) * n_axes,
                                vmem_limit_bytes=VMEM_LIMIT)


def _cmul(a, b):
    return a[0] * b[0] - a[1] * b[1], a[0] * b[1] + a[1] * b[0]


def _zoh(are, aim, ldt):
    dt = jnp.exp(ldt)
    mag = jnp.exp(are * dt)
    ang = aim * dt
    lr = mag * jnp.cos(ang)
    li = mag * jnp.sin(ang)
    den = are * are + aim * aim
    f_re = ((lr - 1.0) * are + li * aim) / den
    f_im = (li * are - (lr - 1.0) * aim) / den
    return (lr, li), (f_re, f_im)


def _write_scan_tables(tab_ref, step, seg):
    pw = [step]
    for _ in range(seg - 1):
        pw.append(_cmul(pw[-1], step))
    shape = (SUBLANES, step[0].shape[-1])
    row = lax.broadcasted_iota(jnp.int32, shape, 0) & (seg - 1)
    zero = jnp.zeros(shape, F32)
    pre, pim = zero, zero
    for r in range(seg):
        pre = jnp.where(row == r, pw[r][0], pre)
        pim = jnp.where(row == r, pw[r][1], pim)
    tab_ref[...] = jnp.zeros(tab_ref.shape, F32)
    tab_ref[0:8, :] = pre
    tab_ref[8:16, :] = pim
    for lvl, d in enumerate(_scan_levels(seg)):
        tab_ref[16 + 16 * lvl:24 + 16 * lvl, :] = jnp.where(row >= d, pw[d - 1][0], zero)
        tab_ref[24 + 16 * lvl:32 + 16 * lvl, :] = jnp.where(row >= d, pw[d - 1][1], zero)


def _scan_levels(seg):
    return [d for d in (1, 2, 4) if d < seg]


def _s5prep_kernel(a_ref, tab_ref, *, seg):
    lam, _ = _zoh(a_ref[0], a_ref[1], a_ref[2])
    lam_c = lam
    for _ in range(S5_SLOTS - 1):
        lam_c = _cmul(lam_c, lam)
    _write_scan_tables(tab_ref, lam_c, seg)


def _s5_prep(a_lanes, seg):
    return pl.pallas_call(
        functools.partial(_s5prep_kernel, seg=seg),
        out_shape=jax.ShapeDtypeStruct((64, S5_LANES), F32),
        compiler_params=pltpu.CompilerParams(vmem_limit_bytes=VMEM_LIMIT),
        name="s5_prep",
    )(a_lanes)


def _s5prepq_kernel(a_ref, bc_ref, pq_ref, qq_ref, mq_ref):
    lam, f = _zoh(a_ref[0], a_ref[1], a_ref[2])
    bb = _cmul(f, (bc_ref[0], bc_ref[1]))
    ct = (bc_ref[2], bc_ref[3])
    one = (jnp.ones_like(lam[0]), jnp.zeros_like(lam[0]))
    pw = [one]
    for _ in range(S5_SLOTS):
        pw.append(_cmul(pw[-1], lam))
    blk = 4 * S5_GROUP
    for s in range(S5_SLOTS):
        p_re, p_im = _cmul(pw[S5_SLOTS - 1 - s], bb)
        pq_ref[:, s * blk:(s + 1) * blk, :QUAD_LANES] = p_re.astype(BF16)
        pq_ref[:, s * blk:(s + 1) * blk, QUAD_LANES:] = p_im.astype(BF16)
    cl = [_cmul(pw[e], ct) for e in range(S5_SLOTS + 1)]
    qt_re = jnp.concatenate([cl[t + 1][0] for t in range(S5_SLOTS)], axis=1)
    qt_im = jnp.concatenate([-cl[t + 1][1] for t in range(S5_SLOTS)], axis=1)
    for n in range(S5_QUADS):
        qq_ref[n, :QUAD_LANES, :] = qt_re[n].T.astype(BF16)
        qq_ref[n, QUAD_LANES:, :] = qt_im[n].T.astype(BF16)
    lag_re = jnp.concatenate([cl[tau][0] for tau in range(S5_SLOTS)], axis=1)
    lag_im = jnp.concatenate([cl[tau][1] for tau in range(S5_SLOTS)], axis=1)
    lags = _bmm_nt3(bb[0], lag_re) - _bmm_nt3(bb[1], lag_im)
    for s in range(S5_SLOTS):
        m = lags if s == 0 else jnp.concatenate(
            [jnp.zeros((S5_QUADS, blk, s * blk), F32), lags[:, :, :QUAD_IN - s * blk]], axis=2)
        mq_ref[:, s * blk:(s + 1) * blk, :] = m.astype(BF16)


def _s5_prep_quads(a_quads, bc_quads):
    w = lambda r, c_: jax.ShapeDtypeStruct((S5_QUADS, r, c_), BF16)
    return pl.pallas_call(
        _s5prepq_kernel,
        out_shape=(w(QUAD_IN, 2 * QUAD_LANES), w(2 * QUAD_LANES, QUAD_IN), w(QUAD_IN, QUAD_IN)),
        compiler_params=pltpu.CompilerParams(vmem_limit_bytes=VMEM_LIMIT),
        name="s5_prep_quads",
    )(a_quads, bc_quads)


def _w_cols(wm_ref, wt_ref, lo, hi):
    return wm_ref[:, lo:hi] if hi <= IN_MAIN else wt_ref[:, lo - IN_MAIN:hi - IN_MAIN]


def _inproj_kernel(x_ref, g_ref, wm_ref, wt_ref, qkv_ref, z_ref, u_ref, ba_ref):
    h = _rms(x_ref[...], g_ref[...]).astype(BF16)
    o1 = QKV_WIDTH
    o2 = o1 + DN_WIDTH
    o3 = o2 + S5_WIDTH
    qkv_ref[...] = _dot(h, _w_cols(wm_ref, wt_ref, 0, o1))
    z_ref[...] = _dot(h, _w_cols(wm_ref, wt_ref, o1, o2))
    u_ref[...] = _dot(h, _w_cols(wm_ref, wt_ref, o2, o3))
    ba_ref[...] = _dot(h, _w_cols(wm_ref, wt_ref, o3, IN_PAD))


def _in_proj(x2d, g, wm, wt, tm):
    n = x2d.shape[0]
    row = lambda w_: pl.BlockSpec((tm, w_), lambda i: (i, 0))
    return pl.pallas_call(
        _inproj_kernel,
        grid=(n // tm,),
        in_specs=[row(D_MODEL), _const_spec((1, D_MODEL)), _const_spec((D_MODEL, IN_MAIN)),
                  _const_spec((D_MODEL, IN_PAD - IN_MAIN))],
        out_specs=(row(QKV_WIDTH), row(DN_WIDTH), row(S5_WIDTH), row(LANES)),
        out_shape=(jax.ShapeDtypeStruct((n, QKV_WIDTH), F32),
                   jax.ShapeDtypeStruct((n, DN_WIDTH), F32),
                   jax.ShapeDtypeStruct((n, S5_WIDTH), F32),
                   jax.ShapeDtypeStruct((n, LANES), F32)),
        compiler_params=_params(1),
        name="in_proj",
    )(x2d, g, wm, wt)


def _stack_heads(a):
    dh = DN_HEAD_DIM
    return jnp.concatenate([a[:, :, h * dh:(h + 1) * dh] for h in range(DN_HEADS)], axis=1)


def _unstack_heads(a, c):
    return jnp.concatenate([a[:, h * c:(h + 1) * c, :] for h in range(DN_HEADS)], axis=2)


def _dn_kernel(qkv_ref, z_ref, ba_ref, cache_ref, s0_ref, cw_ref, gp_ref, ng_ref, o_ref, state_ref, halo_s,
               *, nb, tt, c):
    _dn_restart(pl.program_id(1) == 0, cache_ref, s0_ref, halo_s, state_ref)
    stash = _dn_phase_a(qkv_ref, z_ref, ba_ref, cw_ref, gp_ref, halo_s, nb=nb, tt=tt, c=c)
    for step in _dn_phase_b(stash, ng_ref, o_ref, state_ref, nb=nb, tt=tt, c=c):
        step()


def _dn_restart(first, cache_ref, s0_ref, halo_s, state_ref):
    @pl.when(first)
    def _():
        halo_s[...] = cache_ref[...]
        state_ref[...] = s0_ref[...]


def _dn_phase_a(qkv_ref, z_ref, ba_ref, cw_ref, gp_ref, halo_s, *, nb, tt, c, side=()):
    nh = DN_HEADS
    dh = DN_HEAD_DIM
    nck = tt // c
    npb = nb * nck
    hc = nh * c
    side = list(side)

    def side_work(n=1):
        for _ in range(n):
            if side:
                side.pop(0)()

    stack = _stack_heads
    unstack = functools.partial(_unstack_heads, c=c)

    assert DN_CONV == 4
    piece = DN_WIDTH
    row8 = lax.broadcasted_iota(jnp.int32, (SUBLANES, piece), 0)

    def shift_rows(y, before, d):
        rolled = pltpu.roll(y, d, axis=1)
        top = jnp.where((row8 < d)[None], pltpu.roll(before, d, axis=1), rolled[:, :SUBLANES])
        return jnp.concatenate([top, rolled[:, SUBLANES:]], axis=1)

    conv_parts = []
    for lo in range(0, QKV_WIDTH, piece):
        side_work()
        hi = lo + piece
        cw = cw_ref[:, lo:hi]
        x = qkv_ref[:, :, lo:hi]
        prev = halo_s[:, :, lo:hi]
        x1 = shift_rows(x, prev, 1)
        pair = x * cw[1:2] + x1 * cw[0:1]
        pair_prev = prev * cw[1:2] + pltpu.roll(prev, 1, axis=1) * cw[0:1]
        acc = (x * cw[3:4] + x1 * cw[2:3]) + shift_rows(pair, pair_prev, 2)
        halo_s[:, :, lo:hi] = x[:, tt - SUBLANES:]
        conv_parts.append((acc * jax.nn.sigmoid(acc)).reshape(npb, c, hi - lo))
    act = jnp.concatenate(conv_parts, axis=2)
    q, k, v = [stack(act[:, :, i * DN_WIDTH:(i + 1) * DN_WIDTH]) for i in range(3)]
    side_work()
    q = q * (lax.rsqrt(jnp.sum(q * q, axis=-1, keepdims=True) + EPS) * (dh ** -0.5))
    k = k * lax.rsqrt(jnp.sum(k * k, axis=-1, keepdims=True) + EPS)

    side_work()
    ba = ba_ref[...].reshape(nb * tt, LANES)
    beta = jax.nn.sigmoid(ba).reshape(npb, c, LANES)
    g2 = -jnp.exp(gp_ref[0:1, :]) * jax.nn.softplus(ba + gp_ref[1:2, :])
    row_in_chunk = lax.broadcasted_iota(jnp.int32, (nb * tt, LANES), 0) & (c - 1)
    d = 1
    while d < c:
        g2 = g2 + jnp.where(row_in_chunk >= d, pltpu.roll(g2, d, axis=0), 0.0)
        d *= 2
    g = g2.reshape(npb, c, LANES)

    g_cols = [g[:, :, BA_LANE_G + h:BA_LANE_G + h + 1] for h in range(nh)]
    b_cols = [beta[:, :, h:h + 1] for h in range(nh)]
    g_col = jnp.concatenate(g_cols, axis=1)
    b_col = jnp.concatenate(b_cols, axis=1)
    g_last = jnp.concatenate([jnp.broadcast_to(gc[:, c - 1:c, :], (npb, c, 1)) for gc in g_cols], axis=1)

    hp = min(nh, LANES // c)
    pieces = []
    for h0 in range(0, nh, hp):
        slab = jnp.concatenate(
            [(g2 if hh == 0 else pltpu.roll(g2, LANES - hh, axis=1)).reshape(npb, c, LANES)
             for hh in range(h0, h0 + hp)], axis=1)
        if hp * c < LANES:
            slab = jnp.concatenate([slab, jnp.zeros((npb, LANES - hp * c, LANES), F32)], axis=1)
        rows = [slab[p].T[BA_LANE_G:BA_LANE_G + 1, :hp * c] for p in range(npb)]
        pieces.append(jnp.stack(rows, axis=0))
    g_row = jnp.concatenate(pieces, axis=2)

    ri = lax.broadcasted_iota(jnp.int32, (c, hc), 0)
    lane = lax.broadcasted_iota(jnp.int32, (c, hc), 1)
    cj = lane & (c - 1)
    causal = (ri >= cj)[None]
    strict = (ri > cj)[None]

    def cat_from_cols(cols):
        out = jnp.broadcast_to(cols[nh - 1], (npb, c, hc))
        for h in range(nh - 2, -1, -1):
            out = jnp.where((lane < (h + 1) * c)[None], jnp.broadcast_to(cols[h], (npb, c, hc)), out)
        return out

    decay = jnp.exp(jnp.where(causal, cat_from_cols(g_cols) - g_row, -jnp.inf))

    side_work()
    kb = k.astype(BF16)
    br = lax.broadcasted_iota(jnp.int32, (hc, nh * dh), 0)
    bl = lax.broadcasted_iota(jnp.int32, (hc, nh * dh), 1)
    head_of_row = sum(jnp.where(br >= h * c, 1, 0) for h in range(1, nh))
    head_of_lane = sum(jnp.where(bl >= h * dh, 1, 0) for h in range(1, nh))
    k_bd = jnp.where((head_of_row == head_of_lane)[None], jnp.concatenate([kb] * nh, axis=2), 0.0)
    k_beta = k * b_col
    qk_lhs = jnp.concatenate([unstack(q), unstack(k_beta)], axis=1).astype(BF16)
    qkk = _bmm_nt(qk_lhs, k_bd)
    qk = qkk[:, :c]
    kk_beta = qkk[:, c:]

    sr = lax.broadcasted_iota(jnp.int32, (hc, hc), 0)
    sl = lax.broadcasted_iota(jnp.int32, (hc, hc), 1)
    shift = c.bit_length() - 1
    same_head = ((sr >> shift) == (sl >> shift))[None]

    def bd(m):
        return jnp.where(same_head, jnp.concatenate([m] * nh, axis=1), 0.0)

    lm = jnp.where(strict, kk_beta * decay, 0.0)
    tinv = jnp.where((ri == cj)[None], 1.0, 0.0) - jnp.where(((ri >> 1) == (cj >> 1))[None], lm, 0.0)
    s = 2
    while s < c:
        sh = s.bit_length()
        lower_left = ((ri >> sh) == (cj >> sh)) & ((ri & s) != 0) & ((cj & s) == 0)
        side_work()
        a_off = jnp.where(lower_left[None], lm, 0.0).astype(BF16)
        xm = _bmm(tinv.astype(BF16), bd(a_off))
        tinv = tinv - _bmm(xm.astype(BF16), bd(tinv.astype(BF16)))
        s *= 2

    e_g = jnp.exp(g_col)
    rhs = jnp.concatenate([v * b_col, k_beta * e_g], axis=2).astype(BF16)
    sol = _bmm(bd(tinv.astype(BF16)), rhs)
    sol_v = sol[:, :, :dh]
    sol_k = sol[:, :, dh:]
    qe = q * e_g
    w_qe = jnp.concatenate([part[:, h * c:(h + 1) * c] for h in range(nh) for part in (sol_k, qe)],
                           axis=1).astype(BF16)
    kd = (k * jnp.exp(g_last - g_col)).astype(BF16)
    qkd_bd = bd((qk * decay).astype(BF16))
    s_decay = jnp.exp(g_last)
    side_work(len(side))
    return sol_v, w_qe, kd, qkd_bd, s_decay, lambda: stack(z_ref[...].reshape(npb, c, DN_WIDTH))


def _dn_phase_b(stash, ng_ref, o_ref, state_ref, *, nb, tt, c):
    sol_v, w_qe, kd, qkd_bd, s_decay, zs = stash
    nh = DN_HEADS
    nck = tt // c
    outs = {}
    pending = {}

    def outputs(ck):
        def run():
            prods = {}
            for b in range(nb):
                p = b * nck + ck
                states = [state_ref[b, h] for h in range(nh)]
                prods[b] = [_dot(w_qe[p, h * 2 * c:(h + 1) * 2 * c, :], states[h].astype(BF16)) for h in range(nh)]
                pending[p] = states
            for b in range(nb):
                p = b * nck + ck
                u = sol_v[p] - jnp.concatenate([r[:c] for r in prods[b]], axis=0)
                ub = u.astype(BF16)
                outs[p] = jnp.concatenate([r[c:] for r in prods[b]], axis=0) + _dot(qkd_bd[p], ub)
                pending[p] = (pending[p], ub)
        return run

    def update(ck):
        def run():
            for b in range(nb):
                p = b * nck + ck
                states, ub = pending.pop(p)
                for h in range(nh):
                    r0 = h * c
                    state_ref[b, h] = (states[h] * s_decay[p, r0:r0 + 1, :]
                                       + _dot_tn(kd[p, r0:r0 + c, :], ub[r0:r0 + c, :]))
        return run

    def finish():
        o = jnp.stack([outs[p] for p in range(nb * nck)], axis=0)
        z = zs()
        o = _rms(o, ng_ref[...]) * (z * jax.nn.sigmoid(z))
        o_ref[...] = _unstack_heads(o, c).reshape(nb, tt, DN_WIDTH).astype(BF16)

    return [f(ck) for ck in range(nck) for f in (outputs, update)] + [finish]


def _deltanet(qkv, z, ba, cache8, s0, cw, gp, ng, nb, tt, c):
    b, l, _ = qkv.shape
    seq = lambda w_: pl.BlockSpec((nb, tt, w_), lambda i, j: (i, j, 0))
    per_b = lambda *s: pl.BlockSpec((nb,) + s, lambda i, j: (i,) + (0,) * len(s))
    return pl.pallas_call(
        functools.partial(_dn_kernel, nb=nb, tt=tt, c=c),
        grid=(b // nb, l // tt),
        in_specs=[seq(QKV_WIDTH), seq(DN_WIDTH), seq(LANES), per_b(SUBLANES, QKV_WIDTH),
                  per_b(DN_HEADS, DN_HEAD_DIM, DN_HEAD_DIM),
                  _const_spec((SUBLANES, QKV_WIDTH)), _const_spec((SUBLANES, LANES)),
                  _const_spec((1, DN_HEAD_DIM))],
        out_specs=(seq(DN_WIDTH), per_b(DN_HEADS, DN_HEAD_DIM, DN_HEAD_DIM)),
        out_shape=(jax.ShapeDtypeStruct((b, l, DN_WIDTH), BF16),
                   jax.ShapeDtypeStruct((b, DN_HEADS, DN_HEAD_DIM, DN_HEAD_DIM), F32)),
        scratch_shapes=[pltpu.VMEM((nb, SUBLANES, QKV_WIDTH), F32)],
        compiler_params=_params(2),
        name="deltanet",
    )(qkv, z, ba, cache8, s0, cw, gp, ng)


def _mixer_kernel(x_ref, g_ref, wm_ref, wt_ref, cache_ref, s0_ref, cw_ref, gp_ref, ng_ref,
                  u_ref, o_ref, state_ref, tail_ref, p_cur, p_nxt, halo_s, *, nb, tt, c):
    i = pl.program_id(0)
    o1 = QKV_WIDTH
    o2 = o1 + DN_WIDTH
    o3 = o2 + S5_WIDTH

    @pl.when(i == 0)
    def _():
        p_cur[...] = jnp.zeros(p_cur.shape, F32)

    _dn_restart(i <= 1, cache_ref, s0_ref, halo_s, state_ref)

    h = _rms(x_ref[...].reshape(nb * tt, D_MODEL), g_ref[...]).astype(BF16)

    def project(lo, hi):
        def run():
            res = _dot(h, _w_cols(wm_ref, wt_ref, lo, hi)).reshape(nb, tt, hi - lo)
            if o2 <= lo < o3:
                u_ref[:, :, lo - o2:hi - o2] = res
            else:
                p_nxt[:, :, lo:hi] = res
        return run

    side = [project(lo, min(lo + MXU_DIM, IN_PAD)) for lo in range(0, IN_PAD, MXU_DIM)]
    stash = _dn_phase_a(p_cur.at[:, :, 0:o1], p_cur.at[:, :, o1:o2], p_cur.at[:, :, o3:IN_PAD],
                        cw_ref, gp_ref, halo_s, nb=nb, tt=tt, c=c, side=side)
    for step in _dn_phase_b(stash, ng_ref, o_ref, state_ref, nb=nb, tt=tt, c=c):
        step()
    tail_ref[...] = halo_s[...]
    for lo, hi in ((0, o1), (o1, o2), (o3, IN_PAD)):
        p_cur[:, :, lo:hi] = p_nxt[:, :, lo:hi]


def _mixer(x, g, wm, wt, cache8, s0, cw, gp, ng, tt, c):
    nb, l, _ = x.shape
    nt = l // tt
    cur = lambda w_: pl.BlockSpec((nb, tt, w_), lambda i: (0, jnp.minimum(i, nt - 1), 0))
    prev = lambda w_: pl.BlockSpec((nb, tt, w_), lambda i: (0, jnp.maximum(i - 1, 0), 0))
    return pl.pallas_call(
        functools.partial(_mixer_kernel, nb=nb, tt=tt, c=c),
        grid=(nt + 1,),
        in_specs=[cur(D_MODEL), _const_spec((1, D_MODEL)), _const_spec((D_MODEL, IN_MAIN)),
                  _const_spec((D_MODEL, IN_PAD - IN_MAIN)), _const_spec((nb, SUBLANES, QKV_WIDTH)), _const_spec((nb, DN_HEADS, DN_HEAD_DIM, DN_HEAD_DIM)),
                  _const_spec((SUBLANES, QKV_WIDTH)), _const_spec((SUBLANES, LANES)),
                  _const_spec((1, DN_HEAD_DIM))],
        out_specs=(cur(S5_WIDTH), prev(DN_WIDTH),
                   pl.BlockSpec((nb, DN_HEADS, DN_HEAD_DIM, DN_HEAD_DIM), lambda i: (0, 0, 0, 0)),
                   pl.BlockSpec((nb, SUBLANES, QKV_WIDTH), lambda i: (0, 0, 0))),
        out_shape=(jax.ShapeDtypeStruct((nb, l, S5_WIDTH), F32),
                   jax.ShapeDtypeStruct((nb, l, DN_WIDTH), BF16),
                   jax.ShapeDtypeStruct((nb, DN_HEADS, DN_HEAD_DIM, DN_HEAD_DIM), F32),
                   jax.ShapeDtypeStruct((nb, SUBLANES, QKV_WIDTH), F32)),
        scratch_shapes=[pltpu.VMEM((nb, tt, IN_PAD), F32), pltpu.VMEM((nb, tt, IN_PAD), F32),
                        pltpu.VMEM((nb, SUBLANES, QKV_WIDTH), F32)],
        compiler_params=_params(1),
        name="mixer",
    )(x, g, wm, wt, cache8, s0, cw, gp, ng)


def _s5c_kernel(u_ref, x0re_ref, x0im_ref, pq_ref, qq_ref, mq_ref, tab_ref, dv_ref, gw_ref, gb_ref, ng_ref,
                o_ref, fre_ref, fim_ref, xr_s, xi_s, car_s, u_s, o_s, *, rows, seg):
    carry = seg == SUBLANES
    if carry:
        @pl.when(pl.program_id(1) == 0)
        def _():
            car_s[0:1, :] = x0re_ref[0]
            car_s[1:2, :] = x0im_ref[0]

    ncol = S5_WIDTH // LANES
    for k in range(ncol):
        u_s[k] = u_ref[0, :, k * LANES:(k + 1) * LANES]
    us = [jnp.concatenate([u_s[k, pl.ds(s, rows, stride=S5_SLOTS), :] for k in range(ncol)], axis=1)
          for s in range(S5_SLOTS)]
    ubs = [a.astype(BF16) for a in us]
    blk = 4 * S5_GROUP
    ql = QUAD_LANES
    uq =[jnp.concatenate([ubs[s][:, n * blk:(n + 1) * blk] for s in range(S5_SLOTS)], axis=1)
          for n in range(S5_QUADS)]
    for n in range(S5_QUADS):
        inc = _dot(uq[n], pq_ref[n])
        xr_s[:, n * ql:(n + 1) * ql] = inc[:, :ql]
        xi_s[:, n * ql:(n + 1) * ql] = inc[:, ql:]

    seg_row = lax.broadcasted_iota(jnp.int32, (SUBLANES, S5_LANES), 0) & (seg - 1)

    def block_body(rb, cin):
        r0 = pl.multiple_of(rb * SUBLANES, SUBLANES)
        if carry:
            c_re, c_im = cin
        else:
            c_re = x0re_ref[0, pl.ds(r0, SUBLANES), :]
            c_im = x0im_ref[0, pl.ds(r0, SUBLANES), :]
        xr = xr_s[pl.ds(r0, SUBLANES), :]
        xi = xi_s[pl.ds(r0, SUBLANES), :]
        for lvl, d in enumerate(_scan_levels(seg)):
            m_re = tab_ref[16 + 16 * lvl:24 + 16 * lvl, :]
            m_im = tab_ref[24 + 16 * lvl:32 + 16 * lvl, :]
            sr = pltpu.roll(xr, d, axis=0)
            si = pltpu.roll(xi, d, axis=0)
            xr, xi = xr + (m_re * sr - m_im * si), xi + (m_re * si + m_im * sr)
        p_re = tab_ref[0:8, :]
        p_im = tab_ref[8:16, :]
        xr, xi = xr + (p_re * c_re - p_im * c_im), xi + (p_re * c_im + p_im * c_re)
        xr_s[pl.ds(r0, SUBLANES), :] = jnp.where(seg_row == 0, c_re, pltpu.roll(xr, 1, axis=0))
        xi_s[pl.ds(r0, SUBLANES), :] = jnp.where(seg_row == 0, c_im, pltpu.roll(xi, 1, axis=0))
        if carry:
            return xr[SUBLANES - 1:, :], xi[SUBLANES - 1:, :]
        fre_ref[0, pl.ds(r0, SUBLANES), :] = xr
        fim_ref[0, pl.ds(r0, SUBLANES), :] = xi
        return cin

    if carry:
        c_re, c_im = lax.fori_loop(0, rows // SUBLANES, block_body, (car_s[0:1, :], car_s[1:2, :]))
        car_s[0:1, :] = c_re
        car_s[1:2, :] = c_im
        fre_ref[0] = c_re
        fim_ref[0] = c_im
    else:
        lax.fori_loop(0, rows // SUBLANES, block_body, 0)

    ys = []
    for n in range(S5_QUADS):
        xs = jnp.concatenate([xr_s[:, n * ql:(n + 1) * ql], xi_s[:, n * ql:(n + 1) * ql]], axis=1).astype(BF16)
        half = QUAD_IN // 2
        intra = jnp.concatenate([_dot(uq[n][:, :half], mq_ref[n, :half, :half]), _dot(uq[n], mq_ref[n, :, half:])],
                                axis=1)
        ys.append(intra + _dot(xs, qq_ref[n]))
    for s in range(S5_SLOTS):
        y = jnp.concatenate([ys[n][:, s * blk:(s + 1) * blk] for n in range(S5_QUADS)], axis=1)
        y = jax.nn.gelu(y + dv_ref[...] * us[s]).astype(BF16)
        gl = _dot(y, gw_ref[...]) + gb_ref[...]
        o = gl[:, :S5_WIDTH] * jax.nn.sigmoid(gl[:, S5_WIDTH:])
        o = _rms(o, ng_ref[...])
        for k in range(ncol):
            o_s[k, pl.ds(s, rows, stride=S5_SLOTS), :] = o[:, k * LANES:(k + 1) * LANES]
    o_ref[0] = jnp.concatenate([o_s[k] for k in range(ncol)], axis=1).astype(BF16)


def _s5_chunked(u, x0re, x0im, pq, qq, mq, tabc, dv, gw, gb, ng, rows, seg):
    b, l, _ = u.shape
    tt = rows * S5_SLOTS
    seq = pl.BlockSpec((1, tt, S5_WIDTH), lambda i, j: (i, j, 0))
    if seg == SUBLANES:
        st = pl.BlockSpec((1, 1, S5_LANES), lambda i, j: (i, 0, 0))
    else:
        st = pl.BlockSpec((1, rows, S5_LANES), lambda i, j: (i, j, 0))
    return pl.pallas_call(
        functools.partial(_s5c_kernel, rows=rows, seg=seg),
        grid=(b, l // tt),
        in_specs=[seq, st, st,
                  _const_spec((S5_QUADS, QUAD_IN, 2 * QUAD_LANES)),
                  _const_spec((S5_QUADS, 2 * QUAD_LANES, QUAD_IN)),
                  _const_spec((S5_QUADS, QUAD_IN, QUAD_IN)), _const_spec((64, S5_LANES)),
                  _const_spec((1, S5_WIDTH)), _const_spec((S5_WIDTH, 2 * S5_WIDTH)),
                  _const_spec((1, 2 * S5_WIDTH)), _const_spec((1, S5_WIDTH))],
        out_specs=(seq, st, st),
        out_shape=(jax.ShapeDtypeStruct((b, l, S5_WIDTH), BF16),
                   jax.ShapeDtypeStruct(x0re.shape, F32),
                   jax.ShapeDtypeStruct(x0re.shape, F32)),
        scratch_shapes=[pltpu.VMEM((rows, S5_LANES), F32), pltpu.VMEM((rows, S5_LANES), F32),
                        pltpu.VMEM((SUBLANES, S5_LANES), F32),
                        pltpu.VMEM((S5_WIDTH // LANES, tt, LANES), F32),
                        pltpu.VMEM((S5_WIDTH // LANES, tt, LANES), F32)],
        compiler_params=_params(2),
        name="s5_chunked",
    )(u, x0re, x0im, pq, qq, mq, tabc, dv, gw, gb, ng)


def _ffn_kernel(x_ref, odn_ref, os5_ref, prev_ref, woa_ref, wob_ref, n2_ref, wu_ref,
                fcw_ref, fcb_ref, wd_ref, fg_ref, y_ref, tail_ref, halo_s, *, tm, ls, carry):
    if carry:
        @pl.when(pl.program_id(1) == 0)
        def _():
            halo_s[...] = prev_ref[0]

    x1 = x_ref[0] + _dot(odn_ref[0], woa_ref[...]) + _dot(os5_ref[0], wob_ref[...])
    h2 = _rms(x1, n2_ref[...]).astype(BF16)
    down = None
    for lo in range(0, D_FF, FF_CHUNK):
        hi = min(lo + FF_CHUNK, D_FF)
        gate = _dot(h2, wu_ref[:, lo:hi])
        val = _dot(h2, wu_ref[:, D_FF + lo:D_FF + hi])
        if carry:
            xx = jnp.concatenate([halo_s[:, lo:hi], gate], axis=0)
            g2 = xx[SUBLANES - 2:SUBLANES - 2 + tm]
            g1 = xx[SUBLANES - 1:SUBLANES - 1 + tm]
            halo_s[:, lo:hi] = gate[tm - SUBLANES:, :]
            tail_ref[0, :, lo:hi] = gate[tm - SUBLANES:, :]
        else:
            r = lax.broadcasted_iota(jnp.int32, (tm, hi - lo), 0) & (ls - 1)
            prev = prev_ref[0, :, lo:hi]
            g1 = jnp.where(r < 1, pltpu.roll(prev, tm - 1, axis=0), pltpu.roll(gate, 1, axis=0))
            g2 = jnp.where(r < 2, prev, pltpu.roll(gate, 2, axis=0))
            tail_ref[0, :, lo:hi] = gate
        cw = fcw_ref[:, lo:hi]
        pre = g2 * cw[0:1] + g1 * cw[1:2] + gate * cw[2:3] + fcb_ref[:, lo:hi]
        act = pre * jax.nn.sigmoid(pre) * val
        part = _dot(act.astype(BF16), wd_ref[lo:hi, :])
        down = part if down is None else down + part
    y_ref[0] = _rms(x1 + down, fg_ref[...])


def _ffn(x, odn, os5, prev, woa, wob, n2, wu, fcw, fcb, wd, fg, tm, ls, carry):
    b, l, _ = x.shape
    seq = lambda w_: pl.BlockSpec((1, tm, w_), lambda i, j: (i, j, 0))
    if carry:
        prev_spec = pl.BlockSpec((1, SUBLANES, D_FF), lambda i, j: (i, 0, 0))
        tail_spec = pl.BlockSpec((1, SUBLANES, D_FF), lambda i, j: (i, 0, 0))
        tail_shape = jax.ShapeDtypeStruct((b, SUBLANES, D_FF), F32)
    else:
        prev_spec = seq(D_FF)
        tail_spec = seq(D_FF)
        tail_shape = jax.ShapeDtypeStruct((b, l, D_FF), F32)
    return pl.pallas_call(
        functools.partial(_ffn_kernel, tm=tm, ls=ls, carry=carry),
        grid=(b, l // tm),
        in_specs=[seq(D_MODEL), seq(DN_WIDTH), seq(S5_WIDTH), prev_spec,
                  _const_spec((DN_WIDTH, D_MODEL)), _const_spec((S5_WIDTH, D_MODEL)),
                  _const_spec((1, D_MODEL)), _const_spec((D_MODEL, 2 * D_FF)),
                  _const_spec((SUBLANES, D_FF)), _const_spec((1, D_FF)), _const_spec((D_FF, D_MODEL)),
                  _const_spec((1, D_MODEL))],
        out_specs=(seq(D_MODEL), tail_spec),
        out_shape=(jax.ShapeDtypeStruct((b, l, D_MODEL), F32), tail_shape),
        scratch_shapes=[pltpu.VMEM((SUBLANES, D_FF), F32)],
        compiler_params=_params(2),
        name="ffn",
    )(x, odn, os5, prev, woa, wob, n2, wu, fcw, fcb, wd, fg)


def _pad_rows_top(a, rows):
    return jnp.pad(a, ((0, 0), (rows - a.shape[1], 0), (0, 0)))


def _quad_blocks(m):
    n = m.shape[0]
    same = jnp.eye(4, dtype=bool)[:, None, :, None]
    blocks = jnp.where(same, m.reshape(n, S5_QUADS, 4, S5_GROUP, 1, S5_STATE), 0.0)
    return blocks.reshape(n, S5_QUADS, 4 * S5_GROUP, QUAD_LANES)


def _trunk(x, conv_dn, s_dn, s5_re, s5_im, conv_ffn, w, prompt):
    b, l, _ = x.shape
    n = b * l
    c = CHUNK if l % CHUNK == 0 else l
    assert l >= 2 * SUBLANES, "sequence shorter than two row tiles"
    cache8 = _pad_rows_top(conv_dn, SUBLANES)
    if prompt:
        u, o_dn, s_dn_new, tail = _mixer(x, w['n1'], w['w_in_main'], w['w_in_tail'], cache8, s_dn,
                                         w['dn_cw'], w['dn_gp'], w['dn_g'],
                                         tt=min(l, DN_ROWS_LONG), c=c)
        conv_dn_new = tail[:, SUBLANES - (DN_CONV - 1):]
    else:
        qkv, z, u, ba = _in_proj(x.reshape(n, D_MODEL), w['n1'], w['w_in_main'], w['w_in_tail'], min(n, INPROJ_ROWS))
        qkv = qkv.reshape(b, l, QKV_WIDTH)
        o_dn, s_dn_new = _deltanet(qkv, z.reshape(b, l, DN_WIDTH), ba.reshape(b, l, LANES), cache8, s_dn,
                                   w['dn_cw'], w['dn_gp'], w['dn_g'], nb=min(b, DN_SEQS_SHORT), tt=l, c=c)
        conv_dn_new = qkv[:, l - (DN_CONV - 1):]

    s5_args = (w['pq'], w['qq'], w['mq'], w['tab_long' if prompt else 'tab_short'], w['dv'], w['glu_w'],
               w['glu_b'], w['s5_g'])
    if prompt:
        x0re = s5_re.reshape(b, 1, S5_LANES)
        x0im = s5_im.reshape(b, 1, S5_LANES)
        o_s5, fre, fim = _s5_chunked(u.reshape(b, l, S5_WIDTH), x0re, x0im, *s5_args,
                                     rows=min(l // S5_SLOTS, S5_CHUNKS_LONG), seg=SUBLANES)
    else:
        seg = l // S5_SLOTS
        rep = lambda s: jnp.repeat(s.reshape(b, S5_LANES), seg, axis=0).reshape(1, b * seg, S5_LANES)
        o_s5, fre, fim = _s5_chunked(u.reshape(1, n, S5_WIDTH), rep(s5_re), rep(s5_im), *s5_args,
                                     rows=b * seg, seg=seg)
        o_s5 = o_s5.reshape(b, l, S5_WIDTH)
        fre = fre.reshape(b, seg, S5_LANES)[:, seg - 1]
        fim = fim.reshape(b, seg, S5_LANES)[:, seg - 1]
    s5_re_new = fre.reshape(b, S5_GROUPS, S5_STATE)
    s5_im_new = fim.reshape(b, S5_GROUPS, S5_STATE)

    ffn_args = (w['w_out_a'], w['w_out_b'], w['n2'], w['w_up'], w['fcw'], w['fcb'],
                w['w_down'], w['fg'])
    if prompt:
        prev = _pad_rows_top(conv_ffn, SUBLANES)
        tm = min(l, FFN_ROWS_LONG)
        y, tail = _ffn(x, o_dn, o_s5, prev, *ffn_args, tm=tm, ls=tm, carry=True)
        conv_ffn_new = tail[:, SUBLANES - (FFN_CONV - 1):]
    else:
        prev = jnp.pad(conv_ffn, ((0, 0), (0, l - (FFN_CONV - 1)), (0, 0))).reshape(1, n, D_FF)
        y, tail = _ffn(x.reshape(1, n, D_MODEL), o_dn.reshape(1, n, DN_WIDTH), o_s5.reshape(1, n, S5_WIDTH),
                       prev, *ffn_args, tm=min(n, FFN_ROWS_SHORT), ls=l, carry=False)
        y = y.reshape(b, l, D_MODEL)
        conv_ffn_new = tail.reshape(b, l, D_FF)[:, l - (FFN_CONV - 1):]
    return y, (conv_dn_new[None], s_dn_new[None], s5_re_new[None], s5_im_new[None], conv_ffn_new[None])


def kernel(x_prompt, x_sample, cache_dn_conv, state_dn, state_s5_re, state_s5_im, cache_ffn_conv, norm1_g, w_in, dn_conv_w, dn_A_log, dn_dt_bias, dn_norm_g, s5_A_re, s5_A_im, s5_log_dt, s5_B_re, s5_B_im, s5_C_re, s5_C_im, s5_D, s5_glu_w, s5_glu_b, s5_norm_g, w_out, norm2_g, w_up, ffn_conv_w, ffn_conv_b, w_down, final_norm_g):
    assert w_in.shape[0] == 1, "single-layer trunk"
    o1 = QKV_WIDTH
    o2 = o1 + DN_WIDTH
    o4 = o2 + 2 * DN_HEADS
    wi = w_in[0]
    w_in_main = wi[:, :o2].astype(BF16)
    w_in_tail = jnp.concatenate(
        [wi[:, o4:], wi[:, o2:o4], jnp.zeros((D_MODEL, LANES - 2 * DN_HEADS), wi.dtype)], axis=1).astype(BF16)
    lane_pad = lambda v: jnp.pad(v, (BA_LANE_G, LANES - BA_LANE_G - DN_HEADS))
    seg_short = x_sample.shape[1] // S5_SLOTS
    assert x_sample.shape[1] % S5_SLOTS == 0 and seg_short in (1, 2, 4), "sample sequences of 8, 16 or 32 rows"
    a3 = jnp.concatenate([s5_A_re, s5_A_im, jnp.broadcast_to(s5_log_dt[:, :, None], s5_A_re.shape)], axis=0)
    a_lanes = a3.reshape(3, 1, S5_LANES)
    tab_long, tab_short = [_s5_prep(a_lanes, seg=s) for s in (SUBLANES, seg_short)]
    bc = jnp.concatenate([s5_B_re.transpose(0, 1, 3, 2), s5_B_im.transpose(0, 1, 3, 2), s5_C_re, s5_C_im], axis=0)
    pq, qq, mq = _s5_prep_quads(a3.reshape(3, S5_QUADS, 1, QUAD_LANES), _quad_blocks(bc))
    w = {
        'n1': norm1_g, 'w_in_main': w_in_main, 'w_in_tail': w_in_tail,
        'dn_cw': jnp.pad(dn_conv_w[0], ((0, SUBLANES - DN_CONV), (0, 0))),
        'dn_gp': jnp.pad(jnp.stack([lane_pad(dn_A_log[0]), lane_pad(dn_dt_bias[0])]), ((0, SUBLANES - 2), (0, 0))),
        'dn_g': dn_norm_g,
        'tab_long': tab_long, 'tab_short': tab_short, 'pq': pq, 'qq': qq, 'mq': mq,
        'dv': s5_D, 'glu_w': s5_glu_w[0].astype(BF16), 'glu_b': s5_glu_b, 's5_g': s5_norm_g,
        'w_out_a': w_out[0, :DN_WIDTH].astype(BF16), 'w_out_b': w_out[0, DN_WIDTH:].astype(BF16),
        'n2': norm2_g, 'w_up': w_up[0].astype(BF16),
        'fcw': jnp.pad(ffn_conv_w[0], ((0, SUBLANES - FFN_CONV), (0, 0))), 'fcb': ffn_conv_b,
        'w_down': w_down[0].astype(BF16), 'fg': final_norm_g.reshape(1, D_MODEL),
    }
    bp = x_prompt.shape[0]
    zeros = lambda *s: jnp.zeros(s, F32)
    y_p, st_p = _trunk(x_prompt, zeros(bp, DN_CONV - 1, QKV_WIDTH), zeros(bp, DN_HEADS, DN_HEAD_DIM, DN_HEAD_DIM),
                       zeros(bp, S5_GROUPS, S5_STATE), zeros(bp, S5_GROUPS, S5_STATE),
                       zeros(bp, FFN_CONV - 1, D_FF), w, prompt=True)
    y_s, st_s = _trunk(x_sample, cache_dn_conv[0], state_dn[0], state_s5_re[0], state_s5_im[0],
                       cache_ffn_conv[0], w, prompt=False)
    return (y_p, y_s) + st_p + st_s
```

```python
import functools

import jax
import jax.numpy as jnp
from jax import lax
from jax.experimental import pallas as pl
from jax.experimental.pallas import tpu as pltpu

F32 = jnp.float32
BF16 = jnp.bfloat16
EPS = 1e-6

D_MODEL = 1024
DN_HEADS = 4
DN_HEAD_DIM = 128
DN_WIDTH = DN_HEADS * DN_HEAD_DIM
DN_CONV = 4
QKV_WIDTH = 3 * DN_WIDTH
S5_WIDTH = D_MODEL - DN_WIDTH
S5_GROUP = 16
S5_GROUPS = S5_WIDTH // S5_GROUP
S5_STATE = 64
S5_LANES = S5_GROUPS * S5_STATE
S5_SLOTS = 8
S5_QUADS = S5_GROUPS // 4
QUAD_LANES = 4 * S5_STATE
QUAD_IN = S5_SLOTS * 4 * S5_GROUP
D_FF = 2816
FFN_CONV = 3
CHUNK = 64

SUBLANES = 8
LANES = 128
BA_LANE_G = DN_HEADS
IN_MAIN = QKV_WIDTH + DN_WIDTH
IN_PAD = IN_MAIN + S5_WIDTH + LANES
MXU_DIM = 256
FF_CHUNK = 6 * MXU_DIM

DN_ROWS_LONG = 2 * CHUNK
DN_SEQS_SHORT = 16
S5_CHUNKS_LONG = 256
FFN_ROWS_LONG = 512
FFN_ROWS_SHORT = 256
INPROJ_ROWS = 512

VMEM_LIMIT = 56 * 1024 * 1024


def _dot(a, b):
    return jnp.dot(a, b, preferred_element_type=F32)


def _dot_tn(a, b):
    return lax.dot_general(a, b, (((0,), (0,)), ((), ())), preferred_element_type=F32)


def _split_bf16(a):
    hi = a.astype(BF16)
    lo = (a - hi.astype(F32)).astype(BF16)
    return hi, lo


def _bmm(a, b):
    return lax.dot_general(a, b, (((2,), (1,)), ((0,), (0,))), preferred_element_type=F32)


def _bmm_nt(a, b):
    return lax.dot_general(a, b, (((2,), (2,)), ((0,), (0,))), preferred_element_type=F32)


def _bmm_nt3(a, b):
    ah, al = _split_bf16(a)
    bh, bl = _split_bf16(b)
    return _bmm_nt(ah, bh) + (_bmm_nt(al, bh) + _bmm_nt(ah, bl))


def _rms(x, g):
    return x * lax.rsqrt(jnp.mean(x * x, axis=-1, keepdims=True) + EPS) * g


def _const_spec(shape):
    nd = len(shape)
    return pl.BlockSpec(shape, lambda *_: (0,) * nd, pipeline_mode=pl.Buffered(1))


def _params(n_axes):
    return pltpu.CompilerParams(dimension_semantics=("arbitrary",) * n_axes,
                                vmem_limit_bytes=VMEM_LIMIT)


def _cmul(a, b):
    return a[0] * b[0] - a[1] * b[1], a[0] * b[1] + a[1] * b[0]


def _zoh(are, aim, ldt):
    dt = jnp.exp(ldt)
    mag = jnp.exp(are * dt)
    ang = aim * dt
    lr = mag * jnp.cos(ang)
    li = mag * jnp.sin(ang)
    den = are * are + aim * aim
    f_re = ((lr - 1.0) * are + li * aim) / den
    f_im = (li * are - (lr - 1.0) * aim) / den
    return (lr, li), (f_re, f_im)


def _write_scan_tables(tab_ref, step, seg):
    pw = [step]
    for _ in range(seg - 1):
        pw.append(_cmul(pw[-1], step))
    shape = (SUBLANES, step[0].shape[-1])
    row = lax.broadcasted_iota(jnp.int32, shape, 0) & (seg - 1)
    zero = jnp.zeros(shape, F32)
    pre, pim = zero, zero
    for r in range(seg):
        pre = jnp.where(row == r, pw[r][0], pre)
        pim = jnp.where(row == r, pw[r][1], pim)
    tab_ref[...] = jnp.zeros(tab_ref.shape, F32)
    tab_ref[0:8, :] = pre
    tab_ref[8:16, :] = pim
    for lvl, d in enumerate(_scan_levels(seg)):
        tab_ref[16 + 16 * lvl:24 + 16 * lvl, :] = jnp.where(row >= d, pw[d - 1][0], zero)
        tab_ref[24 + 16 * lvl:32 + 16 * lvl, :] = jnp.where(row >= d, pw[d - 1][1], zero)


def _scan_levels(seg):
    return [d for d in (1, 2, 4) if d < seg]


def _s5prep_kernel(a_ref, tab_ref, *, seg):
    lam, _ = _zoh(a_ref[0], a_ref[1], a_ref[2])
    lam_c = lam
    for _ in range(S5_SLOTS - 1):
        lam_c = _cmul(lam_c, lam)
    _write_scan_tables(tab_ref, lam_c, seg)


def _s5_prep(a_lanes, seg):
    return pl.pallas_call(
        functools.partial(_s5prep_kernel, seg=seg),
        out_shape=jax.ShapeDtypeStruct((64, S5_LANES), F32),
        compiler_params=pltpu.CompilerParams(vmem_limit_bytes=VMEM_LIMIT),
        name="s5_prep",
    )(a_lanes)


def _s5prepq_kernel(a_ref, bc_ref, pq_ref, qq_ref, mq_ref):
    lam, f = _zoh(a_ref[0], a_ref[1], a_ref[2])
    bb = _cmul(f, (bc_ref[0], bc_ref[1]))
    ct = (bc_ref[2], bc_ref[3])
    one = (jnp.ones_like(lam[0]), jnp.zeros_like(lam[0]))
    pw = [one]
    for _ in range(S5_SLOTS):
        pw.append(_cmul(pw[-1], lam))
    blk = 4 * S5_GROUP
    for s in range(S5_SLOTS):
        p_re, p_im = _cmul(pw[S5_SLOTS - 1 - s], bb)
        pq_ref[:, s * blk:(s + 1) * blk, :QUAD_LANES] = p_re.astype(BF16)
        pq_ref[:, s * blk:(s + 1) * blk, QUAD_LANES:] = p_im.astype(BF16)
    cl = [_cmul(pw[e], ct) for e in range(S5_SLOTS + 1)]
    qt_re = jnp.concatenate([cl[t + 1][0] for t in range(S5_SLOTS)], axis=1)
    qt_im = jnp.concatenate([-cl[t + 1][1] for t in range(S5_SLOTS)], axis=1)
    for n in range(S5_QUADS):
        qq_ref[n, :QUAD_LANES, :] = qt_re[n].T.astype(BF16)
        qq_ref[n, QUAD_LANES:, :] = qt_im[n].T.astype(BF16)
    lag_re = jnp.concatenate([cl[tau][0] for tau in range(S5_SLOTS)], axis=1)
    lag_im = jnp.concatenate([cl[tau][1] for tau in range(S5_SLOTS)], axis=1)
    lags = _bmm_nt3(bb[0], lag_re) - _bmm_nt3(bb[1], lag_im)
    for s in range(S5_SLOTS):
        m = lags if s == 0 else jnp.concatenate(
            [jnp.zeros((S5_QUADS, blk, s * blk), F32), lags[:, :, :QUAD_IN - s * blk]], axis=2)
        mq_ref[:, s * blk:(s + 1) * blk, :] = m.astype(BF16)


def _s5_prep_quads(a_quads, bc_quads):
    w = lambda r, c_: jax.ShapeDtypeStruct((S5_QUADS, r, c_), BF16)
    return pl.pallas_call(
        _s5prepq_kernel,
        out_shape=(w(QUAD_IN, 2 * QUAD_LANES), w(2 * QUAD_LANES, QUAD_IN), w(QUAD_IN, QUAD_IN)),
        compiler_params=pltpu.CompilerParams(vmem_limit_bytes=VMEM_LIMIT),
        name="s5_prep_quads",
    )(a_quads, bc_quads)


def _w_cols(wm_ref, wt_ref, lo, hi):
    return wm_ref[:, lo:hi] if hi <= IN_MAIN else wt_ref[:, lo - IN_MAIN:hi - IN_MAIN]


def _inproj_kernel(x_ref, g_ref, wm_ref, wt_ref, qkv_ref, z_ref, u_ref, ba_ref):
    h = _rms(x_ref[...], g_ref[...]).astype(BF16)
    o1 = QKV_WIDTH
    o2 = o1 + DN_WIDTH
    o3 = o2 + S5_WIDTH
    qkv_ref[...] = _dot(h, _w_cols(wm_ref, wt_ref, 0, o1))
    z_ref[...] = _dot(h, _w_cols(wm_ref, wt_ref, o1, o2))
    u_ref[...] = _dot(h, _w_cols(wm_ref, wt_ref, o2, o3))
    ba_ref[...] = _dot(h, _w_cols(wm_ref, wt_ref, o3, IN_PAD))


def _in_proj(x2d, g, wm, wt, tm):
    n = x2d.shape[0]
    row = lambda w_: pl.BlockSpec((tm, w_), lambda i: (i, 0))
    return pl.pallas_call(
        _inproj_kernel,
        grid=(n // tm,),
        in_specs=[row(D_MODEL), _const_spec((1, D_MODEL)), _const_spec((D_MODEL, IN_MAIN)),
                  _const_spec((D_MODEL, IN_PAD - IN_MAIN))],
        out_specs=(row(QKV_WIDTH), row(DN_WIDTH), row(S5_WIDTH), row(LANES)),
        out_shape=(jax.ShapeDtypeStruct((n, QKV_WIDTH), F32),
                   jax.ShapeDtypeStruct((n, DN_WIDTH), F32),
                   jax.ShapeDtypeStruct((n, S5_WIDTH), F32),
                   jax.ShapeDtypeStruct((n, LANES), F32)),
        compiler_params=_params(1),
        name="in_proj",
    )(x2d, g, wm, wt)


def _stack_heads(a):
    dh = DN_HEAD_DIM
    return jnp.concatenate([a[:, :, h * dh:(h + 1) * dh] for h in range(DN_HEADS)], axis=1)


def _unstack_heads(a, c):
    return jnp.concatenate([a[:, h * c:(h + 1) * c, :] for h in range(DN_HEADS)], axis=2)


def _dn_kernel(qkv_ref, z_ref, ba_ref, cache_ref, s0_ref, cw_ref, gp_ref, ng_ref, o_ref, state_ref, halo_s,
               *, nb, tt, c):
    _dn_restart(pl.program_id(1) == 0, cache_ref, s0_ref, halo_s, state_ref)
    stash = _dn_phase_a(qkv_ref, z_ref, ba_ref, cw_ref, gp_ref, halo_s, nb=nb, tt=tt, c=c)
    for step in _dn_phase_b(stash, ng_ref, o_ref, state_ref, nb=nb, tt=tt, c=c):
        step()


def _dn_restart(first, cache_ref, s0_ref, halo_s, state_ref):
    @pl.when(first)
    def _():
        halo_s[...] = cache_ref[...]
        state_ref[...] = s0_ref[...]


def _dn_phase_a(qkv_ref, z_ref, ba_ref, cw_ref, gp_ref, halo_s, *, nb, tt, c, side=()):
    nh = DN_HEADS
    dh = DN_HEAD_DIM
    nck = tt // c
    npb = nb * nck
    hc = nh * c
    side = list(side)

    def side_work(n=1):
        for _ in range(n):
            if side:
                side.pop(0)()

    stack = _stack_heads
    unstack = functools.partial(_unstack_heads, c=c)

    assert DN_CONV == 4
    piece = DN_WIDTH
    row8 = lax.broadcasted_iota(jnp.int32, (SUBLANES, piece), 0)

    def shift_rows(y, before, d):
        rolled = pltpu.roll(y, d, axis=1)
        top = jnp.where((row8 < d)[None], pltpu.roll(before, d, axis=1), rolled[:, :SUBLANES])
        return jnp.concatenate([top, rolled[:, SUBLANES:]], axis=1)

    conv_parts = []
    for lo in range(0, QKV_WIDTH, piece):
        side_work()
        hi = lo + piece
        cw = cw_ref[:, lo:hi]
        x = qkv_ref[:, :, lo:hi]
        prev = halo_s[:, :, lo:hi]
        x1 = shift_rows(x, prev, 1)
        pair = x * cw[1:2] + x1 * cw[0:1]
        pair_prev = prev * cw[1:2] + pltpu.roll(prev, 1, axis=1) * cw[0:1]
        acc = (x * cw[3:4] + x1 * cw[2:3]) + shift_rows(pair, pair_prev, 2)
        halo_s[:, :, lo:hi] = x[:, tt - SUBLANES:]
        conv_parts.append((acc * jax.nn.sigmoid(acc)).reshape(npb, c, hi - lo))
    act = jnp.concatenate(conv_parts, axis=2)
    q, k, v = [stack(act[:, :, i * DN_WIDTH:(i + 1) * DN_WIDTH]) for i in range(3)]
    side_work()
    q = q * (lax.rsqrt(jnp.sum(q * q, axis=-1, keepdims=True) + EPS) * (dh ** -0.5))
    k = k * lax.rsqrt(jnp.sum(k * k, axis=-1, keepdims=True) + EPS)

    side_work()
    ba = ba_ref[...].reshape(nb * tt, LANES)
    beta = jax.nn.sigmoid(ba).reshape(npb, c, LANES)
    g2 = -jnp.exp(gp_ref[0:1, :]) * jax.nn.softplus(ba + gp_ref[1:2, :])
    row_in_chunk = lax.broadcasted_iota(jnp.int32, (nb * tt, LANES), 0) & (c - 1)
    d = 1
    while d < c:
        g2 = g2 + jnp.where(row_in_chunk >= d, pltpu.roll(g2, d, axis=0), 0.0)
        d *= 2
    g = g2.reshape(npb, c, LANES)

    g_cols = [g[:, :, BA_LANE_G + h:BA_LANE_G + h + 1] for h in range(nh)]
    b_cols = [beta[:, :, h:h + 1] for h in range(nh)]
    g_col = jnp.concatenate(g_cols, axis=1)
    b_col = jnp.concatenate(b_cols, axis=1)
    g_last = jnp.concatenate([jnp.broadcast_to(gc[:, c - 1:c, :], (npb, c, 1)) for gc in g_cols], axis=1)

    hp = min(nh, LANES // c)
    pieces = []
    for h0 in range(0, nh, hp):
        slab = jnp.concatenate(
            [(g2 if hh == 0 else pltpu.roll(g2, LANES - hh, axis=1)).reshape(npb, c, LANES)
             for hh in range(h0, h0 + hp)], axis=1)
        if hp * c < LANES:
            slab = jnp.concatenate([slab, jnp.zeros((npb, LANES - hp * c, LANES), F32)], axis=1)
        rows = [slab[p].T[BA_LANE_G:BA_LANE_G + 1, :hp * c] for p in range(npb)]
        pieces.append(jnp.stack(rows, axis=0))
    g_row = jnp.concatenate(pieces, axis=2)

    ri = lax.broadcasted_iota(jnp.int32, (c, hc), 0)
    lane = lax.broadcasted_iota(jnp.int32, (c, hc), 1)
    cj = lane & (c - 1)
    causal = (ri >= cj)[None]
    strict = (ri > cj)[None]

    def cat_from_cols(cols):
        out = jnp.broadcast_to(cols[nh - 1], (npb, c, hc))
        for h in range(nh - 2, -1, -1):
            out = jnp.where((lane < (h + 1) * c)[None], jnp.broadcast_to(cols[h], (npb, c, hc)), out)
        return out

    decay = jnp.exp(jnp.where(causal, cat_from_cols(g_cols) - g_row, -jnp.inf))

    side_work()
    kb = k.astype(BF16)
    br = lax.broadcasted_iota(jnp.int32, (hc, nh * dh), 0)
    bl = lax.broadcasted_iota(jnp.int32, (hc, nh * dh), 1)
    head_of_row = sum(jnp.where(br >= h * c, 1, 0) for h in range(1, nh))
    head_of_lane = sum(jnp.where(bl >= h * dh, 1, 0) for h in range(1, nh))
    k_bd = jnp.where((head_of_row == head_of_lane)[None], jnp.concatenate([kb] * nh, axis=2), 0.0)
    k_beta = k * b_col
    qk_lhs = jnp.concatenate([unstack(q), unstack(k_beta)], axis=1).astype(BF16)
    qkk = _bmm_nt(qk_lhs, k_bd)
    qk = qkk[:, :c]
    kk_beta = qkk[:, c:]

    sr = lax.broadcasted_iota(jnp.int32, (hc, hc), 0)
    sl = lax.broadcasted_iota(jnp.int32, (hc, hc), 1)
    shift = c.bit_length() - 1
    same_head = ((sr >> shift) == (sl >> shift))[None]

    def bd(m):
        return jnp.where(same_head, jnp.concatenate([m] * nh, axis=1), 0.0)

    lm = jnp.where(strict, kk_beta * decay, 0.0)
    tinv = jnp.where((ri == cj)[None], 1.0, 0.0) - jnp.where(((ri >> 1) == (cj >> 1))[None], lm, 0.0)
    s = 2
    while s < c:
        sh = s.bit_length()
        lower_left = ((ri >> sh) == (cj >> sh)) & ((ri & s) != 0) & ((cj & s) == 0)
        side_work()
        a_off = jnp.where(lower_left[None], lm, 0.0).astype(BF16)
        xm = _bmm(tinv.astype(BF16), bd(a_off))
        tinv = tinv - _bmm(xm.astype(BF16), bd(tinv.astype(BF16)))
        s *= 2

    e_g = jnp.exp(g_col)
    rhs = jnp.concatenate([v * b_col, k_beta * e_g], axis=2).astype(BF16)
    sol = _bmm(bd(tinv.astype(BF16)), rhs)
    sol_v = sol[:, :, :dh]
    sol_k = sol[:, :, dh:]
    qe = q * e_g
    w_qe = jnp.concatenate([part[:, h * c:(h + 1) * c] for h in range(nh) for part in (sol_k, qe)],
                           axis=1).astype(BF16)
    kd = (k * jnp.exp(g_last - g_col)).astype(BF16)
    qkd_bd = bd((qk * decay).astype(BF16))
    s_decay = jnp.exp(g_last)
    side_work(len(side))
    return sol_v, w_qe, kd, qkd_bd, s_decay, lambda: stack(z_ref[...].reshape(npb, c, DN_WIDTH))


def _dn_phase_b(stash, ng_ref, o_ref, state_ref, *, nb, tt, c):
    sol_v, w_qe, kd, qkd_bd, s_decay, zs = stash
    nh = DN_HEADS
    nck = tt // c
    outs = {}
    pending = {}

    def outputs(ck):
        def run():
            prods = {}
            for b in range(nb):
                p = b * nck + ck
                states = [state_ref[b, h] for h in range(nh)]
                prods[b] = [_dot(w_qe[p, h * 2 * c:(h + 1) * 2 * c, :], states[h].astype(BF16)) for h in range(nh)]
                pending[p] = states
            for b in range(nb):
                p = b * nck + ck
                u = sol_v[p] - jnp.concatenate([r[:c] for r in prods[b]], axis=0)
                ub = u.astype(BF16)
                outs[p] = jnp.concatenate([r[c:] for r in prods[b]], axis=0) + _dot(qkd_bd[p], ub)
                pending[p] = (pending[p], ub)
        return run

    def update(ck):
        def run():
            for b in range(nb):
                p = b * nck + ck
                states, ub = pending.pop(p)
                for h in range(nh):
                    r0 = h * c
                    state_ref[b, h] = (states[h] * s_decay[p, r0:r0 + 1, :]
                                       + _dot_tn(kd[p, r0:r0 + c, :], ub[r0:r0 + c, :]))
        return run

    def finish():
        o = jnp.stack([outs[p] for p in range(nb * nck)], axis=0)
        z = zs()
        o = _rms(o, ng_ref[...]) * (z * jax.nn.sigmoid(z))
        o_ref[...] = _unstack_heads(o, c).reshape(nb, tt, DN_WIDTH).astype(BF16)

    return [f(ck) for ck in range(nck) for f in (outputs, update)] + [finish]


def _deltanet(qkv, z, ba, cache8, s0, cw, gp, ng, nb, tt, c):
    b, l, _ = qkv.shape
    seq = lambda w_: pl.BlockSpec((nb, tt, w_), lambda i, j: (i, j, 0))
    per_b = lambda *s: pl.BlockSpec((nb,) + s, lambda i, j: (i,) + (0,) * len(s))
    return pl.pallas_call(
        functools.partial(_dn_kernel, nb=nb, tt=tt, c=c),
        grid=(b // nb, l // tt),
        in_specs=[seq(QKV_WIDTH), seq(DN_WIDTH), seq(LANES), per_b(SUBLANES, QKV_WIDTH),
                  per_b(DN_HEADS, DN_HEAD_DIM, DN_HEAD_DIM),
                  _const_spec((SUBLANES, QKV_WIDTH)), _const_spec((SUBLANES, LANES)),
                  _const_spec((1, DN_HEAD_DIM))],
        out_specs=(seq(DN_WIDTH), per_b(DN_HEADS, DN_HEAD_DIM, DN_HEAD_DIM)),
        out_shape=(jax.ShapeDtypeStruct((b, l, DN_WIDTH), BF16),
                   jax.ShapeDtypeStruct((b, DN_HEADS, DN_HEAD_DIM, DN_HEAD_DIM), F32)),
        scratch_shapes=[pltpu.VMEM((nb, SUBLANES, QKV_WIDTH), F32)],
        compiler_params=_params(2),
        name="deltanet",
    )(qkv, z, ba, cache8, s0, cw, gp, ng)


def _mixer_kernel(x_ref, g_ref, wm_ref, wt_ref, cache_ref, s0_ref, cw_ref, gp_ref, ng_ref,
                  u_ref, o_ref, state_ref, tail_ref, p_cur, p_nxt, halo_s, *, nb, tt, c):
    i = pl.program_id(0)
    o1 = QKV_WIDTH
    o2 = o1 + DN_WIDTH
    o3 = o2 + S5_WIDTH

    @pl.when(i == 0)
    def _():
        p_cur[...] = jnp.zeros(p_cur.shape, F32)

    _dn_restart(i <= 1, cache_ref, s0_ref, halo_s, state_ref)

    h = _rms(x_ref[...].reshape(nb * tt, D_MODEL), g_ref[...]).astype(BF16)

    def project(lo, hi):
        def run():
            res = _dot(h, _w_cols(wm_ref, wt_ref, lo, hi)).reshape(nb, tt, hi - lo)
            if o2 <= lo < o3:
                u_ref[:, :, lo - o2:hi - o2] = res
            else:
                p_nxt[:, :, lo:hi] = res
        return run

    side = [project(lo, min(lo + MXU_DIM, IN_PAD)) for lo in range(0, IN_PAD, MXU_DIM)]
    stash = _dn_phase_a(p_cur.at[:, :, 0:o1], p_cur.at[:, :, o1:o2], p_cur.at[:, :, o3:IN_PAD],
                        cw_ref, gp_ref, halo_s, nb=nb, tt=tt, c=c, side=side)
    for step in _dn_phase_b(stash, ng_ref, o_ref, state_ref, nb=nb, tt=tt, c=c):
        step()
    tail_ref[...] = halo_s[...]
    for lo, hi in ((0, o1), (o1, o2), (o3, IN_PAD)):
        p_cur[:, :, lo:hi] = p_nxt[:, :, lo:hi]


def _mixer(x, g, wm, wt, cache8, s0, cw, gp, ng, tt, c):
    nb, l, _ = x.shape
    nt = l // tt
    cur = lambda w_: pl.BlockSpec((nb, tt, w_), lambda i: (0, jnp.minimum(i, nt - 1), 0))
    prev = lambda w_: pl.BlockSpec((nb, tt, w_), lambda i: (0, jnp.maximum(i - 1, 0), 0))
    return pl.pallas_call(
        functools.partial(_mixer_kernel, nb=nb, tt=tt, c=c),
        grid=(nt + 1,),
        in_specs=[cur(D_MODEL), _const_spec((1, D_MODEL)), _const_spec((D_MODEL, IN_MAIN)),
                  _const_spec((D_MODEL, IN_PAD - IN_MAIN)), _const_spec((nb, SUBLANES, QKV_WIDTH)), _const_spec((nb, DN_HEADS, DN_HEAD_DIM, DN_HEAD_DIM)),
                  _const_spec((SUBLANES, QKV_WIDTH)), _const_spec((SUBLANES, LANES)),
                  _const_spec((1, DN_HEAD_DIM))],
        out_specs=(cur(S5_WIDTH), prev(DN_WIDTH),
                   pl.BlockSpec((nb, DN_HEADS, DN_HEAD_DIM, DN_HEAD_DIM), lambda i: (0, 0, 0, 0)),
                   pl.BlockSpec((nb, SUBLANES, QKV_WIDTH), lambda i: (0, 0, 0))),
        out_shape=(jax.ShapeDtypeStruct((nb, l, S5_WIDTH), F32),
                   jax.ShapeDtypeStruct((nb, l, DN_WIDTH), BF16),
                   jax.ShapeDtypeStruct((nb, DN_HEADS, DN_HEAD_DIM, DN_HEAD_DIM), F32),
                   jax.ShapeDtypeStruct((nb, SUBLANES, QKV_WIDTH), F32)),
        scratch_shapes=[pltpu.VMEM((nb, tt, IN_PAD), F32), pltpu.VMEM((nb, tt, IN_PAD), F32),
                        pltpu.VMEM((nb, SUBLANES, QKV_WIDTH), F32)],
        compiler_params=_params(1),
        name="mixer",
    )(x, g, wm, wt, cache8, s0, cw, gp, ng)


def _s5c_kernel(u_ref, x0re_ref, x0im_ref, pq_ref, qq_ref, mq_ref, tab_ref, dv_ref, gw_ref, gb_ref, ng_ref,
                o_ref, fre_ref, fim_ref, xr_s, xi_s, car_s, u_s, o_s, *, rows, seg):
    carry = seg == SUBLANES
    if carry:
        @pl.when(pl.program_id(1) == 0)
        def _():
            car_s[0:1, :] = x0re_ref[0]
            car_s[1:2, :] = x0im_ref[0]

    ncol = S5_WIDTH // LANES
    for k in range(ncol):
        u_s[k] = u_ref[0, :, k * LANES:(k + 1) * LANES]
    us = [jnp.concatenate([u_s[k, pl.ds(s, rows, stride=S5_SLOTS), :] for k in range(ncol)], axis=1)
          for s in range(S5_SLOTS)]
    ubs = [a.astype(BF16) for a in us]
    blk = 4 * S5_GROUP
    ql = QUAD_LANES
    uq =[jnp.concatenate([ubs[s][:, n * blk:(n + 1) * blk] for s in range(S5_SLOTS)], axis=1)
          for n in range(S5_QUADS)]
    for n in range(S5_QUADS):
        inc = _dot(uq[n], pq_ref[n])
        xr_s[:, n * ql:(n + 1) * ql] = inc[:, :ql]
        xi_s[:, n * ql:(n + 1) * ql] = inc[:, ql:]

    seg_row = lax.broadcasted_iota(jnp.int32, (SUBLANES, S5_LANES), 0) & (seg - 1)

    def block_body(rb, cin):
        r0 = pl.multiple_of(rb * SUBLANES, SUBLANES)
        if carry:
            c_re, c_im = cin
        else:
            c_re = x0re_ref[0, pl.ds(r0, SUBLANES), :]
            c_im = x0im_ref[0, pl.ds(r0, SUBLANES), :]
        xr = xr_s[pl.ds(r0, SUBLANES), :]
        xi = xi_s[pl.ds(r0, SUBLANES), :]
        for lvl, d in enumerate(_scan_levels(seg)):
            m_re = tab_ref[16 + 16 * lvl:24 + 16 * lvl, :]
            m_im = tab_ref[24 + 16 * lvl:32 + 16 * lvl, :]
            sr = pltpu.roll(xr, d, axis=0)
            si = pltpu.roll(xi, d, axis=0)
            xr, xi = xr + (m_re * sr - m_im * si), xi + (m_re * si + m_im * sr)
        p_re = tab_ref[0:8, :]
        p_im = tab_ref[8:16, :]
        xr, xi = xr + (p_re * c_re - p_im * c_im), xi + (p_re * c_im + p_im * c_re)
        xr_s[pl.ds(r0, SUBLANES), :] = jnp.where(seg_row == 0, c_re, pltpu.roll(xr, 1, axis=0))
        xi_s[pl.ds(r0, SUBLANES), :] = jnp.where(seg_row == 0, c_im, pltpu.roll(xi, 1, axis=0))
        if carry:
            return xr[SUBLANES - 1:, :], xi[SUBLANES - 1:, :]
        fre_ref[0, pl.ds(r0, SUBLANES), :] = xr
        fim_ref[0, pl.ds(r0, SUBLANES), :] = xi
        return cin

    if carry:
        c_re, c_im = lax.fori_loop(0, rows // SUBLANES, block_body, (car_s[0:1, :], car_s[1:2, :]))
        car_s[0:1, :] = c_re
        car_s[1:2, :] = c_im
        fre_ref[0] = c_re
        fim_ref[0] = c_im
    else:
        lax.fori_loop(0, rows // SUBLANES, block_body, 0)

    ys = []
    for n in range(S5_QUADS):
        xs = jnp.concatenate([xr_s[:, n * ql:(n + 1) * ql], xi_s[:, n * ql:(n + 1) * ql]], axis=1).astype(BF16)
        half = QUAD_IN // 2
        intra = jnp.concatenate([_dot(uq[n][:, :half], mq_ref[n, :half, :half]), _dot(uq[n], mq_ref[n, :, half:])],
                                axis=1)
        ys.append(intra + _dot(xs, qq_ref[n]))
    for s in range(S5_SLOTS):
        y = jnp.concatenate([ys[n][:, s * blk:(s + 1) * blk] for n in range(S5_QUADS)], axis=1)
        y = jax.nn.gelu(y + dv_ref[...] * us[s]).astype(BF16)
        gl = _dot(y, gw_ref[...]) + gb_ref[...]
        o = gl[:, :S5_WIDTH] * jax.nn.sigmoid(gl[:, S5_WIDTH:])
        o = _rms(o, ng_ref[...])
        for k in range(ncol):
            o_s[k, pl.ds(s, rows, stride=S5_SLOTS), :] = o[:, k * LANES:(k + 1) * LANES]
    o_ref[0] = jnp.concatenate([o_s[k] for k in range(ncol)], axis=1).astype(BF16)


def _s5_chunked(u, x0re, x0im, pq, qq, mq, tabc, dv, gw, gb, ng, rows, seg):
    b, l, _ = u.shape
    tt = rows * S5_SLOTS
    seq = pl.BlockSpec((1, tt, S5_WIDTH), lambda i, j: (i, j, 0))
    if seg == SUBLANES:
        st = pl.BlockSpec((1, 1, S5_LANES), lambda i, j: (i, 0, 0))
    else:
        st = pl.BlockSpec((1, rows, S5_LANES), lambda i, j: (i, j, 0))
    return pl.pallas_call(
        functools.partial(_s5c_kernel, rows=rows, seg=seg),
        grid=(b, l // tt),
        in_specs=[seq, st, st,
                  _const_spec((S5_QUADS, QUAD_IN, 2 * QUAD_LANES)),
                  _const_spec((S5_QUADS, 2 * QUAD_LANES, QUAD_IN)),
                  _const_spec((S5_QUADS, QUAD_IN, QUAD_IN)), _const_spec((64, S5_LANES)),
                  _const_spec((1, S5_WIDTH)), _const_spec((S5_WIDTH, 2 * S5_WIDTH)),
                  _const_spec((1, 2 * S5_WIDTH)), _const_spec((1, S5_WIDTH))],
        out_specs=(seq, st, st),
        out_shape=(jax.ShapeDtypeStruct((b, l, S5_WIDTH), BF16),
                   jax.ShapeDtypeStruct(x0re.shape, F32),
                   jax.ShapeDtypeStruct(x0re.shape, F32)),
        scratch_shapes=[pltpu.VMEM((rows, S5_LANES), F32), pltpu.VMEM((rows, S5_LANES), F32),
                        pltpu.VMEM((SUBLANES, S5_LANES), F32),
                        pltpu.VMEM((S5_WIDTH // LANES, tt, LANES), F32),
                        pltpu.VMEM((S5_WIDTH // LANES, tt, LANES), F32)],
        compiler_params=_params(2),
        name="s5_chunked",
    )(u, x0re, x0im, pq, qq, mq, tabc, dv, gw, gb, ng)


def _ffn_kernel(x_ref, odn_ref, os5_ref, prev_ref, woa_ref, wob_ref, n2_ref, wu_ref,
                fcw_ref, fcb_ref, wd_ref, fg_ref, y_ref, tail_ref, halo_s, *, tm, ls, carry):
    if carry:
        @pl.when(pl.program_id(1) == 0)
        def _():
            halo_s[...] = prev_ref[0]

    x1 = x_ref[0] + _dot(odn_ref[0], woa_ref[...]) + _dot(os5_ref[0], wob_ref[...])
    h2 = _rms(x1, n2_ref[...]).astype(BF16)
    down = None
    for lo in range(0, D_FF, FF_CHUNK):
        hi = min(lo + FF_CHUNK, D_FF)
        gate = _dot(h2, wu_ref[:, lo:hi])
        val = _dot(h2, wu_ref[:, D_FF + lo:D_FF + hi])
        if carry:
            before = halo_s[:, lo:hi]
            row8 = lax.broadcasted_iota(jnp.int32, (SUBLANES, hi - lo), 0)
            shifted = []
            for d in (1, 2):
                rolled = pltpu.roll(gate, d, axis=0)
                top = jnp.where(row8 < d, pltpu.roll(before, d, axis=0), rolled[:SUBLANES])
                shifted.append(jnp.concatenate([top, rolled[SUBLANES:]], axis=0))
            g1, g2 = shifted
            halo_s[:, lo:hi] = gate[tm - SUBLANES:, :]
            tail_ref[0, :, lo:hi] = gate[tm - SUBLANES:, :]
        else:
            r = lax.broadcasted_iota(jnp.int32, (tm, hi - lo), 0) & (ls - 1)
            prev = prev_ref[0, :, lo:hi]
            g1 = jnp.where(r < 1, pltpu.roll(prev, tm - 1, axis=0), pltpu.roll(gate, 1, axis=0))
            g2 = jnp.where(r < 2, prev, pltpu.roll(gate, 2, axis=0))
            tail_ref[0, :, lo:hi] = gate
        cw = fcw_ref[:, lo:hi]
        pre = g2 * cw[0:1] + g1 * cw[1:2] + gate * cw[2:3] + fcb_ref[:, lo:hi]
        act = pre * jax.nn.sigmoid(pre) * val
        part = _dot(act.astype(BF16), wd_ref[lo:hi, :])
        down = part if down is None else down + part
    y_ref[0] = _rms(x1 + down, fg_ref[...])


def _ffn(x, odn, os5, prev, woa, wob, n2, wu, fcw, fcb, wd, fg, tm, ls, carry):
    b, l, _ = x.shape
    seq = lambda w_: pl.BlockSpec((1, tm, w_), lambda i, j: (i, j, 0))
    if carry:
        prev_spec = pl.BlockSpec((1, SUBLANES, D_FF), lambda i, j: (i, 0, 0))
        tail_spec = pl.BlockSpec((1, SUBLANES, D_FF), lambda i, j: (i, 0, 0))
        tail_shape = jax.ShapeDtypeStruct((b, SUBLANES, D_FF), F32)
    else:
        prev_spec = seq(D_FF)
        tail_spec = seq(D_FF)
        tail_shape = jax.ShapeDtypeStruct((b, l, D_FF), F32)
    return pl.pallas_call(
        functools.partial(_ffn_kernel, tm=tm, ls=ls, carry=carry),
        grid=(b, l // tm),
        in_specs=[seq(D_MODEL), seq(DN_WIDTH), seq(S5_WIDTH), prev_spec,
                  _const_spec((DN_WIDTH, D_MODEL)), _const_spec((S5_WIDTH, D_MODEL)),
                  _const_spec((1, D_MODEL)), _const_spec((D_MODEL, 2 * D_FF)),
                  _const_spec((SUBLANES, D_FF)), _const_spec((1, D_FF)), _const_spec((D_FF, D_MODEL)),
                  _const_spec((1, D_MODEL))],
        out_specs=(seq(D_MODEL), tail_spec),
        out_shape=(jax.ShapeDtypeStruct((b, l, D_MODEL), F32), tail_shape),
        scratch_shapes=[pltpu.VMEM((SUBLANES, D_FF), F32)],
        compiler_params=_params(2),
        name="ffn",
    )(x, odn, os5, prev, woa, wob, n2, wu, fcw, fcb, wd, fg)


def _pad_rows_top(a, rows):
    return jnp.pad(a, ((0, 0), (rows - a.shape[1], 0), (0, 0)))


def _quad_blocks(m):
    n = m.shape[0]
    same = jnp.eye(4, dtype=bool)[:, None, :, None]
    blocks = jnp.where(same, m.reshape(n, S5_QUADS, 4, S5_GROUP, 1, S5_STATE), 0.0)
    return blocks.reshape(n, S5_QUADS, 4 * S5_GROUP, QUAD_LANES)


def _trunk(x, conv_dn, s_dn, s5_re, s5_im, conv_ffn, w, prompt):
    b, l, _ = x.shape
    n = b * l
    c = CHUNK if l % CHUNK == 0 else l
    assert l >= 2 * SUBLANES, "sequence shorter than two row tiles"
    cache8 = _pad_rows_top(conv_dn, SUBLANES)
    if prompt:
        u, o_dn, s_dn_new, tail = _mixer(x, w['n1'], w['w_in_main'], w['w_in_tail'], cache8, s_dn,
                                         w['dn_cw'], w['dn_gp'], w['dn_g'],
                                         tt=min(l, DN_ROWS_LONG), c=c)
        conv_dn_new = tail[:, SUBLANES - (DN_CONV - 1):]
    else:
        qkv, z, u, ba = _in_proj(x.reshape(n, D_MODEL), w['n1'], w['w_in_main'], w['w_in_tail'], min(n, INPROJ_ROWS))
        qkv = qkv.reshape(b, l, QKV_WIDTH)
        o_dn, s_dn_new = _deltanet(qkv, z.reshape(b, l, DN_WIDTH), ba.reshape(b, l, LANES), cache8, s_dn,
                                   w['dn_cw'], w['dn_gp'], w['dn_g'], nb=min(b, DN_SEQS_SHORT), tt=l, c=c)
        conv_dn_new = qkv[:, l - (DN_CONV - 1):]

    s5_args = (w['pq'], w['qq'], w['mq'], w['tab_long' if prompt else 'tab_short'], w['dv'], w['glu_w'],
               w['glu_b'], w['s5_g'])
    if prompt:
        x0re = s5_re.reshape(b, 1, S5_LANES)
        x0im = s5_im.reshape(b, 1, S5_LANES)
        o_s5, fre, fim = _s5_chunked(u.reshape(b, l, S5_WIDTH), x0re, x0im, *s5_args,
                                     rows=min(l // S5_SLOTS, S5_CHUNKS_LONG), seg=SUBLANES)
    else:
        seg = l // S5_SLOTS
        rep = lambda s: jnp.repeat(s.reshape(b, S5_LANES), seg, axis=0).reshape(1, b * seg, S5_LANES)
        o_s5, fre, fim = _s5_chunked(u.reshape(1, n, S5_WIDTH), rep(s5_re), rep(s5_im), *s5_args,
                                     rows=b * seg, seg=seg)
        o_s5 = o_s5.reshape(b, l, S5_WIDTH)
        fre = fre.reshape(b, seg, S5_LANES)[:, seg - 1]
        fim = fim.reshape(b, seg, S5_LANES)[:, seg - 1]
    s5_re_new = fre.reshape(b, S5_GROUPS, S5_STATE)
    s5_im_new = fim.reshape(b, S5_GROUPS, S5_STATE)

    ffn_args = (w['w_out_a'], w['w_out_b'], w['n2'], w['w_up'], w['fcw'], w['fcb'],
                w['w_down'], w['fg'])
    if prompt:
        prev = _pad_rows_top(conv_ffn, SUBLANES)
        tm = min(l, FFN_ROWS_LONG)
        y, tail = _ffn(x, o_dn, o_s5, prev, *ffn_args, tm=tm, ls=tm, carry=True)
        conv_ffn_new = tail[:, SUBLANES - (FFN_CONV - 1):]
    else:
        prev = jnp.pad(conv_ffn, ((0, 0), (0, l - (FFN_CONV - 1)), (0, 0))).reshape(1, n, D_FF)
        y, tail = _ffn(x.reshape(1, n, D_MODEL), o_dn.reshape(1, n, DN_WIDTH), o_s5.reshape(1, n, S5_WIDTH),
                       prev, *ffn_args, tm=min(n, FFN_ROWS_SHORT), ls=l, carry=False)
        y = y.reshape(b, l, D_MODEL)
        conv_ffn_new = tail.reshape(b, l, D_FF)[:, l - (FFN_CONV - 1):]
    return y, (conv_dn_new[None], s_dn_new[None], s5_re_new[None], s5_im_new[None], conv_ffn_new[None])


def kernel(x_prompt, x_sample, cache_dn_conv, state_dn, state_s5_re, state_s5_im, cache_ffn_conv, norm1_g, w_in, dn_conv_w, dn_A_log, dn_dt_bias, dn_norm_g, s5_A_re, s5_A_im, s5_log_dt, s5_B_re, s5_B_im, s5_C_re, s5_C_im, s5_D, s5_glu_w, s5_glu_b, s5_norm_g, w_out, norm2_g, w_up, ffn_conv_w, ffn_conv_b, w_down, final_norm_g):
    assert w_in.shape[0] == 1, "single-layer trunk"
    o1 = QKV_WIDTH
    o2 = o1 + DN_WIDTH
    o4 = o2 + 2 * DN_HEADS
    wi = w_in[0]
    w_in_main = wi[:, :o2].astype(BF16)
    w_in_tail = jnp.concatenate(
        [wi[:, o4:], wi[:, o2:o4], jnp.zeros((D_MODEL, LANES - 2 * DN_HEADS), wi.dtype)], axis=1).astype(BF16)
    lane_pad = lambda v: jnp.pad(v, (BA_LANE_G, LANES - BA_LANE_G - DN_HEADS))
    seg_short = x_sample.shape[1] // S5_SLOTS
    assert x_sample.shape[1] % S5_SLOTS == 0 and seg_short in (1, 2, 4), "sample sequences of 8, 16 or 32 rows"
    a3 = jnp.concatenate([s5_A_re, s5_A_im, jnp.broadcast_to(s5_log_dt[:, :, None], s5_A_re.shape)], axis=0)
    a_lanes = a3.reshape(3, 1, S5_LANES)
    tab_long, tab_short = [_s5_prep(a_lanes, seg=s) for s in (SUBLANES, seg_short)]
    bc = jnp.concatenate([s5_B_re.transpose(0, 1, 3, 2), s5_B_im.transpose(0, 1, 3, 2), s5_C_re, s5_C_im], axis=0)
    pq, qq, mq = _s5_prep_quads(a3.reshape(3, S5_QUADS, 1, QUAD_LANES), _quad_blocks(bc))
    w = {
        'n1': norm1_g, 'w_in_main': w_in_main, 'w_in_tail': w_in_tail,
        'dn_cw': jnp.pad(dn_conv_w[0], ((0, SUBLANES - DN_CONV), (0, 0))),
        'dn_gp': jnp.pad(jnp.stack([lane_pad(dn_A_log[0]), lane_pad(dn_dt_bias[0])]), ((0, SUBLANES - 2), (0, 0))),
        'dn_g': dn_norm_g,
        'tab_long': tab_long, 'tab_short': tab_short, 'pq': pq, 'qq': qq, 'mq': mq,
        'dv': s5_D, 'glu_w': s5_glu_w[0].astype(BF16), 'glu_b': s5_glu_b, 's5_g': s5_norm_g,
        'w_out_a': w_out[0, :DN_WIDTH].astype(BF16), 'w_out_b': w_out[0, DN_WIDTH:].astype(BF16),
        'n2': norm2_g, 'w_up': w_up[0].astype(BF16),
        'fcw': jnp.pad(ffn_conv_w[0], ((0, SUBLANES - FFN_CONV), (0, 0))), 'fcb': ffn_conv_b,
        'w_down': w_down[0].astype(BF16), 'fg': final_norm_g.reshape(1, D_MODEL),
    }
    bp = x_prompt.shape[0]
    zeros = lambda *s: jnp.zeros(s, F32)
    y_p, st_p = _trunk(x_prompt, zeros(bp, DN_CONV - 1, QKV_WIDTH), zeros(bp, DN_HEADS, DN_HEAD_DIM, DN_HEAD_DIM),
                       zeros(bp, S5_GROUPS, S5_STATE), zeros(bp, S5_GROUPS, S5_STATE),
                       zeros(bp, FFN_CONV - 1, D_FF), w, prompt=True)
    y_s, st_s = _trunk(x_sample, cache_dn_conv[0], state_dn[0], state_s5_re[0], state_s5_im[0],
                       cache_ffn_conv[0], w, prompt=False)
    return (y_p, y_s) + st_p + st_s
```
